```python
import math
import jax, jax.numpy as jnp
from jax import lax
import numpy as np

D_MODEL = 1024
BATCH = 4
SEQ = 4096
DEPTH = 1
DEC_BATCH = 32
DEC_SEQ = 32
PAST_LEN = 2048

CHUNK = 64
D_CONV = 1024
CONV_WIDTH = 3
D_SSM = 1024
GROUP_SIZE = 16
N_GROUPS = D_SSM // GROUP_SIZE
STATE_DIM = 64
D_FF = 2816
N_IN = 3 * D_CONV + D_SSM + 2 * D_MODEL
RMS_EPS = 1e-6
DT_MIN = 1e-3
DT_MAX = 1e-1

kernel_name = "hybrid_shortconv_s5_macaron_step"


def rms_norm(x, g):
    xf = x.astype(jnp.float32)
    y = xf * lax.rsqrt(jnp.mean(xf * xf, axis=-1, keepdims=True) + RMS_EPS)
    return (y * g.astype(jnp.float32)).astype(x.dtype)


def swiglu(h, wg, wu, wd):
    return (jax.nn.silu(h @ wg) * (h @ wu)) @ wd


def short_conv(z, prev, w):
    L = z.shape[1]
    zp = jnp.concatenate([prev.astype(z.dtype), z], axis=1)
    out = w[0] * zp[:, 0:L]
    for k in range(1, CONV_WIDTH):
        out = out + w[k] * zp[:, k:k + L]
    return out, zp[:, L:]


def s5_scan(u, h0, lam_re, lam_im, log_step, b_re, b_im, c_re, c_im, d_skip):
    f32 = jnp.float32
    bsz, L, _ = u.shape
    uf = u.astype(f32)
    ug = uf.reshape(bsz, L, N_GROUPS, GROUP_SIZE)
    lam = lax.complex(lam_re.astype(f32), lam_im.astype(f32))
    step = jnp.exp(log_step.astype(f32))[:, None]
    lam_bar = jnp.exp(lam * step)
    b_bar = ((lam_bar - 1.0) / lam)[..., None] * lax.complex(b_re.astype(f32), b_im.astype(f32))
    bu = lax.complex(jnp.einsum('gpc,blgc->blgp', jnp.real(b_bar), ug),
                     jnp.einsum('gpc,blgc->blgp', jnp.imag(b_bar), ug))
    a = jnp.broadcast_to(lam_bar, bu.shape)

    def combine(left, right):
        a1, b1 = left
        a2, b2 = right
        return a2 * a1, a2 * b1 + b2

    a_cum, h = lax.associative_scan(combine, (a, bu), axis=1)
    if h0 is not None:
        h = h + a_cum * lax.complex(h0[0].astype(f32), h0[1].astype(f32))[:, None]
    c = lax.complex(c_re.astype(f32), c_im.astype(f32))
    y = jnp.real(jnp.einsum('gcp,blgp->blgc', c, h)).reshape(bsz, L, D_SSM)
    y = y + d_skip.astype(f32) * uf
    h_last = h[:, -1]
    return y.astype(u.dtype), jnp.real(h_last).astype(u.dtype), jnp.imag(h_last).astype(u.dtype)


def mixer(h, conv_prev, h0, w_in, w_conv, w_conv_out, lam_re, lam_im, log_step,
          b_re, b_im, c_re, c_im, d_skip, w_glu, w_o):
    proj = h @ w_in
    v, c_gate, b_gate, u, g_conv, g_ssm = jnp.split(
        proj, [D_CONV, 2 * D_CONV, 3 * D_CONV, 3 * D_CONV + D_SSM, 3 * D_CONV + D_SSM + D_MODEL], axis=-1)
    conv_out, conv_state = short_conv(c_gate * v, conv_prev, w_conv)
    y_conv = (b_gate * conv_out) @ w_conv_out
    y_s, s_re, s_im = s5_scan(u, h0, lam_re, lam_im, log_step, b_re, b_im, c_re, c_im, d_skip)
    glu = jax.nn.gelu(y_s) @ w_glu
    y_ssm = glu[..., :D_MODEL] * jax.nn.sigmoid(glu[..., D_MODEL:])
    merged = jax.nn.sigmoid(g_conv) * y_conv + jax.nn.sigmoid(g_ssm) * y_ssm
    return merged @ w_o, conv_state, s_re, s_im


def trunk(x, conv_prev, ssm_re0, ssm_im0, params):
    (norm_ffn1, w_ffn1_gate, w_ffn1_up, w_ffn1_down, norm_mix, w_in, w_conv, w_conv_out,
     ssm_lambda_re, ssm_lambda_im, ssm_log_step, ssm_b_re, ssm_b_im, ssm_c_re, ssm_c_im, ssm_d,
     w_glu, w_o, norm_ffn2, w_ffn2_gate, w_ffn2_up, w_ffn2_down, norm_final) = params
    convs, res, ims = [], [], []
    for l in range(DEPTH):
        x = x + 0.5 * swiglu(rms_norm(x, norm_ffn1[l]), w_ffn1_gate[l], w_ffn1_up[l], w_ffn1_down[l])
        if conv_prev is None:
            prev = jnp.zeros((x.shape[0], CONV_WIDTH - 1, D_CONV), x.dtype)
            h0 = None
        else:
            prev = conv_prev[l]
            h0 = (ssm_re0[l], ssm_im0[l])
        mix, cs, sre, sim = mixer(rms_norm(x, norm_mix[l]), prev, h0, w_in[l], w_conv[l], w_conv_out[l],
                                  ssm_lambda_re[l], ssm_lambda_im[l], ssm_log_step[l], ssm_b_re[l], ssm_b_im[l],
                                  ssm_c_re[l], ssm_c_im[l], ssm_d[l], w_glu[l], w_o[l])
        x = x + mix
        x = x + 0.5 * swiglu(rms_norm(x, norm_ffn2[l]), w_ffn2_gate[l], w_ffn2_up[l], w_ffn2_down[l])
        convs.append(cs)
        res.append(sre)
        ims.append(sim)
    return rms_norm(x, norm_final), jnp.stack(convs), jnp.stack(res), jnp.stack(ims)


def setup_inputs(seed: int = 0) -> dict:
    key = jax.random.key(seed)
    k = jax.random.split(key, 32)
    f32 = jnp.float32

    def nrm(i, shape, scale):
        return jax.random.normal(k[i], shape, f32) * scale

    def gain(i, shape):
        return 1.0 + 0.01 * jax.random.normal(k[i], shape, f32)

    lam_im = jnp.broadcast_to(math.pi * jnp.arange(STATE_DIM, dtype=f32), (DEPTH, N_GROUPS, STATE_DIM))
    return {
        "x_prompt": nrm(0, (BATCH, SEQ, D_MODEL), 1.0),
        "x_sample": nrm(1, (DEC_BATCH, DEC_SEQ, D_MODEL), 1.0),
        "state_conv": nrm(2, (DEPTH, DEC_BATCH, CONV_WIDTH - 1, D_CONV), 1.0),
        "state_ssm_re": nrm(3, (DEPTH, DEC_BATCH, N_GROUPS, STATE_DIM), 0.1),
        "state_ssm_im": nrm(4, (DEPTH, DEC_BATCH, N_GROUPS, STATE_DIM), 0.1),
        "norm_ffn1": gain(5, (DEPTH, D_MODEL)),
        "w_ffn1_gate": nrm(6, (DEPTH, D_MODEL, D_FF), D_MODEL ** -0.5),
        "w_ffn1_up": nrm(7, (DEPTH, D_MODEL, D_FF), D_MODEL ** -0.5),
        "w_ffn1_down": nrm(8, (DEPTH, D_FF, D_MODEL), D_FF ** -0.5),
        "norm_mix": gain(9, (DEPTH, D_MODEL)),
        "w_in": nrm(10, (DEPTH, D_MODEL, N_IN), D_MODEL ** -0.5),
        "w_conv": nrm(11, (DEPTH, CONV_WIDTH, D_CONV), CONV_WIDTH ** -0.5),
        "w_conv_out": nrm(12, (DEPTH, D_CONV, D_MODEL), D_CONV ** -0.5),
        "ssm_lambda_re": -0.5 + nrm(13, (DEPTH, N_GROUPS, STATE_DIM), 0.01),
        "ssm_lambda_im": lam_im + nrm(14, (DEPTH, N_GROUPS, STATE_DIM), 0.01),
        "ssm_log_step": jax.random.uniform(k[15], (DEPTH, N_GROUPS), f32,
                                           minval=math.log(DT_MIN), maxval=math.log(DT_MAX)),
        "ssm_b_re": nrm(16, (DEPTH, N_GROUPS, STATE_DIM, GROUP_SIZE), (2 * GROUP_SIZE) ** -0.5),
        "ssm_b_im": nrm(17, (DEPTH, N_GROUPS, STATE_DIM, GROUP_SIZE), (2 * GROUP_SIZE) ** -0.5),
        "ssm_c_re": nrm(18, (DEPTH, N_GROUPS, GROUP_SIZE, STATE_DIM), STATE_DIM ** -0.5),
        "ssm_c_im": nrm(19, (DEPTH, N_GROUPS, GROUP_SIZE, STATE_DIM), STATE_DIM ** -0.5),
        "ssm_d": nrm(20, (DEPTH, D_SSM), 1.0),
        "w_glu": nrm(21, (DEPTH, D_SSM, 2 * D_MODEL), D_SSM ** -0.5),
        "w_o": nrm(22, (DEPTH, D_MODEL, D_MODEL), D_MODEL ** -0.5),
        "norm_ffn2": gain(23, (DEPTH, D_MODEL)),
        "w_ffn2_gate": nrm(24, (DEPTH, D_MODEL, D_FF), D_MODEL ** -0.5),
        "w_ffn2_up": nrm(25, (DEPTH, D_MODEL, D_FF), D_MODEL ** -0.5),
        "w_ffn2_down": nrm(26, (DEPTH, D_FF, D_MODEL), D_FF ** -0.5),
        "norm_final": gain(27, (D_MODEL,)),
    }


def reference(x_prompt, x_sample, state_conv, state_ssm_re, state_ssm_im,
              norm_ffn1, w_ffn1_gate, w_ffn1_up, w_ffn1_down, norm_mix, w_in, w_conv, w_conv_out,
              ssm_lambda_re, ssm_lambda_im, ssm_log_step, ssm_b_re, ssm_b_im, ssm_c_re, ssm_c_im, ssm_d,
              w_glu, w_o, norm_ffn2, w_ffn2_gate, w_ffn2_up, w_ffn2_down, norm_final):
    params = (norm_ffn1, w_ffn1_gate, w_ffn1_up, w_ffn1_down, norm_mix, w_in, w_conv, w_conv_out,
              ssm_lambda_re, ssm_lambda_im, ssm_log_step, ssm_b_re, ssm_b_im, ssm_c_re, ssm_c_im, ssm_d,
              w_glu, w_o, norm_ffn2, w_ffn2_gate, w_ffn2_up, w_ffn2_down, norm_final)
    y_prompt, conv_p, ssm_re_p, ssm_im_p = trunk(x_prompt, None, None, None, params)
    y_sample, conv_s, ssm_re_s, ssm_im_s = trunk(x_sample, state_conv, state_ssm_re, state_ssm_im, params)
    return (y_prompt, y_sample, conv_p, ssm_re_p, ssm_im_p, conv_s, ssm_re_s, ssm_im_s)
```

```python
import functools

import jax
import jax.numpy as jnp
from jax import lax
from jax.experimental import pallas as pl
from jax.experimental.pallas import tpu as pltpu

F32 = jnp.float32
BF16 = jnp.bfloat16

D_MODEL = 1024
N_PHASE = 8
LANE = 128
SLAB = 128
N_SLAB = D_MODEL // SLAB
GROUPS_PER_SLAB = 8
STATE_DIM = 64
SLAB_STATES = GROUPS_PER_SLAB * STATE_DIM
SEG = 4
RMS_EPS = 1e-6
VMEM_LIMIT = 56 * 1024 * 1024

FFN_ROWS = 512
MIX_ROWS = 64
SSM_ROWS = 256


def _rms(x, g):
    return x * lax.rsqrt(jnp.mean(x * x, axis=-1, keepdims=True) + RMS_EPS) * g


def _dot(a, b):
    return jnp.dot(a, b, preferred_element_type=F32)


def _const_spec(shape):
    zeros = (0,) * len(shape)
    return pl.BlockSpec(shape, lambda *_: zeros, pipeline_mode=pl.Buffered(1))


def _params(semantics):
    return pltpu.CompilerParams(dimension_semantics=semantics, vmem_limit_bytes=VMEM_LIMIT)


def _ffn_body(x_ref, g_ref, wg_ref, wu_ref, wd_ref, gf_ref, o_ref, *, final_norm):
    x = x_ref[...]
    h = _rms(x, g_ref[...]).astype(BF16)
    act = (jax.nn.silu(_dot(h, wg_ref[...])) * _dot(h, wu_ref[...])).astype(BF16)
    y = x + 0.5 * _dot(act, wd_ref[...])
    if final_norm:
        y = _rms(y, gf_ref[...])
    o_ref[...] = y


def _ffn(x, g, wg, wu, wd, gf, final_norm):
    n_tok, d = x.shape
    d_ff = wg.shape[1]
    rows = min(FFN_ROWS, n_tok)
    assert n_tok % rows == 0
    return pl.pallas_call(
        functools.partial(_ffn_body, final_norm=final_norm),
        grid=(n_tok // rows,),
        in_specs=[
            pl.BlockSpec((rows, d), lambda i: (i, 0)),
            _const_spec((1, d)),
            _const_spec((d, d_ff)),
            _const_spec((d, d_ff)),
            _const_spec((d_ff, d)),
            _const_spec((1, d)),
        ],
        out_specs=pl.BlockSpec((rows, d), lambda i: (i, 0)),
        out_shape=jax.ShapeDtypeStruct((n_tok, d), F32),
        name="ffn_final" if final_norm else "ffn",
        compiler_params=_params(("arbitrary",)),
    )(x, g, wg, wu, wd, gf)


def _mix_in_compute(x_blk, g, win_ref, wcv, wco_ref, shift, rows):
    d = D_MODEL
    x = jnp.concatenate([x_blk[:, t * d:(t + 1) * d] for t in range(N_PHASE)], axis=0)
    h = _rms(x, g).astype(BF16)

    def proj(k):
        return _dot(h, win_ref[:, k * d:(k + 1) * d])

    z = proj(1) * proj(0)
    z6 = z[6 * rows:7 * rows]
    z7 = z[7 * rows:8 * rows]
    s6 = shift(z6, 0)
    s7 = shift(z7, 1)
    z1 = jnp.concatenate([s7, z[:7 * rows]], axis=0)
    z2 = jnp.concatenate([s6, s7, z[:6 * rows]], axis=0)
    conv = wcv[2:3] * z + wcv[1:2] * z1 + wcv[0:1] * z2
    y_conv = _dot((proj(2) * conv).astype(BF16), wco_ref[...])
    u = proj(3)
    a = jax.nn.sigmoid(proj(4)) * y_conv
    gs = jax.nn.sigmoid(proj(5))
    return u, a, gs, z6, z7


def _store_phases(ref, val, rows):
    for t in range(N_PHASE):
        ref[t] = val[t * rows:(t + 1) * rows]


def _mix_in_prompt_body(x_ref, g_ref, win_ref, wcv_ref, wco_ref,
                        u_ref, a_ref, gs_ref, z6_ref, z7_ref, carry_ref, *, rows):
    @pl.when(pl.program_id(1) == 0)
    def _():
        carry_ref[...] = jnp.zeros_like(carry_ref)

    row_id = lax.broadcasted_iota(jnp.int32, (rows, D_MODEL), 0)

    def shift(z, k):
        return jnp.where(row_id == 0, carry_ref[k:k + 1, :], pltpu.roll(z, 1, 0))

    u, a, gs, z6, z7 = _mix_in_compute(x_ref[...], g_ref[...], win_ref, wcv_ref[...], wco_ref, shift, rows)
    last6 = z6[rows - 1:rows]
    last7 = z7[rows - 1:rows]
    carry_ref[0:1, :] = last6
    carry_ref[1:2, :] = last7
    z6_ref[...] = last6
    z7_ref[...] = last7
    _store_phases(u_ref, u, rows)
    _store_phases(a_ref, a, rows)
    _store_phases(gs_ref, gs, rows)


def _mix_in_sample_body(x_ref, s6_ref, s7_ref, g_ref, win_ref, wcv_ref, wco_ref,
                        u_ref, a_ref, gs_ref, z6_ref, z7_ref, *, rows, rows_per_seq):
    row_id = lax.broadcasted_iota(jnp.int32, (rows, D_MODEL), 0)
    starts = (s6_ref, s7_ref)

    def shift(z, k):
        return jnp.where(row_id % rows_per_seq == 0, starts[k][...], pltpu.roll(z, 1, 0))

    u, a, gs, z6, z7 = _mix_in_compute(x_ref[...], g_ref[...], win_ref, wcv_ref[...], wco_ref, shift, rows)
    z6_ref[...] = z6
    z7_ref[...] = z7
    _store_phases(u_ref, u, rows)
    _store_phases(a_ref, a, rows)
    _store_phases(gs_ref, gs, rows)


def _mix_in(xp, conv_prev, g, w_in, w_conv, w_conv_out, n_seq):
    n_rows = xp.shape[0]
    d = D_MODEL
    rows_per_seq = n_rows // n_seq
    ph_shape = jax.ShapeDtypeStruct((N_PHASE, n_rows, d), F32)
    weights = (g, w_in, w_conv, w_conv_out)
    weight_specs = [_const_spec(w.shape) for w in weights]
    if conv_prev is None:
        rows = min(MIX_ROWS, rows_per_seq)
        assert rows_per_seq % rows == 0
        tiles = rows_per_seq // rows
        ph_spec = pl.BlockSpec((N_PHASE, rows, d), lambda b, i: (0, b * tiles + i, 0))
        last_spec = pl.BlockSpec((None, 1, d), lambda b, i: (b, 0, 0))
        u8, a8, gs8, z6, z7 = pl.pallas_call(
            functools.partial(_mix_in_prompt_body, rows=rows),
            grid=(n_seq, tiles),
            in_specs=[pl.BlockSpec((rows, N_PHASE * d), lambda b, i: (b * tiles + i, 0))] + weight_specs,
            out_specs=[ph_spec, ph_spec, ph_spec, last_spec, last_spec],
            out_shape=[ph_shape, ph_shape, ph_shape,
                       jax.ShapeDtypeStruct((n_seq, 1, d), F32), jax.ShapeDtypeStruct((n_seq, 1, d), F32)],
            scratch_shapes=[pltpu.VMEM((8, d), F32)],
            name="mix_in_prompt",
            compiler_params=_params(("arbitrary", "arbitrary")),
        )(xp, *weights)
        z6, z7 = z6[:, 0], z7[:, 0]
    else:
        rows = n_rows
        start = jnp.zeros((n_seq, rows_per_seq, 2, d), F32).at[:, 0].set(conv_prev).reshape(n_rows, 2, d)
        full = lambda shape: pl.BlockSpec(shape, lambda i: (0,) * len(shape))
        u8, a8, gs8, z6f, z7f = pl.pallas_call(
            functools.partial(_mix_in_sample_body, rows=rows, rows_per_seq=rows_per_seq),
            grid=(1,),
            in_specs=[full((rows, N_PHASE * d)), full((rows, d)), full((rows, d))] + weight_specs,
            out_specs=[full((N_PHASE, rows, d))] * 3 + [full((rows, d))] * 2,
            out_shape=[ph_shape, ph_shape, ph_shape,
                       jax.ShapeDtypeStruct((n_rows, d), F32), jax.ShapeDtypeStruct((n_rows, d), F32)],
            name="mix_in_sample",
            compiler_params=_params(("arbitrary",)),
        )(xp, start[:, 0], start[:, 1], *weights)
        z6 = z6f[rows_per_seq - 1::rows_per_seq]
        z7 = z7f[rows_per_seq - 1::rows_per_seq]
    return u8, a8, gs8, jnp.stack([z6, z7], axis=1)


def _ssm_prep_body(bbre_ref, bbim_ref, ctre_ref, ctim_ref, lre_ref, lim_ref, lstep_ref,
                   wb_ref, kt_ref, wc_ref, pw_ref):
    ns = SLAB_STATES
    bbre, bbim = bbre_ref[...], bbim_ref[...]
    ctre, ctim = ctre_ref[...], ctim_ref[...]
    lre, lim = lre_ref[...], lim_ref[...]
    step = jnp.exp(lstep_ref[...])

    def lam_pow(n):
        mag = jnp.exp((n * lre) * step)
        ang = (n * lim) * step
        return mag * jnp.cos(ang), mag * jnp.sin(ang)

    l1re, l1im = lam_pow(1)
    den = lre * lre + lim * lim
    fre = ((l1re - 1.0) * lre + l1im * lim) / den
    fim = (l1im * lre - (l1re - 1.0) * lim) / den

    def dot_nt(a, b):
        return lax.dot_general(a, b, (((1,), (1,)), ((), ())),
                               precision=lax.Precision.HIGHEST, preferred_element_type=F32)

    kt_ref[...] = jnp.zeros_like(kt_ref)
    for k in range(N_PHASE):
        pre, pim = lam_pow(k)
        gre = fre * pre - fim * pim
        gim = fre * pim + fim * pre
        are = bbre * gre - bbim * gim
        aim = bbre * gim + bbim * gre
        j = N_PHASE - 1 - k
        wb_ref[j * SLAB:(j + 1) * SLAB, 0:ns] = are.astype(BF16)
        wb_ref[j * SLAB:(j + 1) * SLAB, ns:2 * ns] = aim.astype(BF16)
        kk = (dot_nt(are, ctre) - dot_nt(aim, ctim)).astype(BF16)
        for jj in range(N_PHASE - k):
            t = jj + k
            kt_ref[jj * SLAB:(jj + 1) * SLAB, t * SLAB:(t + 1) * SLAB] = kk
    for t in range(N_PHASE):
        pre, pim = lam_pow(t + 1)
        cre = ctre * pre - ctim * pim
        cim = ctre * pim + ctim * pre
        wc_ref[0:ns, t * SLAB:(t + 1) * SLAB] = cre.T.astype(BF16)
        wc_ref[ns:2 * ns, t * SLAB:(t + 1) * SLAB] = (-cim).T.astype(BF16)
    pw_ref[...] = jnp.zeros_like(pw_ref)
    for r, n in enumerate((N_PHASE, N_PHASE * SEG)):
        pre, pim = lam_pow(n)
        pw_ref[r:r + 1, 0:ns] = pre
        pw_ref[r:r + 1, ns:2 * ns] = pim


def _ssm_prep(lam_re, lam_im, log_step, b_re, b_im, c_re, c_im):
    ns = SLAB_STATES

    def block_diag(m):
        m = m.reshape(N_SLAB, GROUPS_PER_SLAB, 16, STATE_DIM)
        eye = jnp.eye(GROUPS_PER_SLAB, dtype=bool)[None, :, None, :, None]
        return jnp.where(eye, m[:, :, :, None, :], 0.0).reshape(N_SLAB, SLAB, ns)

    def vec(v):
        return v.reshape(N_SLAB, 1, ns)

    mats = [block_diag(jnp.transpose(b_re, (0, 2, 1))), block_diag(jnp.transpose(b_im, (0, 2, 1))),
            block_diag(c_re), block_diag(c_im)]
    vecs = [vec(lam_re), vec(lam_im), vec(jnp.broadcast_to(log_step[:, None], lam_re.shape))]
    w_shape = jax.ShapeDtypeStruct((N_SLAB, 2 * ns, 2 * ns), BF16)
    w_spec = pl.BlockSpec((None, 2 * ns, 2 * ns), lambda q: (q, 0, 0))
    return pl.pallas_call(
        _ssm_prep_body,
        grid=(N_SLAB,),
        in_specs=[pl.BlockSpec((None, SLAB, ns), lambda q: (q, 0, 0))] * 4
        + [pl.BlockSpec((None, 1, ns), lambda q: (q, 0, 0))] * 3,
        out_specs=[w_spec, w_spec, w_spec, pl.BlockSpec((None, 8, 2 * ns), lambda q: (q, 0, 0))],
        out_shape=[w_shape, w_shape, w_shape, jax.ShapeDtypeStruct((N_SLAB, 8, 2 * ns), F32)],
        name="ssm_prep",
        compiler_params=_params(("arbitrary",)),
    )(*mats, *vecs)


def _cmul(are, aim, bre, bim):
    return are * bre - aim * bim, are * bim + aim * bre


def _ssm_body(*refs, rows, has_state):
    if has_state:
        (u_ref, wb_ref, kt_ref, wc_ref, pw_ref, d_ref, h0re_ref, h0im_ref,
         y_ref, sre_ref, sim_ref, e_ref, p_ref, end_ref, cin_ref) = refs
    else:
        (u_ref, wb_ref, kt_ref, wc_ref, pw_ref, d_ref,
         y_ref, sre_ref, sim_ref, e_ref, p_ref, end_ref, cin_ref, carry_ref) = refs
    half = SLAB_STATES // LANE
    n_seg = rows // SEG

    lhs = jnp.concatenate([u_ref[t] for t in range(N_PHASE)], axis=1)
    lhs_b = lhs.astype(BF16)
    e = _dot(lhs_b, wb_ref[...])
    for lt in range(2 * half):
        e_ref[lt] = e[:, lt * LANE:(lt + 1) * LANE]

    def mult(r, c):
        return (pw_ref[r:r + 1, c * LANE:(c + 1) * LANE],
                pw_ref[r:r + 1, (half + c) * LANE:(half + c + 1) * LANE])

    for c in range(half):
        l8re, l8im = mult(0, c)
        for j in range(rows // (8 * SEG)):
            lre = jnp.zeros((8, LANE), F32)
            lim = jnp.zeros((8, LANE), F32)
            for i in range(SEG):
                idx = pl.ds(j * 8 * SEG + i, 8, stride=SEG)
                p_ref[c, idx, :] = lre
                p_ref[half + c, idx, :] = lim
                mre, mim = _cmul(l8re, l8im, lre, lim)
                lre = mre + e_ref[c, idx, :]
                lim = mim + e_ref[half + c, idx, :]
            end_ref[c, j * 8:(j + 1) * 8, :] = lre
            end_ref[half + c, j * 8:(j + 1) * 8, :] = lim

    if has_state:
        for c in range(half):
            cre = h0re_ref[:, c * LANE:(c + 1) * LANE]
            cim = h0im_ref[:, c * LANE:(c + 1) * LANE]
            cin_ref[c] = cre
            cin_ref[half + c] = cim
            lsre, lsim = mult(1, c)
            mre, mim = _cmul(lsre, lsim, cre, cim)
            sre_ref[:, c * LANE:(c + 1) * LANE] = mre + end_ref[c]
            sim_ref[:, c * LANE:(c + 1) * LANE] = mim + end_ref[half + c]
    else:
        @pl.when(pl.program_id(2) == 0)
        def _():
            carry_ref[...] = jnp.zeros_like(carry_ref)

        mults = [mult(1, c) for c in range(half)]

        def chain(s, carry):
            out = []
            for c in range(half):
                cre, cim = carry[2 * c], carry[2 * c + 1]
                cin_ref[c, pl.ds(s, 1), :] = cre
                cin_ref[half + c, pl.ds(s, 1), :] = cim
                mre, mim = _cmul(mults[c][0], mults[c][1], cre, cim)
                out.append(mre + end_ref[c, pl.ds(s, 1), :])
                out.append(mim + end_ref[half + c, pl.ds(s, 1), :])
            return tuple(out)

        carry0 = []
        for c in range(half):
            carry0 += [carry_ref[c, 0:1, :], carry_ref[half + c, 0:1, :]]
        carry = lax.fori_loop(0, n_seg, chain, tuple(carry0))
        for c in range(half):
            carry_ref[c, 0:1, :] = carry[2 * c]
            carry_ref[half + c, 0:1, :] = carry[2 * c + 1]
            sre_ref[:, c * LANE:(c + 1) * LANE] = carry[2 * c]
            sim_ref[:, c * LANE:(c + 1) * LANE] = carry[2 * c + 1]

    for c in range(half):
        l8re, l8im = mult(0, c)
        for j in range(rows // (8 * SEG)):
            cre = cin_ref[c, j * 8:(j + 1) * 8, :]
            cim = cin_ref[half + c, j * 8:(j + 1) * 8, :]
            for i in range(SEG):
                idx = pl.ds(j * 8 * SEG + i, 8, stride=SEG)
                p_ref[c, idx, :] = p_ref[c, idx, :] + cre
                p_ref[half + c, idx, :] = p_ref[half + c, idx, :] + cim
                cre, cim = _cmul(l8re, l8im, cre, cim)

    prev = jnp.concatenate([p_ref[lt] for lt in range(2 * half)], axis=1)
    y = _dot(lhs_b, kt_ref[...]) + _dot(prev.astype(BF16), wc_ref[...]) + d_ref[...] * lhs
    for t in range(N_PHASE):
        y_ref[t] = y[:, t * LANE:(t + 1) * LANE]


def _ssm(u8, h0, wb, kt, wc, pw, d8, n_seq):
    n_rows = u8.shape[1]
    d = D_MODEL
    ns = SLAB_STATES
    rows_per_seq = n_rows // n_seq
    has_state = h0 is not None
    if has_state:
        assert rows_per_seq == SEG, "carried-state path scans one segment per sequence"
        rows, tiles, grid = n_rows, 1, (N_SLAB,)
        row_block = lambda q: (0, 0, q)
        slab = lambda q: (q, 0, 0)
        state_spec = pl.BlockSpec((n_seq, ns), lambda q: (0, q))
        state_shape = jax.ShapeDtypeStruct((n_seq, N_SLAB * ns), F32)
    else:
        rows = min(SSM_ROWS, rows_per_seq)
        assert rows_per_seq % rows == 0
        tiles = rows_per_seq // rows
        grid = (N_SLAB, n_seq, tiles)
        row_block = lambda q, b, i: (0, b * tiles + i, q)
        slab = lambda q, b, i: (q, 0, 0)
        state_spec = pl.BlockSpec((None, 1, ns), lambda q, b, i: (b, 0, q))
        state_shape = jax.ShapeDtypeStruct((n_seq, 1, N_SLAB * ns), F32)
    assert rows % (8 * SEG) == 0
    ph_spec = pl.BlockSpec((N_PHASE, rows, LANE), row_block)
    w_spec = pl.BlockSpec((None, 2 * ns, 2 * ns), slab)
    in_specs = [ph_spec, w_spec, w_spec, w_spec,
                pl.BlockSpec((None, 8, 2 * ns), slab), pl.BlockSpec((None, 1, d), slab)]
    args = [u8, wb, kt, wc, pw, d8]
    scratch = [pltpu.VMEM((2 * ns // LANE, rows, LANE), F32),
               pltpu.VMEM((2 * ns // LANE, rows, LANE), F32),
               pltpu.VMEM((2 * ns // LANE, rows // SEG, LANE), F32),
               pltpu.VMEM((2 * ns // LANE, rows // SEG, LANE), F32)]
    if has_state:
        in_specs += [state_spec, state_spec]
        args += [h0[0].reshape(n_seq, -1), h0[1].reshape(n_seq, -1)]
    else:
        scratch.append(pltpu.VMEM((2 * ns // LANE, 8, LANE), F32))
    return pl.pallas_call(
        functools.partial(_ssm_body, rows=rows, has_state=has_state),
        grid=grid,
        in_specs=in_specs,
        out_specs=[ph_spec, state_spec, state_spec],
        out_shape=[jax.ShapeDtypeStruct((N_PHASE, n_rows, d), F32), state_shape, state_shape],
        scratch_shapes=scratch,
        name="ssm_sample" if has_state else "ssm_prompt",
        compiler_params=_params(("arbitrary",) * len(grid)),
    )(*args)


def _mix_out_body(y_ref, a_ref, gs_ref, x_ref, wglu_ref, wo_ref, o_ref, *, rows):
    d = D_MODEL
    stack = lambda ref: jnp.concatenate([ref[t] for t in range(N_PHASE)], axis=0)
    glu = _dot(jax.nn.gelu(stack(y_ref)).astype(BF16), wglu_ref[...])
    y_ssm = glu[:, :d] * jax.nn.sigmoid(glu[:, d:])
    merged = stack(a_ref) + stack(gs_ref) * y_ssm
    mix = _dot(merged.astype(BF16), wo_ref[...])
    for t in range(N_PHASE):
        o_ref[:, t * d:(t + 1) * d] = x_ref[:, t * d:(t + 1) * d] + mix[t * rows:(t + 1) * rows]


def _mix_out(y8, a8, gs8, xp, w_glu, w_o):
    n_rows = xp.shape[0]
    d = D_MODEL
    rows = min(MIX_ROWS, n_rows)
    assert n_rows % rows == 0
    ph_spec = pl.BlockSpec((N_PHASE, rows, d), lambda i: (0, i, 0))
    x_spec = pl.BlockSpec((rows, N_PHASE * d), lambda i: (i, 0))
    return pl.pallas_call(
        functools.partial(_mix_out_body, rows=rows),
        grid=(n_rows // rows,),
        in_specs=[ph_spec, ph_spec, ph_spec, x_spec, _const_spec(w_glu.shape), _const_spec(w_o.shape)],
        out_specs=x_spec,
        out_shape=jax.ShapeDtypeStruct((n_rows, N_PHASE * d), F32),
        name="mix_out",
        compiler_params=_params(("arbitrary",)),
    )(y8, a8, gs8, xp, w_glu, w_o)


def _trunk(x, conv_prev, ssm_re0, ssm_im0, layers, norm_final):
    n_seq, seq_len, d = x.shape
    n_tok = n_seq * seq_len
    n_rows = n_tok // N_PHASE
    xt = x.reshape(n_tok, d)
    convs, res, ims = [], [], []
    for l, w in enumerate(layers):
        has_state = conv_prev is not None
        x1 = _ffn(xt, w["norm_ffn1"], w["ffn1_gate"], w["ffn1_up"], w["ffn1_down"], norm_final, False)
        xp = x1.reshape(n_rows, N_PHASE * d)
        u8, a8, gs8, conv_state = _mix_in(xp, conv_prev[l] if has_state else None,
                                          w["norm_mix"], w["w_in"], w["w_conv"], w["w_conv_out"], n_seq)
        h0 = (ssm_re0[l], ssm_im0[l]) if has_state else None
        y8, s_re, s_im = _ssm(u8, h0, *w["ssm"], n_seq)
        x2 = _mix_out(y8, a8, gs8, xp, w["w_glu"], w["w_o"])
        xt = _ffn(x2.reshape(n_tok, d), w["norm_ffn2"], w["ffn2_gate"], w["ffn2_up"], w["ffn2_down"],
                  norm_final, l == len(layers) - 1)
        convs.append(conv_state)
        res.append(s_re.reshape(n_seq, -1, STATE_DIM))
        ims.append(s_im.reshape(n_seq, -1, STATE_DIM))
    return xt.reshape(n_seq, seq_len, d), jnp.stack(convs), jnp.stack(res), jnp.stack(ims)


def kernel(x_prompt, x_sample, state_conv, state_ssm_re, state_ssm_im, norm_ffn1, w_ffn1_gate, w_ffn1_up, w_ffn1_down, norm_mix, w_in, w_conv, w_conv_out, ssm_lambda_re, ssm_lambda_im, ssm_log_step, ssm_b_re, ssm_b_im, ssm_c_re, ssm_c_im, ssm_d, w_glu, w_o, norm_ffn2, w_ffn2_gate, w_ffn2_up, w_ffn2_down, norm_final):
    depth = w_in.shape[0]
    bf = lambda w: w.astype(BF16)
    row = lambda v: v.reshape(1, -1)
    layers = []
    for l in range(depth):
        wb, kt, wc, pw = _ssm_prep(ssm_lambda_re[l], ssm_lambda_im[l], ssm_log_step[l],
                                   ssm_b_re[l], ssm_b_im[l], ssm_c_re[l], ssm_c_im[l])
        d8 = jnp.tile(ssm_d[l].reshape(N_SLAB, 1, SLAB), (1, 1, N_PHASE))
        layers.append(dict(
            norm_ffn1=row(norm_ffn1[l]), ffn1_gate=bf(w_ffn1_gate[l]), ffn1_up=bf(w_ffn1_up[l]),
            ffn1_down=bf(w_ffn1_down[l]), norm_mix=row(norm_mix[l]), w_in=bf(w_in[l]), w_conv=w_conv[l],
            w_conv_out=bf(w_conv_out[l]), ssm=(wb, kt, wc, pw, d8), w_glu=bf(w_glu[l]), w_o=bf(w_o[l]),
            norm_ffn2=row(norm_ffn2[l]), ffn2_gate=bf(w_ffn2_gate[l]), ffn2_up=bf(w_ffn2_up[l]),
            ffn2_down=bf(w_ffn2_down[l])))
    nf = row(norm_final)
    y_p, conv_p, re_p, im_p = _trunk(x_prompt, None, None, None, layers, nf)
    y_s, conv_s, re_s, im_s = _trunk(x_sample, state_conv, state_ssm_re, state_ssm_im, layers, nf)
    return (y_p, y_s, conv_p, re_p, im_p, conv_s, re_s, im_s)
```

```python
import functools

import jax
import jax.numpy as jnp
from jax import lax
from jax.experimental import pallas as pl
from jax.experimental.pallas import tpu as pltpu

F32 = jnp.float32
BF16 = jnp.bfloat16

D_MODEL = 1024
N_PHASE = 8
LANE = 128
SLAB = 128
N_SLAB = D_MODEL // SLAB
GROUPS_PER_SLAB = 8
STATE_DIM = 64
SLAB_STATES = GROUPS_PER_SLAB * STATE_DIM
SEG = 4
RMS_EPS = 1e-6
VMEM_LIMIT = 56 * 1024 * 1024

FFN_ROWS = 512
MIX_ROWS = 64
SSM_ROWS = 256


def _rms(x, g):
    return x * lax.rsqrt(jnp.mean(x * x, axis=-1, keepdims=True) + RMS_EPS) * g


def _dot(a, b):
    return jnp.dot(a, b, preferred_element_type=F32)


def _const_spec(shape):
    zeros = (0,) * len(shape)
    return pl.BlockSpec(shape, lambda *_: zeros, pipeline_mode=pl.Buffered(1))


def _params(semantics):
    return pltpu.CompilerParams(dimension_semantics=semantics, vmem_limit_bytes=VMEM_LIMIT)


def _stack_phases(ref):
    return jnp.concatenate([ref[t] for t in range(N_PHASE)], axis=0)


def _ffn_math(x, g_ref, wg_ref, wu_ref, wd_ref, gf_ref, final_norm):
    h = _rms(x, g_ref[...]).astype(BF16)
    act = (jax.nn.silu(_dot(h, wg_ref[...])) * _dot(h, wu_ref[...])).astype(BF16)
    y = x + 0.5 * _dot(act, wd_ref[...])
    return _rms(y, gf_ref[...]) if final_norm else y


def _ffn_to_phase_body(x_ref, g_ref, wg_ref, wu_ref, wd_ref, gf_ref, o_ref, slab_ref, *, rows, final_norm):
    y = _ffn_math(x_ref[...], g_ref, wg_ref, wu_ref, wd_ref, gf_ref, final_norm)
    for lt in range(D_MODEL // LANE):
        slab_ref[lt] = y[:, lt * LANE:(lt + 1) * LANE]
        for t in range(N_PHASE):
            o_ref[t, :, lt * LANE:(lt + 1) * LANE] = slab_ref[lt, pl.ds(t, rows, stride=N_PHASE), :]


def _ffn_from_phase_body(x_ref, g_ref, wg_ref, wu_ref, wd_ref, gf_ref, o_ref, slab_ref, *, rows, final_norm):
    y = _ffn_math(_stack_phases(x_ref), g_ref, wg_ref, wu_ref, wd_ref, gf_ref, final_norm)
    for lt in range(D_MODEL // LANE):
        for t in range(N_PHASE):
            slab_ref[lt, pl.ds(t, rows, stride=N_PHASE), :] = y[t * rows:(t + 1) * rows, lt * LANE:(lt + 1) * LANE]
        o_ref[:, lt * LANE:(lt + 1) * LANE] = slab_ref[lt]


def _ffn(x, g, wg, wu, wd, gf, final_norm, to_phase):
    if to_phase:
        n_tok, d = x.shape
    else:
        n_tok, d = x.shape[0] * x.shape[1], x.shape[2]
    n_rows = n_tok // N_PHASE
    d_ff = wg.shape[1]
    rows = FFN_ROWS // N_PHASE
    assert n_rows % rows == 0
    tok_spec = pl.BlockSpec((N_PHASE * rows, d), lambda i: (i, 0))
    phase_spec = pl.BlockSpec((N_PHASE, rows, d), lambda i: (0, i, 0))
    body = _ffn_to_phase_body if to_phase else _ffn_from_phase_body
    out_dims = (N_PHASE, n_rows, d) if to_phase else (n_tok, d)
    return pl.pallas_call(
        functools.partial(body, rows=rows, final_norm=final_norm),
        grid=(n_rows // rows,),
        in_specs=[
            tok_spec if to_phase else phase_spec,
            _const_spec((1, d)),
            _const_spec((d, d_ff)),
            _const_spec((d, d_ff)),
            _const_spec((d_ff, d)),
            _const_spec((1, d)),
        ],
        out_specs=phase_spec if to_phase else tok_spec,
        out_shape=jax.ShapeDtypeStruct(out_dims, F32),
        scratch_shapes=[pltpu.VMEM((d // LANE, N_PHASE * rows, LANE), F32)],
        name="ffn_to_phase" if to_phase else "ffn_from_phase",
        compiler_params=_params(("arbitrary",)),
    )(x, g, wg, wu, wd, gf)


def _mix_in_compute(x_ref, g, win_ref, wcv, wco_ref, shift, rows):
    d = D_MODEL
    h = _rms(_stack_phases(x_ref), g).astype(BF16)

    def proj(k):
        return _dot(h, win_ref[:, k * d:(k + 1) * d])

    z = proj(1) * proj(0)
    z6 = z[6 * rows:7 * rows]
    z7 = z[7 * rows:8 * rows]
    s6 = shift(z6, 0)
    s7 = shift(z7, 1)
    z1 = jnp.concatenate([s7, z[:7 * rows]], axis=0)
    z2 = jnp.concatenate([s6, s7, z[:6 * rows]], axis=0)
    conv = wcv[2:3] * z + wcv[1:2] * z1 + wcv[0:1] * z2
    y_conv = _dot((proj(2) * conv).astype(BF16), wco_ref[...])
    u = proj(3)
    a = jax.nn.sigmoid(proj(4)) * y_conv
    gs = jax.nn.sigmoid(proj(5))
    return u, a, gs, z6, z7


def _store_phases(ref, val, rows):
    for t in range(N_PHASE):
        ref[t] = val[t * rows:(t + 1) * rows]


def _mix_in_prompt_body(x_ref, g_ref, win_ref, wcv_ref, wco_ref,
                        u_ref, a_ref, gs_ref, z6_ref, z7_ref, carry_ref, *, rows):
    @pl.when(pl.program_id(1) == 0)
    def _():
        carry_ref[...] = jnp.zeros_like(carry_ref)

    row_id = lax.broadcasted_iota(jnp.int32, (rows, D_MODEL), 0)

    def shift(z, k):
        return jnp.where(row_id == 0, carry_ref[k:k + 1, :], pltpu.roll(z, 1, 0))

    u, a, gs, z6, z7 = _mix_in_compute(x_ref, g_ref[...], win_ref, wcv_ref[...], wco_ref, shift, rows)
    last6 = z6[rows - 1:rows]
    last7 = z7[rows - 1:rows]
    carry_ref[0:1, :] = last6
    carry_ref[1:2, :] = last7
    z6_ref[...] = last6
    z7_ref[...] = last7
    _store_phases(u_ref, u, rows)
    _store_phases(a_ref, a, rows)
    _store_phases(gs_ref, gs, rows)


def _mix_in_sample_body(x_ref, s6_ref, s7_ref, g_ref, win_ref, wcv_ref, wco_ref,
                        u_ref, a_ref, gs_ref, z6_ref, z7_ref, *, rows, rows_per_seq):
    row_id = lax.broadcasted_iota(jnp.int32, (rows, D_MODEL), 0)
    starts = (s6_ref, s7_ref)

    def shift(z, k):
        return jnp.where(row_id % rows_per_seq == 0, starts[k][...], pltpu.roll(z, 1, 0))

    u, a, gs, z6, z7 = _mix_in_compute(x_ref, g_ref[...], win_ref, wcv_ref[...], wco_ref, shift, rows)
    z6_ref[...] = z6
    z7_ref[...] = z7
    _store_phases(u_ref, u, rows)
    _store_phases(a_ref, a, rows)
    _store_phases(gs_ref, gs, rows)


def _mix_in(xp, conv_prev, g, w_in, w_conv, w_conv_out, n_seq):
    n_rows = xp.shape[1]
    d = D_MODEL
    rows_per_seq = n_rows // n_seq
    ph_shape = jax.ShapeDtypeStruct((N_PHASE, n_rows, d), F32)
    weights = (g, w_in, w_conv, w_conv_out)
    weight_specs = [_const_spec(w.shape) for w in weights]
    if conv_prev is None:
        rows = min(MIX_ROWS, rows_per_seq)
        assert rows_per_seq % rows == 0
        tiles = rows_per_seq // rows
        ph_spec = pl.BlockSpec((N_PHASE, rows, d), lambda b, i: (0, b * tiles + i, 0))
        last_spec = pl.BlockSpec((None, 1, d), lambda b, i: (b, 0, 0))
        u8, a8, gs8, z6, z7 = pl.pallas_call(
            functools.partial(_mix_in_prompt_body, rows=rows),
            grid=(n_seq, tiles),
            in_specs=[ph_spec] + weight_specs,
            out_specs=[ph_spec, ph_spec, ph_spec, last_spec, last_spec],
            out_shape=[ph_shape, ph_shape, ph_shape,
                       jax.ShapeDtypeStruct((n_seq, 1, d), F32), jax.ShapeDtypeStruct((n_seq, 1, d), F32)],
            scratch_shapes=[pltpu.VMEM((8, d), F32)],
            name="mix_in_prompt",
            compiler_params=_params(("arbitrary", "arbitrary")),
        )(xp, *weights)
        z6, z7 = z6[:, 0], z7[:, 0]
    else:
        rows = n_rows
        start = jnp.zeros((n_seq, rows_per_seq, 2, d), F32).at[:, 0].set(conv_prev).reshape(n_rows, 2, d)
        full = lambda shape: pl.BlockSpec(shape, lambda i: (0,) * len(shape))
        u8, a8, gs8, z6f, z7f = pl.pallas_call(
            functools.partial(_mix_in_sample_body, rows=rows, rows_per_seq=rows_per_seq),
            grid=(1,),
            in_specs=[full((N_PHASE, rows, d)), full((rows, d)), full((rows, d))] + weight_specs,
            out_specs=[full((N_PHASE, rows, d))] * 3 + [full((rows, d))] * 2,
            out_shape=[ph_shape, ph_shape, ph_shape,
                       jax.ShapeDtypeStruct((n_rows, d), F32), jax.ShapeDtypeStruct((n_rows, d), F32)],
            name="mix_in_sample",
            compiler_params=_params(("arbitrary",)),
        )(xp, start[:, 0], start[:, 1], *weights)
        z6 = z6f[rows_per_seq - 1::rows_per_seq]
        z7 = z7f[rows_per_seq - 1::rows_per_seq]
    return u8, a8, gs8, jnp.stack([z6, z7], axis=1)


def _ssm_prep_body(bbre_ref, bbim_ref, ctre_ref, ctim_ref, lre_ref, lim_ref, lstep_ref,
                   wb_ref, kt_ref, wc_ref, pw_ref):
    ns = SLAB_STATES
    bbre, bbim = bbre_ref[...], bbim_ref[...]
    ctre, ctim = ctre_ref[...], ctim_ref[...]
    lre, lim = lre_ref[...], lim_ref[...]
    step = jnp.exp(lstep_ref[...])

    def lam_pow(n):
        mag = jnp.exp((n * lre) * step)
        ang = (n * lim) * step
        return mag * jnp.cos(ang), mag * jnp.sin(ang)

    l1re, l1im = lam_pow(1)
    den = lre * lre + lim * lim
    fre = ((l1re - 1.0) * lre + l1im * lim) / den
    fim = (l1im * lre - (l1re - 1.0) * lim) / den

    def dot_nt(a, b):
        return lax.dot_general(a, b, (((1,), (1,)), ((), ())),
                               precision=lax.Precision.HIGHEST, preferred_element_type=F32)

    kt_ref[...] = jnp.zeros_like(kt_ref)
    for k in range(N_PHASE):
        pre, pim = lam_pow(k)
        gre = fre * pre - fim * pim
        gim = fre * pim + fim * pre
        are = bbre * gre - bbim * gim
        aim = bbre * gim + bbim * gre
        j = N_PHASE - 1 - k
        wb_ref[j * SLAB:(j + 1) * SLAB, 0:ns] = are.astype(BF16)
        wb_ref[j * SLAB:(j + 1) * SLAB, ns:2 * ns] = aim.astype(BF16)
        kk = (dot_nt(are, ctre) - dot_nt(aim, ctim)).astype(BF16)
        for jj in range(N_PHASE - k):
            t = jj + k
            kt_ref[jj * SLAB:(jj + 1) * SLAB, t * SLAB:(t + 1) * SLAB] = kk
    for t in range(N_PHASE):
        pre, pim = lam_pow(t + 1)
        cre = ctre * pre - ctim * pim
        cim = ctre * pim + ctim * pre
        wc_ref[0:ns, t * SLAB:(t + 1) * SLAB] = cre.T.astype(BF16)
        wc_ref[ns:2 * ns, t * SLAB:(t + 1) * SLAB] = (-cim).T.astype(BF16)
    pw_ref[...] = jnp.zeros_like(pw_ref)
    for r, n in enumerate((N_PHASE, N_PHASE * SEG)):
        pre, pim = lam_pow(n)
        pw_ref[r:r + 1, 0:ns] = pre
        pw_ref[r:r + 1, ns:2 * ns] = pim


def _ssm_prep(lam_re, lam_im, log_step, b_re, b_im, c_re, c_im):
    ns = SLAB_STATES

    def block_diag(m):
        m = m.reshape(N_SLAB, GROUPS_PER_SLAB, 16, STATE_DIM)
        eye = jnp.eye(GROUPS_PER_SLAB, dtype=bool)[None, :, None, :, None]
        return jnp.where(eye, m[:, :, :, None, :], 0.0).reshape(N_SLAB, SLAB, ns)

    def vec(v):
        return v.reshape(N_SLAB, 1, ns)

    mats = [block_diag(jnp.transpose(b_re, (0, 2, 1))), block_diag(jnp.transpose(b_im, (0, 2, 1))),
            block_diag(c_re), block_diag(c_im)]
    vecs = [vec(lam_re), vec(lam_im), vec(jnp.broadcast_to(log_step[:, None], lam_re.shape))]
    w_shape = jax.ShapeDtypeStruct((N_SLAB, 2 * ns, 2 * ns), BF16)
    w_spec = pl.BlockSpec((None, 2 * ns, 2 * ns), lambda q: (q, 0, 0))
    return pl.pallas_call(
        _ssm_prep_body,
        grid=(N_SLAB,),
        in_specs=[pl.BlockSpec((None, SLAB, ns), lambda q: (q, 0, 0))] * 4
        + [pl.BlockSpec((None, 1, ns), lambda q: (q, 0, 0))] * 3,
        out_specs=[w_spec, w_spec, w_spec, pl.BlockSpec((None, 8, 2 * ns), lambda q: (q, 0, 0))],
        out_shape=[w_shape, w_shape, w_shape, jax.ShapeDtypeStruct((N_SLAB, 8, 2 * ns), F32)],
        name="ssm_prep",
        compiler_params=_params(("arbitrary",)),
    )(*mats, *vecs)


def _cmul(are, aim, bre, bim):
    return are * bre - aim * bim, are * bim + aim * bre


def _ssm_body(*refs, rows, has_state):
    if has_state:
        (u_ref, wb_ref, kt_ref, wc_ref, pw_ref, d_ref, h0re_ref, h0im_ref,
         y_ref, sre_ref, sim_ref, e_ref, p_ref, end_ref, cin_ref) = refs
    else:
        (u_ref, wb_ref, kt_ref, wc_ref, pw_ref, d_ref,
         y_ref, sre_ref, sim_ref, e_ref, p_ref, end_ref, cin_ref, carry_ref) = refs
    half = SLAB_STATES // LANE
    n_seg = rows // SEG

    lhs = jnp.concatenate([u_ref[t] for t in range(N_PHASE)], axis=1)
    lhs_b = lhs.astype(BF16)
    e = _dot(lhs_b, wb_ref[...])
    for lt in range(2 * half):
        e_ref[lt] = e[:, lt * LANE:(lt + 1) * LANE]

    def mult(r, c):
        return (pw_ref[r:r + 1, c * LANE:(c + 1) * LANE],
                pw_ref[r:r + 1, (half + c) * LANE:(half + c + 1) * LANE])

    for c in range(half):
        l8re, l8im = mult(0, c)
        for j in range(rows // (8 * SEG)):
            lre = jnp.zeros((8, LANE), F32)
            lim = jnp.zeros((8, LANE), F32)
            for i in range(SEG):
                idx = pl.ds(j * 8 * SEG + i, 8, stride=SEG)
                p_ref[c, idx, :] = lre
                p_ref[half + c, idx, :] = lim
                mre, mim = _cmul(l8re, l8im, lre, lim)
                lre = mre + e_ref[c, idx, :]
                lim = mim + e_ref[half + c, idx, :]
            end_ref[c, j * 8:(j + 1) * 8, :] = lre
            end_ref[half + c, j * 8:(j + 1) * 8, :] = lim

    if has_state:
        for c in range(half):
            cre = h0re_ref[:, c * LANE:(c + 1) * LANE]
            cim = h0im_ref[:, c * LANE:(c + 1) * LANE]
            cin_ref[c] = cre
            cin_ref[half + c] = cim
            lsre, lsim = mult(1, c)
            mre, mim = _cmul(lsre, lsim, cre, cim)
            sre_ref[:, c * LANE:(c + 1) * LANE] = mre + end_ref[c]
            sim_ref[:, c * LANE:(c + 1) * LANE] = mim + end_ref[half + c]
    else:
        @pl.when(pl.program_id(2) == 0)
        def _():
            carry_ref[...] = jnp.zeros_like(carry_ref)

        mults = [mult(1, c) for c in range(half)]

        def chain(s, carry):
            out = []
            for c in range(half):
                cre, cim = carry[2 * c], carry[2 * c + 1]
                cin_ref[c, pl.ds(s, 1), :] = cre
                cin_ref[half + c, pl.ds(s, 1), :] = cim
                mre, mim = _cmul(mults[c][0], mults[c][1], cre, cim)
                out.append(mre + end_ref[c, pl.ds(s, 1), :])
                out.append(mim + end_ref[half + c, pl.ds(s, 1), :])
            return tuple(out)

        carry0 = []
        for c in range(half):
            carry0 += [carry_ref[c, 0:1, :], carry_ref[half + c, 0:1, :]]
        carry = lax.fori_loop(0, n_seg, chain, tuple(carry0))
        for c in range(half):
            carry_ref[c, 0:1, :] = carry[2 * c]
            carry_ref[half + c, 0:1, :] = carry[2 * c + 1]
            sre_ref[:, c * LANE:(c + 1) * LANE] = carry[2 * c]
            sim_ref[:, c * LANE:(c + 1) * LANE] = carry[2 * c + 1]

    for c in range(half):
        l8re, l8im = mult(0, c)
        for j in range(rows // (8 * SEG)):
            cre = cin_ref[c, j * 8:(j + 1) * 8, :]
            cim = cin_ref[half + c, j * 8:(j + 1) * 8, :]
            for i in range(SEG):
                idx = pl.ds(j * 8 * SEG + i, 8, stride=SEG)
                p_ref[c, idx, :] = p_ref[c, idx, :] + cre
                p_ref[half + c, idx, :] = p_ref[half + c, idx, :] + cim
                cre, cim = _cmul(l8re, l8im, cre, cim)

    prev = jnp.concatenate([p_ref[lt] for lt in range(2 * half)], axis=1)
    y = _dot(lhs_b, kt_ref[...]) + _dot(prev.astype(BF16), wc_ref[...]) + d_ref[...] * lhs
    for t in range(N_PHASE):
        y_ref[t] = y[:, t * LANE:(t + 1) * LANE]


def _ssm(u8, h0, wb, kt, wc, pw, d8, n_seq):
    n_rows = u8.shape[1]
    d = D_MODEL
    ns = SLAB_STATES
    rows_per_seq = n_rows // n_seq
    has_state = h0 is not None
    if has_state:
        assert rows_per_seq == SEG, "carried-state path scans one segment per sequence"
        rows, tiles, grid = n_rows, 1, (N_SLAB,)
        row_block = lambda q: (0, 0, q)
        slab = lambda q: (q, 0, 0)
        state_spec = pl.BlockSpec((n_seq, ns), lambda q: (0, q))
        state_shape = jax.ShapeDtypeStruct((n_seq, N_SLAB * ns), F32)
    else:
        rows = min(SSM_ROWS, rows_per_seq)
        assert rows_per_seq % rows == 0
        tiles = rows_per_seq // rows
        grid = (N_SLAB, n_seq, tiles)
        row_block = lambda q, b, i: (0, b * tiles + i, q)
        slab = lambda q, b, i: (q, 0, 0)
        state_spec = pl.BlockSpec((None, 1, ns), lambda q, b, i: (b, 0, q))
        state_shape = jax.ShapeDtypeStruct((n_seq, 1, N_SLAB * ns), F32)
    assert rows % (8 * SEG) == 0
    ph_spec = pl.BlockSpec((N_PHASE, rows, LANE), row_block)
    w_spec = pl.BlockSpec((None, 2 * ns, 2 * ns), slab)
    in_specs = [ph_spec, w_spec, w_spec, w_spec,
                pl.BlockSpec((None, 8, 2 * ns), slab), pl.BlockSpec((None, 1, d), slab)]
    args = [u8, wb, kt, wc, pw, d8]
    scratch = [pltpu.VMEM((2 * ns // LANE, rows, LANE), F32),
               pltpu.VMEM((2 * ns // LANE, rows, LANE), F32),
               pltpu.VMEM((2 * ns // LANE, rows // SEG, LANE), F32),
               pltpu.VMEM((2 * ns // LANE, rows // SEG, LANE), F32)]
    if has_state:
        in_specs += [state_spec, state_spec]
        args += [h0[0].reshape(n_seq, -1), h0[1].reshape(n_seq, -1)]
    else:
        scratch.append(pltpu.VMEM((2 * ns // LANE, 8, LANE), F32))
    return pl.pallas_call(
        functools.partial(_ssm_body, rows=rows, has_state=has_state),
        grid=grid,
        in_specs=in_specs,
        out_specs=[ph_spec, state_spec, state_spec],
        out_shape=[jax.ShapeDtypeStruct((N_PHASE, n_rows, d), F32), state_shape, state_shape],
        scratch_shapes=scratch,
        name="ssm_sample" if has_state else "ssm_prompt",
        compiler_params=_params(("arbitrary",) * len(grid)),
    )(*args)


def _mix_out_body(y_ref, a_ref, gs_ref, x_ref, wglu_ref, wo_ref, o_ref, *, rows):
    d = D_MODEL
    glu = _dot(jax.nn.gelu(_stack_phases(y_ref)).astype(BF16), wglu_ref[...])
    y_ssm = glu[:, :d] * jax.nn.sigmoid(glu[:, d:])
    merged = _stack_phases(a_ref) + _stack_phases(gs_ref) * y_ssm
    mix = _dot(merged.astype(BF16), wo_ref[...])
    for t in range(N_PHASE):
        o_ref[t] = x_ref[t] + mix[t * rows:(t + 1) * rows]


def _mix_out(y8, a8, gs8, xp, w_glu, w_o):
    n_rows = xp.shape[1]
    d = D_MODEL
    rows = min(MIX_ROWS, n_rows)
    assert n_rows % rows == 0
    ph_spec = pl.BlockSpec((N_PHASE, rows, d), lambda i: (0, i, 0))
    return pl.pallas_call(
        functools.partial(_mix_out_body, rows=rows),
        grid=(n_rows // rows,),
        in_specs=[ph_spec, ph_spec, ph_spec, ph_spec, _const_spec(w_glu.shape), _const_spec(w_o.shape)],
        out_specs=ph_spec,
        out_shape=jax.ShapeDtypeStruct((N_PHASE, n_rows, d), F32),
        name="mix_out",
        compiler_params=_params(("arbitrary",)),
    )(y8, a8, gs8, xp, w_glu, w_o)


def _trunk(x, conv_prev, ssm_re0, ssm_im0, layers, norm_final):
    n_seq, seq_len, d = x.shape
    n_tok = n_seq * seq_len
    n_rows = n_tok // N_PHASE
    has_state = conv_prev is not None
    xt = x.reshape(n_tok, d)
    convs, res, ims = [], [], []
    for l, w in enumerate(layers):
        last = l == len(layers) - 1
        xp = _ffn(xt, w["norm_ffn1"], w["ffn1_gate"], w["ffn1_up"], w["ffn1_down"], norm_final, False, True)
        u8, a8, gs8, conv_state = _mix_in(xp, conv_prev[l] if has_state else None,
                                          w["norm_mix"], w["w_in"], w["w_conv"], w["w_conv_out"], n_seq)
        h0 = (ssm_re0[l], ssm_im0[l]) if has_state else None
        y8, s_re, s_im = _ssm(u8, h0, *w["ssm"], n_seq)
        x2 = _mix_out(y8, a8, gs8, xp, w["w_glu"], w["w_o"])
        xt = _ffn(x2, w["norm_ffn2"], w["ffn2_gate"], w["ffn2_up"], w["ffn2_down"], norm_final, last, False)
        convs.append(conv_state)
        res.append(s_re.reshape(n_seq, -1, STATE_DIM))
        ims.append(s_im.reshape(n_seq, -1, STATE_DIM))
    return xt.reshape(n_seq, seq_len, d), jnp.stack(convs), jnp.stack(res), jnp.stack(ims)


def kernel(x_prompt, x_sample, state_conv, state_ssm_re, state_ssm_im, norm_ffn1, w_ffn1_gate, w_ffn1_up, w_ffn1_down, norm_mix, w_in, w_conv, w_conv_out, ssm_lambda_re, ssm_lambda_im, ssm_log_step, ssm_b_re, ssm_b_im, ssm_c_re, ssm_c_im, ssm_d, w_glu, w_o, norm_ffn2, w_ffn2_gate, w_ffn2_up, w_ffn2_down, norm_final):
    depth = w_in.shape[0]
    bf = lambda w: w.astype(BF16)
    row = lambda v: v.reshape(1, -1)
    layers = []
    for l in range(depth):
        wb, kt, wc, pw = _ssm_prep(ssm_lambda_re[l], ssm_lambda_im[l], ssm_log_step[l],
                                   ssm_b_re[l], ssm_b_im[l], ssm_c_re[l], ssm_c_im[l])
        d8 = jnp.tile(ssm_d[l].reshape(N_SLAB, 1, SLAB), (1, 1, N_PHASE))
        layers.append(dict(
            norm_ffn1=row(norm_ffn1[l]), ffn1_gate=bf(w_ffn1_gate[l]), ffn1_up=bf(w_ffn1_up[l]),
            ffn1_down=bf(w_ffn1_down[l]), norm_mix=row(norm_mix[l]), w_in=bf(w_in[l]), w_conv=w_conv[l],
            w_conv_out=bf(w_conv_out[l]), ssm=(wb, kt, wc, pw, d8), w_glu=bf(w_glu[l]), w_o=bf(w_o[l]),
            norm_ffn2=row(norm_ffn2[l]), ffn2_gate=bf(w_ffn2_gate[l]), ffn2_up=bf(w_ffn2_up[l]),
            ffn2_down=bf(w_ffn2_down[l])))
    nf = row(norm_final)
    y_p, conv_p, re_p, im_p = _trunk(x_prompt, None, None, None, layers, nf)
    y_s, conv_s, re_s, im_s = _trunk(x_sample, state_conv, state_ssm_re, state_ssm_im, layers, nf)
    return (y_p, y_s, conv_p, re_p, im_p, conv_s, re_s, im_s)
```

```python
import functools

import jax
import jax.numpy as jnp
from jax import lax
from jax.experimental import pallas as pl
from jax.experimental.pallas import tpu as pltpu

F32 = jnp.float32
BF16 = jnp.bfloat16

D_MODEL = 1024
N_PHASE = 8
LANE = 128
SLAB = 128
N_SLAB = D_MODEL // SLAB
GROUPS_PER_SLAB = 8
STATE_DIM = 64
SLAB_STATES = GROUPS_PER_SLAB * STATE_DIM
SEG = 4
RMS_EPS = 1e-6
VMEM_LIMIT = 56 * 1024 * 1024

FFN_ROWS = 512
MIX_ROWS = 64
SSM_ROWS = 128


def _rms(x, g):
    return x * lax.rsqrt(jnp.mean(x * x, axis=-1, keepdims=True) + RMS_EPS) * g


def _dot(a, b):
    return jnp.dot(a, b, preferred_element_type=F32)


def _const_spec(shape):
    zeros = (0,) * len(shape)
    return pl.BlockSpec(shape, lambda *_: zeros, pipeline_mode=pl.Buffered(1))


def _params(semantics):
    return pltpu.CompilerParams(dimension_semantics=semantics, vmem_limit_bytes=VMEM_LIMIT)


def _stack_phases(ref):
    return jnp.concatenate([ref[t] for t in range(N_PHASE)], axis=0)


def _ffn_math(x, g_ref, wg_ref, wu_ref, wd_ref, gf_ref, final_norm):
    h = _rms(x, g_ref[...]).astype(BF16)
    act = (jax.nn.silu(_dot(h, wg_ref[...])) * _dot(h, wu_ref[...])).astype(BF16)
    y = x + 0.5 * _dot(act, wd_ref[...])
    return _rms(y, gf_ref[...]) if final_norm else y


def _ffn_to_phase_body(x_ref, g_ref, wg_ref, wu_ref, wd_ref, gf_ref, o_ref, slab_ref, *, rows, final_norm):
    y = _ffn_math(x_ref[...], g_ref, wg_ref, wu_ref, wd_ref, gf_ref, final_norm)
    for lt in range(D_MODEL // LANE):
        slab_ref[lt] = y[:, lt * LANE:(lt + 1) * LANE]
        for t in range(N_PHASE):
            o_ref[t, :, lt * LANE:(lt + 1) * LANE] = slab_ref[lt, pl.ds(t, rows, stride=N_PHASE), :]


def _ffn_from_phase_body(x_ref, g_ref, wg_ref, wu_ref, wd_ref, gf_ref, o_ref, slab_ref, *, rows, final_norm):
    y = _ffn_math(_stack_phases(x_ref), g_ref, wg_ref, wu_ref, wd_ref, gf_ref, final_norm)
    for lt in range(D_MODEL // LANE):
        for t in range(N_PHASE):
            slab_ref[lt, pl.ds(t, rows, stride=N_PHASE), :] = y[t * rows:(t + 1) * rows, lt * LANE:(lt + 1) * LANE]
        o_ref[:, lt * LANE:(lt + 1) * LANE] = slab_ref[lt]


def _ffn(x, g, wg, wu, wd, gf, final_norm, to_phase):
    if to_phase:
        n_tok, d = x.shape
    else:
        n_tok, d = x.shape[0] * x.shape[1], x.shape[2]
    n_rows = n_tok // N_PHASE
    d_ff = wg.shape[1]
    rows = FFN_ROWS // N_PHASE
    assert n_rows % rows == 0
    tok_spec = pl.BlockSpec((N_PHASE * rows, d), lambda i: (i, 0))
    phase_spec = pl.BlockSpec((N_PHASE, rows, d), lambda i: (0, i, 0))
    body = _ffn_to_phase_body if to_phase else _ffn_from_phase_body
    out_dims = (N_PHASE, n_rows, d) if to_phase else (n_tok, d)
    return pl.pallas_call(
        functools.partial(body, rows=rows, final_norm=final_norm),
        grid=(n_rows // rows,),
        in_specs=[
            tok_spec if to_phase else phase_spec,
            _const_spec((1, d)),
            _const_spec((d, d_ff)),
            _const_spec((d, d_ff)),
            _const_spec((d_ff, d)),
            _const_spec((1, d)),
        ],
        out_specs=phase_spec if to_phase else tok_spec,
        out_shape=jax.ShapeDtypeStruct(out_dims, F32),
        scratch_shapes=[pltpu.VMEM((d // LANE, N_PHASE * rows, LANE), F32)],
        name="ffn_to_phase" if to_phase else "ffn_from_phase",
        compiler_params=_params(("arbitrary",)),
    )(x, g, wg, wu, wd, gf)


def _mix_in_compute(x_ref, g, win_ref, wcv, wco_ref, shift, rows):
    d = D_MODEL
    h = _rms(_stack_phases(x_ref), g).astype(BF16)

    def proj(k):
        return _dot(h, win_ref[:, k * d:(k + 1) * d])

    z = proj(1) * proj(0)
    z6 = z[6 * rows:7 * rows]
    z7 = z[7 * rows:8 * rows]
    s6 = shift(z6, 0)
    s7 = shift(z7, 1)
    z1 = jnp.concatenate([s7, z[:7 * rows]], axis=0)
    z2 = jnp.concatenate([s6, s7, z[:6 * rows]], axis=0)
    conv = wcv[2:3] * z + wcv[1:2] * z1 + wcv[0:1] * z2
    y_conv = _dot((proj(2) * conv).astype(BF16), wco_ref[...])
    u = proj(3)
    a = jax.nn.sigmoid(proj(4)) * y_conv
    gs = jax.nn.sigmoid(proj(5))
    return u, a, gs, z6, z7


def _store_phases(ref, val, rows):
    val = val.astype(ref.dtype)
    for t in range(N_PHASE):
        ref[t] = val[t * rows:(t + 1) * rows]


def _mix_in_prompt_body(x_ref, g_ref, win_ref, wcv_ref, wco_ref,
                        u_ref, a_ref, gs_ref, z6_ref, z7_ref, carry_ref, *, rows):
    @pl.when(pl.program_id(1) == 0)
    def _():
        carry_ref[...] = jnp.zeros_like(carry_ref)

    row_id = lax.broadcasted_iota(jnp.int32, (rows, D_MODEL), 0)

    def shift(z, k):
        return jnp.where(row_id == 0, carry_ref[k:k + 1, :], pltpu.roll(z, 1, 0))

    u, a, gs, z6, z7 = _mix_in_compute(x_ref, g_ref[...], win_ref, wcv_ref[...], wco_ref, shift, rows)
    last6 = z6[rows - 1:rows]
    last7 = z7[rows - 1:rows]
    carry_ref[0:1, :] = last6
    carry_ref[1:2, :] = last7
    z6_ref[...] = last6
    z7_ref[...] = last7
    _store_phases(u_ref, u, rows)
    _store_phases(a_ref, a, rows)
    _store_phases(gs_ref, gs, rows)


def _mix_in_sample_body(x_ref, s6_ref, s7_ref, g_ref, win_ref, wcv_ref, wco_ref,
                        u_ref, a_ref, gs_ref, z6_ref, z7_ref, *, rows, rows_per_seq):
    row_id = lax.broadcasted_iota(jnp.int32, (rows, D_MODEL), 0)
    starts = (s6_ref, s7_ref)

    def shift(z, k):
        return jnp.where(row_id % rows_per_seq == 0, starts[k][...], pltpu.roll(z, 1, 0))

    u, a, gs, z6, z7 = _mix_in_compute(x_ref, g_ref[...], win_ref, wcv_ref[...], wco_ref, shift, rows)
    z6_ref[...] = z6
    z7_ref[...] = z7
    _store_phases(u_ref, u, rows)
    _store_phases(a_ref, a, rows)
    _store_phases(gs_ref, gs, rows)


def _mix_in(xp, conv_prev, g, w_in, w_conv, w_conv_out, n_seq):
    n_rows = xp.shape[1]
    d = D_MODEL
    rows_per_seq = n_rows // n_seq
    ph_shape = jax.ShapeDtypeStruct((N_PHASE, n_rows, d), BF16)
    weights = (g, w_in, w_conv, w_conv_out)
    weight_specs = [_const_spec(w.shape) for w in weights]
    if conv_prev is None:
        rows = min(MIX_ROWS, rows_per_seq)
        assert rows_per_seq % rows == 0
        tiles = rows_per_seq // rows
        ph_spec = pl.BlockSpec((N_PHASE, rows, d), lambda b, i: (0, b * tiles + i, 0))
        last_spec = pl.BlockSpec((None, 1, d), lambda b, i: (b, 0, 0))
        u8, a8, gs8, z6, z7 = pl.pallas_call(
            functools.partial(_mix_in_prompt_body, rows=rows),
            grid=(n_seq, tiles),
            in_specs=[ph_spec] + weight_specs,
            out_specs=[ph_spec, ph_spec, ph_spec, last_spec, last_spec],
            out_shape=[ph_shape, ph_shape, ph_shape,
                       jax.ShapeDtypeStruct((n_seq, 1, d), F32), jax.ShapeDtypeStruct((n_seq, 1, d), F32)],
            scratch_shapes=[pltpu.VMEM((8, d), F32)],
            name="mix_in_prompt",
            compiler_params=_params(("arbitrary", "arbitrary")),
        )(xp, *weights)
        z6, z7 = z6[:, 0], z7[:, 0]
    else:
        rows = n_rows
        start = jnp.zeros((n_seq, rows_per_seq, 2, d), F32).at[:, 0].set(conv_prev).reshape(n_rows, 2, d)
        full = lambda shape: pl.BlockSpec(shape, lambda i: (0,) * len(shape))
        u8, a8, gs8, z6f, z7f = pl.pallas_call(
            functools.partial(_mix_in_sample_body, rows=rows, rows_per_seq=rows_per_seq),
            grid=(1,),
            in_specs=[full((N_PHASE, rows, d)), full((rows, d)), full((rows, d))] + weight_specs,
            out_specs=[full((N_PHASE, rows, d))] * 3 + [full((rows, d))] * 2,
            out_shape=[ph_shape, ph_shape, ph_shape,
                       jax.ShapeDtypeStruct((n_rows, d), F32), jax.ShapeDtypeStruct((n_rows, d), F32)],
            name="mix_in_sample",
            compiler_params=_params(("arbitrary",)),
        )(xp, start[:, 0], start[:, 1], *weights)
        z6 = z6f[rows_per_seq - 1::rows_per_seq]
        z7 = z7f[rows_per_seq - 1::rows_per_seq]
    return u8, a8, gs8, jnp.stack([z6, z7], axis=1)


def _ssm_prep_body(bbre_ref, bbim_ref, ctre_ref, ctim_ref, lre_ref, lim_ref, lstep_ref,
                   wb_ref, kt_ref, wc_ref, pw_ref):
    ns = SLAB_STATES
    bbre, bbim = bbre_ref[...], bbim_ref[...]
    ctre, ctim = ctre_ref[...], ctim_ref[...]
    lre, lim = lre_ref[...], lim_ref[...]
    step = jnp.exp(lstep_ref[...])

    def lam_pow(n):
        mag = jnp.exp((n * lre) * step)
        ang = (n * lim) * step
        return mag * jnp.cos(ang), mag * jnp.sin(ang)

    l1re, l1im = lam_pow(1)
    den = lre * lre + lim * lim
    fre = ((l1re - 1.0) * lre + l1im * lim) / den
    fim = (l1im * lre - (l1re - 1.0) * lim) / den

    def dot_nt(a, b):
        return lax.dot_general(a, b, (((1,), (1,)), ((), ())),
                               precision=lax.Precision.HIGHEST, preferred_element_type=F32)

    kt_ref[...] = jnp.zeros_like(kt_ref)
    for k in range(N_PHASE):
        pre, pim = lam_pow(k)
        gre = fre * pre - fim * pim
        gim = fre * pim + fim * pre
        are = bbre * gre - bbim * gim
        aim = bbre * gim + bbim * gre
        j = N_PHASE - 1 - k
        wb_ref[j * SLAB:(j + 1) * SLAB, 0:ns] = are.astype(BF16)
        wb_ref[j * SLAB:(j + 1) * SLAB, ns:2 * ns] = aim.astype(BF16)
        kk = (dot_nt(are, ctre) - dot_nt(aim, ctim)).astype(BF16)
        for jj in range(N_PHASE - k):
            t = jj + k
            kt_ref[jj * SLAB:(jj + 1) * SLAB, t * SLAB:(t + 1) * SLAB] = kk
    for t in range(N_PHASE):
        pre, pim = lam_pow(t + 1)
        cre = ctre * pre - ctim * pim
        cim = ctre * pim + ctim * pre
        wc_ref[0:ns, t * SLAB:(t + 1) * SLAB] = cre.T.astype(BF16)
        wc_ref[ns:2 * ns, t * SLAB:(t + 1) * SLAB] = (-cim).T.astype(BF16)
    pw_ref[...] = jnp.zeros_like(pw_ref)
    for r, n in enumerate((N_PHASE, N_PHASE * SEG)):
        pre, pim = lam_pow(n)
        pw_ref[r:r + 1, 0:ns] = pre
        pw_ref[r:r + 1, ns:2 * ns] = pim


def _ssm_prep(lam_re, lam_im, log_step, b_re, b_im, c_re, c_im):
    ns = SLAB_STATES

    def block_diag(m):
        m = m.reshape(N_SLAB, GROUPS_PER_SLAB, 16, STATE_DIM)
        eye = jnp.eye(GROUPS_PER_SLAB, dtype=bool)[None, :, None, :, None]
        return jnp.where(eye, m[:, :, :, None, :], 0.0).reshape(N_SLAB, SLAB, ns)

    def vec(v):
        return v.reshape(N_SLAB, 1, ns)

    mats = [block_diag(jnp.transpose(b_re, (0, 2, 1))), block_diag(jnp.transpose(b_im, (0, 2, 1))),
            block_diag(c_re), block_diag(c_im)]
    vecs = [vec(lam_re), vec(lam_im), vec(jnp.broadcast_to(log_step[:, None], lam_re.shape))]
    w_shape = jax.ShapeDtypeStruct((N_SLAB, 2 * ns, 2 * ns), BF16)
    w_spec = pl.BlockSpec((None, 2 * ns, 2 * ns), lambda q: (q, 0, 0))
    return pl.pallas_call(
        _ssm_prep_body,
        grid=(N_SLAB,),
        in_specs=[pl.BlockSpec((None, SLAB, ns), lambda q: (q, 0, 0))] * 4
        + [pl.BlockSpec((None, 1, ns), lambda q: (q, 0, 0))] * 3,
        out_specs=[w_spec, w_spec, w_spec, pl.BlockSpec((None, 8, 2 * ns), lambda q: (q, 0, 0))],
        out_shape=[w_shape, w_shape, w_shape, jax.ShapeDtypeStruct((N_SLAB, 8, 2 * ns), F32)],
        name="ssm_prep",
        compiler_params=_params(("arbitrary",)),
    )(*mats, *vecs)


def _cmul(are, aim, bre, bim):
    return are * bre - aim * bim, are * bim + aim * bre


HALF = SLAB_STATES // LANE
GROUP_ROWS = 8 * SEG


def _mult(pw_ref, r, c):
    return (pw_ref[r:r + 1, c * LANE:(c + 1) * LANE],
            pw_ref[r:r + 1, (HALF + c) * LANE:(HALF + c + 1) * LANE])


def _state_increments(lhs, wb_ref, e_ref):
    e = _dot(lhs, wb_ref[...])
    for lt in range(2 * HALF):
        e_ref[lt] = e[:, lt * LANE:(lt + 1) * LANE]


def _scan_local(e_ref, p_ref, pw_ref, c, base):
    l8re, l8im = _mult(pw_ref, 0, c)
    lre = jnp.zeros((8, LANE), F32)
    lim = jnp.zeros((8, LANE), F32)
    for i in range(SEG):
        idx = pl.ds(base + i, 8, stride=SEG)
        p_ref[c, idx, :] = lre
        p_ref[HALF + c, idx, :] = lim
        mre, mim = _cmul(l8re, l8im, lre, lim)
        lre = mre + e_ref[c, idx, :]
        lim = mim + e_ref[HALF + c, idx, :]
    return lre, lim


def _scan_fixup(p_ref, pw_ref, c, base, cre, cim):
    l8re, l8im = _mult(pw_ref, 0, c)
    for i in range(SEG):
        idx = pl.ds(base + i, 8, stride=SEG)
        p_ref[c, idx, :] = p_ref[c, idx, :] + cre
        p_ref[HALF + c, idx, :] = p_ref[HALF + c, idx, :] + cim
        cre, cim = _cmul(l8re, l8im, cre, cim)


def _readout(lhs, p_ref, kt_ref, wc_ref, d_ref):
    prev = jnp.concatenate([p_ref[lt] for lt in range(2 * HALF)], axis=1)
    y = _dot(lhs, kt_ref[...]) + _dot(prev.astype(BF16), wc_ref[...]) + d_ref[...] * lhs.astype(F32)
    return y.astype(BF16)


def _ssm_prompt_body(u_ref, wb_ref, kt_ref, wc_ref, pw_ref, d_ref, y_ref, sre_ref, sim_ref,
                     e_ref, p_ref, end_ref, cin_ref, carry_ref, *, n_seq, rows):
    total = n_seq * rows
    lhs = jnp.concatenate([u_ref[t].reshape(total, LANE) for t in range(N_PHASE)], axis=1)
    _state_increments(lhs, wb_ref, e_ref)

    @pl.when((pl.program_id(0) == 0) & (pl.program_id(1) == 0))
    def _():
        end_ref[...] = jnp.zeros_like(end_ref)

    @pl.when(pl.program_id(1) == 0)
    def _():
        carry_ref[...] = jnp.zeros_like(carry_ref)

    groups = rows // GROUP_ROWS
    for c in range(HALF):
        for b in range(n_seq):
            for j in range(groups):
                lre, lim = _scan_local(e_ref, p_ref, pw_ref, c, b * rows + j * GROUP_ROWS)
                idx = pl.ds(j * 64 + b, 8, stride=8)
                end_ref[c, idx, :] = lre
                end_ref[HALF + c, idx, :] = lim
    for c in range(HALF):
        lsre, lsim = _mult(pw_ref, 1, c)
        cre = carry_ref[c]
        cim = carry_ref[HALF + c]
        for s in range(rows // SEG):
            cin_ref[c, 8 * s:8 * s + 8, :] = cre
            cin_ref[HALF + c, 8 * s:8 * s + 8, :] = cim
            mre, mim = _cmul(lsre, lsim, cre, cim)
            cre = mre + end_ref[c, 8 * s:8 * s + 8, :]
            cim = mim + end_ref[HALF + c, 8 * s:8 * s + 8, :]
        carry_ref[c] = cre
        carry_ref[HALF + c] = cim
        sre_ref[:, c * LANE:(c + 1) * LANE] = cre[:n_seq]
        sim_ref[:, c * LANE:(c + 1) * LANE] = cim[:n_seq]
    for c in range(HALF):
        for b in range(n_seq):
            for j in range(groups):
                idx = pl.ds(j * 64 + b, 8, stride=8)
                _scan_fixup(p_ref, pw_ref, c, b * rows + j * GROUP_ROWS,
                            cin_ref[c, idx, :], cin_ref[HALF + c, idx, :])
    y = _readout(lhs, p_ref, kt_ref, wc_ref, d_ref)
    for t in range(N_PHASE):
        y_ref[t] = y[:, t * LANE:(t + 1) * LANE].reshape(n_seq, rows, LANE)


def _ssm_sample_body(u_ref, wb_ref, kt_ref, wc_ref, pw_ref, d_ref, h0re_ref, h0im_ref,
                     y_ref, sre_ref, sim_ref, e_ref, p_ref, *, rows):
    lhs = jnp.concatenate([u_ref[t] for t in range(N_PHASE)], axis=1)
    _state_increments(lhs, wb_ref, e_ref)
    for c in range(HALF):
        lsre, lsim = _mult(pw_ref, 1, c)
        for j in range(rows // GROUP_ROWS):
            lre, lim = _scan_local(e_ref, p_ref, pw_ref, c, j * GROUP_ROWS)
            cre = h0re_ref[8 * j:8 * j + 8, c * LANE:(c + 1) * LANE]
            cim = h0im_ref[8 * j:8 * j + 8, c * LANE:(c + 1) * LANE]
            mre, mim = _cmul(lsre, lsim, cre, cim)
            sre_ref[8 * j:8 * j + 8, c * LANE:(c + 1) * LANE] = mre + lre
            sim_ref[8 * j:8 * j + 8, c * LANE:(c + 1) * LANE] = mim + lim
            _scan_fixup(p_ref, pw_ref, c, j * GROUP_ROWS, cre, cim)
    y = _readout(lhs, p_ref, kt_ref, wc_ref, d_ref)
    for t in range(N_PHASE):
        y_ref[t] = y[:, t * LANE:(t + 1) * LANE]


def _ssm(u8, h0, wb, kt, wc, pw, d8, n_seq):
    n_rows = u8.shape[1]
    d = D_MODEL
    ns = SLAB_STATES
    rows_per_seq = n_rows // n_seq
    has_state = h0 is not None
    state_shape = jax.ShapeDtypeStruct((n_seq, N_SLAB * ns), F32)
    if has_state:
        assert rows_per_seq == SEG, "carried-state path scans one segment per sequence"
        rows, total, grid = n_rows, n_rows, (N_SLAB,)
        ph_spec = pl.BlockSpec((N_PHASE, rows, LANE), lambda q: (0, 0, q))
        slab = lambda q: (q, 0, 0)
        state_spec = pl.BlockSpec((n_seq, ns), lambda q: (0, q))
        body = functools.partial(_ssm_sample_body, rows=rows)
        u_in, y_dims = u8, (N_PHASE, n_rows, d)
    else:
        assert n_seq <= 8, "sequences ride the sublanes of the segment chain"
        rows = min(SSM_ROWS, rows_per_seq)
        assert rows_per_seq % rows == 0
        total = n_seq * rows
        grid = (N_SLAB, rows_per_seq // rows)
        ph_spec = pl.BlockSpec((N_PHASE, n_seq, rows, LANE), lambda q, i: (0, 0, i, q))
        slab = lambda q, i: (q, 0, 0)
        state_spec = pl.BlockSpec((n_seq, ns), lambda q, i: (0, q))
        body = functools.partial(_ssm_prompt_body, n_seq=n_seq, rows=rows)
        y_dims = (N_PHASE, n_seq, rows_per_seq, d)
        u_in = u8.reshape(y_dims)
    assert rows % GROUP_ROWS == 0
    w_spec = pl.BlockSpec((None, 2 * ns, 2 * ns), slab)
    in_specs = [ph_spec, w_spec, w_spec, w_spec,
                pl.BlockSpec((None, 8, 2 * ns), slab), pl.BlockSpec((None, 1, d), slab)]
    args = [u_in, wb, kt, wc, pw, d8]
    scratch = [pltpu.VMEM((2 * HALF, total, LANE), F32),
               pltpu.VMEM((2 * HALF, total, LANE), F32)]
    if has_state:
        in_specs += [state_spec, state_spec]
        args += [h0[0].reshape(n_seq, -1), h0[1].reshape(n_seq, -1)]
    else:
        seg_rows = 8 * (rows // SEG)
        scratch += [pltpu.VMEM((2 * HALF, seg_rows, LANE), F32),
                    pltpu.VMEM((2 * HALF, seg_rows, LANE), F32),
                    pltpu.VMEM((2 * HALF, 8, LANE), F32)]
    y8, s_re, s_im = pl.pallas_call(
        body,
        grid=grid,
        in_specs=in_specs,
        out_specs=[ph_spec, state_spec, state_spec],
        out_shape=[jax.ShapeDtypeStruct(y_dims, BF16), state_shape, state_shape],
        scratch_shapes=scratch,
        name="ssm_sample" if has_state else "ssm_prompt",
        compiler_params=_params(("arbitrary",) * len(grid)),
    )(*args)
    return y8.reshape(N_PHASE, n_rows, d), s_re, s_im


def _mix_out_body(y_ref, a_ref, gs_ref, x_ref, wglu_ref, wo_ref, o_ref, *, rows):
    d = D_MODEL
    glu = _dot(jax.nn.gelu(_stack_phases(y_ref).astype(F32)).astype(BF16), wglu_ref[...])
    y_ssm = glu[:, :d] * jax.nn.sigmoid(glu[:, d:])
    merged = _stack_phases(a_ref).astype(F32) + _stack_phases(gs_ref).astype(F32) * y_ssm
    mix = _dot(merged.astype(BF16), wo_ref[...])
    for t in range(N_PHASE):
        o_ref[t] = x_ref[t] + mix[t * rows:(t + 1) * rows]


def _mix_out(y8, a8, gs8, xp, w_glu, w_o):
    n_rows = xp.shape[1]
    d = D_MODEL
    rows = min(MIX_ROWS, n_rows)
    assert n_rows % rows == 0
    ph_spec = pl.BlockSpec((N_PHASE, rows, d), lambda i: (0, i, 0))
    return pl.pallas_call(
        functools.partial(_mix_out_body, rows=rows),
        grid=(n_rows // rows,),
        in_specs=[ph_spec, ph_spec, ph_spec, ph_spec, _const_spec(w_glu.shape), _const_spec(w_o.shape)],
        out_specs=ph_spec,
        out_shape=jax.ShapeDtypeStruct((N_PHASE, n_rows, d), F32),
        name="mix_out",
        compiler_params=_params(("arbitrary",)),
    )(y8, a8, gs8, xp, w_glu, w_o)


def _trunk(x, conv_prev, ssm_re0, ssm_im0, layers, norm_final):
    n_seq, seq_len, d = x.shape
    n_tok = n_seq * seq_len
    n_rows = n_tok // N_PHASE
    has_state = conv_prev is not None
    xt = x.reshape(n_tok, d)
    convs, res, ims = [], [], []
    for l, w in enumerate(layers):
        last = l == len(layers) - 1
        xp = _ffn(xt, w["norm_ffn1"], w["ffn1_gate"], w["ffn1_up"], w["ffn1_down"], norm_final, False, True)
        u8, a8, gs8, conv_state = _mix_in(xp, conv_prev[l] if has_state else None,
                                          w["norm_mix"], w["w_in"], w["w_conv"], w["w_conv_out"], n_seq)
        h0 = (ssm_re0[l], ssm_im0[l]) if has_state else None
        y8, s_re, s_im = _ssm(u8, h0, *w["ssm"], n_seq)
        x2 = _mix_out(y8, a8, gs8, xp, w["w_glu"], w["w_o"])
        xt = _ffn(x2, w["norm_ffn2"], w["ffn2_gate"], w["ffn2_up"], w["ffn2_down"], norm_final, last, False)
        convs.append(conv_state)
        res.append(s_re.reshape(n_seq, -1, STATE_DIM))
        ims.append(s_im.reshape(n_seq, -1, STATE_DIM))
    return xt.reshape(n_seq, seq_len, d), jnp.stack(convs), jnp.stack(res), jnp.stack(ims)


def kernel(x_prompt, x_sample, state_conv, state_ssm_re, state_ssm_im, norm_ffn1, w_ffn1_gate, w_ffn1_up, w_ffn1_down, norm_mix, w_in, w_conv, w_conv_out, ssm_lambda_re, ssm_lambda_im, ssm_log_step, ssm_b_re, ssm_b_im, ssm_c_re, ssm_c_im, ssm_d, w_glu, w_o, norm_ffn2, w_ffn2_gate, w_ffn2_up, w_ffn2_down, norm_final):
    depth = w_in.shape[0]
    bf = lambda w: w.astype(BF16)
    row = lambda v: v.reshape(1, -1)
    layers = []
    for l in range(depth):
        wb, kt, wc, pw = _ssm_prep(ssm_lambda_re[l], ssm_lambda_im[l], ssm_log_step[l],
                                   ssm_b_re[l], ssm_b_im[l], ssm_c_re[l], ssm_c_im[l])
        d8 = jnp.tile(ssm_d[l].reshape(N_SLAB, 1, SLAB), (1, 1, N_PHASE))
        layers.append(dict(
            norm_ffn1=row(norm_ffn1[l]), ffn1_gate=bf(w_ffn1_gate[l]), ffn1_up=bf(w_ffn1_up[l]),
            ffn1_down=bf(w_ffn1_down[l]), norm_mix=row(norm_mix[l]), w_in=bf(w_in[l]), w_conv=w_conv[l],
            w_conv_out=bf(w_conv_out[l]), ssm=(wb, kt, wc, pw, d8), w_glu=bf(w_glu[l]), w_o=bf(w_o[l]),
            norm_ffn2=row(norm_ffn2[l]), ffn2_gate=bf(w_ffn2_gate[l]), ffn2_up=bf(w_ffn2_up[l]),
            ffn2_down=bf(w_ffn2_down[l])))
    nf = row(norm_final)
    y_p, conv_p, re_p, im_p = _trunk(x_prompt, None, None, None, layers, nf)
    y_s, conv_s, re_s, im_s = _trunk(x_sample, state_conv, state_ssm_re, state_ssm_im, layers, nf)
    return (y_p, y_s, conv_p, re_p, im_p, conv_s, re_s, im_s)
```

```python
import functools

import jax
import jax.numpy as jnp
from jax import lax
from jax.experimental import pallas as pl
from jax.experimental.pallas import tpu as pltpu

F32 = jnp.float32
BF16 = jnp.bfloat16

D_MODEL = 1024
N_PHASE = 8
LANE = 128
SLAB = 128
N_SLAB = D_MODEL // SLAB
GROUPS_PER_SLAB = 8
STATE_DIM = 64
SLAB_STATES = GROUPS_PER_SLAB * STATE_DIM
SEG = 4
RMS_EPS = 1e-6
VMEM_LIMIT = 56 * 1024 * 1024

FFN_ROWS = 512
MIX_ROWS = 64
SSM_ROWS = 128


def _rms(x, g):
    return x * lax.rsqrt(jnp.mean(x * x, axis=-1, keepdims=True) + RMS_EPS) * g


def _dot(a, b):
    return jnp.dot(a, b, preferred_element_type=F32)


def _const_spec(shape):
    zeros = (0,) * len(shape)
    return pl.BlockSpec(shape, lambda *_: zeros, pipeline_mode=pl.Buffered(1))


def _params(semantics):
    return pltpu.CompilerParams(dimension_semantics=semantics, vmem_limit_bytes=VMEM_LIMIT)


def _stack_phases(ref):
    return jnp.concatenate([ref[t] for t in range(N_PHASE)], axis=0)


def _ffn_math(x, g_ref, wg_ref, wu_ref, wd_ref, gf_ref, final_norm):
    h = _rms(x, g_ref[...]).astype(BF16)
    act = (jax.nn.silu(_dot(h, wg_ref[...])) * _dot(h, wu_ref[...])).astype(BF16)
    y = x + 0.5 * _dot(act, wd_ref[...])
    return _rms(y, gf_ref[...]) if final_norm else y


def _rider_specs(weights, n_steps):
    in_specs, out_specs, out_shapes = [], [], []
    for w in weights:
        n_blocks = n_steps
        while w.shape[0] % (16 * n_blocks):
            assert n_blocks % 2 == 0
            n_blocks //= 2
        spec = pl.BlockSpec((w.shape[0] // n_blocks, w.shape[1]),
                            lambda i, rep=n_steps // n_blocks: (i // rep, 0))
        in_specs.append(spec)
        out_specs.append(spec)
        out_shapes.append(jax.ShapeDtypeStruct(w.shape, BF16))
    return in_specs, out_specs, out_shapes


def _convert_riders(in_refs, out_refs):
    for w_ref, o_ref in zip(in_refs, out_refs):
        o_ref[...] = w_ref[...].astype(BF16)


def _ffn_to_phase_body(*refs, rows, final_norm, n_riders):
    x_ref, g_ref, wg_ref, wu_ref, wd_ref, gf_ref = refs[:6]
    o_ref, slab_ref = refs[6 + n_riders], refs[-1]
    y = _ffn_math(x_ref[...], g_ref, wg_ref, wu_ref, wd_ref, gf_ref, final_norm)
    for lt in range(D_MODEL // LANE):
        slab_ref[lt] = y[:, lt * LANE:(lt + 1) * LANE]
        for t in range(N_PHASE):
            o_ref[t, :, lt * LANE:(lt + 1) * LANE] = slab_ref[lt, pl.ds(t, rows, stride=N_PHASE), :]
    _convert_riders(refs[6:6 + n_riders], refs[7 + n_riders:7 + 2 * n_riders])


def _ffn_from_phase_body(x_ref, g_ref, wg_ref, wu_ref, wd_ref, gf_ref, o_ref, slab_ref, *,
                         rows, final_norm, n_riders):
    assert n_riders == 0
    y = _ffn_math(_stack_phases(x_ref), g_ref, wg_ref, wu_ref, wd_ref, gf_ref, final_norm)
    for lt in range(D_MODEL // LANE):
        for t in range(N_PHASE):
            slab_ref[lt, pl.ds(t, rows, stride=N_PHASE), :] = y[t * rows:(t + 1) * rows, lt * LANE:(lt + 1) * LANE]
        o_ref[:, lt * LANE:(lt + 1) * LANE] = slab_ref[lt]


def _ffn(x, g, wg, wu, wd, gf, final_norm, to_phase, riders=()):
    if to_phase:
        n_tok, d = x.shape
    else:
        n_tok, d = x.shape[0] * x.shape[1], x.shape[2]
    n_rows = n_tok // N_PHASE
    d_ff = wg.shape[1]
    rows = FFN_ROWS // N_PHASE
    assert n_rows % rows == 0
    tok_spec = pl.BlockSpec((N_PHASE * rows, d), lambda i: (i, 0))
    phase_spec = pl.BlockSpec((N_PHASE, rows, d), lambda i: (0, i, 0))
    body = _ffn_to_phase_body if to_phase else _ffn_from_phase_body
    out_dims = (N_PHASE, n_rows, d) if to_phase else (n_tok, d)
    n_steps = n_rows // rows
    rider_in, rider_out, rider_shapes = _rider_specs(riders, n_steps)
    out = pl.pallas_call(
        functools.partial(body, rows=rows, final_norm=final_norm, n_riders=len(riders)),
        grid=(n_steps,),
        in_specs=[
            tok_spec if to_phase else phase_spec,
            _const_spec((1, d)),
            _const_spec((d, d_ff)),
            _const_spec((d, d_ff)),
            _const_spec((d_ff, d)),
            _const_spec((1, d)),
        ] + rider_in,
        out_specs=[phase_spec if to_phase else tok_spec] + rider_out,
        out_shape=[jax.ShapeDtypeStruct(out_dims, F32)] + rider_shapes,
        scratch_shapes=[pltpu.VMEM((d // LANE, N_PHASE * rows, LANE), F32)],
        name="ffn_to_phase" if to_phase else "ffn_from_phase",
        compiler_params=_params(("arbitrary",)),
    )(x, g, wg, wu, wd, gf, *riders)
    return out[0], tuple(out[1:])


def _mix_in_compute(x_ref, g, win_ref, wcv, wco_ref, shift, rows):
    d = D_MODEL
    h = _rms(_stack_phases(x_ref), g).astype(BF16)

    def proj(k):
        return _dot(h, win_ref[:, k * d:(k + 1) * d])

    z = proj(1) * proj(0)
    z6 = z[6 * rows:7 * rows]
    z7 = z[7 * rows:8 * rows]
    s6 = shift(z6, 0)
    s7 = shift(z7, 1)
    z1 = jnp.concatenate([s7, z[:7 * rows]], axis=0)
    z2 = jnp.concatenate([s6, s7, z[:6 * rows]], axis=0)
    conv = wcv[2:3] * z + wcv[1:2] * z1 + wcv[0:1] * z2
    y_conv = _dot((proj(2) * conv).astype(BF16), wco_ref[...])
    u = proj(3)
    a = jax.nn.sigmoid(proj(4)) * y_conv
    gs = jax.nn.sigmoid(proj(5))
    return u, a, gs, z6, z7


def _store_phases(ref, val, rows):
    val = val.astype(ref.dtype)
    for t in range(N_PHASE):
        ref[t] = val[t * rows:(t + 1) * rows]


def _mix_in_prompt_body(x_ref, g_ref, win_ref, wcv_ref, wco_ref,
                        u_ref, a_ref, gs_ref, z6_ref, z7_ref, carry_ref, *, rows):
    @pl.when(pl.program_id(1) == 0)
    def _():
        carry_ref[...] = jnp.zeros_like(carry_ref)

    row_id = lax.broadcasted_iota(jnp.int32, (rows, D_MODEL), 0)

    def shift(z, k):
        return jnp.where(row_id == 0, carry_ref[k:k + 1, :], pltpu.roll(z, 1, 0))

    u, a, gs, z6, z7 = _mix_in_compute(x_ref, g_ref[...], win_ref, wcv_ref[...], wco_ref, shift, rows)
    last6 = z6[rows - 1:rows]
    last7 = z7[rows - 1:rows]
    carry_ref[0:1, :] = last6
    carry_ref[1:2, :] = last7
    z6_ref[...] = last6
    z7_ref[...] = last7
    _store_phases(u_ref, u, rows)
    _store_phases(a_ref, a, rows)
    _store_phases(gs_ref, gs, rows)


def _mix_in_sample_body(x_ref, s6_ref, s7_ref, g_ref, win_ref, wcv_ref, wco_ref,
                        u_ref, a_ref, gs_ref, z6_ref, z7_ref, *, rows, rows_per_seq):
    row_id = lax.broadcasted_iota(jnp.int32, (rows, D_MODEL), 0)
    starts = (s6_ref, s7_ref)

    def shift(z, k):
        return jnp.where(row_id % rows_per_seq == 0, starts[k][...], pltpu.roll(z, 1, 0))

    u, a, gs, z6, z7 = _mix_in_compute(x_ref, g_ref[...], win_ref, wcv_ref[...], wco_ref, shift, rows)
    z6_ref[...] = z6
    z7_ref[...] = z7
    _store_phases(u_ref, u, rows)
    _store_phases(a_ref, a, rows)
    _store_phases(gs_ref, gs, rows)


def _mix_in(xp, conv_prev, g, w_in, w_conv, w_conv_out, n_seq):
    n_rows = xp.shape[1]
    d = D_MODEL
    rows_per_seq = n_rows // n_seq
    ph_shape = jax.ShapeDtypeStruct((N_PHASE, n_rows, d), BF16)
    weights = (g, w_in, w_conv, w_conv_out)
    weight_specs = [_const_spec(w.shape) for w in weights]
    if conv_prev is None:
        rows = min(MIX_ROWS, rows_per_seq)
        assert rows_per_seq % rows == 0
        tiles = rows_per_seq // rows
        ph_spec = pl.BlockSpec((N_PHASE, rows, d), lambda b, i: (0, b * tiles + i, 0))
        last_spec = pl.BlockSpec((None, 1, d), lambda b, i: (b, 0, 0))
        u8, a8, gs8, z6, z7 = pl.pallas_call(
            functools.partial(_mix_in_prompt_body, rows=rows),
            grid=(n_seq, tiles),
            in_specs=[ph_spec] + weight_specs,
            out_specs=[ph_spec, ph_spec, ph_spec, last_spec, last_spec],
            out_shape=[ph_shape, ph_shape, ph_shape,
                       jax.ShapeDtypeStruct((n_seq, 1, d), F32), jax.ShapeDtypeStruct((n_seq, 1, d), F32)],
            scratch_shapes=[pltpu.VMEM((8, d), F32)],
            name="mix_in_prompt",
            compiler_params=_params(("arbitrary", "arbitrary")),
        )(xp, *weights)
        z6, z7 = z6[:, 0], z7[:, 0]
    else:
        rows = n_rows
        start = jnp.zeros((n_seq, rows_per_seq, 2, d), F32).at[:, 0].set(conv_prev).reshape(n_rows, 2, d)
        full = lambda shape: pl.BlockSpec(shape, lambda i: (0,) * len(shape))
        u8, a8, gs8, z6f, z7f = pl.pallas_call(
            functools.partial(_mix_in_sample_body, rows=rows, rows_per_seq=rows_per_seq),
            grid=(1,),
            in_specs=[full((N_PHASE, rows, d)), full((rows, d)), full((rows, d))] + weight_specs,
            out_specs=[full((N_PHASE, rows, d))] * 3 + [full((rows, d))] * 2,
            out_shape=[ph_shape, ph_shape, ph_shape,
                       jax.ShapeDtypeStruct((n_rows, d), F32), jax.ShapeDtypeStruct((n_rows, d), F32)],
            name="mix_in_sample",
            compiler_params=_params(("arbitrary",)),
        )(xp, start[:, 0], start[:, 1], *weights)
        z6 = z6f[rows_per_seq - 1::rows_per_seq]
        z7 = z7f[rows_per_seq - 1::rows_per_seq]
    return u8, a8, gs8, jnp.stack([z6, z7], axis=1)


def _split_bf16(a):
    hi = a.astype(BF16)
    return hi, (a - hi.astype(F32)).astype(BF16)


def _ssm_prep_body(*refs, n_riders):
    bbre_ref, bbim_ref, ctre_ref, ctim_ref, lre_ref, lim_ref, lstep_ref = refs[:7]
    wb_ref, kt_ref, wc_ref, pw_ref = refs[7 + n_riders:11 + n_riders]
    _convert_riders(refs[7:7 + n_riders], refs[11 + n_riders:])
    ns = SLAB_STATES
    bbre, bbim = bbre_ref[...], bbim_ref[...]
    ctre, ctim = ctre_ref[...], ctim_ref[...]
    lre, lim = lre_ref[...], lim_ref[...]
    step = jnp.exp(lstep_ref[...])

    def lam_pow(n):
        mag = jnp.exp((n * lre) * step)
        ang = (n * lim) * step
        return mag * jnp.cos(ang), mag * jnp.sin(ang)

    l1re, l1im = lam_pow(1)
    den = lre * lre + lim * lim
    fre = ((l1re - 1.0) * lre + l1im * lim) / den
    fim = (l1im * lre - (l1re - 1.0) * lim) / den

    def dot_nt(a, b_split):
        nt = lambda p, q: lax.dot_general(p, q, (((1,), (1,)), ((), ())), preferred_element_type=F32)
        a_hi, a_lo = _split_bf16(a)
        b_hi, b_lo = b_split
        return nt(a_hi, b_hi) + nt(a_hi, b_lo) + nt(a_lo, b_hi)

    ctre_split = _split_bf16(ctre)
    ctim_split = _split_bf16(ctim)
    kt_ref[...] = jnp.zeros_like(kt_ref)
    for k in range(N_PHASE):
        pre, pim = lam_pow(k)
        gre = fre * pre - fim * pim
        gim = fre * pim + fim * pre
        are = bbre * gre - bbim * gim
        aim = bbre * gim + bbim * gre
        j = N_PHASE - 1 - k
        wb_ref[j * SLAB:(j + 1) * SLAB, 0:ns] = are.astype(BF16)
        wb_ref[j * SLAB:(j + 1) * SLAB, ns:2 * ns] = aim.astype(BF16)
        kk = (dot_nt(are, ctre_split) - dot_nt(aim, ctim_split)).astype(BF16)
        for jj in range(N_PHASE - k):
            t = jj + k
            kt_ref[jj * SLAB:(jj + 1) * SLAB, t * SLAB:(t + 1) * SLAB] = kk
    for t in range(N_PHASE):
        pre, pim = lam_pow(t + 1)
        cre = ctre * pre - ctim * pim
        cim = ctre * pim + ctim * pre
        wc_ref[0:ns, t * SLAB:(t + 1) * SLAB] = cre.T.astype(BF16)
        wc_ref[ns:2 * ns, t * SLAB:(t + 1) * SLAB] = (-cim).T.astype(BF16)
    pw_ref[...] = jnp.zeros_like(pw_ref)
    for r, n in enumerate((N_PHASE, N_PHASE * SEG)):
        pre, pim = lam_pow(n)
        pw_ref[r:r + 1, 0:ns] = pre
        pw_ref[r:r + 1, ns:2 * ns] = pim


def _ssm_prep(lam_re, lam_im, log_step, b_re, b_im, c_re, c_im, riders=()):
    ns = SLAB_STATES

    def block_diag(m):
        m = m.reshape(N_SLAB, GROUPS_PER_SLAB, 16, STATE_DIM)
        eye = jnp.eye(GROUPS_PER_SLAB, dtype=bool)[None, :, None, :, None]
        return jnp.where(eye, m[:, :, :, None, :], 0.0).reshape(N_SLAB, SLAB, ns)

    def vec(v):
        return v.reshape(N_SLAB, 1, ns)

    mats = [block_diag(jnp.transpose(b_re, (0, 2, 1))), block_diag(jnp.transpose(b_im, (0, 2, 1))),
            block_diag(c_re), block_diag(c_im)]
    vecs = [vec(lam_re), vec(lam_im), vec(jnp.broadcast_to(log_step[:, None], lam_re.shape))]
    w_shape = jax.ShapeDtypeStruct((N_SLAB, 2 * ns, 2 * ns), BF16)
    w_spec = pl.BlockSpec((None, 2 * ns, 2 * ns), lambda q: (q, 0, 0))
    rider_in, rider_out, rider_shapes = _rider_specs(riders, N_SLAB)
    out = pl.pallas_call(
        functools.partial(_ssm_prep_body, n_riders=len(riders)),
        grid=(N_SLAB,),
        in_specs=[pl.BlockSpec((None, SLAB, ns), lambda q: (q, 0, 0))] * 4
        + [pl.BlockSpec((None, 1, ns), lambda q: (q, 0, 0))] * 3 + rider_in,
        out_specs=[w_spec, w_spec, w_spec, pl.BlockSpec((None, 8, 2 * ns), lambda q: (q, 0, 0))] + rider_out,
        out_shape=[w_shape, w_shape, w_shape, jax.ShapeDtypeStruct((N_SLAB, 8, 2 * ns), F32)] + rider_shapes,
        name="ssm_prep",
        compiler_params=_params(("arbitrary",)),
    )(*mats, *vecs, *riders)
    return tuple(out[:4]), tuple(out[4:])


def _cmul(are, aim, bre, bim):
    return are * bre - aim * bim, are * bim + aim * bre


HALF = SLAB_STATES // LANE
GROUP_ROWS = 8 * SEG


def _mult(pw_ref, r, c):
    return (pw_ref[r:r + 1, c * LANE:(c + 1) * LANE],
            pw_ref[r:r + 1, (HALF + c) * LANE:(HALF + c + 1) * LANE])


def _state_increments(lhs, wb_ref, e_ref):
    e = _dot(lhs, wb_ref[...])
    for lt in range(2 * HALF):
        e_ref[lt] = e[:, lt * LANE:(lt + 1) * LANE]


def _scan_local(e_ref, p_ref, pw_ref, c, base):
    l8re, l8im = _mult(pw_ref, 0, c)
    lre = jnp.zeros((8, LANE), F32)
    lim = jnp.zeros((8, LANE), F32)
    for i in range(SEG):
        idx = pl.ds(base + i, 8, stride=SEG)
        p_ref[c, idx, :] = lre
        p_ref[HALF + c, idx, :] = lim
        mre, mim = _cmul(l8re, l8im, lre, lim)
        lre = mre + e_ref[c, idx, :]
        lim = mim + e_ref[HALF + c, idx, :]
    return lre, lim


def _scan_fixup(p_ref, pw_ref, c, base, cre, cim):
    l8re, l8im = _mult(pw_ref, 0, c)
    for i in range(SEG):
        idx = pl.ds(base + i, 8, stride=SEG)
        p_ref[c, idx, :] = p_ref[c, idx, :] + cre
        p_ref[HALF + c, idx, :] = p_ref[HALF + c, idx, :] + cim
        cre, cim = _cmul(l8re, l8im, cre, cim)


def _readout(lhs, p_ref, kt_ref, wc_ref, d_ref):
    prev = jnp.concatenate([p_ref[lt] for lt in range(2 * HALF)], axis=1)
    y = _dot(lhs, kt_ref[...]) + _dot(prev.astype(BF16), wc_ref[...]) + d_ref[...] * lhs.astype(F32)
    return y.astype(BF16)


def _ssm_prompt_body(u_ref, wb_ref, kt_ref, wc_ref, pw_ref, d_ref, y_ref, sre_ref, sim_ref,
                     e_ref, p_ref, end_ref, cin_ref, carry_ref, *, n_seq, rows):
    total = n_seq * rows
    lhs = jnp.concatenate([u_ref[t].reshape(total, LANE) for t in range(N_PHASE)], axis=1)
    _state_increments(lhs, wb_ref, e_ref)

    @pl.when((pl.program_id(0) == 0) & (pl.program_id(1) == 0))
    def _():
        end_ref[...] = jnp.zeros_like(end_ref)

    @pl.when(pl.program_id(1) == 0)
    def _():
        carry_ref[...] = jnp.zeros_like(carry_ref)

    groups = rows // GROUP_ROWS
    for c in range(HALF):
        for b in range(n_seq):
            for j in range(groups):
                lre, lim = _scan_local(e_ref, p_ref, pw_ref, c, b * rows + j * GROUP_ROWS)
                idx = pl.ds(j * 64 + b, 8, stride=8)
                end_ref[c, idx, :] = lre
                end_ref[HALF + c, idx, :] = lim
    for c in range(HALF):
        lsre, lsim = _mult(pw_ref, 1, c)
        cre = carry_ref[c]
        cim = carry_ref[HALF + c]
        for s in range(rows // SEG):
            cin_ref[c, 8 * s:8 * s + 8, :] = cre
            cin_ref[HALF + c, 8 * s:8 * s + 8, :] = cim
            mre, mim = _cmul(lsre, lsim, cre, cim)
            cre = mre + end_ref[c, 8 * s:8 * s + 8, :]
            cim = mim + end_ref[HALF + c, 8 * s:8 * s + 8, :]
        carry_ref[c] = cre
        carry_ref[HALF + c] = cim
        sre_ref[:, c * LANE:(c + 1) * LANE] = cre[:n_seq]
        sim_ref[:, c * LANE:(c + 1) * LANE] = cim[:n_seq]
    for c in range(HALF):
        for b in range(n_seq):
            for j in range(groups):
                idx = pl.ds(j * 64 + b, 8, stride=8)
                _scan_fixup(p_ref, pw_ref, c, b * rows + j * GROUP_ROWS,
                            cin_ref[c, idx, :], cin_ref[HALF + c, idx, :])
    y = _readout(lhs, p_ref, kt_ref, wc_ref, d_ref)
    for t in range(N_PHASE):
        y_ref[t] = y[:, t * LANE:(t + 1) * LANE].reshape(n_seq, rows, LANE)


def _ssm_sample_body(u_ref, wb_ref, kt_ref, wc_ref, pw_ref, d_ref, h0re_ref, h0im_ref,
                     y_ref, sre_ref, sim_ref, e_ref, p_ref, *, rows):
    lhs = jnp.concatenate([u_ref[t] for t in range(N_PHASE)], axis=1)
    _state_increments(lhs, wb_ref, e_ref)
    for c in range(HALF):
        lsre, lsim = _mult(pw_ref, 1, c)
        for j in range(rows // GROUP_ROWS):
            lre, lim = _scan_local(e_ref, p_ref, pw_ref, c, j * GROUP_ROWS)
            cre = h0re_ref[8 * j:8 * j + 8, c * LANE:(c + 1) * LANE]
            cim = h0im_ref[8 * j:8 * j + 8, c * LANE:(c + 1) * LANE]
            mre, mim = _cmul(lsre, lsim, cre, cim)
            sre_ref[8 * j:8 * j + 8, c * LANE:(c + 1) * LANE] = mre + lre
            sim_ref[8 * j:8 * j + 8, c * LANE:(c + 1) * LANE] = mim + lim
            _scan_fixup(p_ref, pw_ref, c, j * GROUP_ROWS, cre, cim)
    y = _readout(lhs, p_ref, kt_ref, wc_ref, d_ref)
    for t in range(N_PHASE):
        y_ref[t] = y[:, t * LANE:(t + 1) * LANE]


def _ssm(u8, h0, wb, kt, wc, pw, d8, n_seq):
    n_rows = u8.shape[1]
    d = D_MODEL
    ns = SLAB_STATES
    rows_per_seq = n_rows // n_seq
    has_state = h0 is not None
    state_shape = jax.ShapeDtypeStruct((n_seq, N_SLAB * ns), F32)
    if has_state:
        assert rows_per_seq == SEG, "carried-state path scans one segment per sequence"
        rows, total, grid = n_rows, n_rows, (N_SLAB,)
        ph_spec = pl.BlockSpec((N_PHASE, rows, LANE), lambda q: (0, 0, q))
        slab = lambda q: (q, 0, 0)
        state_spec = pl.BlockSpec((n_seq, ns), lambda q: (0, q))
        body = functools.partial(_ssm_sample_body, rows=rows)
        u_in, y_dims = u8, (N_PHASE, n_rows, d)
    else:
        assert n_seq <= 8, "sequences ride the sublanes of the segment chain"
        rows = min(SSM_ROWS, rows_per_seq)
        assert rows_per_seq % rows == 0
        total = n_seq * rows
        grid = (N_SLAB, rows_per_seq // rows)
        ph_spec = pl.BlockSpec((N_PHASE, n_seq, rows, LANE), lambda q, i: (0, 0, i, q))
        slab = lambda q, i: (q, 0, 0)
        state_spec = pl.BlockSpec((n_seq, ns), lambda q, i: (0, q))
        body = functools.partial(_ssm_prompt_body, n_seq=n_seq, rows=rows)
        y_dims = (N_PHASE, n_seq, rows_per_seq, d)
        u_in = u8.reshape(y_dims)
    assert rows % GROUP_ROWS == 0
    w_spec = pl.BlockSpec((None, 2 * ns, 2 * ns), slab)
    in_specs = [ph_spec, w_spec, w_spec, w_spec,
                pl.BlockSpec((None, 8, 2 * ns), slab), pl.BlockSpec((None, 1, d), slab)]
    args = [u_in, wb, kt, wc, pw, d8]
    scratch = [pltpu.VMEM((2 * HALF, total, LANE), F32),
               pltpu.VMEM((2 * HALF, total, LANE), F32)]
    if has_state:
        in_specs += [state_spec, state_spec]
        args += [h0[0].reshape(n_seq, -1), h0[1].reshape(n_seq, -1)]
    else:
        seg_rows = 8 * (rows // SEG)
        scratch += [pltpu.VMEM((2 * HALF, seg_rows, LANE), F32),
                    pltpu.VMEM((2 * HALF, seg_rows, LANE), F32),
                    pltpu.VMEM((2 * HALF, 8, LANE), F32)]
    y8, s_re, s_im = pl.pallas_call(
        body,
        grid=grid,
        in_specs=in_specs,
        out_specs=[ph_spec, state_spec, state_spec],
        out_shape=[jax.ShapeDtypeStruct(y_dims, BF16), state_shape, state_shape],
        scratch_shapes=scratch,
        name="ssm_sample" if has_state else "ssm_prompt",
        compiler_params=_params(("arbitrary",) * len(grid)),
    )(*args)
    return y8.reshape(N_PHASE, n_rows, d), s_re, s_im


def _mix_out_body(y_ref, a_ref, gs_ref, x_ref, wglu_ref, wo_ref, o_ref, *, rows):
    d = D_MODEL
    glu = _dot(jax.nn.gelu(_stack_phases(y_ref).astype(F32)).astype(BF16), wglu_ref[...])
    y_ssm = glu[:, :d] * jax.nn.sigmoid(glu[:, d:])
    merged = _stack_phases(a_ref).astype(F32) + _stack_phases(gs_ref).astype(F32) * y_ssm
    mix = _dot(merged.astype(BF16), wo_ref[...])
    for t in range(N_PHASE):
        o_ref[t] = x_ref[t] + mix[t * rows:(t + 1) * rows]


def _mix_out(y8, a8, gs8, xp, w_glu, w_o):
    n_rows = xp.shape[1]
    d = D_MODEL
    rows = min(MIX_ROWS, n_rows)
    assert n_rows % rows == 0
    ph_spec = pl.BlockSpec((N_PHASE, rows, d), lambda i: (0, i, 0))
    return pl.pallas_call(
        functools.partial(_mix_out_body, rows=rows),
        grid=(n_rows // rows,),
        in_specs=[ph_spec, ph_spec, ph_spec, ph_spec, _const_spec(w_glu.shape), _const_spec(w_o.shape)],
        out_specs=ph_spec,
        out_shape=jax.ShapeDtypeStruct((N_PHASE, n_rows, d), F32),
        name="mix_out",
        compiler_params=_params(("arbitrary",)),
    )(y8, a8, gs8, xp, w_glu, w_o)


def _after_ffn1(xp, n_seq, conv_prev, h0, w, norm_final, last):
    u8, a8, gs8, conv_state = _mix_in(xp, conv_prev, w["norm_mix"], w["w_in"], w["w_conv"], w["w_conv_out"], n_seq)
    y8, s_re, s_im = _ssm(u8, h0, *w["ssm"], n_seq)
    x2 = _mix_out(y8, a8, gs8, xp, w["w_glu"], w["w_o"])
    xt, _ = _ffn(x2, w["norm_ffn2"], w["ffn2_gate"], w["ffn2_up"], w["ffn2_down"], norm_final, last, False)
    return xt, conv_state, s_re.reshape(n_seq, -1, STATE_DIM), s_im.reshape(n_seq, -1, STATE_DIM)


def kernel(x_prompt, x_sample, state_conv, state_ssm_re, state_ssm_im, norm_ffn1, w_ffn1_gate, w_ffn1_up, w_ffn1_down, norm_mix, w_in, w_conv, w_conv_out, ssm_lambda_re, ssm_lambda_im, ssm_log_step, ssm_b_re, ssm_b_im, ssm_c_re, ssm_c_im, ssm_d, w_glu, w_o, norm_ffn2, w_ffn2_gate, w_ffn2_up, w_ffn2_down, norm_final):
    depth, d = w_in.shape[0], w_in.shape[1]
    row = lambda v: v.reshape(1, -1)
    nf = row(norm_final)
    xt_p = x_prompt.reshape(-1, d)
    xt_s = x_sample.reshape(-1, d)
    n_p, n_s = x_prompt.shape[0], x_sample.shape[0]
    outs_p, outs_s = [], []
    for l in range(depth):
        last = l == depth - 1
        ssm_w, (f1g, f1u, f1d) = _ssm_prep(
            ssm_lambda_re[l], ssm_lambda_im[l], ssm_log_step[l], ssm_b_re[l], ssm_b_im[l], ssm_c_re[l],
            ssm_c_im[l], riders=(w_ffn1_gate[l], w_ffn1_up[l], w_ffn1_down[l]))
        ffn1 = (row(norm_ffn1[l]), f1g, f1u, f1d, nf, False, True)
        xp_p, (win, wco, wgl, wo, f2g, f2u, f2d) = _ffn(
            xt_p, *ffn1, riders=(w_in[l], w_conv_out[l], w_glu[l], w_o[l],
                                 w_ffn2_gate[l], w_ffn2_up[l], w_ffn2_down[l]))
        xp_s, _ = _ffn(xt_s, *ffn1)
        d8 = jnp.tile(ssm_d[l].reshape(N_SLAB, 1, SLAB), (1, 1, N_PHASE))
        w = dict(norm_mix=row(norm_mix[l]), w_in=win, w_conv=w_conv[l], w_conv_out=wco, ssm=(*ssm_w, d8),
                 w_glu=wgl, w_o=wo, norm_ffn2=row(norm_ffn2[l]), ffn2_gate=f2g, ffn2_up=f2u, ffn2_down=f2d)
        xt_p, *rest_p = _after_ffn1(xp_p, n_p, None, None, w, nf, last)
        xt_s, *rest_s = _after_ffn1(xp_s, n_s, state_conv[l], (state_ssm_re[l], state_ssm_im[l]), w, nf, last)
        outs_p.append(rest_p)
        outs_s.append(rest_s)
    stack = lambda outs: tuple(jnp.stack(leaf) for leaf in zip(*outs))
    return (xt_p.reshape(x_prompt.shape), xt_s.reshape(x_sample.shape), *stack(outs_p), *stack(outs_s))
```

```python
import functools

import jax
import jax.numpy as jnp
from jax import lax
from jax.experimental import pallas as pl
from jax.experimental.pallas import tpu as pltpu

F32 = jnp.float32
BF16 = jnp.bfloat16

D_MODEL = 1024
N_PHASE = 8
LANE = 128
SLAB = 128
N_SLAB = D_MODEL // SLAB
GROUPS_PER_SLAB = 8
STATE_DIM = 64
SLAB_STATES = GROUPS_PER_SLAB * STATE_DIM
SEG = 4
RMS_EPS = 1e-6
VMEM_LIMIT = 56 * 1024 * 1024

FFN_ROWS = 512
MIX_ROWS = 64
SSM_ROWS = 256


def _rms(x, g):
    return x * lax.rsqrt(jnp.mean(x * x, axis=-1, keepdims=True) + RMS_EPS) * g


def _dot(a, b):
    return jnp.dot(a, b, preferred_element_type=F32)


def _const_spec(shape):
    zeros = (0,) * len(shape)
    return pl.BlockSpec(shape, lambda *_: zeros, pipeline_mode=pl.Buffered(1))


def _params(semantics):
    return pltpu.CompilerParams(dimension_semantics=semantics, vmem_limit_bytes=VMEM_LIMIT)


def _stack_phases(ref):
    return jnp.concatenate([ref[t] for t in range(N_PHASE)], axis=0)


def _ffn_math(x, g_ref, wg_ref, wu_ref, wd_ref):
    h = _rms(x, g_ref[...]).astype(BF16)
    act = (jax.nn.silu(_dot(h, wg_ref[...])) * _dot(h, wu_ref[...])).astype(BF16)
    return x + 0.5 * _dot(act, wd_ref[...])


def _rider_specs(weights, n_steps):
    in_specs, out_specs, out_shapes = [], [], []
    for w in weights:
        n_blocks = n_steps
        while w.shape[0] % (16 * n_blocks):
            assert n_blocks % 2 == 0
            n_blocks //= 2
        spec = pl.BlockSpec((w.shape[0] // n_blocks, w.shape[1]),
                            lambda i, rep=n_steps // n_blocks: (i // rep, 0))
        in_specs.append(spec)
        out_specs.append(spec)
        out_shapes.append(jax.ShapeDtypeStruct(w.shape, BF16))
    return in_specs, out_specs, out_shapes


def _convert_riders(in_refs, out_refs):
    for w_ref, o_ref in zip(in_refs, out_refs):
        o_ref[...] = w_ref[...].astype(BF16)


def _ffn_to_phase_body(*refs, rows, n_riders):
    x_ref, g_ref, wg_ref, wu_ref, wd_ref = refs[:5]
    o_ref, slab_ref = refs[5 + n_riders], refs[-1]
    y = _ffn_math(x_ref[...], g_ref, wg_ref, wu_ref, wd_ref)
    for lt in range(D_MODEL // LANE):
        slab_ref[lt] = y[:, lt * LANE:(lt + 1) * LANE]
        for t in range(N_PHASE):
            o_ref[t, :, lt * LANE:(lt + 1) * LANE] = slab_ref[lt, pl.ds(t, rows, stride=N_PHASE), :]
    _convert_riders(refs[5:5 + n_riders], refs[6 + n_riders:6 + 2 * n_riders])


def _store_token_major(y, o_ref, slab_ref, rows):
    for lt in range(D_MODEL // LANE):
        for t in range(N_PHASE):
            slab_ref[lt, pl.ds(t, rows, stride=N_PHASE), :] = y[t * rows:(t + 1) * rows, lt * LANE:(lt + 1) * LANE]
        o_ref[:, lt * LANE:(lt + 1) * LANE] = slab_ref[lt]


def _ffn_to_phase(x, g, wg, wu, wd, riders=()):
    n_tok, d = x.shape
    n_rows = n_tok // N_PHASE
    d_ff = wg.shape[1]
    rows = FFN_ROWS // N_PHASE
    assert n_rows % rows == 0
    n_steps = n_rows // rows
    rider_in, rider_out, rider_shapes = _rider_specs(riders, n_steps)
    out = pl.pallas_call(
        functools.partial(_ffn_to_phase_body, rows=rows, n_riders=len(riders)),
        grid=(n_steps,),
        in_specs=[
            pl.BlockSpec((N_PHASE * rows, d), lambda i: (i, 0)),
            _const_spec((1, d)),
            _const_spec((d, d_ff)),
            _const_spec((d, d_ff)),
            _const_spec((d_ff, d)),
        ] + rider_in,
        out_specs=[pl.BlockSpec((N_PHASE, rows, d), lambda i: (0, i, 0))] + rider_out,
        out_shape=[jax.ShapeDtypeStruct((N_PHASE, n_rows, d), F32)] + rider_shapes,
        scratch_shapes=[pltpu.VMEM((d // LANE, N_PHASE * rows, LANE), F32)],
        name="ffn_to_phase",
        compiler_params=_params(("arbitrary",)),
    )(x, g, wg, wu, wd, *riders)
    return out[0], tuple(out[1:])


def _mix_in_compute(x_ref, g, win_ref, wcv, wco_ref, shift, rows):
    d = D_MODEL
    h = _rms(_stack_phases(x_ref), g).astype(BF16)

    def proj(k):
        return _dot(h, win_ref[:, k * d:(k + 1) * d])

    z = proj(1) * proj(0)
    z6 = z[6 * rows:7 * rows]
    z7 = z[7 * rows:8 * rows]
    s6 = shift(z6, 0)
    s7 = shift(z7, 1)
    z1 = jnp.concatenate([s7, z[:7 * rows]], axis=0)
    z2 = jnp.concatenate([s6, s7, z[:6 * rows]], axis=0)
    conv = wcv[2:3] * z + wcv[1:2] * z1 + wcv[0:1] * z2
    y_conv = _dot((proj(2) * conv).astype(BF16), wco_ref[...])
    u = proj(3)
    a = jax.nn.sigmoid(proj(4)) * y_conv
    gs = jax.nn.sigmoid(proj(5))
    return u, a, gs, z6, z7


def _store_phases(ref, val, rows):
    val = val.astype(ref.dtype)
    for t in range(N_PHASE):
        ref[t] = val[t * rows:(t + 1) * rows]


def _mix_in_prompt_body(x_ref, g_ref, win_ref, wcv_ref, wco_ref,
                        u_ref, a_ref, gs_ref, z6_ref, z7_ref, carry_ref, *, rows):
    @pl.when(pl.program_id(1) == 0)
    def _():
        carry_ref[...] = jnp.zeros_like(carry_ref)

    row_id = lax.broadcasted_iota(jnp.int32, (rows, D_MODEL), 0)

    def shift(z, k):
        return jnp.where(row_id == 0, carry_ref[k:k + 1, :], pltpu.roll(z, 1, 0))

    u, a, gs, z6, z7 = _mix_in_compute(x_ref, g_ref[...], win_ref, wcv_ref[...], wco_ref, shift, rows)
    last6 = z6[rows - 1:rows]
    last7 = z7[rows - 1:rows]
    carry_ref[0:1, :] = last6
    carry_ref[1:2, :] = last7
    z6_ref[...] = last6
    z7_ref[...] = last7
    _store_phases(u_ref, u, rows)
    _store_phases(a_ref, a, rows)
    _store_phases(gs_ref, gs, rows)


def _mix_in_sample_body(x_ref, s6_ref, s7_ref, g_ref, win_ref, wcv_ref, wco_ref,
                        u_ref, a_ref, gs_ref, z6_ref, z7_ref, *, rows, rows_per_seq):
    row_id = lax.broadcasted_iota(jnp.int32, (rows, D_MODEL), 0)
    starts = (s6_ref, s7_ref)

    def shift(z, k):
        return jnp.where(row_id % rows_per_seq == 0, starts[k][...], pltpu.roll(z, 1, 0))

    u, a, gs, z6, z7 = _mix_in_compute(x_ref, g_ref[...], win_ref, wcv_ref[...], wco_ref, shift, rows)
    z6_ref[...] = z6
    z7_ref[...] = z7
    _store_phases(u_ref, u, rows)
    _store_phases(a_ref, a, rows)
    _store_phases(gs_ref, gs, rows)


def _mix_in(xp, conv_prev, g, w_in, w_conv, w_conv_out, n_seq):
    n_rows = xp.shape[1]
    d = D_MODEL
    rows_per_seq = n_rows // n_seq
    ph_shape = jax.ShapeDtypeStruct((N_PHASE, n_rows, d), BF16)
    weights = (g, w_in, w_conv, w_conv_out)
    weight_specs = [_const_spec(w.shape) for w in weights]
    if conv_prev is None:
        rows = min(MIX_ROWS, rows_per_seq)
        assert rows_per_seq % rows == 0
        tiles = rows_per_seq // rows
        ph_spec = pl.BlockSpec((N_PHASE, rows, d), lambda b, i: (0, b * tiles + i, 0))
        last_spec = pl.BlockSpec((None, 1, d), lambda b, i: (b, 0, 0))
        u8, a8, gs8, z6, z7 = pl.pallas_call(
            functools.partial(_mix_in_prompt_body, rows=rows),
            grid=(n_seq, tiles),
            in_specs=[ph_spec] + weight_specs,
            out_specs=[ph_spec, ph_spec, ph_spec, last_spec, last_spec],
            out_shape=[ph_shape, ph_shape, ph_shape,
                       jax.ShapeDtypeStruct((n_seq, 1, d), F32), jax.ShapeDtypeStruct((n_seq, 1, d), F32)],
            scratch_shapes=[pltpu.VMEM((8, d), F32)],
            name="mix_in_prompt",
            compiler_params=_params(("arbitrary", "arbitrary")),
        )(xp, *weights)
        z6, z7 = z6[:, 0], z7[:, 0]
    else:
        rows = n_rows
        start = jnp.zeros((n_seq, rows_per_seq, 2, d), F32).at[:, 0].set(conv_prev).reshape(n_rows, 2, d)
        full = lambda shape: pl.BlockSpec(shape, lambda i: (0,) * len(shape))
        u8, a8, gs8, z6f, z7f = pl.pallas_call(
            functools.partial(_mix_in_sample_body, rows=rows, rows_per_seq=rows_per_seq),
            grid=(1,),
            in_specs=[full((N_PHASE, rows, d)), full((rows, d)), full((rows, d))] + weight_specs,
            out_specs=[full((N_PHASE, rows, d))] * 3 + [full((rows, d))] * 2,
            out_shape=[ph_shape, ph_shape, ph_shape,
                       jax.ShapeDtypeStruct((n_rows, d), F32), jax.ShapeDtypeStruct((n_rows, d), F32)],
            name="mix_in_sample",
            compiler_params=_params(("arbitrary",)),
        )(xp, start[:, 0], start[:, 1], *weights)
        z6 = z6f[rows_per_seq - 1::rows_per_seq]
        z7 = z7f[rows_per_seq - 1::rows_per_seq]
    return u8, a8, gs8, jnp.stack([z6, z7], axis=1)


def _split_bf16(a):
    hi = a.astype(BF16)
    return hi, (a - hi.astype(F32)).astype(BF16)


def _ssm_prep_body(*refs, n_riders):
    bbre_ref, bbim_ref, ctre_ref, ctim_ref, lre_ref, lim_ref, lstep_ref = refs[:7]
    wb_ref, kt_ref, wc_ref, pw_ref = refs[7 + n_riders:11 + n_riders]
    _convert_riders(refs[7:7 + n_riders], refs[11 + n_riders:])
    ns = SLAB_STATES
    bbre, bbim = bbre_ref[...], bbim_ref[...]
    ctre, ctim = ctre_ref[...], ctim_ref[...]
    lre, lim = lre_ref[...], lim_ref[...]
    step = jnp.exp(lstep_ref[...])

    def lam_pow(n):
        mag = jnp.exp((n * lre) * step)
        ang = (n * lim) * step
        return mag * jnp.cos(ang), mag * jnp.sin(ang)

    l1re, l1im = lam_pow(1)
    den = lre * lre + lim * lim
    fre = ((l1re - 1.0) * lre + l1im * lim) / den
    fim = (l1im * lre - (l1re - 1.0) * lim) / den

    def dot_nt(a, b_split):
        nt = lambda p, q: lax.dot_general(p, q, (((1,), (1,)), ((), ())), preferred_element_type=F32)
        a_hi, a_lo = _split_bf16(a)
        b_hi, b_lo = b_split
        return nt(a_hi, b_hi) + nt(a_hi, b_lo) + nt(a_lo, b_hi)

    ctre_split = _split_bf16(ctre)
    ctim_split = _split_bf16(ctim)
    kt_ref[...] = jnp.zeros_like(kt_ref)
    for k in range(N_PHASE):
        pre, pim = lam_pow(k)
        gre = fre * pre - fim * pim
        gim = fre * pim + fim * pre
        are = bbre * gre - bbim * gim
        aim = bbre * gim + bbim * gre
        j = N_PHASE - 1 - k
        wb_ref[j * SLAB:(j + 1) * SLAB, 0:ns] = are.astype(BF16)
        wb_ref[j * SLAB:(j + 1) * SLAB, ns:2 * ns] = aim.astype(BF16)
        kk = (dot_nt(are, ctre_split) - dot_nt(aim, ctim_split)).astype(BF16)
        for jj in range(N_PHASE - k):
            t = jj + k
            kt_ref[jj * SLAB:(jj + 1) * SLAB, t * SLAB:(t + 1) * SLAB] = kk
    for t in range(N_PHASE):
        pre, pim = lam_pow(t + 1)
        cre = ctre * pre - ctim * pim
        cim = ctre * pim + ctim * pre
        wc_ref[0:ns, t * SLAB:(t + 1) * SLAB] = cre.T.astype(BF16)
        wc_ref[ns:2 * ns, t * SLAB:(t + 1) * SLAB] = (-cim).T.astype(BF16)
    pw_ref[...] = jnp.zeros_like(pw_ref)
    for r, n in enumerate((N_PHASE, N_PHASE * SEG)):
        pre, pim = lam_pow(n)
        pw_ref[r:r + 1, 0:ns] = pre
        pw_ref[r:r + 1, ns:2 * ns] = pim


def _ssm_prep(lam_re, lam_im, log_step, b_re, b_im, c_re, c_im, riders=()):
    ns = SLAB_STATES

    def block_diag(m):
        m = m.reshape(N_SLAB, GROUPS_PER_SLAB, 16, STATE_DIM)
        eye = jnp.eye(GROUPS_PER_SLAB, dtype=bool)[None, :, None, :, None]
        return jnp.where(eye, m[:, :, :, None, :], 0.0).reshape(N_SLAB, SLAB, ns)

    def vec(v):
        return v.reshape(N_SLAB, 1, ns)

    mats = [block_diag(jnp.transpose(b_re, (0, 2, 1))), block_diag(jnp.transpose(b_im, (0, 2, 1))),
            block_diag(c_re), block_diag(c_im)]
    vecs = [vec(lam_re), vec(lam_im), vec(jnp.broadcast_to(log_step[:, None], lam_re.shape))]
    w_shape = jax.ShapeDtypeStruct((N_SLAB, 2 * ns, 2 * ns), BF16)
    w_spec = pl.BlockSpec((None, 2 * ns, 2 * ns), lambda q: (q, 0, 0))
    rider_in, rider_out, rider_shapes = _rider_specs(riders, N_SLAB)
    out = pl.pallas_call(
        functools.partial(_ssm_prep_body, n_riders=len(riders)),
        grid=(N_SLAB,),
        in_specs=[pl.BlockSpec((None, SLAB, ns), lambda q: (q, 0, 0))] * 4
        + [pl.BlockSpec((None, 1, ns), lambda q: (q, 0, 0))] * 3 + rider_in,
        out_specs=[w_spec, w_spec, w_spec, pl.BlockSpec((None, 8, 2 * ns), lambda q: (q, 0, 0))] + rider_out,
        out_shape=[w_shape, w_shape, w_shape, jax.ShapeDtypeStruct((N_SLAB, 8, 2 * ns), F32)] + rider_shapes,
        name="ssm_prep",
        compiler_params=_params(("arbitrary",)),
    )(*mats, *vecs, *riders)
    return tuple(out[:4]), tuple(out[4:])


def _cmul(are, aim, bre, bim):
    return are * bre - aim * bim, are * bim + aim * bre


HALF = SLAB_STATES // LANE
GROUP_ROWS = 8 * SEG


def _mult(pw_ref, r, c):
    return (pw_ref[r:r + 1, c * LANE:(c + 1) * LANE],
            pw_ref[r:r + 1, (HALF + c) * LANE:(HALF + c + 1) * LANE])


def _state_increments(lhs, wb_ref, e_ref):
    e = _dot(lhs, wb_ref[...])
    for lt in range(2 * HALF):
        e_ref[lt] = e[:, lt * LANE:(lt + 1) * LANE]


def _scan_local(e_ref, p_ref, pw_ref, c, base):
    l8re, l8im = _mult(pw_ref, 0, c)
    lre = jnp.zeros((8, LANE), F32)
    lim = jnp.zeros((8, LANE), F32)
    for i in range(SEG):
        idx = pl.ds(base + i, 8, stride=SEG)
        p_ref[c, idx, :] = lre
        p_ref[HALF + c, idx, :] = lim
        mre, mim = _cmul(l8re, l8im, lre, lim)
        lre = mre + e_ref[c, idx, :]
        lim = mim + e_ref[HALF + c, idx, :]
    return lre, lim


def _scan_fixup(p_ref, pw_ref, c, base, cre, cim):
    l8re, l8im = _mult(pw_ref, 0, c)
    for i in range(SEG):
        idx = pl.ds(base + i, 8, stride=SEG)
        p_ref[c, idx, :] = p_ref[c, idx, :] + cre
        p_ref[HALF + c, idx, :] = p_ref[HALF + c, idx, :] + cim
        cre, cim = _cmul(l8re, l8im, cre, cim)


def _readout(lhs, p_ref, kt_ref, wc_ref, d_ref):
    prev = jnp.concatenate([p_ref[lt] for lt in range(2 * HALF)], axis=1)
    y = _dot(lhs, kt_ref[...]) + _dot(prev.astype(BF16), wc_ref[...]) + d_ref[...] * lhs.astype(F32)
    return y.astype(BF16)


def _ssm_prompt_body(u_ref, wb_ref, kt_ref, wc_ref, pw_ref, d_ref, y_ref, sre_ref, sim_ref,
                     e_ref, p_ref, end_ref, cin_ref, carry_ref, *, n_seq, rows):
    total = n_seq * rows
    lhs = jnp.concatenate([u_ref[t].reshape(total, LANE) for t in range(N_PHASE)], axis=1)
    _state_increments(lhs, wb_ref, e_ref)

    @pl.when((pl.program_id(0) == 0) & (pl.program_id(1) == 0))
    def _():
        end_ref[...] = jnp.zeros_like(end_ref)

    @pl.when(pl.program_id(1) == 0)
    def _():
        carry_ref[...] = jnp.zeros_like(carry_ref)

    groups = rows // GROUP_ROWS
    for c in range(HALF):
        for b in range(n_seq):
            for j in range(groups):
                lre, lim = _scan_local(e_ref, p_ref, pw_ref, c, b * rows + j * GROUP_ROWS)
                idx = pl.ds(j * 64 + b, 8, stride=8)
                end_ref[c, idx, :] = lre
                end_ref[HALF + c, idx, :] = lim
    for c in range(HALF):
        lsre, lsim = _mult(pw_ref, 1, c)
        cre = carry_ref[c]
        cim = carry_ref[HALF + c]
        for s in range(rows // SEG):
            cin_ref[c, 8 * s:8 * s + 8, :] = cre
            cin_ref[HALF + c, 8 * s:8 * s + 8, :] = cim
            mre, mim = _cmul(lsre, lsim, cre, cim)
            cre = mre + end_ref[c, 8 * s:8 * s + 8, :]
            cim = mim + end_ref[HALF + c, 8 * s:8 * s + 8, :]
        carry_ref[c] = cre
        carry_ref[HALF + c] = cim
        sre_ref[:, c * LANE:(c + 1) * LANE] = cre[:n_seq]
        sim_ref[:, c * LANE:(c + 1) * LANE] = cim[:n_seq]
    for c in range(HALF):
        for b in range(n_seq):
            for j in range(groups):
                idx = pl.ds(j * 64 + b, 8, stride=8)
                _scan_fixup(p_ref, pw_ref, c, b * rows + j * GROUP_ROWS,
                            cin_ref[c, idx, :], cin_ref[HALF + c, idx, :])
    y = _readout(lhs, p_ref, kt_ref, wc_ref, d_ref)
    for t in range(N_PHASE):
        y_ref[t] = y[:, t * LANE:(t + 1) * LANE].reshape(n_seq, rows, LANE)


def _ssm_sample_body(u_ref, wb_ref, kt_ref, wc_ref, pw_ref, d_ref, h0re_ref, h0im_ref,
                     y_ref, sre_ref, sim_ref, e_ref, p_ref, *, rows):
    lhs = jnp.concatenate([u_ref[t] for t in range(N_PHASE)], axis=1)
    _state_increments(lhs, wb_ref, e_ref)
    for c in range(HALF):
        lsre, lsim = _mult(pw_ref, 1, c)
        for j in range(rows // GROUP_ROWS):
            lre, lim = _scan_local(e_ref, p_ref, pw_ref, c, j * GROUP_ROWS)
            cre = h0re_ref[8 * j:8 * j + 8, c * LANE:(c + 1) * LANE]
            cim = h0im_ref[8 * j:8 * j + 8, c * LANE:(c + 1) * LANE]
            mre, mim = _cmul(lsre, lsim, cre, cim)
            sre_ref[8 * j:8 * j + 8, c * LANE:(c + 1) * LANE] = mre + lre
            sim_ref[8 * j:8 * j + 8, c * LANE:(c + 1) * LANE] = mim + lim
            _scan_fixup(p_ref, pw_ref, c, j * GROUP_ROWS, cre, cim)
    y = _readout(lhs, p_ref, kt_ref, wc_ref, d_ref)
    for t in range(N_PHASE):
        y_ref[t] = y[:, t * LANE:(t + 1) * LANE]


def _ssm(u8, h0, wb, kt, wc, pw, d8, n_seq):
    n_rows = u8.shape[1]
    d = D_MODEL
    ns = SLAB_STATES
    rows_per_seq = n_rows // n_seq
    has_state = h0 is not None
    state_shape = jax.ShapeDtypeStruct((n_seq, N_SLAB * ns), F32)
    if has_state:
        assert rows_per_seq == SEG, "carried-state path scans one segment per sequence"
        rows, total, grid = n_rows, n_rows, (N_SLAB,)
        ph_spec = pl.BlockSpec((N_PHASE, rows, LANE), lambda q: (0, 0, q))
        slab = lambda q: (q, 0, 0)
        state_spec = pl.BlockSpec((n_seq, ns), lambda q: (0, q))
        body = functools.partial(_ssm_sample_body, rows=rows)
        u_in, y_dims = u8, (N_PHASE, n_rows, d)
    else:
        assert n_seq <= 8, "sequences ride the sublanes of the segment chain"
        rows = min(SSM_ROWS, rows_per_seq)
        assert rows_per_seq % rows == 0
        total = n_seq * rows
        grid = (N_SLAB, rows_per_seq // rows)
        ph_spec = pl.BlockSpec((N_PHASE, n_seq, rows, LANE), lambda q, i: (0, 0, i, q))
        slab = lambda q, i: (q, 0, 0)
        state_spec = pl.BlockSpec((n_seq, ns), lambda q, i: (0, q))
        body = functools.partial(_ssm_prompt_body, n_seq=n_seq, rows=rows)
        y_dims = (N_PHASE, n_seq, rows_per_seq, d)
        u_in = u8.reshape(y_dims)
    assert rows % GROUP_ROWS == 0
    w_spec = pl.BlockSpec((None, 2 * ns, 2 * ns), slab)
    in_specs = [ph_spec, w_spec, w_spec, w_spec,
                pl.BlockSpec((None, 8, 2 * ns), slab), pl.BlockSpec((None, 1, d), slab)]
    args = [u_in, wb, kt, wc, pw, d8]
    scratch = [pltpu.VMEM((2 * HALF, total, LANE), F32),
               pltpu.VMEM((2 * HALF, total, LANE), F32)]
    if has_state:
        in_specs += [state_spec, state_spec]
        args += [h0[0].reshape(n_seq, -1), h0[1].reshape(n_seq, -1)]
    else:
        seg_rows = 8 * (rows // SEG)
        scratch += [pltpu.VMEM((2 * HALF, seg_rows, LANE), F32),
                    pltpu.VMEM((2 * HALF, seg_rows, LANE), F32),
                    pltpu.VMEM((2 * HALF, 8, LANE), F32)]
    y8, s_re, s_im = pl.pallas_call(
        body,
        grid=grid,
        in_specs=in_specs,
        out_specs=[ph_spec, state_spec, state_spec],
        out_shape=[jax.ShapeDtypeStruct(y_dims, BF16), state_shape, state_shape],
        scratch_shapes=scratch,
        name="ssm_sample" if has_state else "ssm_prompt",
        compiler_params=_params(("arbitrary",) * len(grid)),
    )(*args)
    return y8.reshape(N_PHASE, n_rows, d), s_re, s_im


def _mix_out_ffn_body(y_ref, a_ref, gs_ref, x_ref, wglu_ref, wo_ref, g_ref, wg_ref, wu_ref, wd_ref, gf_ref,
                      o_ref, slab_ref, *, rows, final_norm):
    d = D_MODEL
    glu = _dot(jax.nn.gelu(_stack_phases(y_ref).astype(F32)).astype(BF16), wglu_ref[...])
    y_ssm = glu[:, :d] * jax.nn.sigmoid(glu[:, d:])
    merged = _stack_phases(a_ref).astype(F32) + _stack_phases(gs_ref).astype(F32) * y_ssm
    x2 = _stack_phases(x_ref) + _dot(merged.astype(BF16), wo_ref[...])
    y = _ffn_math(x2, g_ref, wg_ref, wu_ref, wd_ref)
    if final_norm:
        y = _rms(y, gf_ref[...])
    _store_token_major(y, o_ref, slab_ref, rows)


def _mix_out_ffn(y8, a8, gs8, xp, w_glu, w_o, g, wg, wu, wd, gf, final_norm):
    n_rows = xp.shape[1]
    d = D_MODEL
    rows = min(MIX_ROWS, n_rows)
    assert n_rows % rows == 0
    ph_spec = pl.BlockSpec((N_PHASE, rows, d), lambda i: (0, i, 0))
    weights = (w_glu, w_o, g, wg, wu, wd, gf)
    return pl.pallas_call(
        functools.partial(_mix_out_ffn_body, rows=rows, final_norm=final_norm),
        grid=(n_rows // rows,),
        in_specs=[ph_spec, ph_spec, ph_spec, ph_spec] + [_const_spec(w.shape) for w in weights],
        out_specs=pl.BlockSpec((N_PHASE * rows, d), lambda i: (i, 0)),
        out_shape=jax.ShapeDtypeStruct((N_PHASE * n_rows, d), F32),
        scratch_shapes=[pltpu.VMEM((d // LANE, N_PHASE * rows, LANE), F32)],
        name="mix_out_ffn",
        compiler_params=_params(("arbitrary",)),
    )(y8, a8, gs8, xp, *weights)


def _after_ffn1(xp, n_seq, conv_prev, h0, w, norm_final, last):
    u8, a8, gs8, conv_state = _mix_in(xp, conv_prev, w["norm_mix"], w["w_in"], w["w_conv"], w["w_conv_out"], n_seq)
    y8, s_re, s_im = _ssm(u8, h0, *w["ssm"], n_seq)
    xt = _mix_out_ffn(y8, a8, gs8, xp, w["w_glu"], w["w_o"], w["norm_ffn2"], w["ffn2_gate"], w["ffn2_up"],
                      w["ffn2_down"], norm_final, last)
    return xt, conv_state, s_re.reshape(n_seq, -1, STATE_DIM), s_im.reshape(n_seq, -1, STATE_DIM)


def kernel(x_prompt, x_sample, state_conv, state_ssm_re, state_ssm_im, norm_ffn1, w_ffn1_gate, w_ffn1_up, w_ffn1_down, norm_mix, w_in, w_conv, w_conv_out, ssm_lambda_re, ssm_lambda_im, ssm_log_step, ssm_b_re, ssm_b_im, ssm_c_re, ssm_c_im, ssm_d, w_glu, w_o, norm_ffn2, w_ffn2_gate, w_ffn2_up, w_ffn2_down, norm_final):
    depth, d = w_in.shape[0], w_in.shape[1]
    row = lambda v: v.reshape(1, -1)
    nf = row(norm_final)
    xt_p = x_prompt.reshape(-1, d)
    xt_s = x_sample.reshape(-1, d)
    n_p, n_s = x_prompt.shape[0], x_sample.shape[0]
    outs_p, outs_s = [], []
    for l in range(depth):
        last = l == depth - 1
        ssm_w, (f1g, f1u, f1d) = _ssm_prep(
            ssm_lambda_re[l], ssm_lambda_im[l], ssm_log_step[l], ssm_b_re[l], ssm_b_im[l], ssm_c_re[l],
            ssm_c_im[l], riders=(w_ffn1_gate[l], w_ffn1_up[l], w_ffn1_down[l]))
        ffn1 = (row(norm_ffn1[l]), f1g, f1u, f1d)
        xp_p, (win, wco, wgl, wo, f2g, f2u, f2d) = _ffn_to_phase(
            xt_p, *ffn1, riders=(w_in[l], w_conv_out[l], w_glu[l], w_o[l],
                                 w_ffn2_gate[l], w_ffn2_up[l], w_ffn2_down[l]))
        xp_s, _ = _ffn_to_phase(xt_s, *ffn1)
        d8 = jnp.tile(ssm_d[l].reshape(N_SLAB, 1, SLAB), (1, 1, N_PHASE))
        w = dict(norm_mix=row(norm_mix[l]), w_in=win, w_conv=w_conv[l], w_conv_out=wco, ssm=(*ssm_w, d8),
                 w_glu=wgl, w_o=wo, norm_ffn2=row(norm_ffn2[l]), ffn2_gate=f2g, ffn2_up=f2u, ffn2_down=f2d)
        xt_p, *rest_p = _after_ffn1(xp_p, n_p, None, None, w, nf, last)
        xt_s, *rest_s = _after_ffn1(xp_s, n_s, state_conv[l], (state_ssm_re[l], state_ssm_im[l]), w, nf, last)
        outs_p.append(rest_p)
        outs_s.append(rest_s)
    stack = lambda outs: tuple(jnp.stack(leaf) for leaf in zip(*outs))
    return (xt_p.reshape(x_prompt.shape), xt_s.reshape(x_sample.shape), *stack(outs_p), *stack(outs_s))
```

```python
import functools

import jax
import jax.numpy as jnp
from jax import lax
from jax.experimental import pallas as pl
from jax.experimental.pallas import tpu as pltpu

F32 = jnp.float32
BF16 = jnp.bfloat16

D_MODEL = 1024
N_PHASE = 8
LANE = 128
MXU_TILE = 256
SLAB = 128
N_SLAB = D_MODEL // SLAB
GROUPS_PER_SLAB = 8
STATE_DIM = 64
SLAB_STATES = GROUPS_PER_SLAB * STATE_DIM
SEG = 4
RMS_EPS = 1e-6
VMEM_LIMIT = 60 * 1024 * 1024

FFN_ROWS = 1024
MIX_IN_ROWS = 128
MIX_OUT_ROWS = 64
SSM_ROWS = 256


def _rms(x, g):
    return x * lax.rsqrt(jnp.mean(x * x, axis=-1, keepdims=True) + RMS_EPS) * g


def _dot(a, b):
    return jnp.dot(a, b, preferred_element_type=F32)


def _const_spec(shape):
    zeros = (0,) * len(shape)
    return pl.BlockSpec(shape, lambda *_: zeros, pipeline_mode=pl.Buffered(1))


def _params(semantics):
    return pltpu.CompilerParams(dimension_semantics=semantics, vmem_limit_bytes=VMEM_LIMIT)


def _stack_phases(ref):
    return jnp.concatenate([ref[t] for t in range(N_PHASE)], axis=0)


def _ffn_math(x, g_ref, wg_ref, wu_ref, wd_ref):
    h = _rms(x, g_ref[...]).astype(BF16)
    act = (jax.nn.silu(_dot(h, wg_ref[...])) * _dot(h, wu_ref[...])).astype(BF16)
    return x + 0.5 * _dot(act, wd_ref[...])


def _rider_specs(weights, n_steps, step=lambda i: i):
    in_specs, out_specs, out_shapes = [], [], []
    for w in weights:
        n_blocks = n_steps
        while w.shape[0] % (16 * n_blocks):
            assert n_blocks % 2 == 0
            n_blocks //= 2
        spec = pl.BlockSpec((w.shape[0] // n_blocks, w.shape[1]),
                            lambda *idx, rep=n_steps // n_blocks: (step(*idx) // rep, 0))
        in_specs.append(spec)
        out_specs.append(spec)
        out_shapes.append(jax.ShapeDtypeStruct(w.shape, BF16))
    return in_specs, out_specs, out_shapes


def _convert_riders(in_refs, out_refs):
    for w_ref, o_ref in zip(in_refs, out_refs):
        o_ref[...] = w_ref[...].astype(BF16)


def _ffn_to_phase_body(*refs, rows, n_riders):
    x_ref, g_ref, wg_ref, wu_ref, wd_ref = refs[:5]
    o_ref, slab_ref = refs[5 + n_riders], refs[-1]
    _convert_riders(refs[5:5 + n_riders], refs[6 + n_riders:6 + 2 * n_riders])
    y = _ffn_math(x_ref[...], g_ref, wg_ref, wu_ref, wd_ref)
    for lt in range(D_MODEL // LANE):
        slab_ref[lt] = y[:, lt * LANE:(lt + 1) * LANE]
        for t in range(N_PHASE):
            o_ref[t, :, lt * LANE:(lt + 1) * LANE] = slab_ref[lt, pl.ds(t, rows, stride=N_PHASE), :]


def _store_token_major(y, o_ref, slab_ref, rows):
    for lt in range(D_MODEL // LANE):
        for t in range(N_PHASE):
            slab_ref[lt, pl.ds(t, rows, stride=N_PHASE), :] = y[t * rows:(t + 1) * rows, lt * LANE:(lt + 1) * LANE]
        o_ref[:, lt * LANE:(lt + 1) * LANE] = slab_ref[lt]


def _ffn_to_phase(x, g, wg, wu, wd, riders=()):
    n_tok, d = x.shape
    n_rows = n_tok // N_PHASE
    d_ff = wg.shape[1]
    rows = FFN_ROWS // N_PHASE
    assert n_rows % rows == 0
    n_steps = n_rows // rows
    rider_in, rider_out, rider_shapes = _rider_specs(riders, n_steps)
    out = pl.pallas_call(
        functools.partial(_ffn_to_phase_body, rows=rows, n_riders=len(riders)),
        grid=(n_steps,),
        in_specs=[
            pl.BlockSpec((N_PHASE * rows, d), lambda i: (i, 0)),
            _const_spec((1, d)),
            _const_spec((d, d_ff)),
            _const_spec((d, d_ff)),
            _const_spec((d_ff, d)),
        ] + rider_in,
        out_specs=[pl.BlockSpec((N_PHASE, rows, d), lambda i: (0, i, 0))] + rider_out,
        out_shape=[jax.ShapeDtypeStruct((N_PHASE, n_rows, d), F32)] + rider_shapes,
        scratch_shapes=[pltpu.VMEM((d // LANE, N_PHASE * rows, LANE), F32)],
        name="ffn_to_phase",
        compiler_params=_params(("arbitrary",)),
    )(x, g, wg, wu, wd, *riders)
    return out[0], tuple(out[1:])


def _mix_in_compute(x_ref, g, win_ref, wcv, wco_ref, shift, rows):
    d = D_MODEL
    h = _rms(_stack_phases(x_ref), g).astype(BF16)

    def proj(k):
        return _dot(h, win_ref[:, k * d:(k + 1) * d])

    z = proj(1) * proj(0)
    z6 = z[6 * rows:7 * rows]
    z7 = z[7 * rows:8 * rows]
    s6 = shift(z6, 0)
    s7 = shift(z7, 1)
    z1 = jnp.concatenate([s7, z[:7 * rows]], axis=0)
    z2 = jnp.concatenate([s6, s7, z[:6 * rows]], axis=0)
    conv = wcv[2:3] * z + wcv[1:2] * z1 + wcv[0:1] * z2
    y_conv = _dot((proj(2) * conv).astype(BF16), wco_ref[...])
    u = proj(3)
    a = jax.nn.sigmoid(proj(4)) * y_conv
    gs = jax.nn.sigmoid(proj(5))
    return u, a, gs, z6, z7


def _store_phases(ref, val, rows):
    val = val.astype(ref.dtype)
    for t in range(N_PHASE):
        ref[t] = val[t * rows:(t + 1) * rows]


def _mix_in_prompt_body(*refs, rows, n_riders):
    x_ref, g_ref, win_ref, wcv_ref, wco_ref = refs[:5]
    u_ref, a_ref, gs_ref, z6_ref, z7_ref = refs[5 + n_riders:10 + n_riders]
    carry_ref = refs[-1]
    _convert_riders(refs[5:5 + n_riders], refs[10 + n_riders:10 + 2 * n_riders])

    @pl.when(pl.program_id(1) == 0)
    def _():
        carry_ref[...] = jnp.zeros_like(carry_ref)

    row_id = lax.broadcasted_iota(jnp.int32, (rows, D_MODEL), 0)

    def shift(z, k):
        return jnp.where(row_id == 0, carry_ref[k:k + 1, :], pltpu.roll(z, 1, 0))

    u, a, gs, z6, z7 = _mix_in_compute(x_ref, g_ref[...], win_ref, wcv_ref[...], wco_ref, shift, rows)
    last6 = z6[rows - 1:rows]
    last7 = z7[rows - 1:rows]
    carry_ref[0:1, :] = last6
    carry_ref[1:2, :] = last7
    z6_ref[...] = last6
    z7_ref[...] = last7
    _store_phases(u_ref, u, rows)
    _store_phases(a_ref, a, rows)
    _store_phases(gs_ref, gs, rows)


def _mix_in_sample_body(x_ref, s6_ref, s7_ref, g_ref, win_ref, wcv_ref, wco_ref,
                        u_ref, a_ref, gs_ref, z6_ref, z7_ref, slab_ref, *, rows, rows_per_seq):
    row_id = lax.broadcasted_iota(jnp.int32, (rows, D_MODEL), 0)
    starts = (s6_ref, s7_ref)

    def shift(z, k):
        return jnp.where(row_id % rows_per_seq == 0, starts[k][...], pltpu.roll(z, 1, 0))

    u, a, gs, z6, z7 = _mix_in_compute(x_ref, g_ref[...], win_ref, wcv_ref[...], wco_ref, shift, rows)
    for z, z_ref in ((z6, z6_ref), (z7, z7_ref)):
        for lt in range(D_MODEL // LANE):
            slab_ref[...] = z[:, lt * LANE:(lt + 1) * LANE]
            z_ref[:, lt * LANE:(lt + 1) * LANE] = slab_ref[pl.ds(rows_per_seq - 1, rows // rows_per_seq,
                                                                stride=rows_per_seq), :]
    _store_phases(u_ref, u, rows)
    _store_phases(a_ref, a, rows)
    _store_phases(gs_ref, gs, rows)


def _mix_in(xp, conv_prev, g, w_in, w_conv, w_conv_out, n_seq, riders=()):
    n_rows = xp.shape[1]
    d = D_MODEL
    rows_per_seq = n_rows // n_seq
    ph_shape = jax.ShapeDtypeStruct((N_PHASE, n_rows, d), BF16)
    weights = (g, w_in, w_conv, w_conv_out)
    weight_specs = [_const_spec(w.shape) for w in weights]
    if conv_prev is None:
        rows = min(MIX_IN_ROWS, rows_per_seq)
        assert rows_per_seq % rows == 0
        tiles = rows_per_seq // rows
        ph_spec = pl.BlockSpec((N_PHASE, rows, d), lambda b, i: (0, b * tiles + i, 0))
        last_spec = pl.BlockSpec((None, 1, d), lambda b, i: (b, 0, 0))
        rider_in, rider_out, rider_shapes = _rider_specs(riders, n_seq * tiles, lambda b, i: b * tiles + i)
        u8, a8, gs8, z6, z7, *converted = pl.pallas_call(
            functools.partial(_mix_in_prompt_body, rows=rows, n_riders=len(riders)),
            grid=(n_seq, tiles),
            in_specs=[ph_spec] + weight_specs + rider_in,
            out_specs=[ph_spec, ph_spec, ph_spec, last_spec, last_spec] + rider_out,
            out_shape=[ph_shape, ph_shape, ph_shape, jax.ShapeDtypeStruct((n_seq, 1, d), F32),
                       jax.ShapeDtypeStruct((n_seq, 1, d), F32)] + rider_shapes,
            scratch_shapes=[pltpu.VMEM((8, d), F32)],
            name="mix_in_prompt",
            compiler_params=_params(("arbitrary", "arbitrary")),
        )(xp, *weights, *riders)
        z6, z7 = z6[:, 0], z7[:, 0]
    else:
        assert not riders
        converted = []
        rows = n_rows
        start = jnp.zeros((n_seq, rows_per_seq, 2, d), F32).at[:, 0].set(conv_prev).reshape(n_rows, 2, d)
        full = lambda shape: pl.BlockSpec(shape, lambda i: (0,) * len(shape))
        u8, a8, gs8, z6, z7 = pl.pallas_call(
            functools.partial(_mix_in_sample_body, rows=rows, rows_per_seq=rows_per_seq),
            grid=(1,),
            in_specs=[full((N_PHASE, rows, d)), full((rows, d)), full((rows, d))] + weight_specs,
            out_specs=[full((N_PHASE, rows, d))] * 3 + [full((n_seq, d))] * 2,
            out_shape=[ph_shape, ph_shape, ph_shape,
                       jax.ShapeDtypeStruct((n_seq, d), F32), jax.ShapeDtypeStruct((n_seq, d), F32)],
            scratch_shapes=[pltpu.VMEM((rows, LANE), F32)],
            name="mix_in_sample",
            compiler_params=_params(("arbitrary",)),
        )(xp, start[:, 0], start[:, 1], *weights)
    return u8, a8, gs8, jnp.stack([z6, z7], axis=1), tuple(converted)


def _split_bf16(a):
    hi = a.astype(BF16)
    return hi, (a - hi.astype(F32)).astype(BF16)


def _ssm_prep_body(*refs, n_riders):
    bbre_ref, bbim_ref, ctre_ref, ctim_ref, lre_ref, lim_ref, lstep_ref = refs[:7]
    wb_ref, kt_ref, wc_ref, pw_ref = refs[7 + n_riders:11 + n_riders]
    _convert_riders(refs[7:7 + n_riders], refs[11 + n_riders:])
    ns = SLAB_STATES
    row_group = lax.broadcasted_iota(jnp.int32, (SLAB, LANE), 0) // (SLAB // GROUPS_PER_SLAB)
    lane_half = lax.broadcasted_iota(jnp.int32, (SLAB, LANE), 1) // STATE_DIM

    def block_diag(ref):
        tiles = [jnp.where(row_group == 2 * k + lane_half, ref[...], 0.0) for k in range(GROUPS_PER_SLAB // 2)]
        return jnp.concatenate(tiles, axis=1)

    bbre, bbim = block_diag(bbre_ref), block_diag(bbim_ref)
    ctre, ctim = block_diag(ctre_ref), block_diag(ctim_ref)
    lre, lim = lre_ref[...], lim_ref[...]
    step = jnp.exp(lstep_ref[...])

    def lam_pow(n):
        mag = jnp.exp((n * lre) * step)
        ang = (n * lim) * step
        return mag * jnp.cos(ang), mag * jnp.sin(ang)

    l1re, l1im = lam_pow(1)
    den = lre * lre + lim * lim
    fre = ((l1re - 1.0) * lre + l1im * lim) / den
    fim = (l1im * lre - (l1re - 1.0) * lim) / den

    def dot_nt(a, b_split):
        nt = lambda p, q: lax.dot_general(p, q, (((1,), (1,)), ((), ())), preferred_element_type=F32)
        a_hi, a_lo = _split_bf16(a)
        b_hi, b_lo = b_split
        return nt(a_hi, b_hi) + nt(a_hi, b_lo) + nt(a_lo, b_hi)

    ctre_split = _split_bf16(ctre)
    ctim_split = _split_bf16(ctim)
    kt_ref[...] = jnp.zeros_like(kt_ref)
    for k in range(N_PHASE):
        pre, pim = lam_pow(k)
        gre = fre * pre - fim * pim
        gim = fre * pim + fim * pre
        are = bbre * gre - bbim * gim
        aim = bbre * gim + bbim * gre
        j = N_PHASE - 1 - k
        wb_ref[j * SLAB:(j + 1) * SLAB, 0:ns] = are.astype(BF16)
        wb_ref[j * SLAB:(j + 1) * SLAB, ns:2 * ns] = aim.astype(BF16)
        kk = (dot_nt(are, ctre_split) - dot_nt(aim, ctim_split)).astype(BF16)
        for jj in range(N_PHASE - k):
            t = jj + k
            kt_ref[jj * SLAB:(jj + 1) * SLAB, t * SLAB:(t + 1) * SLAB] = kk
    for t in range(N_PHASE):
        pre, pim = lam_pow(t + 1)
        cre = ctre * pre - ctim * pim
        cim = ctre * pim + ctim * pre
        wc_ref[0:ns, t * SLAB:(t + 1) * SLAB] = cre.T.astype(BF16)
        wc_ref[ns:2 * ns, t * SLAB:(t + 1) * SLAB] = (-cim).T.astype(BF16)
    pw_ref[...] = jnp.zeros_like(pw_ref)
    for r, n in enumerate((N_PHASE, N_PHASE * SEG)):
        pre, pim = lam_pow(n)
        pw_ref[r:r + 1, 0:ns] = pre
        pw_ref[r:r + 1, ns:2 * ns] = pim


def _ssm_prep(lam_re, lam_im, log_step, b_re, b_im, c_re, c_im, riders=()):
    ns = SLAB_STATES

    def twice(m):
        m = m.reshape(N_SLAB, SLAB, STATE_DIM)
        return jnp.concatenate([m, m], axis=-1)

    def vec(v):
        return v.reshape(N_SLAB, 1, ns)

    mats = [twice(jnp.transpose(b_re, (0, 2, 1))), twice(jnp.transpose(b_im, (0, 2, 1))), twice(c_re), twice(c_im)]
    vecs = [vec(lam_re), vec(lam_im), vec(jnp.broadcast_to(log_step[:, None], lam_re.shape))]
    w_shape = jax.ShapeDtypeStruct((N_SLAB, 2 * ns, 2 * ns), BF16)
    w_spec = pl.BlockSpec((None, 2 * ns, 2 * ns), lambda q: (q, 0, 0))
    rider_in, rider_out, rider_shapes = _rider_specs(riders, N_SLAB)
    out = pl.pallas_call(
        functools.partial(_ssm_prep_body, n_riders=len(riders)),
        grid=(N_SLAB,),
        in_specs=[pl.BlockSpec((None, SLAB, 2 * STATE_DIM), lambda q: (q, 0, 0))] * 4
        + [pl.BlockSpec((None, 1, ns), lambda q: (q, 0, 0))] * 3 + rider_in,
        out_specs=[w_spec, w_spec, w_spec, pl.BlockSpec((None, 8, 2 * ns), lambda q: (q, 0, 0))] + rider_out,
        out_shape=[w_shape, w_shape, w_shape, jax.ShapeDtypeStruct((N_SLAB, 8, 2 * ns), F32)] + rider_shapes,
        name="ssm_prep",
        compiler_params=_params(("arbitrary",)),
    )(*mats, *vecs, *riders)
    return tuple(out[:4]), tuple(out[4:])


def _cmul(are, aim, bre, bim):
    return are * bre - aim * bim, are * bim + aim * bre


HALF = SLAB_STATES // LANE
GROUP_ROWS = 8 * SEG


def _mult(pw_ref, r, c):
    return (pw_ref[r:r + 1, c * LANE:(c + 1) * LANE],
            pw_ref[r:r + 1, (HALF + c) * LANE:(HALF + c + 1) * LANE])


def _state_increments(lhs, wb_ref, e_ref):
    e = _dot(lhs, wb_ref[...])
    for lt in range(2 * HALF):
        e_ref[lt] = e[:, lt * LANE:(lt + 1) * LANE]


def _scan_local(e_ref, p_ref, pw_ref, c, base):
    l8re, l8im = _mult(pw_ref, 0, c)
    lre = jnp.zeros((8, LANE), F32)
    lim = jnp.zeros((8, LANE), F32)
    for i in range(SEG):
        idx = pl.ds(base + i, 8, stride=SEG)
        p_ref[c, idx, :] = lre
        p_ref[HALF + c, idx, :] = lim
        mre, mim = _cmul(l8re, l8im, lre, lim)
        lre = mre + e_ref[c, idx, :]
        lim = mim + e_ref[HALF + c, idx, :]
    return lre, lim


def _scan_fixup(p_ref, pw_ref, c, base, cre, cim):
    l8re, l8im = _mult(pw_ref, 0, c)
    for i in range(SEG):
        idx = pl.ds(base + i, 8, stride=SEG)
        p_ref[c, idx, :] = p_ref[c, idx, :] + cre
        p_ref[HALF + c, idx, :] = p_ref[HALF + c, idx, :] + cim
        cre, cim = _cmul(l8re, l8im, cre, cim)


def _readout(lhs, p_ref, kt_ref, wc_ref, d_ref):
    prev = jnp.concatenate([p_ref[lt] for lt in range(2 * HALF)], axis=1)
    toep = jnp.concatenate(
        [_dot(lhs[:, :(k + 1) * MXU_TILE], kt_ref[:(k + 1) * MXU_TILE, k * MXU_TILE:(k + 1) * MXU_TILE])
         for k in range(N_PHASE * LANE // MXU_TILE)], axis=1)
    y = toep + _dot(prev.astype(BF16), wc_ref[...]) + d_ref[...] * lhs.astype(F32)
    return y.astype(BF16)


def _ssm_prompt_body(u_ref, wb_ref, kt_ref, wc_ref, pw_ref, d_ref, y_ref, sre_ref, sim_ref,
                     e_ref, p_ref, end_ref, cin_ref, carry_ref, *, n_seq, rows):
    total = n_seq * rows
    lhs = jnp.concatenate([u_ref[t].reshape(total, LANE) for t in range(N_PHASE)], axis=1)
    _state_increments(lhs, wb_ref, e_ref)

    @pl.when((pl.program_id(0) == 0) & (pl.program_id(1) == 0))
    def _():
        end_ref[...] = jnp.zeros_like(end_ref)

    @pl.when(pl.program_id(1) == 0)
    def _():
        carry_ref[...] = jnp.zeros_like(carry_ref)

    groups = rows // GROUP_ROWS
    for c in range(HALF):
        for b in range(n_seq):
            for j in range(groups):
                lre, lim = _scan_local(e_ref, p_ref, pw_ref, c, b * rows + j * GROUP_ROWS)
                idx = pl.ds(j * 64 + b, 8, stride=8)
                end_ref[c, idx, :] = lre
                end_ref[HALF + c, idx, :] = lim
    for c in range(HALF):
        lsre, lsim = _mult(pw_ref, 1, c)
        cre = carry_ref[c]
        cim = carry_ref[HALF + c]
        for s in range(rows // SEG):
            cin_ref[c, 8 * s:8 * s + 8, :] = cre
            cin_ref[HALF + c, 8 * s:8 * s + 8, :] = cim
            mre, mim = _cmul(lsre, lsim, cre, cim)
            cre = mre + end_ref[c, 8 * s:8 * s + 8, :]
            cim = mim + end_ref[HALF + c, 8 * s:8 * s + 8, :]
        carry_ref[c] = cre
        carry_ref[HALF + c] = cim
        sre_ref[:, c * LANE:(c + 1) * LANE] = cre[:n_seq]
        sim_ref[:, c * LANE:(c + 1) * LANE] = cim[:n_seq]
    for c in range(HALF):
        for b in range(n_seq):
            for j in range(groups):
                idx = pl.ds(j * 64 + b, 8, stride=8)
                _scan_fixup(p_ref, pw_ref, c, b * rows + j * GROUP_ROWS,
                            cin_ref[c, idx, :], cin_ref[HALF + c, idx, :])
    y = _readout(lhs, p_ref, kt_ref, wc_ref, d_ref)
    for t in range(N_PHASE):
        y_ref[t] = y[:, t * LANE:(t + 1) * LANE].reshape(n_seq, rows, LANE)


def _ssm_sample_body(u_ref, wb_ref, kt_ref, wc_ref, pw_ref, d_ref, h0re_ref, h0im_ref,
                     y_ref, sre_ref, sim_ref, e_ref, p_ref, *, rows):
    lhs = jnp.concatenate([u_ref[t] for t in range(N_PHASE)], axis=1)
    _state_increments(lhs, wb_ref, e_ref)
    for c in range(HALF):
        lsre, lsim = _mult(pw_ref, 1, c)
        for j in range(rows // GROUP_ROWS):
            lre, lim = _scan_local(e_ref, p_ref, pw_ref, c, j * GROUP_ROWS)
            cre = h0re_ref[8 * j:8 * j + 8, c * LANE:(c + 1) * LANE]
            cim = h0im_ref[8 * j:8 * j + 8, c * LANE:(c + 1) * LANE]
            mre, mim = _cmul(lsre, lsim, cre, cim)
            sre_ref[8 * j:8 * j + 8, c * LANE:(c + 1) * LANE] = mre + lre
            sim_ref[8 * j:8 * j + 8, c * LANE:(c + 1) * LANE] = mim + lim
            _scan_fixup(p_ref, pw_ref, c, j * GROUP_ROWS, cre, cim)
    y = _readout(lhs, p_ref, kt_ref, wc_ref, d_ref)
    for t in range(N_PHASE):
        y_ref[t] = y[:, t * LANE:(t + 1) * LANE]


def _ssm(u8, h0, wb, kt, wc, pw, d8, n_seq):
    n_rows = u8.shape[1]
    d = D_MODEL
    ns = SLAB_STATES
    rows_per_seq = n_rows // n_seq
    has_state = h0 is not None
    state_shape = jax.ShapeDtypeStruct((n_seq, N_SLAB * ns), F32)
    if has_state:
        assert rows_per_seq == SEG, "carried-state path scans one segment per sequence"
        rows, grid = n_rows, (N_SLAB,)
        ph_spec = pl.BlockSpec((N_PHASE, rows, LANE), lambda q: (0, 0, q))
        slab = lambda q: (q, 0, 0)
        state_spec = pl.BlockSpec((n_seq, ns), lambda q: (0, q))
        body = functools.partial(_ssm_sample_body, rows=rows)
        u_in, y_dims = u8, (N_PHASE, n_rows, d)
        scratch = [pltpu.VMEM((2 * HALF, rows, LANE), F32),
                   pltpu.VMEM((2 * HALF, rows, LANE), F32)]
    else:
        assert n_seq <= 8, "sequences ride the sublanes of the segment chain"
        rows = min(SSM_ROWS, rows_per_seq)
        assert rows_per_seq % rows == 0
        grid = (N_SLAB, rows_per_seq // rows)
        ph_spec = pl.BlockSpec((N_PHASE, n_seq, rows, LANE), lambda q, i: (0, 0, i, q))
        slab = lambda q, i: (q, 0, 0)
        state_spec = pl.BlockSpec((n_seq, ns), lambda q, i: (0, q))
        body = functools.partial(_ssm_prompt_body, n_seq=n_seq, rows=rows)
        y_dims = (N_PHASE, n_seq, rows_per_seq, d)
        u_in = u8.reshape(y_dims)
        seg_rows = 8 * (rows // SEG)
        scratch = [pltpu.VMEM((2 * HALF, n_seq * rows, LANE), F32),
                   pltpu.VMEM((2 * HALF, n_seq * rows, LANE), F32),
                   pltpu.VMEM((2 * HALF, seg_rows, LANE), F32),
                   pltpu.VMEM((2 * HALF, seg_rows, LANE), F32),
                   pltpu.VMEM((2 * HALF, 8, LANE), F32)]
    assert rows % GROUP_ROWS == 0
    w_spec = pl.BlockSpec((None, 2 * ns, 2 * ns), slab)
    in_specs = [ph_spec, w_spec, w_spec, w_spec,
                pl.BlockSpec((None, 8, 2 * ns), slab), pl.BlockSpec((None, 1, d), slab)]
    args = [u_in, wb, kt, wc, pw, d8]
    if has_state:
        in_specs += [state_spec, state_spec]
        args += [h0[0].reshape(n_seq, -1), h0[1].reshape(n_seq, -1)]
    y8, s_re, s_im = pl.pallas_call(
        body,
        grid=grid,
        in_specs=in_specs,
        out_specs=[ph_spec, state_spec, state_spec],
        out_shape=[jax.ShapeDtypeStruct(y_dims, BF16), state_shape, state_shape],
        scratch_shapes=scratch,
        name="ssm_sample" if has_state else "ssm_prompt",
        compiler_params=_params(("arbitrary",) * len(grid)),
    )(*args)
    return y8.reshape(N_PHASE, n_rows, d), s_re, s_im


def _mix_out_ffn_body(y_ref, a_ref, gs_ref, x_ref, wglu_ref, wo_ref, g_ref, wg_ref, wu_ref, wd_ref, gf_ref,
                      o_ref, slab_ref, *, rows, final_norm):
    d = D_MODEL
    glu = _dot(jax.nn.gelu(_stack_phases(y_ref).astype(F32)).astype(BF16), wglu_ref[...])
    y_ssm = glu[:, :d] * jax.nn.sigmoid(glu[:, d:])
    merged = _stack_phases(a_ref).astype(F32) + _stack_phases(gs_ref).astype(F32) * y_ssm
    x2 = _stack_phases(x_ref) + _dot(merged.astype(BF16), wo_ref[...])
    y = _ffn_math(x2, g_ref, wg_ref, wu_ref, wd_ref)
    if final_norm:
        y = _rms(y, gf_ref[...])
    _store_token_major(y, o_ref, slab_ref, rows)


def _mix_out_ffn(y8, a8, gs8, xp, w_glu, w_o, g, wg, wu, wd, gf, final_norm):
    n_rows = xp.shape[1]
    d = D_MODEL
    rows = min(MIX_OUT_ROWS, n_rows)
    assert n_rows % rows == 0
    ph_spec = pl.BlockSpec((N_PHASE, rows, d), lambda i: (0, i, 0))
    weights = (w_glu, w_o, g, wg, wu, wd, gf)
    return pl.pallas_call(
        functools.partial(_mix_out_ffn_body, rows=rows, final_norm=final_norm),
        grid=(n_rows // rows,),
        in_specs=[ph_spec, ph_spec, ph_spec, ph_spec] + [_const_spec(w.shape) for w in weights],
        out_specs=pl.BlockSpec((N_PHASE * rows, d), lambda i: (i, 0)),
        out_shape=jax.ShapeDtypeStruct((N_PHASE * n_rows, d), F32),
        scratch_shapes=[pltpu.VMEM((d // LANE, N_PHASE * rows, LANE), F32)],
        name="mix_out_ffn",
        compiler_params=_params(("arbitrary",)),
    )(y8, a8, gs8, xp, *weights)


def _mixer_branches(xp, n_seq, conv_prev, h0, w, riders=()):
    u8, a8, gs8, conv_state, converted = _mix_in(xp, conv_prev, w["norm_mix"], w["w_in"], w["w_conv"],
                                                 w["w_conv_out"], n_seq, riders)
    y8, s_re, s_im = _ssm(u8, h0, *w["ssm"], n_seq)
    states = [conv_state, s_re.reshape(n_seq, -1, STATE_DIM), s_im.reshape(n_seq, -1, STATE_DIM)]
    return (y8, a8, gs8, xp), states, converted


def kernel(x_prompt, x_sample, state_conv, state_ssm_re, state_ssm_im, norm_ffn1, w_ffn1_gate, w_ffn1_up, w_ffn1_down, norm_mix, w_in, w_conv, w_conv_out, ssm_lambda_re, ssm_lambda_im, ssm_log_step, ssm_b_re, ssm_b_im, ssm_c_re, ssm_c_im, ssm_d, w_glu, w_o, norm_ffn2, w_ffn2_gate, w_ffn2_up, w_ffn2_down, norm_final):
    depth, d = w_in.shape[0], w_in.shape[1]
    row = lambda v: v.reshape(1, -1)
    nf = row(norm_final)
    xt_p = x_prompt.reshape(-1, d)
    xt_s = x_sample.reshape(-1, d)
    n_p, n_s = x_prompt.shape[0], x_sample.shape[0]
    outs_p, outs_s = [], []
    for l in range(depth):
        last = l == depth - 1
        ssm_w, (f1g, f1u, f1d) = _ssm_prep(
            ssm_lambda_re[l], ssm_lambda_im[l], ssm_log_step[l], ssm_b_re[l], ssm_b_im[l], ssm_c_re[l],
            ssm_c_im[l], riders=(w_ffn1_gate[l], w_ffn1_up[l], w_ffn1_down[l]))
        ffn1 = (row(norm_ffn1[l]), f1g, f1u, f1d)
        xp_p, (win, wco, wgl, wo) = _ffn_to_phase(xt_p, *ffn1, riders=(w_in[l], w_conv_out[l], w_glu[l], w_o[l]))
        xp_s, _ = _ffn_to_phase(xt_s, *ffn1)
        d8 = jnp.tile(ssm_d[l].reshape(N_SLAB, 1, SLAB), (1, 1, N_PHASE))
        w = dict(norm_mix=row(norm_mix[l]), w_in=win, w_conv=w_conv[l], w_conv_out=wco, ssm=(*ssm_w, d8))
        acts_p, states_p, ffn2_w = _mixer_branches(
            xp_p, n_p, None, None, w, riders=(w_ffn2_gate[l], w_ffn2_up[l], w_ffn2_down[l]))
        acts_s, states_s, _ = _mixer_branches(xp_s, n_s, state_conv[l], (state_ssm_re[l], state_ssm_im[l]), w)
        tail = (wgl, wo, row(norm_ffn2[l]), *ffn2_w, nf, last)
        xt_p = _mix_out_ffn(*acts_p, *tail)
        xt_s = _mix_out_ffn(*acts_s, *tail)
        outs_p.append(states_p)
        outs_s.append(states_s)
    stack = lambda outs: tuple(jnp.stack(leaf) for leaf in zip(*outs))
    return (xt_p.reshape(x_prompt.shape), xt_s.reshape(x_sample.shape), *stack(outs_p), *stack(outs_s))
```

```python
import functools

import jax
import jax.numpy as jnp
from jax import lax
from jax.experimental import pallas as pl
from jax.experimental.pallas import tpu as pltpu

F32 = jnp.float32
BF16 = jnp.bfloat16

D_MODEL = 1024
N_PHASE = 8
LANE = 128
MXU_TILE = 256
SLAB = 128
N_SLAB = D_MODEL // SLAB
GROUPS_PER_SLAB = 8
STATE_DIM = 64
SLAB_STATES = GROUPS_PER_SLAB * STATE_DIM
SEG = 4
RMS_EPS = 1e-6
VMEM_LIMIT = 60 * 1024 * 1024

FFN_ROWS = 1024
MIX_IN_ROWS = 128
MIX_OUT_ROWS = 64
SSM_ROWS = 256


def _rms(x, g):
    return x * lax.rsqrt(jnp.mean(x * x, axis=-1, keepdims=True) + RMS_EPS) * g


def _dot(a, b):
    return jnp.dot(a, b, preferred_element_type=F32)


def _const_spec(shape):
    zeros = (0,) * len(shape)
    return pl.BlockSpec(shape, lambda *_: zeros, pipeline_mode=pl.Buffered(1))


def _params(semantics):
    return pltpu.CompilerParams(dimension_semantics=semantics, vmem_limit_bytes=VMEM_LIMIT)


def _stack_phases(ref):
    return jnp.concatenate([ref[t] for t in range(N_PHASE)], axis=0)


def _ffn_math(x, g_ref, wg_ref, wu_ref, wd_ref):
    h = _rms(x, g_ref[...]).astype(BF16)
    act = (jax.nn.silu(_dot(h, wg_ref[...])) * _dot(h, wu_ref[...])).astype(BF16)
    return x + 0.5 * _dot(act, wd_ref[...])


def _rider_specs(weights, n_steps, step=lambda i: i):
    in_specs, out_specs, out_shapes = [], [], []
    for w in weights:
        n_blocks = n_steps
        while w.shape[0] % (16 * n_blocks):
            assert n_blocks % 2 == 0
            n_blocks //= 2
        spec = pl.BlockSpec((w.shape[0] // n_blocks, w.shape[1]),
                            lambda *idx, rep=n_steps // n_blocks: (step(*idx) // rep, 0))
        in_specs.append(spec)
        out_specs.append(spec)
        out_shapes.append(jax.ShapeDtypeStruct(w.shape, BF16))
    return in_specs, out_specs, out_shapes


def _convert_riders(in_refs, out_refs):
    for w_ref, o_ref in zip(in_refs, out_refs):
        o_ref[...] = w_ref[...].astype(BF16)


def _ffn_to_phase_body(*refs, rows, n_riders):
    x_ref, g_ref, wg_ref, wu_ref, wd_ref = refs[:5]
    o_ref, slab_ref = refs[5 + n_riders], refs[-1]
    _convert_riders(refs[5:5 + n_riders], refs[6 + n_riders:6 + 2 * n_riders])
    y = _ffn_math(x_ref[...], g_ref, wg_ref, wu_ref, wd_ref)
    for lt in range(D_MODEL // LANE):
        slab_ref[lt] = y[:, lt * LANE:(lt + 1) * LANE]
        for t in range(N_PHASE):
            o_ref[t, :, lt * LANE:(lt + 1) * LANE] = slab_ref[lt, pl.ds(t, rows, stride=N_PHASE), :]


def _store_token_major(y, o_ref, slab_ref, rows):
    for lt in range(D_MODEL // LANE):
        for t in range(N_PHASE):
            slab_ref[lt, pl.ds(t, rows, stride=N_PHASE), :] = y[t * rows:(t + 1) * rows, lt * LANE:(lt + 1) * LANE]
        o_ref[:, lt * LANE:(lt + 1) * LANE] = slab_ref[lt]


def _ffn_to_phase(x, g, wg, wu, wd, riders=()):
    n_tok, d = x.shape
    n_rows = n_tok // N_PHASE
    d_ff = wg.shape[1]
    rows = FFN_ROWS // N_PHASE
    assert n_rows % rows == 0
    n_steps = n_rows // rows
    rider_in, rider_out, rider_shapes = _rider_specs(riders, n_steps)
    out = pl.pallas_call(
        functools.partial(_ffn_to_phase_body, rows=rows, n_riders=len(riders)),
        grid=(n_steps,),
        in_specs=[
            pl.BlockSpec((N_PHASE * rows, d), lambda i: (i, 0)),
            _const_spec((1, d)),
            _const_spec((d, d_ff)),
            _const_spec((d, d_ff)),
            _const_spec((d_ff, d)),
        ] + rider_in,
        out_specs=[pl.BlockSpec((N_PHASE, rows, d), lambda i: (0, i, 0))] + rider_out,
        out_shape=[jax.ShapeDtypeStruct((N_PHASE, n_rows, d), F32)] + rider_shapes,
        scratch_shapes=[pltpu.VMEM((d // LANE, N_PHASE * rows, LANE), F32)],
        name="ffn_to_phase",
        compiler_params=_params(("arbitrary",)),
    )(x, g, wg, wu, wd, *riders)
    return out[0], tuple(out[1:])


def _mix_in_compute(x_ref, g, win_ref, wcv, wco_ref, shift, rows):
    d = D_MODEL
    h = _rms(_stack_phases(x_ref), g).astype(BF16)

    def proj(k):
        return _dot(h, win_ref[:, k * d:(k + 1) * d])

    z = proj(1) * proj(0)
    z6 = z[6 * rows:7 * rows]
    z7 = z[7 * rows:8 * rows]
    s6 = shift(z6, 0)
    s7 = shift(z7, 1)
    z1 = jnp.concatenate([s7, z[:7 * rows]], axis=0)
    z2 = jnp.concatenate([s6, s7, z[:6 * rows]], axis=0)
    conv = wcv[2:3] * z + wcv[1:2] * z1 + wcv[0:1] * z2
    y_conv = _dot((proj(2) * conv).astype(BF16), wco_ref[...])
    u = proj(3)
    a = jax.nn.sigmoid(proj(4)) * y_conv
    gs = jax.nn.sigmoid(proj(5))
    return u, a, gs, z6, z7


def _store_phases(ref, val, rows):
    val = val.astype(ref.dtype)
    for t in range(N_PHASE):
        ref[t] = val[t * rows:(t + 1) * rows]


def _mix_in_prompt_body(*refs, rows, n_riders):
    x_ref, g_ref, win_ref, wcv_ref, wco_ref = refs[:5]
    u_ref, a_ref, gs_ref, z6_ref, z7_ref = refs[5 + n_riders:10 + n_riders]
    carry_ref = refs[-1]
    _convert_riders(refs[5:5 + n_riders], refs[10 + n_riders:10 + 2 * n_riders])

    @pl.when(pl.program_id(1) == 0)
    def _():
        carry_ref[...] = jnp.zeros_like(carry_ref)

    row_id = lax.broadcasted_iota(jnp.int32, (rows, D_MODEL), 0)

    def shift(z, k):
        return jnp.where(row_id == 0, carry_ref[k:k + 1, :], pltpu.roll(z, 1, 0))

    u, a, gs, z6, z7 = _mix_in_compute(x_ref, g_ref[...], win_ref, wcv_ref[...], wco_ref, shift, rows)
    last6 = z6[rows - 1:rows]
    last7 = z7[rows - 1:rows]
    carry_ref[0:1, :] = last6
    carry_ref[1:2, :] = last7
    z6_ref[...] = last6
    z7_ref[...] = last7
    _store_phases(u_ref, u, rows)
    _store_phases(a_ref, a, rows)
    _store_phases(gs_ref, gs, rows)


def _mix_in_sample_body(x_ref, s6_ref, s7_ref, g_ref, win_ref, wcv_ref, wco_ref,
                        u_ref, a_ref, gs_ref, z6_ref, z7_ref, slab_ref, *, rows, rows_per_seq):
    row_id = lax.broadcasted_iota(jnp.int32, (rows, D_MODEL), 0)
    starts = (s6_ref, s7_ref)

    def shift(z, k):
        return jnp.where(row_id % rows_per_seq == 0, starts[k][...], pltpu.roll(z, 1, 0))

    u, a, gs, z6, z7 = _mix_in_compute(x_ref, g_ref[...], win_ref, wcv_ref[...], wco_ref, shift, rows)
    for z, z_ref in ((z6, z6_ref), (z7, z7_ref)):
        for lt in range(D_MODEL // LANE):
            slab_ref[...] = z[:, lt * LANE:(lt + 1) * LANE]
            z_ref[:, lt * LANE:(lt + 1) * LANE] = slab_ref[pl.ds(rows_per_seq - 1, rows // rows_per_seq,
                                                                stride=rows_per_seq), :]
    _store_phases(u_ref, u, rows)
    _store_phases(a_ref, a, rows)
    _store_phases(gs_ref, gs, rows)


def _mix_in(xp, conv_prev, g, w_in, w_conv, w_conv_out, n_seq, riders=()):
    n_rows = xp.shape[1]
    d = D_MODEL
    rows_per_seq = n_rows // n_seq
    ph_shape = jax.ShapeDtypeStruct((N_PHASE, n_rows, d), BF16)
    weights = (g, w_in, w_conv, w_conv_out)
    weight_specs = [_const_spec(w.shape) for w in weights]
    if conv_prev is None:
        rows = min(MIX_IN_ROWS, rows_per_seq)
        assert rows_per_seq % rows == 0
        tiles = rows_per_seq // rows
        ph_spec = pl.BlockSpec((N_PHASE, rows, d), lambda b, i: (0, b * tiles + i, 0))
        last_spec = pl.BlockSpec((None, 1, d), lambda b, i: (b, 0, 0))
        rider_in, rider_out, rider_shapes = _rider_specs(riders, n_seq * tiles, lambda b, i: b * tiles + i)
        u8, a8, gs8, z6, z7, *converted = pl.pallas_call(
            functools.partial(_mix_in_prompt_body, rows=rows, n_riders=len(riders)),
            grid=(n_seq, tiles),
            in_specs=[ph_spec] + weight_specs + rider_in,
            out_specs=[ph_spec, ph_spec, ph_spec, last_spec, last_spec] + rider_out,
            out_shape=[ph_shape, ph_shape, ph_shape, jax.ShapeDtypeStruct((n_seq, 1, d), F32),
                       jax.ShapeDtypeStruct((n_seq, 1, d), F32)] + rider_shapes,
            scratch_shapes=[pltpu.VMEM((8, d), F32)],
            name="mix_in_prompt",
            compiler_params=_params(("arbitrary", "arbitrary")),
        )(xp, *weights, *riders)
        z6, z7 = z6[:, 0], z7[:, 0]
    else:
        assert not riders
        converted = []
        rows = n_rows
        start = jnp.zeros((n_seq, rows_per_seq, 2, d), F32).at[:, 0].set(conv_prev).reshape(n_rows, 2, d)
        full = lambda shape: pl.BlockSpec(shape, lambda i: (0,) * len(shape))
        u8, a8, gs8, z6, z7 = pl.pallas_call(
            functools.partial(_mix_in_sample_body, rows=rows, rows_per_seq=rows_per_seq),
            grid=(1,),
            in_specs=[full((N_PHASE, rows, d)), full((rows, d)), full((rows, d))] + weight_specs,
            out_specs=[full((N_PHASE, rows, d))] * 3 + [full((n_seq, d))] * 2,
            out_shape=[ph_shape, ph_shape, ph_shape,
                       jax.ShapeDtypeStruct((n_seq, d), F32), jax.ShapeDtypeStruct((n_seq, d), F32)],
            scratch_shapes=[pltpu.VMEM((rows, LANE), F32)],
            name="mix_in_sample",
            compiler_params=_params(("arbitrary",)),
        )(xp, start[:, 0], start[:, 1], *weights)
    return u8, a8, gs8, jnp.stack([z6, z7], axis=1), tuple(converted)


def _split_bf16(a):
    hi = a.astype(BF16)
    return hi, (a - hi.astype(F32)).astype(BF16)


def _ssm_prep_body(*refs, n_riders):
    bbre_ref, bbim_ref, ctre_ref, ctim_ref, lre_ref, lim_ref, lstep_ref = refs[:7]
    wb_ref, kt_ref, wc_ref, pw_ref = refs[7 + n_riders:11 + n_riders]
    _convert_riders(refs[7:7 + n_riders], refs[11 + n_riders:])
    ns = SLAB_STATES
    row_group = lax.broadcasted_iota(jnp.int32, (SLAB, LANE), 0) // (SLAB // GROUPS_PER_SLAB)
    lane_half = lax.broadcasted_iota(jnp.int32, (SLAB, LANE), 1) // STATE_DIM

    def block_diag(ref):
        tiles = [jnp.where(row_group == 2 * k + lane_half, ref[...], 0.0) for k in range(GROUPS_PER_SLAB // 2)]
        return jnp.concatenate(tiles, axis=1)

    bbre, bbim = block_diag(bbre_ref), block_diag(bbim_ref)
    ctre, ctim = block_diag(ctre_ref), block_diag(ctim_ref)
    lre, lim = lre_ref[...], lim_ref[...]
    step = jnp.exp(lstep_ref[...])

    def lam_pow(n):
        mag = jnp.exp((n * lre) * step)
        ang = (n * lim) * step
        return mag * jnp.cos(ang), mag * jnp.sin(ang)

    l1re, l1im = lam_pow(1)
    den = lre * lre + lim * lim
    fre = ((l1re - 1.0) * lre + l1im * lim) / den
    fim = (l1im * lre - (l1re - 1.0) * lim) / den

    def dot_nt(a, b_split):
        nt = lambda p, q: lax.dot_general(p, q, (((1,), (1,)), ((), ())), preferred_element_type=F32)
        a_hi, a_lo = _split_bf16(a)
        b_hi, b_lo = b_split
        return nt(a_hi, b_hi) + nt(a_hi, b_lo) + nt(a_lo, b_hi)

    ctre_split = _split_bf16(ctre)
    ctim_split = _split_bf16(ctim)
    kt_ref[...] = jnp.zeros_like(kt_ref)
    for k in range(N_PHASE):
        pre, pim = lam_pow(k)
        gre = fre * pre - fim * pim
        gim = fre * pim + fim * pre
        are = bbre * gre - bbim * gim
        aim = bbre * gim + bbim * gre
        j = N_PHASE - 1 - k
        wb_ref[j * SLAB:(j + 1) * SLAB, 0:ns] = are.astype(BF16)
        wb_ref[j * SLAB:(j + 1) * SLAB, ns:2 * ns] = aim.astype(BF16)
        kk = (dot_nt(are, ctre_split) - dot_nt(aim, ctim_split)).astype(BF16)
        for jj in range(N_PHASE - k):
            t = jj + k
            kt_ref[jj * SLAB:(jj + 1) * SLAB, t * SLAB:(t + 1) * SLAB] = kk
    for t in range(N_PHASE):
        pre, pim = lam_pow(t + 1)
        cre = ctre * pre - ctim * pim
        cim = ctre * pim + ctim * pre
        wc_ref[0:ns, t * SLAB:(t + 1) * SLAB] = cre.T.astype(BF16)
        wc_ref[ns:2 * ns, t * SLAB:(t + 1) * SLAB] = (-cim).T.astype(BF16)
    pw_ref[...] = jnp.zeros_like(pw_ref)
    for r, n in enumerate((N_PHASE, N_PHASE * SEG)):
        pre, pim = lam_pow(n)
        pw_ref[r:r + 1, 0:ns] = pre
        pw_ref[r:r + 1, ns:2 * ns] = pim


def _ssm_prep(lam_re, lam_im, log_step, b_re, b_im, c_re, c_im, riders=()):
    ns = SLAB_STATES

    def twice(m):
        m = m.reshape(N_SLAB, SLAB, STATE_DIM)
        return jnp.concatenate([m, m], axis=-1)

    def vec(v):
        return v.reshape(N_SLAB, 1, ns)

    mats = [twice(jnp.transpose(b_re, (0, 2, 1))), twice(jnp.transpose(b_im, (0, 2, 1))), twice(c_re), twice(c_im)]
    vecs = [vec(lam_re), vec(lam_im), vec(jnp.broadcast_to(log_step[:, None], lam_re.shape))]
    w_shape = jax.ShapeDtypeStruct((N_SLAB, 2 * ns, 2 * ns), BF16)
    w_spec = pl.BlockSpec((None, 2 * ns, 2 * ns), lambda q: (q, 0, 0))
    rider_in, rider_out, rider_shapes = _rider_specs(riders, N_SLAB)
    out = pl.pallas_call(
        functools.partial(_ssm_prep_body, n_riders=len(riders)),
        grid=(N_SLAB,),
        in_specs=[pl.BlockSpec((None, SLAB, 2 * STATE_DIM), lambda q: (q, 0, 0))] * 4
        + [pl.BlockSpec((None, 1, ns), lambda q: (q, 0, 0))] * 3 + rider_in,
        out_specs=[w_spec, w_spec, w_spec, pl.BlockSpec((None, 8, 2 * ns), lambda q: (q, 0, 0))] + rider_out,
        out_shape=[w_shape, w_shape, w_shape, jax.ShapeDtypeStruct((N_SLAB, 8, 2 * ns), F32)] + rider_shapes,
        name="ssm_prep",
        compiler_params=_params(("arbitrary",)),
    )(*mats, *vecs, *riders)
    return tuple(out[:4]), tuple(out[4:])


def _cmul(are, aim, bre, bim):
    return are * bre - aim * bim, are * bim + aim * bre


HALF = SLAB_STATES // LANE
GROUP_ROWS = 8 * SEG


def _mult(pw_ref, r, c):
    return (pw_ref[r:r + 1, c * LANE:(c + 1) * LANE],
            pw_ref[r:r + 1, (HALF + c) * LANE:(HALF + c + 1) * LANE])


def _state_increments(lhs, wb_ref, e_ref):
    e = _dot(lhs, wb_ref[...])
    for lt in range(2 * HALF):
        e_ref[lt, :e.shape[0]] = e[:, lt * LANE:(lt + 1) * LANE]


def _scan_local(e_ref, p_ref, pw_ref, c, base):
    l8re, l8im = _mult(pw_ref, 0, c)
    lre = jnp.zeros((8, LANE), F32)
    lim = jnp.zeros((8, LANE), F32)
    for i in range(SEG):
        idx = pl.ds(base + i, 8, stride=SEG)
        p_ref[c, idx, :] = lre
        p_ref[HALF + c, idx, :] = lim
        mre, mim = _cmul(l8re, l8im, lre, lim)
        lre = mre + e_ref[c, idx, :]
        lim = mim + e_ref[HALF + c, idx, :]
    return lre, lim


def _scan_fixup(p_ref, pw_ref, c, base, cre, cim):
    l8re, l8im = _mult(pw_ref, 0, c)
    for i in range(SEG):
        idx = pl.ds(base + i, 8, stride=SEG)
        p_ref[c, idx, :] = p_ref[c, idx, :] + cre
        p_ref[HALF + c, idx, :] = p_ref[HALF + c, idx, :] + cim
        cre, cim = _cmul(l8re, l8im, cre, cim)


def _readout(lhs, p_ref, kt_ref, wc_ref, d_ref):
    prev = jnp.concatenate([p_ref[lt, :lhs.shape[0]] for lt in range(2 * HALF)], axis=1)
    toep = jnp.concatenate(
        [_dot(lhs[:, :(k + 1) * MXU_TILE], kt_ref[:(k + 1) * MXU_TILE, k * MXU_TILE:(k + 1) * MXU_TILE])
         for k in range(N_PHASE * LANE // MXU_TILE)], axis=1)
    y = toep + _dot(prev.astype(BF16), wc_ref[...]) + d_ref[...] * lhs.astype(F32)
    return y.astype(BF16)


def _ssm_prompt_body(u_ref, wb_ref, kt_ref, wc_ref, pw_ref, d_ref, y_ref, sre_ref, sim_ref,
                     e_ref, p_ref, end_ref, cin_ref, carry_ref, *, n_seq, rows, tile):
    total = n_seq * rows
    lhs = jnp.concatenate([u_ref[t].reshape(total, LANE) for t in range(N_PHASE)], axis=1)
    _state_increments(lhs, wb_ref, e_ref)

    @pl.when((pl.program_id(0) == 0) & (tile == 0))
    def _():
        end_ref[...] = jnp.zeros_like(end_ref)

    @pl.when(tile == 0)
    def _():
        carry_ref[...] = jnp.zeros_like(carry_ref)

    groups = rows // GROUP_ROWS
    for c in range(HALF):
        for b in range(n_seq):
            for j in range(groups):
                lre, lim = _scan_local(e_ref, p_ref, pw_ref, c, b * rows + j * GROUP_ROWS)
                idx = pl.ds(j * 64 + b, 8, stride=8)
                end_ref[c, idx, :] = lre
                end_ref[HALF + c, idx, :] = lim
    for c in range(HALF):
        lsre, lsim = _mult(pw_ref, 1, c)
        cre = carry_ref[c]
        cim = carry_ref[HALF + c]
        for s in range(rows // SEG):
            cin_ref[c, 8 * s:8 * s + 8, :] = cre
            cin_ref[HALF + c, 8 * s:8 * s + 8, :] = cim
            mre, mim = _cmul(lsre, lsim, cre, cim)
            cre = mre + end_ref[c, 8 * s:8 * s + 8, :]
            cim = mim + end_ref[HALF + c, 8 * s:8 * s + 8, :]
        carry_ref[c] = cre
        carry_ref[HALF + c] = cim
        sre_ref[:, c * LANE:(c + 1) * LANE] = cre[:n_seq]
        sim_ref[:, c * LANE:(c + 1) * LANE] = cim[:n_seq]
    for c in range(HALF):
        for b in range(n_seq):
            for j in range(groups):
                idx = pl.ds(j * 64 + b, 8, stride=8)
                _scan_fixup(p_ref, pw_ref, c, b * rows + j * GROUP_ROWS,
                            cin_ref[c, idx, :], cin_ref[HALF + c, idx, :])
    y = _readout(lhs, p_ref, kt_ref, wc_ref, d_ref)
    for t in range(N_PHASE):
        y_ref[t] = y[:, t * LANE:(t + 1) * LANE].reshape(n_seq, rows, LANE)


def _ssm_sample_body(u_ref, wb_ref, kt_ref, wc_ref, pw_ref, d_ref, h0re_ref, h0im_ref,
                     y_ref, sre_ref, sim_ref, e_ref, p_ref, *, rows):
    lhs = jnp.concatenate([u_ref[t] for t in range(N_PHASE)], axis=1)
    _state_increments(lhs, wb_ref, e_ref)
    for c in range(HALF):
        lsre, lsim = _mult(pw_ref, 1, c)
        for j in range(rows // GROUP_ROWS):
            lre, lim = _scan_local(e_ref, p_ref, pw_ref, c, j * GROUP_ROWS)
            cre = h0re_ref[8 * j:8 * j + 8, c * LANE:(c + 1) * LANE]
            cim = h0im_ref[8 * j:8 * j + 8, c * LANE:(c + 1) * LANE]
            mre, mim = _cmul(lsre, lsim, cre, cim)
            sre_ref[8 * j:8 * j + 8, c * LANE:(c + 1) * LANE] = mre + lre
            sim_ref[8 * j:8 * j + 8, c * LANE:(c + 1) * LANE] = mim + lim
            _scan_fixup(p_ref, pw_ref, c, j * GROUP_ROWS, cre, cim)
    y = _readout(lhs, p_ref, kt_ref, wc_ref, d_ref)
    for t in range(N_PHASE):
        y_ref[t] = y[:, t * LANE:(t + 1) * LANE]


def _ssm_body(u_ref, wb_ref, kt_ref, wc_ref, pw_ref, d_ref, us_ref, h0re_ref, h0im_ref,
              y_ref, sre_ref, sim_ref, ys_ref, sres_ref, sims_ref,
              e_ref, p_ref, end_ref, cin_ref, carry_ref, *, n_seq, rows, rows_s):
    step = pl.program_id(1)

    @pl.when(step == 0)
    def _():
        _ssm_sample_body(us_ref, wb_ref, kt_ref, wc_ref, pw_ref, d_ref, h0re_ref, h0im_ref,
                         ys_ref, sres_ref, sims_ref, e_ref, p_ref, rows=rows_s)

    @pl.when(step > 0)
    def _():
        _ssm_prompt_body(u_ref, wb_ref, kt_ref, wc_ref, pw_ref, d_ref, y_ref, sre_ref, sim_ref,
                         e_ref, p_ref, end_ref, cin_ref, carry_ref, n_seq=n_seq, rows=rows, tile=step - 1)


def _ssm(u8_p, n_p, u8_s, n_s, h0, wb, kt, wc, pw, d8):
    d = D_MODEL
    ns = SLAB_STATES
    rows_p, rows_s = u8_p.shape[1], u8_s.shape[1]
    per_seq = rows_p // n_p
    assert rows_s // n_s == SEG, "carried-state path scans one segment per sequence"
    assert n_p <= 8, "sequences ride the sublanes of the segment chain"
    rows = min(SSM_ROWS, per_seq)
    assert per_seq % rows == 0 and rows % GROUP_ROWS == 0 and rows_s % GROUP_ROWS == 0
    assert rows_s <= n_p * rows, "the carried-state step reuses the scan scratch"
    tiles = per_seq // rows
    slab = lambda q, i: (q, 0, 0)
    p_spec = pl.BlockSpec((N_PHASE, n_p, rows, LANE), lambda q, i: (0, 0, jnp.maximum(i - 1, 0), q))
    s_spec = pl.BlockSpec((N_PHASE, rows_s, LANE), lambda q, i: (0, 0, q))
    w_spec = pl.BlockSpec((None, 2 * ns, 2 * ns), slab)
    state_p = pl.BlockSpec((n_p, ns), lambda q, i: (0, q))
    state_s = pl.BlockSpec((n_s, ns), lambda q, i: (0, q))
    seg_rows = 8 * (rows // SEG)
    dims_p = (N_PHASE, n_p, per_seq, d)
    y_p, re_p, im_p, y_s, re_s, im_s = pl.pallas_call(
        functools.partial(_ssm_body, n_seq=n_p, rows=rows, rows_s=rows_s),
        grid=(N_SLAB, tiles + 1),
        in_specs=[p_spec, w_spec, w_spec, w_spec, pl.BlockSpec((None, 8, 2 * ns), slab),
                  pl.BlockSpec((None, 1, d), slab), s_spec, state_s, state_s],
        out_specs=[p_spec, state_p, state_p, s_spec, state_s, state_s],
        out_shape=[jax.ShapeDtypeStruct(dims_p, BF16),
                   jax.ShapeDtypeStruct((n_p, N_SLAB * ns), F32), jax.ShapeDtypeStruct((n_p, N_SLAB * ns), F32),
                   jax.ShapeDtypeStruct((N_PHASE, rows_s, d), BF16),
                   jax.ShapeDtypeStruct((n_s, N_SLAB * ns), F32), jax.ShapeDtypeStruct((n_s, N_SLAB * ns), F32)],
        scratch_shapes=[pltpu.VMEM((2 * HALF, n_p * rows, LANE), F32),
                        pltpu.VMEM((2 * HALF, n_p * rows, LANE), F32),
                        pltpu.VMEM((2 * HALF, seg_rows, LANE), F32),
                        pltpu.VMEM((2 * HALF, seg_rows, LANE), F32),
                        pltpu.VMEM((2 * HALF, 8, LANE), F32)],
        name="ssm",
        compiler_params=_params(("arbitrary", "arbitrary")),
    )(u8_p.reshape(dims_p), wb, kt, wc, pw, d8, u8_s, h0[0].reshape(n_s, -1), h0[1].reshape(n_s, -1))
    return (y_p.reshape(N_PHASE, rows_p, d), re_p, im_p), (y_s, re_s, im_s)


def _mix_out_ffn_body(y_ref, a_ref, gs_ref, x_ref, wglu_ref, wo_ref, g_ref, wg_ref, wu_ref, wd_ref, gf_ref,
                      o_ref, slab_ref, *, rows, final_norm):
    d = D_MODEL
    glu = _dot(jax.nn.gelu(_stack_phases(y_ref).astype(F32)).astype(BF16), wglu_ref[...])
    y_ssm = glu[:, :d] * jax.nn.sigmoid(glu[:, d:])
    merged = _stack_phases(a_ref).astype(F32) + _stack_phases(gs_ref).astype(F32) * y_ssm
    x2 = _stack_phases(x_ref) + _dot(merged.astype(BF16), wo_ref[...])
    y = _ffn_math(x2, g_ref, wg_ref, wu_ref, wd_ref)
    if final_norm:
        y = _rms(y, gf_ref[...])
    _store_token_major(y, o_ref, slab_ref, rows)


def _mix_out_ffn(y8, a8, gs8, xp, w_glu, w_o, g, wg, wu, wd, gf, final_norm):
    n_rows = xp.shape[1]
    d = D_MODEL
    rows = min(MIX_OUT_ROWS, n_rows)
    assert n_rows % rows == 0
    ph_spec = pl.BlockSpec((N_PHASE, rows, d), lambda i: (0, i, 0))
    weights = (w_glu, w_o, g, wg, wu, wd, gf)
    return pl.pallas_call(
        functools.partial(_mix_out_ffn_body, rows=rows, final_norm=final_norm),
        grid=(n_rows // rows,),
        in_specs=[ph_spec, ph_spec, ph_spec, ph_spec] + [_const_spec(w.shape) for w in weights],
        out_specs=pl.BlockSpec((N_PHASE * rows, d), lambda i: (i, 0)),
        out_shape=jax.ShapeDtypeStruct((N_PHASE * n_rows, d), F32),
        scratch_shapes=[pltpu.VMEM((d // LANE, N_PHASE * rows, LANE), F32)],
        name="mix_out_ffn",
        compiler_params=_params(("arbitrary",)),
    )(y8, a8, gs8, xp, *weights)


def kernel(x_prompt, x_sample, state_conv, state_ssm_re, state_ssm_im, norm_ffn1, w_ffn1_gate, w_ffn1_up, w_ffn1_down, norm_mix, w_in, w_conv, w_conv_out, ssm_lambda_re, ssm_lambda_im, ssm_log_step, ssm_b_re, ssm_b_im, ssm_c_re, ssm_c_im, ssm_d, w_glu, w_o, norm_ffn2, w_ffn2_gate, w_ffn2_up, w_ffn2_down, norm_final):
    depth, d = w_in.shape[0], w_in.shape[1]
    row = lambda v: v.reshape(1, -1)
    nf = row(norm_final)
    xt_p = x_prompt.reshape(-1, d)
    xt_s = x_sample.reshape(-1, d)
    n_p, n_s = x_prompt.shape[0], x_sample.shape[0]
    outs_p, outs_s = [], []
    for l in range(depth):
        last = l == depth - 1
        ssm_w, (f1g, f1u, f1d) = _ssm_prep(
            ssm_lambda_re[l], ssm_lambda_im[l], ssm_log_step[l], ssm_b_re[l], ssm_b_im[l], ssm_c_re[l],
            ssm_c_im[l], riders=(w_ffn1_gate[l], w_ffn1_up[l], w_ffn1_down[l]))
        ffn1 = (row(norm_ffn1[l]), f1g, f1u, f1d)
        xp_p, (win, wco, wgl, wo) = _ffn_to_phase(xt_p, *ffn1, riders=(w_in[l], w_conv_out[l], w_glu[l], w_o[l]))
        xp_s, _ = _ffn_to_phase(xt_s, *ffn1)
        d8 = jnp.tile(ssm_d[l].reshape(N_SLAB, 1, SLAB), (1, 1, N_PHASE))
        mix_w = (row(norm_mix[l]), win, w_conv[l], wco)
        u_p, a_p, gs_p, conv_p, ffn2_w = _mix_in(xp_p, None, *mix_w, n_p,
                                                 riders=(w_ffn2_gate[l], w_ffn2_up[l], w_ffn2_down[l]))
        u_s, a_s, gs_s, conv_s, _ = _mix_in(xp_s, state_conv[l], *mix_w, n_s)
        (y_p, re_p, im_p), (y_s, re_s, im_s) = _ssm(u_p, n_p, u_s, n_s, (state_ssm_re[l], state_ssm_im[l]),
                                                    *ssm_w, d8)
        tail = (wgl, wo, row(norm_ffn2[l]), *ffn2_w, nf, last)
        xt_p = _mix_out_ffn(y_p, a_p, gs_p, xp_p, *tail)
        xt_s = _mix_out_ffn(y_s, a_s, gs_s, xp_s, *tail)
        outs_p.append([conv_p, re_p.reshape(n_p, -1, STATE_DIM), im_p.reshape(n_p, -1, STATE_DIM)])
        outs_s.append([conv_s, re_s.reshape(n_s, -1, STATE_DIM), im_s.reshape(n_s, -1, STATE_DIM)])
    stack = lambda outs: tuple(jnp.stack(leaf) for leaf in zip(*outs))
    return (xt_p.reshape(x_prompt.shape), xt_s.reshape(x_sample.shape), *stack(outs_p), *stack(outs_s))
```

```python
import functools

import jax
import jax.numpy as jnp
from jax import lax
from jax.experimental import pallas as pl
from jax.experimental.pallas import tpu as pltpu

F32 = jnp.float32
BF16 = jnp.bfloat16

D_MODEL = 1024
N_PHASE = 8
LANE = 128
MXU_TILE = 256
SLAB = 128
N_SLAB = D_MODEL // SLAB
GROUPS_PER_SLAB = 8
STATE_DIM = 64
SLAB_STATES = GROUPS_PER_SLAB * STATE_DIM
SEG = 4
RMS_EPS = 1e-6
VMEM_LIMIT = 60 * 1024 * 1024

FFN_ROWS = 1024
MIX_IN_ROWS = 128
MIX_OUT_ROWS = 64
SSM_ROWS = 256


def _rms(x, g):
    return x * lax.rsqrt(jnp.mean(x * x, axis=-1, keepdims=True) + RMS_EPS) * g


def _dot(a, b):
    return jnp.dot(a, b, preferred_element_type=F32)


def _const_spec(shape):
    zeros = (0,) * len(shape)
    return pl.BlockSpec(shape, lambda *_: zeros, pipeline_mode=pl.Buffered(1))


def _params(semantics):
    return pltpu.CompilerParams(dimension_semantics=semantics, vmem_limit_bytes=VMEM_LIMIT)


def _stack_phases(ref):
    return jnp.concatenate([ref[t] for t in range(N_PHASE)], axis=0)


def _ffn_math(x, g_ref, wg_ref, wu_ref, wd_ref):
    h = _rms(x, g_ref[...]).astype(BF16)
    act = (jax.nn.silu(_dot(h, wg_ref[...])) * _dot(h, wu_ref[...])).astype(BF16)
    return x + 0.5 * _dot(act, wd_ref[...])


def _rider_specs(weights, n_steps, step=lambda i: i):
    in_specs, out_specs, out_shapes = [], [], []
    for w in weights:
        n_blocks = n_steps
        while w.shape[0] % (16 * n_blocks):
            assert n_blocks % 2 == 0
            n_blocks //= 2
        spec = pl.BlockSpec((w.shape[0] // n_blocks, w.shape[1]),
                            lambda *idx, rep=n_steps // n_blocks: (step(*idx) // rep, 0))
        in_specs.append(spec)
        out_specs.append(spec)
        out_shapes.append(jax.ShapeDtypeStruct(w.shape, BF16))
    return in_specs, out_specs, out_shapes


def _convert_riders(in_refs, out_refs):
    for w_ref, o_ref in zip(in_refs, out_refs):
        o_ref[...] = w_ref[...].astype(BF16)


def _ffn_to_phase_body(*refs, rows, n_riders):
    x_ref, g_ref, wg_ref, wu_ref, wd_ref = refs[:5]
    o_ref, slab_ref = refs[5 + n_riders], refs[-1]
    _convert_riders(refs[5:5 + n_riders], refs[6 + n_riders:6 + 2 * n_riders])
    y = _ffn_math(x_ref[...], g_ref, wg_ref, wu_ref, wd_ref)
    for lt in range(D_MODEL // LANE):
        slab_ref[lt] = y[:, lt * LANE:(lt + 1) * LANE]
        for t in range(N_PHASE):
            o_ref[t, :, lt * LANE:(lt + 1) * LANE] = slab_ref[lt, pl.ds(t, rows, stride=N_PHASE), :]


def _store_token_major(y, o_ref, slab_ref, rows):
    for lt in range(D_MODEL // LANE):
        for t in range(N_PHASE):
            slab_ref[lt, pl.ds(t, rows, stride=N_PHASE), :] = y[t * rows:(t + 1) * rows, lt * LANE:(lt + 1) * LANE]
        o_ref[:, lt * LANE:(lt + 1) * LANE] = slab_ref[lt]


def _ffn_to_phase(x, g, wg, wu, wd, riders=()):
    n_tok, d = x.shape
    n_rows = n_tok // N_PHASE
    d_ff = wg.shape[1]
    rows = FFN_ROWS // N_PHASE
    assert n_rows % rows == 0
    n_steps = n_rows // rows
    rider_in, rider_out, rider_shapes = _rider_specs(riders, n_steps)
    out = pl.pallas_call(
        functools.partial(_ffn_to_phase_body, rows=rows, n_riders=len(riders)),
        grid=(n_steps,),
        in_specs=[
            pl.BlockSpec((N_PHASE * rows, d), lambda i: (i, 0)),
            _const_spec((1, d)),
            _const_spec((d, d_ff)),
            _const_spec((d, d_ff)),
            _const_spec((d_ff, d)),
        ] + rider_in,
        out_specs=[pl.BlockSpec((N_PHASE, rows, d), lambda i: (0, i, 0))] + rider_out,
        out_shape=[jax.ShapeDtypeStruct((N_PHASE, n_rows, d), F32)] + rider_shapes,
        scratch_shapes=[pltpu.VMEM((d // LANE, N_PHASE * rows, LANE), F32)],
        name="ffn_to_phase",
        compiler_params=_params(("arbitrary",)),
    )(x, g, wg, wu, wd, *riders)
    return out[0], tuple(out[1:])


def _mix_in_compute(x_ref, g, win_ref, wcv, wco_ref, shift, rows):
    d = D_MODEL
    h = _rms(_stack_phases(x_ref), g).astype(BF16)

    def proj(k):
        return _dot(h, win_ref[:, k * d:(k + 1) * d])

    z = proj(1) * proj(0)
    z6 = z[6 * rows:7 * rows]
    z7 = z[7 * rows:8 * rows]
    s6 = shift(z6, 0)
    s7 = shift(z7, 1)
    z1 = jnp.concatenate([s7, z[:7 * rows]], axis=0)
    z2 = jnp.concatenate([s6, s7, z[:6 * rows]], axis=0)
    conv = wcv[2:3] * z + wcv[1:2] * z1 + wcv[0:1] * z2
    y_conv = _dot((proj(2) * conv).astype(BF16), wco_ref[...])
    u = proj(3)
    a = jax.nn.sigmoid(proj(4)) * y_conv
    gs = jax.nn.sigmoid(proj(5))
    return u, a, gs, z6, z7


def _store_phases(ref, val, rows):
    val = val.astype(ref.dtype)
    for t in range(N_PHASE):
        ref[t] = val[t * rows:(t + 1) * rows]


def _mix_in_prompt_body(*refs, rows, n_riders):
    x_ref, g_ref, win_ref, wcv_ref, wco_ref = refs[:5]
    u_ref, a_ref, gs_ref, z_ref = refs[5 + n_riders:9 + n_riders]
    carry_ref = refs[-1]
    _convert_riders(refs[5:5 + n_riders], refs[9 + n_riders:9 + 2 * n_riders])

    @pl.when(pl.program_id(1) == 0)
    def _():
        carry_ref[...] = jnp.zeros_like(carry_ref)

    row_id = lax.broadcasted_iota(jnp.int32, (rows, D_MODEL), 0)

    def shift(z, k):
        return jnp.where(row_id == 0, carry_ref[k:k + 1, :], pltpu.roll(z, 1, 0))

    u, a, gs, z6, z7 = _mix_in_compute(x_ref, g_ref[...], win_ref, wcv_ref[...], wco_ref, shift, rows)
    last6 = z6[rows - 1:rows]
    last7 = z7[rows - 1:rows]
    carry_ref[0:1, :] = last6
    carry_ref[1:2, :] = last7
    z_ref[0:1, :] = last6
    z_ref[1:2, :] = last7
    _store_phases(u_ref, u, rows)
    _store_phases(a_ref, a, rows)
    _store_phases(gs_ref, gs, rows)


def _mix_in_sample_body(x_ref, start_ref, g_ref, win_ref, wcv_ref, wco_ref,
                        u_ref, a_ref, gs_ref, z6_ref, z7_ref, slab_ref, *, rows, rows_per_seq):
    row_id = lax.broadcasted_iota(jnp.int32, (rows, D_MODEL), 0)

    def shift(z, k):
        return jnp.where(row_id % rows_per_seq == 0, start_ref[k], pltpu.roll(z, 1, 0))

    u, a, gs, z6, z7 = _mix_in_compute(x_ref, g_ref[...], win_ref, wcv_ref[...], wco_ref, shift, rows)
    for z, z_ref in ((z6, z6_ref), (z7, z7_ref)):
        for lt in range(D_MODEL // LANE):
            slab_ref[...] = z[:, lt * LANE:(lt + 1) * LANE]
            z_ref[:, lt * LANE:(lt + 1) * LANE] = slab_ref[pl.ds(rows_per_seq - 1, rows // rows_per_seq,
                                                                stride=rows_per_seq), :]
    _store_phases(u_ref, u, rows)
    _store_phases(a_ref, a, rows)
    _store_phases(gs_ref, gs, rows)


def _mix_in(xp, conv_prev, g, w_in, w_conv, w_conv_out, n_seq, riders=()):
    n_rows = xp.shape[1]
    d = D_MODEL
    rows_per_seq = n_rows // n_seq
    ph_shape = jax.ShapeDtypeStruct((N_PHASE, n_rows, d), BF16)
    weights = (g, w_in, w_conv, w_conv_out)
    weight_specs = [_const_spec(w.shape) for w in weights]
    if conv_prev is None:
        rows = min(MIX_IN_ROWS, rows_per_seq)
        assert rows_per_seq % rows == 0
        tiles = rows_per_seq // rows
        ph_spec = pl.BlockSpec((N_PHASE, rows, d), lambda b, i: (0, b * tiles + i, 0))
        last_spec = pl.BlockSpec((None, 2, d), lambda b, i: (b, 0, 0))
        rider_in, rider_out, rider_shapes = _rider_specs(riders, n_seq * tiles, lambda b, i: b * tiles + i)
        u8, a8, gs8, conv_state, *converted = pl.pallas_call(
            functools.partial(_mix_in_prompt_body, rows=rows, n_riders=len(riders)),
            grid=(n_seq, tiles),
            in_specs=[ph_spec] + weight_specs + rider_in,
            out_specs=[ph_spec, ph_spec, ph_spec, last_spec] + rider_out,
            out_shape=[ph_shape, ph_shape, ph_shape, jax.ShapeDtypeStruct((n_seq, 2, d), F32)] + rider_shapes,
            scratch_shapes=[pltpu.VMEM((8, d), F32)],
            name="mix_in_prompt",
            compiler_params=_params(("arbitrary", "arbitrary")),
        )(xp, *weights, *riders)
    else:
        assert not riders
        converted = []
        rows = n_rows
        start = jnp.repeat(jnp.swapaxes(conv_prev, 0, 1), rows_per_seq, axis=1)
        full = lambda shape: pl.BlockSpec(shape, lambda i: (0,) * len(shape))
        u8, a8, gs8, z6, z7 = pl.pallas_call(
            functools.partial(_mix_in_sample_body, rows=rows, rows_per_seq=rows_per_seq),
            grid=(1,),
            in_specs=[full((N_PHASE, rows, d)), full((2, rows, d))] + weight_specs,
            out_specs=[full((N_PHASE, rows, d))] * 3 + [full((n_seq, d))] * 2,
            out_shape=[ph_shape, ph_shape, ph_shape,
                       jax.ShapeDtypeStruct((n_seq, d), F32), jax.ShapeDtypeStruct((n_seq, d), F32)],
            scratch_shapes=[pltpu.VMEM((rows, LANE), F32)],
            name="mix_in_sample",
            compiler_params=_params(("arbitrary",)),
        )(xp, start, *weights)
        conv_state = jnp.stack([z6, z7], axis=1)
    return u8, a8, gs8, conv_state, tuple(converted)


def _split_bf16(a):
    hi = a.astype(BF16)
    return hi, (a - hi.astype(F32)).astype(BF16)


def _ssm_prep_body(*refs, n_riders):
    bbre_ref, bbim_ref, ctre_ref, ctim_ref, lre_ref, lim_ref, lstep_ref = refs[:7]
    wb_ref, kt_ref, wc_ref, pw_ref = refs[7 + n_riders:11 + n_riders]
    _convert_riders(refs[7:7 + n_riders], refs[11 + n_riders:])
    ns = SLAB_STATES
    row_group = lax.broadcasted_iota(jnp.int32, (SLAB, LANE), 0) // (SLAB // GROUPS_PER_SLAB)
    lane_half = lax.broadcasted_iota(jnp.int32, (SLAB, LANE), 1) // STATE_DIM

    def block_diag(ref):
        tiles = [jnp.where(row_group == 2 * k + lane_half, ref[...], 0.0) for k in range(GROUPS_PER_SLAB // 2)]
        return jnp.concatenate(tiles, axis=1)

    bbre, bbim = block_diag(bbre_ref), block_diag(bbim_ref)
    ctre, ctim = block_diag(ctre_ref), block_diag(ctim_ref)
    lre, lim = lre_ref[...], lim_ref[...]
    step = jnp.exp(lstep_ref[...])

    def lam_pow(n):
        mag = jnp.exp((n * lre) * step)
        ang = (n * lim) * step
        return mag * jnp.cos(ang), mag * jnp.sin(ang)

    l1re, l1im = lam_pow(1)
    den = lre * lre + lim * lim
    fre = ((l1re - 1.0) * lre + l1im * lim) / den
    fim = (l1im * lre - (l1re - 1.0) * lim) / den

    def dot_nt(a, b_split):
        nt = lambda p, q: lax.dot_general(p, q, (((1,), (1,)), ((), ())), preferred_element_type=F32)
        a_hi, a_lo = _split_bf16(a)
        b_hi, b_lo = b_split
        return nt(a_hi, b_hi) + nt(a_hi, b_lo) + nt(a_lo, b_hi)

    ctre_split = _split_bf16(ctre)
    ctim_split = _split_bf16(ctim)
    kt_ref[...] = jnp.zeros_like(kt_ref)
    for k in range(N_PHASE):
        pre, pim = lam_pow(k)
        gre = fre * pre - fim * pim
        gim = fre * pim + fim * pre
        are = bbre * gre - bbim * gim
        aim = bbre * gim + bbim * gre
        j = N_PHASE - 1 - k
        wb_ref[j * SLAB:(j + 1) * SLAB, 0:ns] = are.astype(BF16)
        wb_ref[j * SLAB:(j + 1) * SLAB, ns:2 * ns] = aim.astype(BF16)
        kk = (dot_nt(are, ctre_split) - dot_nt(aim, ctim_split)).astype(BF16)
        for jj in range(N_PHASE - k):
            t = jj + k
            kt_ref[jj * SLAB:(jj + 1) * SLAB, t * SLAB:(t + 1) * SLAB] = kk
    for t in range(N_PHASE):
        pre, pim = lam_pow(t + 1)
        cre = ctre * pre - ctim * pim
        cim = ctre * pim + ctim * pre
        wc_ref[0:ns, t * SLAB:(t + 1) * SLAB] = cre.T.astype(BF16)
        wc_ref[ns:2 * ns, t * SLAB:(t + 1) * SLAB] = (-cim).T.astype(BF16)
    pw_ref[...] = jnp.zeros_like(pw_ref)
    for r, n in enumerate((N_PHASE, N_PHASE * SEG)):
        pre, pim = lam_pow(n)
        pw_ref[r:r + 1, 0:ns] = pre
        pw_ref[r:r + 1, ns:2 * ns] = pim


def _ssm_prep(lam_re, lam_im, log_step, b_re, b_im, c_re, c_im, riders=()):
    ns = SLAB_STATES

    def twice(m):
        m = m.reshape(N_SLAB, SLAB, STATE_DIM)
        return jnp.concatenate([m, m], axis=-1)

    def vec(v):
        return v.reshape(N_SLAB, 1, ns)

    mats = [twice(jnp.transpose(b_re, (0, 2, 1))), twice(jnp.transpose(b_im, (0, 2, 1))), twice(c_re), twice(c_im)]
    vecs = [vec(lam_re), vec(lam_im), vec(jnp.broadcast_to(log_step[:, None], lam_re.shape))]
    w_shape = jax.ShapeDtypeStruct((N_SLAB, 2 * ns, 2 * ns), BF16)
    w_spec = pl.BlockSpec((None, 2 * ns, 2 * ns), lambda q: (q, 0, 0))
    rider_in, rider_out, rider_shapes = _rider_specs(riders, N_SLAB)
    out = pl.pallas_call(
        functools.partial(_ssm_prep_body, n_riders=len(riders)),
        grid=(N_SLAB,),
        in_specs=[pl.BlockSpec((None, SLAB, 2 * STATE_DIM), lambda q: (q, 0, 0))] * 4
        + [pl.BlockSpec((None, 1, ns), lambda q: (q, 0, 0))] * 3 + rider_in,
        out_specs=[w_spec, w_spec, w_spec, pl.BlockSpec((None, 8, 2 * ns), lambda q: (q, 0, 0))] + rider_out,
        out_shape=[w_shape, w_shape, w_shape, jax.ShapeDtypeStruct((N_SLAB, 8, 2 * ns), F32)] + rider_shapes,
        name="ssm_prep",
        compiler_params=_params(("arbitrary",)),
    )(*mats, *vecs, *riders)
    return tuple(out[:4]), tuple(out[4:])


def _cmul(are, aim, bre, bim):
    return are * bre - aim * bim, are * bim + aim * bre


HALF = SLAB_STATES // LANE
GROUP_ROWS = 8 * SEG


def _mult(pw_ref, r, c):
    return (pw_ref[r:r + 1, c * LANE:(c + 1) * LANE],
            pw_ref[r:r + 1, (HALF + c) * LANE:(HALF + c + 1) * LANE])


def _state_increments(lhs, wb_ref, e_ref):
    e = _dot(lhs, wb_ref[...])
    for lt in range(2 * HALF):
        e_ref[lt, :e.shape[0]] = e[:, lt * LANE:(lt + 1) * LANE]


def _scan_local(e_ref, p_ref, pw_ref, c, base):
    l8re, l8im = _mult(pw_ref, 0, c)
    lre = jnp.zeros((8, LANE), F32)
    lim = jnp.zeros((8, LANE), F32)
    for i in range(SEG):
        idx = pl.ds(base + i, 8, stride=SEG)
        p_ref[c, idx, :] = lre
        p_ref[HALF + c, idx, :] = lim
        mre, mim = _cmul(l8re, l8im, lre, lim)
        lre = mre + e_ref[c, idx, :]
        lim = mim + e_ref[HALF + c, idx, :]
    return lre, lim


def _scan_fixup(p_ref, pw_ref, c, base, cre, cim):
    l8re, l8im = _mult(pw_ref, 0, c)
    for i in range(SEG):
        idx = pl.ds(base + i, 8, stride=SEG)
        p_ref[c, idx, :] = p_ref[c, idx, :] + cre
        p_ref[HALF + c, idx, :] = p_ref[HALF + c, idx, :] + cim
        cre, cim = _cmul(l8re, l8im, cre, cim)


def _readout(lhs, p_ref, kt_ref, wc_ref, d_ref):
    prev = jnp.concatenate([p_ref[lt, :lhs.shape[0]] for lt in range(2 * HALF)], axis=1)
    toep = jnp.concatenate(
        [_dot(lhs[:, :(k + 1) * MXU_TILE], kt_ref[:(k + 1) * MXU_TILE, k * MXU_TILE:(k + 1) * MXU_TILE])
         for k in range(N_PHASE * LANE // MXU_TILE)], axis=1)
    y = toep + _dot(prev.astype(BF16), wc_ref[...]) + d_ref[...] * lhs.astype(F32)
    return y.astype(BF16)


def _ssm_prompt_body(u_ref, wb_ref, kt_ref, wc_ref, pw_ref, d_ref, y_ref, sre_ref, sim_ref,
                     e_ref, p_ref, end_ref, cin_ref, carry_ref, *, n_seq, rows, tile):
    total = n_seq * rows
    lhs = jnp.concatenate([u_ref[t].reshape(total, LANE) for t in range(N_PHASE)], axis=1)
    _state_increments(lhs, wb_ref, e_ref)

    @pl.when((pl.program_id(0) == 0) & (tile == 0))
    def _():
        end_ref[...] = jnp.zeros_like(end_ref)

    @pl.when(tile == 0)
    def _():
        carry_ref[...] = jnp.zeros_like(carry_ref)

    groups = rows // GROUP_ROWS
    for c in range(HALF):
        for b in range(n_seq):
            for j in range(groups):
                lre, lim = _scan_local(e_ref, p_ref, pw_ref, c, b * rows + j * GROUP_ROWS)
                idx = pl.ds(j * 64 + b, 8, stride=8)
                end_ref[c, idx, :] = lre
                end_ref[HALF + c, idx, :] = lim
    for c in range(HALF):
        lsre, lsim = _mult(pw_ref, 1, c)
        cre = carry_ref[c]
        cim = carry_ref[HALF + c]
        for s in range(rows // SEG):
            cin_ref[c, 8 * s:8 * s + 8, :] = cre
            cin_ref[HALF + c, 8 * s:8 * s + 8, :] = cim
            mre, mim = _cmul(lsre, lsim, cre, cim)
            cre = mre + end_ref[c, 8 * s:8 * s + 8, :]
            cim = mim + end_ref[HALF + c, 8 * s:8 * s + 8, :]
        carry_ref[c] = cre
        carry_ref[HALF + c] = cim
        sre_ref[:, c * LANE:(c + 1) * LANE] = cre[:n_seq]
        sim_ref[:, c * LANE:(c + 1) * LANE] = cim[:n_seq]
    for c in range(HALF):
        for b in range(n_seq):
            for j in range(groups):
                idx = pl.ds(j * 64 + b, 8, stride=8)
                _scan_fixup(p_ref, pw_ref, c, b * rows + j * GROUP_ROWS,
                            cin_ref[c, idx, :], cin_ref[HALF + c, idx, :])
    y = _readout(lhs, p_ref, kt_ref, wc_ref, d_ref)
    for t in range(N_PHASE):
        y_ref[t] = y[:, t * LANE:(t + 1) * LANE].reshape(n_seq, rows, LANE)


def _ssm_sample_body(u_ref, wb_ref, kt_ref, wc_ref, pw_ref, d_ref, h0re_ref, h0im_ref,
                     y_ref, sre_ref, sim_ref, e_ref, p_ref, *, rows):
    lhs = jnp.concatenate([u_ref[t] for t in range(N_PHASE)], axis=1)
    _state_increments(lhs, wb_ref, e_ref)
    for c in range(HALF):
        lsre, lsim = _mult(pw_ref, 1, c)
        for j in range(rows // GROUP_ROWS):
            lre, lim = _scan_local(e_ref, p_ref, pw_ref, c, j * GROUP_ROWS)
            cre = h0re_ref[8 * j:8 * j + 8, c * LANE:(c + 1) * LANE]
            cim = h0im_ref[8 * j:8 * j + 8, c * LANE:(c + 1) * LANE]
            mre, mim = _cmul(lsre, lsim, cre, cim)
            sre_ref[8 * j:8 * j + 8, c * LANE:(c + 1) * LANE] = mre + lre
            sim_ref[8 * j:8 * j + 8, c * LANE:(c + 1) * LANE] = mim + lim
            _scan_fixup(p_ref, pw_ref, c, j * GROUP_ROWS, cre, cim)
    y = _readout(lhs, p_ref, kt_ref, wc_ref, d_ref)
    for t in range(N_PHASE):
        y_ref[t] = y[:, t * LANE:(t + 1) * LANE]


def _ssm_body(u_ref, wb_ref, kt_ref, wc_ref, pw_ref, d_ref, us_ref, h0re_ref, h0im_ref,
              y_ref, sre_ref, sim_ref, ys_ref, sres_ref, sims_ref,
              e_ref, p_ref, end_ref, cin_ref, carry_ref, *, n_seq, rows, rows_s):
    step = pl.program_id(1)

    @pl.when(step == 0)
    def _():
        _ssm_sample_body(us_ref, wb_ref, kt_ref, wc_ref, pw_ref, d_ref, h0re_ref, h0im_ref,
                         ys_ref, sres_ref, sims_ref, e_ref, p_ref, rows=rows_s)

    @pl.when(step > 0)
    def _():
        _ssm_prompt_body(u_ref, wb_ref, kt_ref, wc_ref, pw_ref, d_ref, y_ref, sre_ref, sim_ref,
                         e_ref, p_ref, end_ref, cin_ref, carry_ref, n_seq=n_seq, rows=rows, tile=step - 1)


def _ssm(u8_p, n_p, u8_s, n_s, h0, wb, kt, wc, pw, d8):
    d = D_MODEL
    ns = SLAB_STATES
    rows_p, rows_s = u8_p.shape[1], u8_s.shape[1]
    per_seq = rows_p // n_p
    assert rows_s // n_s == SEG, "carried-state path scans one segment per sequence"
    assert n_p <= 8, "sequences ride the sublanes of the segment chain"
    rows = min(SSM_ROWS, per_seq)
    assert per_seq % rows == 0 and rows % GROUP_ROWS == 0 and rows_s % GROUP_ROWS == 0
    assert rows_s <= n_p * rows, "the carried-state step reuses the scan scratch"
    tiles = per_seq // rows
    slab = lambda q, i: (q, 0, 0)
    p_spec = pl.BlockSpec((N_PHASE, n_p, rows, LANE), lambda q, i: (0, 0, jnp.maximum(i - 1, 0), q))
    s_spec = pl.BlockSpec((N_PHASE, rows_s, LANE), lambda q, i: (0, 0, q))
    w_spec = pl.BlockSpec((None, 2 * ns, 2 * ns), slab)
    state_p = pl.BlockSpec((n_p, ns), lambda q, i: (0, q))
    state_s = pl.BlockSpec((n_s, ns), lambda q, i: (0, q))
    seg_rows = 8 * (rows // SEG)
    dims_p = (N_PHASE, n_p, per_seq, d)
    y_p, re_p, im_p, y_s, re_s, im_s = pl.pallas_call(
        functools.partial(_ssm_body, n_seq=n_p, rows=rows, rows_s=rows_s),
        grid=(N_SLAB, tiles + 1),
        in_specs=[p_spec, w_spec, w_spec, w_spec, pl.BlockSpec((None, 8, 2 * ns), slab),
                  pl.BlockSpec((None, 1, d), slab), s_spec, state_s, state_s],
        out_specs=[p_spec, state_p, state_p, s_spec, state_s, state_s],
        out_shape=[jax.ShapeDtypeStruct(dims_p, BF16),
                   jax.ShapeDtypeStruct((n_p, N_SLAB * ns), F32), jax.ShapeDtypeStruct((n_p, N_SLAB * ns), F32),
                   jax.ShapeDtypeStruct((N_PHASE, rows_s, d), BF16),
                   jax.ShapeDtypeStruct((n_s, N_SLAB * ns), F32), jax.ShapeDtypeStruct((n_s, N_SLAB * ns), F32)],
        scratch_shapes=[pltpu.VMEM((2 * HALF, n_p * rows, LANE), F32),
                        pltpu.VMEM((2 * HALF, n_p * rows, LANE), F32),
                        pltpu.VMEM((2 * HALF, seg_rows, LANE), F32),
                        pltpu.VMEM((2 * HALF, seg_rows, LANE), F32),
                        pltpu.VMEM((2 * HALF, 8, LANE), F32)],
        name="ssm",
        compiler_params=_params(("arbitrary", "arbitrary")),
    )(u8_p.reshape(dims_p), wb, kt, wc, pw, d8, u8_s, h0[0].reshape(n_s, -1), h0[1].reshape(n_s, -1))
    return (y_p.reshape(N_PHASE, rows_p, d), re_p, im_p), (y_s, re_s, im_s)


def _mix_out_ffn_body(y_ref, a_ref, gs_ref, x_ref, wglu_ref, wo_ref, g_ref, wg_ref, wu_ref, wd_ref, gf_ref,
                      o_ref, slab_ref, *, rows, final_norm):
    d = D_MODEL
    glu = _dot(jax.nn.gelu(_stack_phases(y_ref).astype(F32)).astype(BF16), wglu_ref[...])
    y_ssm = glu[:, :d] * jax.nn.sigmoid(glu[:, d:])
    merged = _stack_phases(a_ref).astype(F32) + _stack_phases(gs_ref).astype(F32) * y_ssm
    x2 = _stack_phases(x_ref) + _dot(merged.astype(BF16), wo_ref[...])
    y = _ffn_math(x2, g_ref, wg_ref, wu_ref, wd_ref)
    if final_norm:
        y = _rms(y, gf_ref[...])
    _store_token_major(y, o_ref, slab_ref, rows)


def _mix_out_ffn(y8, a8, gs8, xp, w_glu, w_o, g, wg, wu, wd, gf, final_norm):
    n_rows = xp.shape[1]
    d = D_MODEL
    rows = min(MIX_OUT_ROWS, n_rows)
    assert n_rows % rows == 0
    ph_spec = pl.BlockSpec((N_PHASE, rows, d), lambda i: (0, i, 0))
    weights = (w_glu, w_o, g, wg, wu, wd, gf)
    return pl.pallas_call(
        functools.partial(_mix_out_ffn_body, rows=rows, final_norm=final_norm),
        grid=(n_rows // rows,),
        in_specs=[ph_spec, ph_spec, ph_spec, ph_spec] + [_const_spec(w.shape) for w in weights],
        out_specs=pl.BlockSpec((N_PHASE * rows, d), lambda i: (i, 0)),
        out_shape=jax.ShapeDtypeStruct((N_PHASE * n_rows, d), F32),
        scratch_shapes=[pltpu.VMEM((d // LANE, N_PHASE * rows, LANE), F32)],
        name="mix_out_ffn",
        compiler_params=_params(("arbitrary",)),
    )(y8, a8, gs8, xp, *weights)


def kernel(x_prompt, x_sample, state_conv, state_ssm_re, state_ssm_im, norm_ffn1, w_ffn1_gate, w_ffn1_up, w_ffn1_down, norm_mix, w_in, w_conv, w_conv_out, ssm_lambda_re, ssm_lambda_im, ssm_log_step, ssm_b_re, ssm_b_im, ssm_c_re, ssm_c_im, ssm_d, w_glu, w_o, norm_ffn2, w_ffn2_gate, w_ffn2_up, w_ffn2_down, norm_final):
    depth, d = w_in.shape[0], w_in.shape[1]
    row = lambda v: v.reshape(1, -1)
    nf = row(norm_final)
    xt_p = x_prompt.reshape(-1, d)
    xt_s = x_sample.reshape(-1, d)
    n_p, n_s = x_prompt.shape[0], x_sample.shape[0]
    outs_p, outs_s = [], []
    for l in range(depth):
        last = l == depth - 1
        ssm_w, (f1g, f1u, f1d) = _ssm_prep(
            ssm_lambda_re[l], ssm_lambda_im[l], ssm_log_step[l], ssm_b_re[l], ssm_b_im[l], ssm_c_re[l],
            ssm_c_im[l], riders=(w_ffn1_gate[l], w_ffn1_up[l], w_ffn1_down[l]))
        ffn1 = (row(norm_ffn1[l]), f1g, f1u, f1d)
        xp_p, (win, wco, wgl, wo) = _ffn_to_phase(xt_p, *ffn1, riders=(w_in[l], w_conv_out[l], w_glu[l], w_o[l]))
        xp_s, _ = _ffn_to_phase(xt_s, *ffn1)
        d8 = jnp.tile(ssm_d[l].reshape(N_SLAB, 1, SLAB), (1, 1, N_PHASE))
        mix_w = (row(norm_mix[l]), win, w_conv[l], wco)
        u_p, a_p, gs_p, conv_p, ffn2_w = _mix_in(xp_p, None, *mix_w, n_p,
                                                 riders=(w_ffn2_gate[l], w_ffn2_up[l], w_ffn2_down[l]))
        u_s, a_s, gs_s, conv_s, _ = _mix_in(xp_s, state_conv[l], *mix_w, n_s)
        (y_p, re_p, im_p), (y_s, re_s, im_s) = _ssm(u_p, n_p, u_s, n_s, (state_ssm_re[l], state_ssm_im[l]),
                                                    *ssm_w, d8)
        tail = (wgl, wo, row(norm_ffn2[l]), *ffn2_w, nf, last)
        xt_p = _mix_out_ffn(y_p, a_p, gs_p, xp_p, *tail)
        xt_s = _mix_out_ffn(y_s, a_s, gs_s, xp_s, *tail)
        outs_p.append([conv_p, re_p.reshape(n_p, -1, STATE_DIM), im_p.reshape(n_p, -1, STATE_DIM)])
        outs_s.append([conv_s, re_s.reshape(n_s, -1, STATE_DIM), im_s.reshape(n_s, -1, STATE_DIM)])
    stack = lambda outs: tuple(jnp.stack(leaf) for leaf in zip(*outs))
    return (xt_p.reshape(x_prompt.shape), xt_s.reshape(x_sample.shape), *stack(outs_p), *stack(outs_s))
```

```python
import functools

import jax
import jax.numpy as jnp
from jax import lax
from jax.experimental import pallas as pl
from jax.experimental.pallas import tpu as pltpu

F32 = jnp.float32
BF16 = jnp.bfloat16

D_MODEL = 1024
N_PHASE = 8
LANE = 128
MXU_TILE = 256
SLAB = 128
N_SLAB = D_MODEL // SLAB
GROUPS_PER_SLAB = 8
STATE_DIM = 64
SLAB_STATES = GROUPS_PER_SLAB * STATE_DIM
SEG = 4
RMS_EPS = 1e-6
VMEM_LIMIT = 60 * 1024 * 1024

FFN_ROWS = 1024
MIX_IN_ROWS = 128
MIX_OUT_ROWS = 64
SSM_ROWS = 256


def _rms(x, g):
    return x * lax.rsqrt(jnp.mean(x * x, axis=-1, keepdims=True) + RMS_EPS) * g


def _dot(a, b):
    return jnp.dot(a, b, preferred_element_type=F32)


def _const_spec(shape):
    zeros = (0,) * len(shape)
    return pl.BlockSpec(shape, lambda *_: zeros, pipeline_mode=pl.Buffered(1))


def _params(semantics):
    return pltpu.CompilerParams(dimension_semantics=semantics, vmem_limit_bytes=VMEM_LIMIT)


def _stack_phases(ref):
    return jnp.concatenate([ref[t] for t in range(N_PHASE)], axis=0)


def _ffn_math(x, g_ref, wg_ref, wu_ref, wd_ref):
    h = _rms(x, g_ref[...]).astype(BF16)
    act = (jax.nn.silu(_dot(h, wg_ref[...])) * _dot(h, wu_ref[...])).astype(BF16)
    return x + 0.5 * _dot(act, wd_ref[...])


def _rider_specs(weights, n_steps, step=lambda i: i):
    in_specs, out_specs, out_shapes = [], [], []
    for w in weights:
        n_blocks = n_steps
        while w.shape[0] % (16 * n_blocks):
            assert n_blocks % 2 == 0
            n_blocks //= 2
        spec = pl.BlockSpec((w.shape[0] // n_blocks, w.shape[1]),
                            lambda *idx, rep=n_steps // n_blocks: (step(*idx) // rep, 0))
        in_specs.append(spec)
        out_specs.append(spec)
        out_shapes.append(jax.ShapeDtypeStruct(w.shape, BF16))
    return in_specs, out_specs, out_shapes


def _convert_riders(in_refs, out_refs):
    for w_ref, o_ref in zip(in_refs, out_refs):
        o_ref[...] = w_ref[...].astype(BF16)


def _ffn_to_phase_body(*refs, rows, n_riders):
    x_ref, g_ref, wg_ref, wu_ref, wd_ref = refs[:5]
    o_ref, slab_ref = refs[5 + n_riders], refs[-1]
    _convert_riders(refs[5:5 + n_riders], refs[6 + n_riders:6 + 2 * n_riders])
    y = _ffn_math(x_ref[...], g_ref, wg_ref, wu_ref, wd_ref)
    for lt in range(D_MODEL // LANE):
        slab_ref[lt] = y[:, lt * LANE:(lt + 1) * LANE]
        for t in range(N_PHASE):
            o_ref[t, :, lt * LANE:(lt + 1) * LANE] = slab_ref[lt, pl.ds(t, rows, stride=N_PHASE), :]


def _store_token_major(y, o_ref, slab_ref, rows):
    for lt in range(D_MODEL // LANE):
        for t in range(N_PHASE):
            slab_ref[lt, pl.ds(t, rows, stride=N_PHASE), :] = y[t * rows:(t + 1) * rows, lt * LANE:(lt + 1) * LANE]
        o_ref[:, lt * LANE:(lt + 1) * LANE] = slab_ref[lt]


def _ffn_to_phase(x, g, wg, wu, wd, riders=()):
    n_tok, d = x.shape
    n_rows = n_tok // N_PHASE
    d_ff = wg.shape[1]
    rows = FFN_ROWS // N_PHASE
    assert n_rows % rows == 0
    n_steps = n_rows // rows
    rider_in, rider_out, rider_shapes = _rider_specs(riders, n_steps)
    out = pl.pallas_call(
        functools.partial(_ffn_to_phase_body, rows=rows, n_riders=len(riders)),
        grid=(n_steps,),
        in_specs=[
            pl.BlockSpec((N_PHASE * rows, d), lambda i: (i, 0)),
            _const_spec((1, d)),
            _const_spec((d, d_ff)),
            _const_spec((d, d_ff)),
            _const_spec((d_ff, d)),
        ] + rider_in,
        out_specs=[pl.BlockSpec((N_PHASE, rows, d), lambda i: (0, i, 0))] + rider_out,
        out_shape=[jax.ShapeDtypeStruct((N_PHASE, n_rows, d), F32)] + rider_shapes,
        scratch_shapes=[pltpu.VMEM((d // LANE, N_PHASE * rows, LANE), F32)],
        name="ffn_to_phase",
        compiler_params=_params(("arbitrary",)),
    )(x, g, wg, wu, wd, *riders)
    return out[0], tuple(out[1:])


def _mix_in_compute(x_ref, g, win_ref, wcv, wco_ref, shift, rows):
    d = D_MODEL
    h = _rms(_stack_phases(x_ref), g).astype(BF16)

    def proj(k):
        return _dot(h, win_ref[:, k * d:(k + 1) * d])

    z = proj(1) * proj(0)
    z6 = z[6 * rows:7 * rows]
    z7 = z[7 * rows:8 * rows]
    s6 = shift(z6, 0)
    s7 = shift(z7, 1)
    z1 = jnp.concatenate([s7, z[:7 * rows]], axis=0)
    z2 = jnp.concatenate([s6, s7, z[:6 * rows]], axis=0)
    conv = wcv[2:3] * z + wcv[1:2] * z1 + wcv[0:1] * z2
    y_conv = _dot((proj(2) * conv).astype(BF16), wco_ref[...])
    u = proj(3)
    a = jax.nn.sigmoid(proj(4)) * y_conv
    gs = jax.nn.sigmoid(proj(5))
    return u, a, gs, z6, z7


def _store_phases(ref, val, rows):
    val = val.astype(ref.dtype)
    for t in range(N_PHASE):
        ref[t] = val[t * rows:(t + 1) * rows]


def _mix_in_prompt_body(*refs, rows, n_riders):
    x_ref, g_ref, win_ref, wcv_ref, wco_ref = refs[:5]
    u_ref, a_ref, gs_ref, z_ref = refs[5 + n_riders:9 + n_riders]
    carry_ref = refs[-1]
    _convert_riders(refs[5:5 + n_riders], refs[9 + n_riders:9 + 2 * n_riders])

    @pl.when(pl.program_id(1) == 0)
    def _():
        carry_ref[...] = jnp.zeros_like(carry_ref)

    row_id = lax.broadcasted_iota(jnp.int32, (rows, D_MODEL), 0)

    def shift(z, k):
        return jnp.where(row_id == 0, carry_ref[k:k + 1, :], pltpu.roll(z, 1, 0))

    u, a, gs, z6, z7 = _mix_in_compute(x_ref, g_ref[...], win_ref, wcv_ref[...], wco_ref, shift, rows)
    last6 = z6[rows - 1:rows]
    last7 = z7[rows - 1:rows]
    carry_ref[0:1, :] = last6
    carry_ref[1:2, :] = last7
    z_ref[0:1, :] = last6
    z_ref[1:2, :] = last7
    _store_phases(u_ref, u, rows)
    _store_phases(a_ref, a, rows)
    _store_phases(gs_ref, gs, rows)


def _mix_in_sample_body(x_ref, start_ref, g_ref, win_ref, wcv_ref, wco_ref,
                        u_ref, a_ref, gs_ref, z6_ref, z7_ref, slab_ref, *, rows, rows_per_seq):
    row_id = lax.broadcasted_iota(jnp.int32, (rows, D_MODEL), 0)

    def shift(z, k):
        return jnp.where(row_id % rows_per_seq == 0, start_ref[k], pltpu.roll(z, 1, 0))

    u, a, gs, z6, z7 = _mix_in_compute(x_ref, g_ref[...], win_ref, wcv_ref[...], wco_ref, shift, rows)
    for z, z_ref in ((z6, z6_ref), (z7, z7_ref)):
        for lt in range(D_MODEL // LANE):
            slab_ref[...] = z[:, lt * LANE:(lt + 1) * LANE]
            z_ref[:, lt * LANE:(lt + 1) * LANE] = slab_ref[pl.ds(rows_per_seq - 1, rows // rows_per_seq,
                                                                stride=rows_per_seq), :]
    _store_phases(u_ref, u, rows)
    _store_phases(a_ref, a, rows)
    _store_phases(gs_ref, gs, rows)


def _mix_in(xp, conv_prev, g, w_in, w_conv, w_conv_out, n_seq, riders=()):
    n_rows = xp.shape[1]
    d = D_MODEL
    rows_per_seq = n_rows // n_seq
    ph_shape = jax.ShapeDtypeStruct((N_PHASE, n_rows, d), BF16)
    weights = (g, w_in, w_conv, w_conv_out)
    weight_specs = [_const_spec(w.shape) for w in weights]
    if conv_prev is None:
        rows = min(MIX_IN_ROWS, rows_per_seq)
        assert rows_per_seq % rows == 0
        tiles = rows_per_seq // rows
        ph_spec = pl.BlockSpec((N_PHASE, rows, d), lambda b, i: (0, b * tiles + i, 0))
        last_spec = pl.BlockSpec((None, 2, d), lambda b, i: (b, 0, 0))
        rider_in, rider_out, rider_shapes = _rider_specs(riders, n_seq * tiles, lambda b, i: b * tiles + i)
        u8, a8, gs8, conv_state, *converted = pl.pallas_call(
            functools.partial(_mix_in_prompt_body, rows=rows, n_riders=len(riders)),
            grid=(n_seq, tiles),
            in_specs=[ph_spec] + weight_specs + rider_in,
            out_specs=[ph_spec, ph_spec, ph_spec, last_spec] + rider_out,
            out_shape=[ph_shape, ph_shape, ph_shape, jax.ShapeDtypeStruct((n_seq, 2, d), F32)] + rider_shapes,
            scratch_shapes=[pltpu.VMEM((8, d), F32)],
            name="mix_in_prompt",
            compiler_params=_params(("arbitrary", "arbitrary")),
        )(xp, *weights, *riders)
    else:
        assert not riders
        converted = []
        rows = n_rows
        start = jnp.repeat(jnp.swapaxes(conv_prev, 0, 1), rows_per_seq, axis=1)
        full = lambda shape: pl.BlockSpec(shape, lambda i: (0,) * len(shape))
        u8, a8, gs8, z6, z7 = pl.pallas_call(
            functools.partial(_mix_in_sample_body, rows=rows, rows_per_seq=rows_per_seq),
            grid=(1,),
            in_specs=[full((N_PHASE, rows, d)), full((2, rows, d))] + weight_specs,
            out_specs=[full((N_PHASE, rows, d))] * 3 + [full((n_seq, d))] * 2,
            out_shape=[ph_shape, ph_shape, ph_shape,
                       jax.ShapeDtypeStruct((n_seq, d), F32), jax.ShapeDtypeStruct((n_seq, d), F32)],
            scratch_shapes=[pltpu.VMEM((rows, LANE), F32)],
            name="mix_in_sample",
            compiler_params=_params(("arbitrary",)),
        )(xp, start, *weights)
        conv_state = jnp.stack([z6, z7], axis=1)
    return u8, a8, gs8, conv_state, tuple(converted)


def _split_bf16(a):
    hi = a.astype(BF16)
    return hi, (a - hi.astype(F32)).astype(BF16)


def _ssm_prep_body(*refs, n_riders):
    bbre_ref, bbim_ref, ctre_ref, ctim_ref, lre_ref, lim_ref, lstep_ref = refs[:7]
    wb_ref, kt_ref, wc_ref, pw_ref = refs[7 + n_riders:11 + n_riders]
    _convert_riders(refs[7:7 + n_riders], refs[11 + n_riders:])
    ns = SLAB_STATES
    row_group = lax.broadcasted_iota(jnp.int32, (SLAB, LANE), 0) // (SLAB // GROUPS_PER_SLAB)
    lane_half = lax.broadcasted_iota(jnp.int32, (SLAB, LANE), 1) // STATE_DIM

    def block_diag(ref):
        tiles = [jnp.where(row_group == 2 * k + lane_half, ref[...], 0.0) for k in range(GROUPS_PER_SLAB // 2)]
        return jnp.concatenate(tiles, axis=1)

    bbre, bbim = block_diag(bbre_ref), block_diag(bbim_ref)
    ctre, ctim = block_diag(ctre_ref), block_diag(ctim_ref)
    lre, lim = lre_ref[...], lim_ref[...]
    step = jnp.exp(lstep_ref[...])

    def lam_pow(n):
        mag = jnp.exp((n * lre) * step)
        ang = (n * lim) * step
        return mag * jnp.cos(ang), mag * jnp.sin(ang)

    l1re, l1im = lam_pow(1)
    den = lre * lre + lim * lim
    fre = ((l1re - 1.0) * lre + l1im * lim) / den
    fim = (l1im * lre - (l1re - 1.0) * lim) / den

    def dot_nt(a, b_split):
        nt = lambda p, q: lax.dot_general(p, q, (((1,), (1,)), ((), ())), preferred_element_type=F32)
        a_hi, a_lo = _split_bf16(a)
        b_hi, b_lo = b_split
        return nt(a_hi, b_hi) + nt(a_hi, b_lo) + nt(a_lo, b_hi)

    ctre_split = _split_bf16(ctre)
    ctim_split = _split_bf16(ctim)
    kt_ref[...] = jnp.zeros_like(kt_ref)
    for k in range(N_PHASE):
        pre, pim = lam_pow(k)
        gre = fre * pre - fim * pim
        gim = fre * pim + fim * pre
        are = bbre * gre - bbim * gim
        aim = bbre * gim + bbim * gre
        j = N_PHASE - 1 - k
        wb_ref[j * SLAB:(j + 1) * SLAB, 0:ns] = are.astype(BF16)
        wb_ref[j * SLAB:(j + 1) * SLAB, ns:2 * ns] = aim.astype(BF16)
        kk = (dot_nt(are, ctre_split) - dot_nt(aim, ctim_split)).astype(BF16)
        for jj in range(N_PHASE - k):
            t = jj + k
            kt_ref[jj * SLAB:(jj + 1) * SLAB, t * SLAB:(t + 1) * SLAB] = kk
    for t in range(N_PHASE):
        pre, pim = lam_pow(t + 1)
        cre = ctre * pre - ctim * pim
        cim = ctre * pim + ctim * pre
        wc_ref[0:ns, t * SLAB:(t + 1) * SLAB] = cre.T.astype(BF16)
        wc_ref[ns:2 * ns, t * SLAB:(t + 1) * SLAB] = (-cim).T.astype(BF16)
    pw_ref[...] = jnp.zeros_like(pw_ref)
    for r, n in enumerate((N_PHASE, N_PHASE * SEG)):
        pre, pim = lam_pow(n)
        pw_ref[r:r + 1, 0:ns] = pre
        pw_ref[r:r + 1, ns:2 * ns] = pim


def _ssm_prep(lam_re, lam_im, log_step, b_re, b_im, c_re, c_im, riders=()):
    ns = SLAB_STATES

    def twice(m):
        m = m.reshape(N_SLAB, SLAB, STATE_DIM)
        return jnp.concatenate([m, m], axis=-1)

    def vec(v):
        return v.reshape(N_SLAB, 1, ns)

    mats = [twice(jnp.transpose(b_re, (0, 2, 1))), twice(jnp.transpose(b_im, (0, 2, 1))), twice(c_re), twice(c_im)]
    vecs = [vec(lam_re), vec(lam_im), vec(jnp.broadcast_to(log_step[:, None], lam_re.shape))]
    w_shape = jax.ShapeDtypeStruct((N_SLAB, 2 * ns, 2 * ns), BF16)
    w_spec = pl.BlockSpec((None, 2 * ns, 2 * ns), lambda q: (q, 0, 0))
    rider_in, rider_out, rider_shapes = _rider_specs(riders, N_SLAB)
    out = pl.pallas_call(
        functools.partial(_ssm_prep_body, n_riders=len(riders)),
        grid=(N_SLAB,),
        in_specs=[pl.BlockSpec((None, SLAB, 2 * STATE_DIM), lambda q: (q, 0, 0))] * 4
        + [pl.BlockSpec((None, 1, ns), lambda q: (q, 0, 0))] * 3 + rider_in,
        out_specs=[w_spec, w_spec, w_spec, pl.BlockSpec((None, 8, 2 * ns), lambda q: (q, 0, 0))] + rider_out,
        out_shape=[w_shape, w_shape, w_shape, jax.ShapeDtypeStruct((N_SLAB, 8, 2 * ns), F32)] + rider_shapes,
        name="ssm_prep",
        compiler_params=_params(("arbitrary",)),
    )(*mats, *vecs, *riders)
    return tuple(out[:4]), tuple(out[4:])


def _cmul(are, aim, bre, bim):
    return are * bre - aim * bim, are * bim + aim * bre


HALF = SLAB_STATES // LANE
GROUP_ROWS = 8 * SEG


def _mult(pw_ref, r, c):
    return (pw_ref[r:r + 1, c * LANE:(c + 1) * LANE],
            pw_ref[r:r + 1, (HALF + c) * LANE:(HALF + c + 1) * LANE])


def _state_increments(lhs, wb_ref, e_ref):
    e = _dot(lhs, wb_ref[...])
    for lt in range(2 * HALF):
        e_ref[lt, :e.shape[0]] = e[:, lt * LANE:(lt + 1) * LANE]


def _scan_local(e_ref, p_ref, pw_ref, c, base):
    l8re, l8im = _mult(pw_ref, 0, c)
    idx = pl.ds(base, 8, stride=SEG)
    lre = e_ref[c, idx, :]
    lim = e_ref[HALF + c, idx, :]
    for i in range(1, SEG):
        idx = pl.ds(base + i, 8, stride=SEG)
        p_ref[c, idx, :] = lre
        p_ref[HALF + c, idx, :] = lim
        mre, mim = _cmul(l8re, l8im, lre, lim)
        lre = mre + e_ref[c, idx, :]
        lim = mim + e_ref[HALF + c, idx, :]
    return lre, lim


def _scan_fixup(p_ref, pw_ref, c, base, cre, cim):
    l8re, l8im = _mult(pw_ref, 0, c)
    idx = pl.ds(base, 8, stride=SEG)
    p_ref[c, idx, :] = cre
    p_ref[HALF + c, idx, :] = cim
    for i in range(1, SEG):
        idx = pl.ds(base + i, 8, stride=SEG)
        cre, cim = _cmul(l8re, l8im, cre, cim)
        p_ref[c, idx, :] = p_ref[c, idx, :] + cre
        p_ref[HALF + c, idx, :] = p_ref[HALF + c, idx, :] + cim


def _readout(lhs, p_ref, kt_ref, wc_ref, d_ref):
    prev = jnp.concatenate([p_ref[lt, :lhs.shape[0]] for lt in range(2 * HALF)], axis=1)
    toep = jnp.concatenate(
        [_dot(lhs[:, :(k + 1) * MXU_TILE], kt_ref[:(k + 1) * MXU_TILE, k * MXU_TILE:(k + 1) * MXU_TILE])
         for k in range(N_PHASE * LANE // MXU_TILE)], axis=1)
    y = toep + _dot(prev.astype(BF16), wc_ref[...]) + d_ref[...] * lhs.astype(F32)
    return y.astype(BF16)


def _ssm_prompt_body(u_ref, wb_ref, kt_ref, wc_ref, pw_ref, d_ref, y_ref, sre_ref, sim_ref,
                     e_ref, p_ref, end_ref, cin_ref, carry_ref, *, n_seq, rows, tile):
    total = n_seq * rows
    lhs = jnp.concatenate([u_ref[t].reshape(total, LANE) for t in range(N_PHASE)], axis=1)
    _state_increments(lhs, wb_ref, e_ref)

    @pl.when((pl.program_id(0) == 0) & (tile == 0))
    def _():
        end_ref[...] = jnp.zeros_like(end_ref)

    @pl.when(tile == 0)
    def _():
        carry_ref[...] = jnp.zeros_like(carry_ref)

    groups = rows // GROUP_ROWS
    for c in range(HALF):
        for b in range(n_seq):
            for j in range(groups):
                lre, lim = _scan_local(e_ref, p_ref, pw_ref, c, b * rows + j * GROUP_ROWS)
                idx = pl.ds(j * 64 + b, 8, stride=8)
                end_ref[c, idx, :] = lre
                end_ref[HALF + c, idx, :] = lim
    for c in range(HALF):
        lsre, lsim = _mult(pw_ref, 1, c)
        cre = carry_ref[c]
        cim = carry_ref[HALF + c]
        for s in range(rows // SEG):
            cin_ref[c, 8 * s:8 * s + 8, :] = cre
            cin_ref[HALF + c, 8 * s:8 * s + 8, :] = cim
            mre, mim = _cmul(lsre, lsim, cre, cim)
            cre = mre + end_ref[c, 8 * s:8 * s + 8, :]
            cim = mim + end_ref[HALF + c, 8 * s:8 * s + 8, :]
        carry_ref[c] = cre
        carry_ref[HALF + c] = cim
        sre_ref[:, c * LANE:(c + 1) * LANE] = cre[:n_seq]
        sim_ref[:, c * LANE:(c + 1) * LANE] = cim[:n_seq]
    for c in range(HALF):
        for b in range(n_seq):
            for j in range(groups):
                idx = pl.ds(j * 64 + b, 8, stride=8)
                _scan_fixup(p_ref, pw_ref, c, b * rows + j * GROUP_ROWS,
                            cin_ref[c, idx, :], cin_ref[HALF + c, idx, :])
    y = _readout(lhs, p_ref, kt_ref, wc_ref, d_ref)
    for t in range(N_PHASE):
        y_ref[t] = y[:, t * LANE:(t + 1) * LANE].reshape(n_seq, rows, LANE)


def _ssm_sample_body(u_ref, wb_ref, kt_ref, wc_ref, pw_ref, d_ref, h0re_ref, h0im_ref,
                     y_ref, sre_ref, sim_ref, e_ref, p_ref, *, rows):
    lhs = jnp.concatenate([u_ref[t] for t in range(N_PHASE)], axis=1)
    _state_increments(lhs, wb_ref, e_ref)
    for c in range(HALF):
        lsre, lsim = _mult(pw_ref, 1, c)
        for j in range(rows // GROUP_ROWS):
            lre, lim = _scan_local(e_ref, p_ref, pw_ref, c, j * GROUP_ROWS)
            cre = h0re_ref[8 * j:8 * j + 8, c * LANE:(c + 1) * LANE]
            cim = h0im_ref[8 * j:8 * j + 8, c * LANE:(c + 1) * LANE]
            mre, mim = _cmul(lsre, lsim, cre, cim)
            sre_ref[8 * j:8 * j + 8, c * LANE:(c + 1) * LANE] = mre + lre
            sim_ref[8 * j:8 * j + 8, c * LANE:(c + 1) * LANE] = mim + lim
            _scan_fixup(p_ref, pw_ref, c, j * GROUP_ROWS, cre, cim)
    y = _readout(lhs, p_ref, kt_ref, wc_ref, d_ref)
    for t in range(N_PHASE):
        y_ref[t] = y[:, t * LANE:(t + 1) * LANE]


def _ssm_body(u_ref, wb_ref, kt_ref, wc_ref, pw_ref, d_ref, us_ref, h0re_ref, h0im_ref,
              y_ref, sre_ref, sim_ref, ys_ref, sres_ref, sims_ref,
              e_ref, p_ref, end_ref, cin_ref, carry_ref, *, n_seq, rows, rows_s):
    step = pl.program_id(1)

    @pl.when(step == 0)
    def _():
        _ssm_sample_body(us_ref, wb_ref, kt_ref, wc_ref, pw_ref, d_ref, h0re_ref, h0im_ref,
                         ys_ref, sres_ref, sims_ref, e_ref, p_ref, rows=rows_s)

    @pl.when(step > 0)
    def _():
        _ssm_prompt_body(u_ref, wb_ref, kt_ref, wc_ref, pw_ref, d_ref, y_ref, sre_ref, sim_ref,
                         e_ref, p_ref, end_ref, cin_ref, carry_ref, n_seq=n_seq, rows=rows, tile=step - 1)


def _ssm(u8_p, n_p, u8_s, n_s, h0, wb, kt, wc, pw, d8):
    d = D_MODEL
    ns = SLAB_STATES
    rows_p, rows_s = u8_p.shape[1], u8_s.shape[1]
    per_seq = rows_p // n_p
    assert rows_s // n_s == SEG, "carried-state path scans one segment per sequence"
    assert n_p <= 8, "sequences ride the sublanes of the segment chain"
    rows = min(SSM_ROWS, per_seq)
    assert per_seq % rows == 0 and rows % GROUP_ROWS == 0 and rows_s % GROUP_ROWS == 0
    assert rows_s <= n_p * rows, "the carried-state step reuses the scan scratch"
    tiles = per_seq // rows
    slab = lambda q, i: (q, 0, 0)
    p_spec = pl.BlockSpec((N_PHASE, n_p, rows, LANE), lambda q, i: (0, 0, jnp.maximum(i - 1, 0), q))
    s_spec = pl.BlockSpec((N_PHASE, rows_s, LANE), lambda q, i: (0, 0, q))
    w_spec = pl.BlockSpec((None, 2 * ns, 2 * ns), slab)
    state_p = pl.BlockSpec((n_p, ns), lambda q, i: (0, q))
    state_s = pl.BlockSpec((n_s, ns), lambda q, i: (0, q))
    seg_rows = 8 * (rows // SEG)
    dims_p = (N_PHASE, n_p, per_seq, d)
    y_p, re_p, im_p, y_s, re_s, im_s = pl.pallas_call(
        functools.partial(_ssm_body, n_seq=n_p, rows=rows, rows_s=rows_s),
        grid=(N_SLAB, tiles + 1),
        in_specs=[p_spec, w_spec, w_spec, w_spec, pl.BlockSpec((None, 8, 2 * ns), slab),
                  pl.BlockSpec((None, 1, d), slab), s_spec, state_s, state_s],
        out_specs=[p_spec, state_p, state_p, s_spec, state_s, state_s],
        out_shape=[jax.ShapeDtypeStruct(dims_p, BF16),
                   jax.ShapeDtypeStruct((n_p, N_SLAB * ns), F32), jax.ShapeDtypeStruct((n_p, N_SLAB * ns), F32),
                   jax.ShapeDtypeStruct((N_PHASE, rows_s, d), BF16),
                   jax.ShapeDtypeStruct((n_s, N_SLAB * ns), F32), jax.ShapeDtypeStruct((n_s, N_SLAB * ns), F32)],
        scratch_shapes=[pltpu.VMEM((2 * HALF, n_p * rows, LANE), F32),
                        pltpu.VMEM((2 * HALF, n_p * rows, LANE), F32),
                        pltpu.VMEM((2 * HALF, seg_rows, LANE), F32),
                        pltpu.VMEM((2 * HALF, seg_rows, LANE), F32),
                        pltpu.VMEM((2 * HALF, 8, LANE), F32)],
        name="ssm",
        compiler_params=_params(("arbitrary", "arbitrary")),
    )(u8_p.reshape(dims_p), wb, kt, wc, pw, d8, u8_s, h0[0].reshape(n_s, -1), h0[1].reshape(n_s, -1))
    return (y_p.reshape(N_PHASE, rows_p, d), re_p, im_p), (y_s, re_s, im_s)


def _mix_out_ffn_body(y_ref, a_ref, gs_ref, x_ref, wglu_ref, wo_ref, g_ref, wg_ref, wu_ref, wd_ref, gf_ref,
                      o_ref, slab_ref, *, rows, final_norm):
    d = D_MODEL
    glu = _dot(jax.nn.gelu(_stack_phases(y_ref).astype(F32)).astype(BF16), wglu_ref[...])
    y_ssm = glu[:, :d] * jax.nn.sigmoid(glu[:, d:])
    merged = _stack_phases(a_ref).astype(F32) + _stack_phases(gs_ref).astype(F32) * y_ssm
    x2 = _stack_phases(x_ref) + _dot(merged.astype(BF16), wo_ref[...])
    y = _ffn_math(x2, g_ref, wg_ref, wu_ref, wd_ref)
    if final_norm:
        y = _rms(y, gf_ref[...])
    _store_token_major(y, o_ref, slab_ref, rows)


def _mix_out_ffn(y8, a8, gs8, xp, w_glu, w_o, g, wg, wu, wd, gf, final_norm):
    n_rows = xp.shape[1]
    d = D_MODEL
    rows = min(MIX_OUT_ROWS, n_rows)
    assert n_rows % rows == 0
    ph_spec = pl.BlockSpec((N_PHASE, rows, d), lambda i: (0, i, 0))
    weights = (w_glu, w_o, g, wg, wu, wd, gf)
    return pl.pallas_call(
        functools.partial(_mix_out_ffn_body, rows=rows, final_norm=final_norm),
        grid=(n_rows // rows,),
        in_specs=[ph_spec, ph_spec, ph_spec, ph_spec] + [_const_spec(w.shape) for w in weights],
        out_specs=pl.BlockSpec((N_PHASE * rows, d), lambda i: (i, 0)),
        out_shape=jax.ShapeDtypeStruct((N_PHASE * n_rows, d), F32),
        scratch_shapes=[pltpu.VMEM((d // LANE, N_PHASE * rows, LANE), F32)],
        name="mix_out_ffn",
        compiler_params=_params(("arbitrary",)),
    )(y8, a8, gs8, xp, *weights)


def kernel(x_prompt, x_sample, state_conv, state_ssm_re, state_ssm_im, norm_ffn1, w_ffn1_gate, w_ffn1_up, w_ffn1_down, norm_mix, w_in, w_conv, w_conv_out, ssm_lambda_re, ssm_lambda_im, ssm_log_step, ssm_b_re, ssm_b_im, ssm_c_re, ssm_c_im, ssm_d, w_glu, w_o, norm_ffn2, w_ffn2_gate, w_ffn2_up, w_ffn2_down, norm_final):
    depth, d = w_in.shape[0], w_in.shape[1]
    row = lambda v: v.reshape(1, -1)
    nf = row(norm_final)
    xt_p = x_prompt.reshape(-1, d)
    xt_s = x_sample.reshape(-1, d)
    n_p, n_s = x_prompt.shape[0], x_sample.shape[0]
    outs_p, outs_s = [], []
    for l in range(depth):
        last = l == depth - 1
        ssm_w, (f1g, f1u, f1d) = _ssm_prep(
            ssm_lambda_re[l], ssm_lambda_im[l], ssm_log_step[l], ssm_b_re[l], ssm_b_im[l], ssm_c_re[l],
            ssm_c_im[l], riders=(w_ffn1_gate[l], w_ffn1_up[l], w_ffn1_down[l]))
        ffn1 = (row(norm_ffn1[l]), f1g, f1u, f1d)
        xp_p, (win, wco, wgl, wo) = _ffn_to_phase(xt_p, *ffn1, riders=(w_in[l], w_conv_out[l], w_glu[l], w_o[l]))
        xp_s, _ = _ffn_to_phase(xt_s, *ffn1)
        d8 = jnp.tile(ssm_d[l].reshape(N_SLAB, 1, SLAB), (1, 1, N_PHASE))
        mix_w = (row(norm_mix[l]), win, w_conv[l], wco)
        u_p, a_p, gs_p, conv_p, ffn2_w = _mix_in(xp_p, None, *mix_w, n_p,
                                                 riders=(w_ffn2_gate[l], w_ffn2_up[l], w_ffn2_down[l]))
        u_s, a_s, gs_s, conv_s, _ = _mix_in(xp_s, state_conv[l], *mix_w, n_s)
        (y_p, re_p, im_p), (y_s, re_s, im_s) = _ssm(u_p, n_p, u_s, n_s, (state_ssm_re[l], state_ssm_im[l]),
                                                    *ssm_w, d8)
        tail = (wgl, wo, row(norm_ffn2[l]), *ffn2_w, nf, last)
        xt_p = _mix_out_ffn(y_p, a_p, gs_p, xp_p, *tail)
        xt_s = _mix_out_ffn(y_s, a_s, gs_s, xp_s, *tail)
        outs_p.append([conv_p, re_p.reshape(n_p, -1, STATE_DIM), im_p.reshape(n_p, -1, STATE_DIM)])
        outs_s.append([conv_s, re_s.reshape(n_s, -1, STATE_DIM), im_s.reshape(n_s, -1, STATE_DIM)])
    stack = lambda outs: tuple(jnp.stack(leaf) for leaf in zip(*outs))
    return (xt_p.reshape(x_prompt.shape), xt_s.reshape(x_sample.shape), *stack(outs_p), *stack(outs_s))
```

```python
import functools

import jax
import jax.numpy as jnp
from jax import lax
from jax.experimental import pallas as pl
from jax.experimental.pallas import tpu as pltpu

F32 = jnp.float32
BF16 = jnp.bfloat16

D_MODEL = 1024
N_PHASE = 8
LANE = 128
MXU_TILE = 256
SLAB = 128
N_SLAB = D_MODEL // SLAB
GROUPS_PER_SLAB = 8
STATE_DIM = 64
SLAB_STATES = GROUPS_PER_SLAB * STATE_DIM
SEG = 4
RMS_EPS = 1e-6
VMEM_LIMIT = 60 * 1024 * 1024

FFN_ROWS = 1024
MIX_IN_ROWS = 128
MIX_OUT_ROWS = 64
SSM_ROWS = 256


def _rms(x, g):
    return x * lax.rsqrt(jnp.mean(x * x, axis=-1, keepdims=True) + RMS_EPS) * g


def _dot(a, b):
    return jnp.dot(a, b, preferred_element_type=F32)


def _const_spec(shape):
    zeros = (0,) * len(shape)
    return pl.BlockSpec(shape, lambda *_: zeros, pipeline_mode=pl.Buffered(1))


def _params(semantics):
    return pltpu.CompilerParams(dimension_semantics=semantics, vmem_limit_bytes=VMEM_LIMIT)


def _stack_phases(ref):
    return jnp.concatenate([ref[t] for t in range(N_PHASE)], axis=0)


def _ffn_math(x, g_ref, wg_ref, wu_ref, wd_ref):
    h = _rms(x, g_ref[...]).astype(BF16)
    act = (jax.nn.silu(_dot(h, wg_ref[...])) * _dot(h, wu_ref[...])).astype(BF16)
    return x + 0.5 * _dot(act, wd_ref[...])


def _rider_specs(weights, n_steps, step=lambda i: i):
    in_specs, out_specs, out_shapes = [], [], []
    for w in weights:
        n_blocks = n_steps
        while w.shape[0] % (16 * n_blocks):
            assert n_blocks % 2 == 0
            n_blocks //= 2
        spec = pl.BlockSpec((w.shape[0] // n_blocks, w.shape[1]),
                            lambda *idx, rep=n_steps // n_blocks: (step(*idx) // rep, 0))
        in_specs.append(spec)
        out_specs.append(spec)
        out_shapes.append(jax.ShapeDtypeStruct(w.shape, BF16))
    return in_specs, out_specs, out_shapes


def _convert_riders(in_refs, out_refs):
    for w_ref, o_ref in zip(in_refs, out_refs):
        o_ref[...] = w_ref[...].astype(BF16)


def _ffn_to_phase_body(*refs, rows, n_riders):
    x_ref, g_ref, wg_ref, wu_ref, wd_ref = refs[:5]
    o_ref, slab_ref = refs[5 + n_riders], refs[-1]
    _convert_riders(refs[5:5 + n_riders], refs[6 + n_riders:6 + 2 * n_riders])
    y = _ffn_math(x_ref[...], g_ref, wg_ref, wu_ref, wd_ref)
    for lt in range(D_MODEL // LANE):
        slab_ref[lt] = y[:, lt * LANE:(lt + 1) * LANE]
        for t in range(N_PHASE):
            o_ref[t, :, lt * LANE:(lt + 1) * LANE] = slab_ref[lt, pl.ds(t, rows, stride=N_PHASE), :]


def _ffn_to_phase(x, g, wg, wu, wd, riders=()):
    n_tok, d = x.shape
    n_rows = n_tok // N_PHASE
    d_ff = wg.shape[1]
    rows = FFN_ROWS // N_PHASE
    assert n_rows % rows == 0
    n_steps = n_rows // rows
    rider_in, rider_out, rider_shapes = _rider_specs(riders, n_steps)
    out = pl.pallas_call(
        functools.partial(_ffn_to_phase_body, rows=rows, n_riders=len(riders)),
        grid=(n_steps,),
        in_specs=[
            pl.BlockSpec((N_PHASE * rows, d), lambda i: (i, 0)),
            _const_spec((1, d)),
            _const_spec((d, d_ff)),
            _const_spec((d, d_ff)),
            _const_spec((d_ff, d)),
        ] + rider_in,
        out_specs=[pl.BlockSpec((N_PHASE, rows, d), lambda i: (0, i, 0))] + rider_out,
        out_shape=[jax.ShapeDtypeStruct((N_PHASE, n_rows, d), F32)] + rider_shapes,
        scratch_shapes=[pltpu.VMEM((d // LANE, N_PHASE * rows, LANE), F32)],
        name="ffn_to_phase",
        compiler_params=_params(("arbitrary",)),
    )(x, g, wg, wu, wd, *riders)
    return out[0], tuple(out[1:])


def _mix_in_compute(x_ref, g, win_ref, wcv, wco_ref, shift, rows):
    d = D_MODEL
    h = _rms(_stack_phases(x_ref), g).astype(BF16)

    def proj(k):
        return _dot(h, win_ref[:, k * d:(k + 1) * d])

    z = proj(1) * proj(0)
    z6 = z[6 * rows:7 * rows]
    z7 = z[7 * rows:8 * rows]
    s6 = shift(z6, 0)
    s7 = shift(z7, 1)
    z1 = jnp.concatenate([s7, z[:7 * rows]], axis=0)
    z2 = jnp.concatenate([s6, s7, z[:6 * rows]], axis=0)
    conv = wcv[2:3] * z + wcv[1:2] * z1 + wcv[0:1] * z2
    y_conv = _dot((proj(2) * conv).astype(BF16), wco_ref[...])
    u = proj(3)
    a = jax.nn.sigmoid(proj(4)) * y_conv
    gs = jax.nn.sigmoid(proj(5))
    return u, a, gs, z6, z7


def _store_phases(ref, val, rows):
    val = val.astype(ref.dtype)
    for t in range(N_PHASE):
        ref[t] = val[t * rows:(t + 1) * rows]


def _mix_in_prompt_body(*refs, rows, n_riders):
    x_ref, g_ref, win_ref, wcv_ref, wco_ref = refs[:5]
    u_ref, a_ref, gs_ref, z_ref = refs[5 + n_riders:9 + n_riders]
    carry_ref = refs[-1]
    _convert_riders(refs[5:5 + n_riders], refs[9 + n_riders:9 + 2 * n_riders])

    @pl.when(pl.program_id(1) == 0)
    def _():
        carry_ref[...] = jnp.zeros_like(carry_ref)

    row_id = lax.broadcasted_iota(jnp.int32, (rows, D_MODEL), 0)

    def shift(z, k):
        return jnp.where(row_id == 0, carry_ref[k:k + 1, :], pltpu.roll(z, 1, 0))

    u, a, gs, z6, z7 = _mix_in_compute(x_ref, g_ref[...], win_ref, wcv_ref[...], wco_ref, shift, rows)
    last6 = z6[rows - 1:rows]
    last7 = z7[rows - 1:rows]
    carry_ref[0:1, :] = last6
    carry_ref[1:2, :] = last7
    z_ref[0:1, :] = last6
    z_ref[1:2, :] = last7
    _store_phases(u_ref, u, rows)
    _store_phases(a_ref, a, rows)
    _store_phases(gs_ref, gs, rows)


def _mix_in_sample_body(x_ref, start_ref, g_ref, win_ref, wcv_ref, wco_ref,
                        u_ref, a_ref, gs_ref, z6_ref, z7_ref, slab_ref, *, rows, rows_per_seq):
    row_id = lax.broadcasted_iota(jnp.int32, (rows, D_MODEL), 0)

    def shift(z, k):
        return jnp.where(row_id % rows_per_seq == 0, start_ref[k], pltpu.roll(z, 1, 0))

    u, a, gs, z6, z7 = _mix_in_compute(x_ref, g_ref[...], win_ref, wcv_ref[...], wco_ref, shift, rows)
    for z, z_ref in ((z6, z6_ref), (z7, z7_ref)):
        for lt in range(D_MODEL // LANE):
            slab_ref[...] = z[:, lt * LANE:(lt + 1) * LANE]
            z_ref[:, lt * LANE:(lt + 1) * LANE] = slab_ref[pl.ds(rows_per_seq - 1, rows // rows_per_seq,
                                                                stride=rows_per_seq), :]
    _store_phases(u_ref, u, rows)
    _store_phases(a_ref, a, rows)
    _store_phases(gs_ref, gs, rows)


def _mix_in(xp, conv_prev, g, w_in, w_conv, w_conv_out, n_seq, riders=()):
    n_rows = xp.shape[1]
    d = D_MODEL
    rows_per_seq = n_rows // n_seq
    ph_shape = jax.ShapeDtypeStruct((N_PHASE, n_rows, d), BF16)
    weights = (g, w_in, w_conv, w_conv_out)
    weight_specs = [_const_spec(w.shape) for w in weights]
    if conv_prev is None:
        rows = min(MIX_IN_ROWS, rows_per_seq)
        assert rows_per_seq % rows == 0
        tiles = rows_per_seq // rows
        ph_spec = pl.BlockSpec((N_PHASE, rows, d), lambda b, i: (0, b * tiles + i, 0))
        last_spec = pl.BlockSpec((None, 2, d), lambda b, i: (b, 0, 0))
        rider_in, rider_out, rider_shapes = _rider_specs(riders, n_seq * tiles, lambda b, i: b * tiles + i)
        u8, a8, gs8, conv_state, *converted = pl.pallas_call(
            functools.partial(_mix_in_prompt_body, rows=rows, n_riders=len(riders)),
            grid=(n_seq, tiles),
            in_specs=[ph_spec] + weight_specs + rider_in,
            out_specs=[ph_spec, ph_spec, ph_spec, last_spec] + rider_out,
            out_shape=[ph_shape, ph_shape, ph_shape, jax.ShapeDtypeStruct((n_seq, 2, d), F32)] + rider_shapes,
            scratch_shapes=[pltpu.VMEM((8, d), F32)],
            name="mix_in_prompt",
            compiler_params=_params(("arbitrary", "arbitrary")),
        )(xp, *weights, *riders)
    else:
        assert not riders
        converted = []
        rows = n_rows
        start = jnp.repeat(jnp.swapaxes(conv_prev, 0, 1), rows_per_seq, axis=1)
        full = lambda shape: pl.BlockSpec(shape, lambda i: (0,) * len(shape))
        u8, a8, gs8, z6, z7 = pl.pallas_call(
            functools.partial(_mix_in_sample_body, rows=rows, rows_per_seq=rows_per_seq),
            grid=(1,),
            in_specs=[full((N_PHASE, rows, d)), full((2, rows, d))] + weight_specs,
            out_specs=[full((N_PHASE, rows, d))] * 3 + [full((n_seq, d))] * 2,
            out_shape=[ph_shape, ph_shape, ph_shape,
                       jax.ShapeDtypeStruct((n_seq, d), F32), jax.ShapeDtypeStruct((n_seq, d), F32)],
            scratch_shapes=[pltpu.VMEM((rows, LANE), F32)],
            name="mix_in_sample",
            compiler_params=_params(("arbitrary",)),
        )(xp, start, *weights)
        conv_state = jnp.stack([z6, z7], axis=1)
    return u8, a8, gs8, conv_state, tuple(converted)


def _split_bf16(a):
    hi = a.astype(BF16)
    return hi, (a - hi.astype(F32)).astype(BF16)


def _ssm_prep_body(*refs, n_riders):
    bbre_ref, bbim_ref, ctre_ref, ctim_ref, lre_ref, lim_ref, lstep_ref = refs[:7]
    wb_ref, kt_ref, wc_ref, pw_ref = refs[7 + n_riders:11 + n_riders]
    _convert_riders(refs[7:7 + n_riders], refs[11 + n_riders:])
    ns = SLAB_STATES
    row_group = lax.broadcasted_iota(jnp.int32, (SLAB, LANE), 0) // (SLAB // GROUPS_PER_SLAB)
    lane_half = lax.broadcasted_iota(jnp.int32, (SLAB, LANE), 1) // STATE_DIM

    def block_diag(ref):
        tiles = [jnp.where(row_group == 2 * k + lane_half, ref[...], 0.0) for k in range(GROUPS_PER_SLAB // 2)]
        return jnp.concatenate(tiles, axis=1)

    bbre, bbim = block_diag(bbre_ref), block_diag(bbim_ref)
    ctre, ctim = block_diag(ctre_ref), block_diag(ctim_ref)
    lre, lim = lre_ref[...], lim_ref[...]
    step = jnp.exp(lstep_ref[...])

    def lam_pow(n):
        mag = jnp.exp((n * lre) * step)
        ang = (n * lim) * step
        return mag * jnp.cos(ang), mag * jnp.sin(ang)

    l1re, l1im = lam_pow(1)
    den = lre * lre + lim * lim
    fre = ((l1re - 1.0) * lre + l1im * lim) / den
    fim = (l1im * lre - (l1re - 1.0) * lim) / den

    def dot_nt(a, b_split):
        nt = lambda p, q: lax.dot_general(p, q, (((1,), (1,)), ((), ())), preferred_element_type=F32)
        a_hi, a_lo = _split_bf16(a)
        b_hi, b_lo = b_split
        return nt(a_hi, b_hi) + nt(a_hi, b_lo) + nt(a_lo, b_hi)

    ctre_split = _split_bf16(ctre)
    ctim_split = _split_bf16(ctim)
    kt_ref[...] = jnp.zeros_like(kt_ref)
    for k in range(N_PHASE):
        pre, pim = lam_pow(k)
        gre = fre * pre - fim * pim
        gim = fre * pim + fim * pre
        are = bbre * gre - bbim * gim
        aim = bbre * gim + bbim * gre
        j = N_PHASE - 1 - k
        wb_ref[j * SLAB:(j + 1) * SLAB, 0:ns] = are.astype(BF16)
        wb_ref[j * SLAB:(j + 1) * SLAB, ns:2 * ns] = aim.astype(BF16)
        kk = (dot_nt(are, ctre_split) - dot_nt(aim, ctim_split)).astype(BF16)
        for jj in range(N_PHASE - k):
            t = jj + k
            kt_ref[jj * SLAB:(jj + 1) * SLAB, t * SLAB:(t + 1) * SLAB] = kk
    for t in range(N_PHASE):
        pre, pim = lam_pow(t + 1)
        cre = ctre * pre - ctim * pim
        cim = ctre * pim + ctim * pre
        wc_ref[0:ns, t * SLAB:(t + 1) * SLAB] = cre.T.astype(BF16)
        wc_ref[ns:2 * ns, t * SLAB:(t + 1) * SLAB] = (-cim).T.astype(BF16)
    pw_ref[...] = jnp.zeros_like(pw_ref)
    for r, n in enumerate((N_PHASE, N_PHASE * SEG)):
        pre, pim = lam_pow(n)
        pw_ref[r:r + 1, 0:ns] = pre
        pw_ref[r:r + 1, ns:2 * ns] = pim


def _ssm_prep(lam_re, lam_im, log_step, b_re, b_im, c_re, c_im, riders=()):
    ns = SLAB_STATES

    def twice(m):
        m = m.reshape(N_SLAB, SLAB, STATE_DIM)
        return jnp.concatenate([m, m], axis=-1)

    def vec(v):
        return v.reshape(N_SLAB, 1, ns)

    mats = [twice(jnp.transpose(b_re, (0, 2, 1))), twice(jnp.transpose(b_im, (0, 2, 1))), twice(c_re), twice(c_im)]
    vecs = [vec(lam_re), vec(lam_im), vec(jnp.broadcast_to(log_step[:, None], lam_re.shape))]
    w_shape = jax.ShapeDtypeStruct((N_SLAB, 2 * ns, 2 * ns), BF16)
    w_spec = pl.BlockSpec((None, 2 * ns, 2 * ns), lambda q: (q, 0, 0))
    rider_in, rider_out, rider_shapes = _rider_specs(riders, N_SLAB)
    out = pl.pallas_call(
        functools.partial(_ssm_prep_body, n_riders=len(riders)),
        grid=(N_SLAB,),
        in_specs=[pl.BlockSpec((None, SLAB, 2 * STATE_DIM), lambda q: (q, 0, 0))] * 4
        + [pl.BlockSpec((None, 1, ns), lambda q: (q, 0, 0))] * 3 + rider_in,
        out_specs=[w_spec, w_spec, w_spec, pl.BlockSpec((None, 8, 2 * ns), lambda q: (q, 0, 0))] + rider_out,
        out_shape=[w_shape, w_shape, w_shape, jax.ShapeDtypeStruct((N_SLAB, 8, 2 * ns), F32)] + rider_shapes,
        name="ssm_prep",
        compiler_params=_params(("arbitrary",)),
    )(*mats, *vecs, *riders)
    return tuple(out[:4]), tuple(out[4:])


def _cmul(are, aim, bre, bim):
    return are * bre - aim * bim, are * bim + aim * bre


HALF = SLAB_STATES // LANE
GROUP_ROWS = 8 * SEG


def _mult(pw_ref, r, c):
    return (pw_ref[r:r + 1, c * LANE:(c + 1) * LANE],
            pw_ref[r:r + 1, (HALF + c) * LANE:(HALF + c + 1) * LANE])


def _state_increments(lhs, wb_ref, e_ref):
    e = _dot(lhs, wb_ref[...])
    for lt in range(2 * HALF):
        e_ref[lt, :e.shape[0]] = e[:, lt * LANE:(lt + 1) * LANE]


def _scan_local(e_ref, p_ref, pw_ref, c, base):
    l8re, l8im = _mult(pw_ref, 0, c)
    idx = pl.ds(base, 8, stride=SEG)
    lre = e_ref[c, idx, :]
    lim = e_ref[HALF + c, idx, :]
    for i in range(1, SEG):
        idx = pl.ds(base + i, 8, stride=SEG)
        p_ref[c, idx, :] = lre
        p_ref[HALF + c, idx, :] = lim
        mre, mim = _cmul(l8re, l8im, lre, lim)
        lre = mre + e_ref[c, idx, :]
        lim = mim + e_ref[HALF + c, idx, :]
    return lre, lim


def _scan_fixup(p_ref, pw_ref, c, base, cre, cim):
    l8re, l8im = _mult(pw_ref, 0, c)
    idx = pl.ds(base, 8, stride=SEG)
    p_ref[c, idx, :] = cre
    p_ref[HALF + c, idx, :] = cim
    for i in range(1, SEG):
        idx = pl.ds(base + i, 8, stride=SEG)
        cre, cim = _cmul(l8re, l8im, cre, cim)
        p_ref[c, idx, :] = p_ref[c, idx, :] + cre
        p_ref[HALF + c, idx, :] = p_ref[HALF + c, idx, :] + cim


def _readout(lhs, p_ref, kt_ref, wc_ref, d_ref):
    prev = jnp.concatenate([p_ref[lt, :lhs.shape[0]] for lt in range(2 * HALF)], axis=1)
    toep = jnp.concatenate(
        [_dot(lhs[:, :(k + 1) * MXU_TILE], kt_ref[:(k + 1) * MXU_TILE, k * MXU_TILE:(k + 1) * MXU_TILE])
         for k in range(N_PHASE * LANE // MXU_TILE)], axis=1)
    y = toep + _dot(prev.astype(BF16), wc_ref[...]) + d_ref[...] * lhs.astype(F32)
    return y.astype(BF16)


def _ssm_prompt_body(u_ref, wb_ref, kt_ref, wc_ref, pw_ref, d_ref, y_ref, sre_ref, sim_ref,
                     e_ref, p_ref, end_ref, cin_ref, carry_ref, *, n_seq, rows, tile):
    total = n_seq * rows
    lhs = jnp.concatenate([u_ref[t].reshape(total, LANE) for t in range(N_PHASE)], axis=1)
    _state_increments(lhs, wb_ref, e_ref)

    @pl.when((pl.program_id(0) == 0) & (tile == 0))
    def _():
        end_ref[...] = jnp.zeros_like(end_ref)

    @pl.when(tile == 0)
    def _():
        carry_ref[...] = jnp.zeros_like(carry_ref)

    groups = rows // GROUP_ROWS
    for c in range(HALF):
        for b in range(n_seq):
            for j in range(groups):
                lre, lim = _scan_local(e_ref, p_ref, pw_ref, c, b * rows + j * GROUP_ROWS)
                idx = pl.ds(j * 64 + b, 8, stride=8)
                end_ref[c, idx, :] = lre
                end_ref[HALF + c, idx, :] = lim
    for c in range(HALF):
        lsre, lsim = _mult(pw_ref, 1, c)
        cre = carry_ref[c]
        cim = carry_ref[HALF + c]
        for s in range(rows // SEG):
            cin_ref[c, 8 * s:8 * s + 8, :] = cre
            cin_ref[HALF + c, 8 * s:8 * s + 8, :] = cim
            mre, mim = _cmul(lsre, lsim, cre, cim)
            cre = mre + end_ref[c, 8 * s:8 * s + 8, :]
            cim = mim + end_ref[HALF + c, 8 * s:8 * s + 8, :]
        carry_ref[c] = cre
        carry_ref[HALF + c] = cim
        sre_ref[:, c * LANE:(c + 1) * LANE] = cre[:n_seq]
        sim_ref[:, c * LANE:(c + 1) * LANE] = cim[:n_seq]
    for c in range(HALF):
        for b in range(n_seq):
            for j in range(groups):
                idx = pl.ds(j * 64 + b, 8, stride=8)
                _scan_fixup(p_ref, pw_ref, c, b * rows + j * GROUP_ROWS,
                            cin_ref[c, idx, :], cin_ref[HALF + c, idx, :])
    y = _readout(lhs, p_ref, kt_ref, wc_ref, d_ref)
    for t in range(N_PHASE):
        y_ref[t] = y[:, t * LANE:(t + 1) * LANE].reshape(n_seq, rows, LANE)


def _ssm_sample_body(u_ref, wb_ref, kt_ref, wc_ref, pw_ref, d_ref, h0re_ref, h0im_ref,
                     y_ref, sre_ref, sim_ref, e_ref, p_ref, *, rows):
    lhs = jnp.concatenate([u_ref[t] for t in range(N_PHASE)], axis=1)
    _state_increments(lhs, wb_ref, e_ref)
    for c in range(HALF):
        lsre, lsim = _mult(pw_ref, 1, c)
        for j in range(rows // GROUP_ROWS):
            lre, lim = _scan_local(e_ref, p_ref, pw_ref, c, j * GROUP_ROWS)
            cre = h0re_ref[8 * j:8 * j + 8, c * LANE:(c + 1) * LANE]
            cim = h0im_ref[8 * j:8 * j + 8, c * LANE:(c + 1) * LANE]
            mre, mim = _cmul(lsre, lsim, cre, cim)
            sre_ref[8 * j:8 * j + 8, c * LANE:(c + 1) * LANE] = mre + lre
            sim_ref[8 * j:8 * j + 8, c * LANE:(c + 1) * LANE] = mim + lim
            _scan_fixup(p_ref, pw_ref, c, j * GROUP_ROWS, cre, cim)
    y = _readout(lhs, p_ref, kt_ref, wc_ref, d_ref)
    for t in range(N_PHASE):
        y_ref[t] = y[:, t * LANE:(t + 1) * LANE]


def _ssm_body(u_ref, wb_ref, kt_ref, wc_ref, pw_ref, d_ref, us_ref, h0re_ref, h0im_ref,
              y_ref, sre_ref, sim_ref, ys_ref, sres_ref, sims_ref,
              e_ref, p_ref, end_ref, cin_ref, carry_ref, *, n_seq, rows, rows_s):
    step = pl.program_id(1)

    @pl.when(step == 0)
    def _():
        _ssm_sample_body(us_ref, wb_ref, kt_ref, wc_ref, pw_ref, d_ref, h0re_ref, h0im_ref,
                         ys_ref, sres_ref, sims_ref, e_ref, p_ref, rows=rows_s)

    @pl.when(step > 0)
    def _():
        _ssm_prompt_body(u_ref, wb_ref, kt_ref, wc_ref, pw_ref, d_ref, y_ref, sre_ref, sim_ref,
                         e_ref, p_ref, end_ref, cin_ref, carry_ref, n_seq=n_seq, rows=rows, tile=step - 1)


def _ssm(u8_p, n_p, u8_s, n_s, h0, wb, kt, wc, pw, d8):
    d = D_MODEL
    ns = SLAB_STATES
    rows_p, rows_s = u8_p.shape[1], u8_s.shape[1]
    per_seq = rows_p // n_p
    assert rows_s // n_s == SEG, "carried-state path scans one segment per sequence"
    assert n_p <= 8, "sequences ride the sublanes of the segment chain"
    rows = min(SSM_ROWS, per_seq)
    assert per_seq % rows == 0 and rows % GROUP_ROWS == 0 and rows_s % GROUP_ROWS == 0
    assert rows_s <= n_p * rows, "the carried-state step reuses the scan scratch"
    tiles = per_seq // rows
    slab = lambda q, i: (q, 0, 0)
    p_spec = pl.BlockSpec((N_PHASE, n_p, rows, LANE), lambda q, i: (0, 0, jnp.maximum(i - 1, 0), q))
    s_spec = pl.BlockSpec((N_PHASE, rows_s, LANE), lambda q, i: (0, 0, q))
    w_spec = pl.BlockSpec((None, 2 * ns, 2 * ns), slab)
    state_p = pl.BlockSpec((n_p, ns), lambda q, i: (0, q))
    state_s = pl.BlockSpec((n_s, ns), lambda q, i: (0, q))
    seg_rows = 8 * (rows // SEG)
    dims_p = (N_PHASE, n_p, per_seq, d)
    y_p, re_p, im_p, y_s, re_s, im_s = pl.pallas_call(
        functools.partial(_ssm_body, n_seq=n_p, rows=rows, rows_s=rows_s),
        grid=(N_SLAB, tiles + 1),
        in_specs=[p_spec, w_spec, w_spec, w_spec, pl.BlockSpec((None, 8, 2 * ns), slab),
                  pl.BlockSpec((None, 1, d), slab), s_spec, state_s, state_s],
        out_specs=[p_spec, state_p, state_p, s_spec, state_s, state_s],
        out_shape=[jax.ShapeDtypeStruct(dims_p, BF16),
                   jax.ShapeDtypeStruct((n_p, N_SLAB * ns), F32), jax.ShapeDtypeStruct((n_p, N_SLAB * ns), F32),
                   jax.ShapeDtypeStruct((N_PHASE, rows_s, d), BF16),
                   jax.ShapeDtypeStruct((n_s, N_SLAB * ns), F32), jax.ShapeDtypeStruct((n_s, N_SLAB * ns), F32)],
        scratch_shapes=[pltpu.VMEM((2 * HALF, n_p * rows, LANE), F32),
                        pltpu.VMEM((2 * HALF, n_p * rows, LANE), F32),
                        pltpu.VMEM((2 * HALF, seg_rows, LANE), F32),
                        pltpu.VMEM((2 * HALF, seg_rows, LANE), F32),
                        pltpu.VMEM((2 * HALF, 8, LANE), F32)],
        name="ssm",
        compiler_params=_params(("arbitrary", "arbitrary")),
    )(u8_p.reshape(dims_p), wb, kt, wc, pw, d8, u8_s, h0[0].reshape(n_s, -1), h0[1].reshape(n_s, -1))
    return (y_p.reshape(N_PHASE, rows_p, d), re_p, im_p), (y_s, re_s, im_s)


def _mix_out_ffn_body(y_ref, a_ref, gs_ref, x_ref, wglu_ref, wo_ref, g_ref, wg_ref, wu_ref, wd_ref, gf_ref,
                      o_hbm, ybuf_ref, sem, *, rows, n_steps, final_norm):
    d = D_MODEL
    step = pl.program_id(0)
    slot = step % 2

    def copies(of_step, of_slot):
        return [pltpu.make_async_copy(ybuf_ref.at[of_slot, t], o_hbm.at[pl.ds(of_step * rows, rows), t, :],
                                      sem.at[of_slot]) for t in range(N_PHASE)]

    @pl.when(step >= 2)
    def _():
        for c in copies(step - 2, slot):
            c.wait()

    glu = _dot(jax.nn.gelu(_stack_phases(y_ref).astype(F32)).astype(BF16), wglu_ref[...])
    y_ssm = glu[:, :d] * jax.nn.sigmoid(glu[:, d:])
    merged = _stack_phases(a_ref).astype(F32) + _stack_phases(gs_ref).astype(F32) * y_ssm
    x2 = _stack_phases(x_ref) + _dot(merged.astype(BF16), wo_ref[...])
    y = _ffn_math(x2, g_ref, wg_ref, wu_ref, wd_ref)
    if final_norm:
        y = _rms(y, gf_ref[...])
    for t in range(N_PHASE):
        ybuf_ref[slot, t] = y[t * rows:(t + 1) * rows]
    for c in copies(step, slot):
        c.start()

    @pl.when(step == n_steps - 1)
    def _():
        if n_steps > 1:
            for c in copies(step - 1, 1 - slot):
                c.wait()
        for c in copies(step, slot):
            c.wait()


def _mix_out_ffn(y8, a8, gs8, xp, w_glu, w_o, g, wg, wu, wd, gf, final_norm):
    n_rows = xp.shape[1]
    d = D_MODEL
    rows = min(MIX_OUT_ROWS, n_rows)
    assert n_rows % rows == 0
    ph_spec = pl.BlockSpec((N_PHASE, rows, d), lambda i: (0, i, 0))
    weights = (w_glu, w_o, g, wg, wu, wd, gf)
    n_steps = n_rows // rows
    return pl.pallas_call(
        functools.partial(_mix_out_ffn_body, rows=rows, n_steps=n_steps, final_norm=final_norm),
        grid=(n_steps,),
        in_specs=[ph_spec, ph_spec, ph_spec, ph_spec] + [_const_spec(w.shape) for w in weights],
        out_specs=pl.BlockSpec(memory_space=pl.ANY),
        out_shape=jax.ShapeDtypeStruct((n_rows, N_PHASE, d), F32),
        scratch_shapes=[pltpu.VMEM((2, N_PHASE, rows, d), F32), pltpu.SemaphoreType.DMA((2,))],
        name="mix_out_ffn",
        compiler_params=_params(("arbitrary",)),
    )(y8, a8, gs8, xp, *weights).reshape(N_PHASE * n_rows, d)


def kernel(x_prompt, x_sample, state_conv, state_ssm_re, state_ssm_im, norm_ffn1, w_ffn1_gate, w_ffn1_up, w_ffn1_down, norm_mix, w_in, w_conv, w_conv_out, ssm_lambda_re, ssm_lambda_im, ssm_log_step, ssm_b_re, ssm_b_im, ssm_c_re, ssm_c_im, ssm_d, w_glu, w_o, norm_ffn2, w_ffn2_gate, w_ffn2_up, w_ffn2_down, norm_final):
    depth, d = w_in.shape[0], w_in.shape[1]
    row = lambda v: v.reshape(1, -1)
    nf = row(norm_final)
    xt_p = x_prompt.reshape(-1, d)
    xt_s = x_sample.reshape(-1, d)
    n_p, n_s = x_prompt.shape[0], x_sample.shape[0]
    outs_p, outs_s = [], []
    for l in range(depth):
        last = l == depth - 1
        ssm_w, (f1g, f1u, f1d) = _ssm_prep(
            ssm_lambda_re[l], ssm_lambda_im[l], ssm_log_step[l], ssm_b_re[l], ssm_b_im[l], ssm_c_re[l],
            ssm_c_im[l], riders=(w_ffn1_gate[l], w_ffn1_up[l], w_ffn1_down[l]))
        ffn1 = (row(norm_ffn1[l]), f1g, f1u, f1d)
        xp_p, (win, wco, wgl, wo) = _ffn_to_phase(xt_p, *ffn1, riders=(w_in[l], w_conv_out[l], w_glu[l], w_o[l]))
        xp_s, _ = _ffn_to_phase(xt_s, *ffn1)
        d8 = jnp.tile(ssm_d[l].reshape(N_SLAB, 1, SLAB), (1, 1, N_PHASE))
        mix_w = (row(norm_mix[l]), win, w_conv[l], wco)
        u_p, a_p, gs_p, conv_p, ffn2_w = _mix_in(xp_p, None, *mix_w, n_p,
                                                 riders=(w_ffn2_gate[l], w_ffn2_up[l], w_ffn2_down[l]))
        u_s, a_s, gs_s, conv_s, _ = _mix_in(xp_s, state_conv[l], *mix_w, n_s)
        (y_p, re_p, im_p), (y_s, re_s, im_s) = _ssm(u_p, n_p, u_s, n_s, (state_ssm_re[l], state_ssm_im[l]),
                                                    *ssm_w, d8)
        tail = (wgl, wo, row(norm_ffn2[l]), *ffn2_w, nf, last)
        xt_p = _mix_out_ffn(y_p, a_p, gs_p, xp_p, *tail)
        xt_s = _mix_out_ffn(y_s, a_s, gs_s, xp_s, *tail)
        outs_p.append([conv_p, re_p.reshape(n_p, -1, STATE_DIM), im_p.reshape(n_p, -1, STATE_DIM)])
        outs_s.append([conv_s, re_s.reshape(n_s, -1, STATE_DIM), im_s.reshape(n_s, -1, STATE_DIM)])
    stack = lambda outs: tuple(jnp.stack(leaf) for leaf in zip(*outs))
    return (xt_p.reshape(x_prompt.shape), xt_s.reshape(x_sample.shape), *stack(outs_p), *stack(outs_s))
```

```python
import functools

import jax
import jax.numpy as jnp
from jax import lax
from jax.experimental import pallas as pl
from jax.experimental.pallas import tpu as pltpu

F32 = jnp.float32
BF16 = jnp.bfloat16

D_MODEL = 1024
N_PHASE = 8
LANE = 128
MXU_TILE = 256
SLAB = 128
N_SLAB = D_MODEL // SLAB
GROUPS_PER_SLAB = 8
STATE_DIM = 64
SLAB_STATES = GROUPS_PER_SLAB * STATE_DIM
SEG = 4
RMS_EPS = 1e-6
VMEM_LIMIT = 60 * 1024 * 1024

FFN_ROWS = 1024
MIX_IN_ROWS = 128
MIX_OUT_ROWS = 64
SSM_ROWS = 256


def _rms(x, g):
    return x * lax.rsqrt(jnp.mean(x * x, axis=-1, keepdims=True) + RMS_EPS) * g


def _dot(a, b):
    return jnp.dot(a, b, preferred_element_type=F32)


def _const_spec(shape):
    zeros = (0,) * len(shape)
    return pl.BlockSpec(shape, lambda *_: zeros, pipeline_mode=pl.Buffered(1))


def _params(semantics):
    return pltpu.CompilerParams(dimension_semantics=semantics, vmem_limit_bytes=VMEM_LIMIT)


def _stack_phases(ref):
    return jnp.concatenate([ref[t] for t in range(N_PHASE)], axis=0)


def _ffn_math(x, g_ref, wg_ref, wu_ref, wd_ref):
    h = _rms(x, g_ref[...]).astype(BF16)
    act = (jax.nn.silu(_dot(h, wg_ref[...])) * _dot(h, wu_ref[...])).astype(BF16)
    return x + 0.5 * _dot(act, wd_ref[...])


def _rider_specs(weights, n_steps, step=lambda i: i):
    in_specs, out_specs, out_shapes = [], [], []
    for w in weights:
        n_blocks = n_steps
        while w.shape[0] % (16 * n_blocks):
            assert n_blocks % 2 == 0
            n_blocks //= 2
        spec = pl.BlockSpec((w.shape[0] // n_blocks, w.shape[1]),
                            lambda *idx, rep=n_steps // n_blocks: (step(*idx) // rep, 0))
        in_specs.append(spec)
        out_specs.append(spec)
        out_shapes.append(jax.ShapeDtypeStruct(w.shape, BF16))
    return in_specs, out_specs, out_shapes


def _convert_riders(in_refs, out_refs):
    for w_ref, o_ref in zip(in_refs, out_refs):
        o_ref[...] = w_ref[...].astype(BF16)


def _ffn_to_phase_body(*refs, rows, n_steps, n_riders):
    x_hbm, g_ref, wg_ref, wu_ref, wd_ref = refs[:5]
    o_ref, xbuf_ref, sem = refs[5 + n_riders], refs[-2], refs[-1]
    step = pl.program_id(0)
    slot = step % 2

    def copies(of_step, of_slot):
        return [pltpu.make_async_copy(x_hbm.at[pl.ds(of_step * rows, rows), t, :], xbuf_ref.at[of_slot, t],
                                      sem.at[of_slot]) for t in range(N_PHASE)]

    @pl.when(step == 0)
    def _():
        for c in copies(step, slot):
            c.start()

    @pl.when(step + 1 < n_steps)
    def _():
        for c in copies(step + 1, 1 - slot):
            c.start()

    for c in copies(step, slot):
        c.wait()
    _convert_riders(refs[5:5 + n_riders], refs[6 + n_riders:6 + 2 * n_riders])
    x = jnp.concatenate([xbuf_ref[slot, t] for t in range(N_PHASE)], axis=0)
    y = _ffn_math(x, g_ref, wg_ref, wu_ref, wd_ref)
    for t in range(N_PHASE):
        o_ref[t] = y[t * rows:(t + 1) * rows]


def _ffn_to_phase(x, g, wg, wu, wd, riders=()):
    n_tok, d = x.shape
    n_rows = n_tok // N_PHASE
    d_ff = wg.shape[1]
    rows = FFN_ROWS // N_PHASE
    assert n_rows % rows == 0
    n_steps = n_rows // rows
    rider_in, rider_out, rider_shapes = _rider_specs(riders, n_steps)
    out = pl.pallas_call(
        functools.partial(_ffn_to_phase_body, rows=rows, n_steps=n_steps, n_riders=len(riders)),
        grid=(n_steps,),
        in_specs=[
            pl.BlockSpec(memory_space=pl.ANY),
            _const_spec((1, d)),
            _const_spec((d, d_ff)),
            _const_spec((d, d_ff)),
            _const_spec((d_ff, d)),
        ] + rider_in,
        out_specs=[pl.BlockSpec((N_PHASE, rows, d), lambda i: (0, i, 0))] + rider_out,
        out_shape=[jax.ShapeDtypeStruct((N_PHASE, n_rows, d), F32)] + rider_shapes,
        scratch_shapes=[pltpu.VMEM((2, N_PHASE, rows, d), F32), pltpu.SemaphoreType.DMA((2,))],
        name="ffn_to_phase",
        compiler_params=_params(("arbitrary",)),
    )(x.reshape(n_rows, N_PHASE, d), g, wg, wu, wd, *riders)
    return out[0], tuple(out[1:])


def _mix_in_compute(x_ref, g, win_ref, wcv, wco_ref, shift, rows):
    d = D_MODEL
    h = _rms(_stack_phases(x_ref), g).astype(BF16)

    def proj(k):
        return _dot(h, win_ref[:, k * d:(k + 1) * d])

    z = proj(1) * proj(0)
    z6 = z[6 * rows:7 * rows]
    z7 = z[7 * rows:8 * rows]
    s6 = shift(z6, 0)
    s7 = shift(z7, 1)
    z1 = jnp.concatenate([s7, z[:7 * rows]], axis=0)
    z2 = jnp.concatenate([s6, s7, z[:6 * rows]], axis=0)
    conv = wcv[2:3] * z + wcv[1:2] * z1 + wcv[0:1] * z2
    y_conv = _dot((proj(2) * conv).astype(BF16), wco_ref[...])
    u = proj(3)
    a = jax.nn.sigmoid(proj(4)) * y_conv
    gs = jax.nn.sigmoid(proj(5))
    return u, a, gs, z6, z7


def _store_phases(ref, val, rows):
    val = val.astype(ref.dtype)
    for t in range(N_PHASE):
        ref[t] = val[t * rows:(t + 1) * rows]


def _mix_in_prompt_body(*refs, rows, n_riders):
    x_ref, g_ref, win_ref, wcv_ref, wco_ref = refs[:5]
    u_ref, a_ref, gs_ref, z_ref = refs[5 + n_riders:9 + n_riders]
    carry_ref = refs[-1]
    _convert_riders(refs[5:5 + n_riders], refs[9 + n_riders:9 + 2 * n_riders])

    @pl.when(pl.program_id(1) == 0)
    def _():
        carry_ref[...] = jnp.zeros_like(carry_ref)

    row_id = lax.broadcasted_iota(jnp.int32, (rows, D_MODEL), 0)

    def shift(z, k):
        return jnp.where(row_id == 0, carry_ref[k:k + 1, :], pltpu.roll(z, 1, 0))

    u, a, gs, z6, z7 = _mix_in_compute(x_ref, g_ref[...], win_ref, wcv_ref[...], wco_ref, shift, rows)
    last6 = z6[rows - 1:rows]
    last7 = z7[rows - 1:rows]
    carry_ref[0:1, :] = last6
    carry_ref[1:2, :] = last7
    z_ref[0:1, :] = last6
    z_ref[1:2, :] = last7
    _store_phases(u_ref, u, rows)
    _store_phases(a_ref, a, rows)
    _store_phases(gs_ref, gs, rows)


def _mix_in_sample_body(x_ref, start_ref, g_ref, win_ref, wcv_ref, wco_ref,
                        u_ref, a_ref, gs_ref, z6_ref, z7_ref, slab_ref, *, rows, rows_per_seq):
    row_id = lax.broadcasted_iota(jnp.int32, (rows, D_MODEL), 0)

    def shift(z, k):
        return jnp.where(row_id % rows_per_seq == 0, start_ref[k], pltpu.roll(z, 1, 0))

    u, a, gs, z6, z7 = _mix_in_compute(x_ref, g_ref[...], win_ref, wcv_ref[...], wco_ref, shift, rows)
    for z, z_ref in ((z6, z6_ref), (z7, z7_ref)):
        for lt in range(D_MODEL // LANE):
            slab_ref[...] = z[:, lt * LANE:(lt + 1) * LANE]
            z_ref[:, lt * LANE:(lt + 1) * LANE] = slab_ref[pl.ds(rows_per_seq - 1, rows // rows_per_seq,
                                                                stride=rows_per_seq), :]
    _store_phases(u_ref, u, rows)
    _store_phases(a_ref, a, rows)
    _store_phases(gs_ref, gs, rows)


def _mix_in(xp, conv_prev, g, w_in, w_conv, w_conv_out, n_seq, riders=()):
    n_rows = xp.shape[1]
    d = D_MODEL
    rows_per_seq = n_rows // n_seq
    ph_shape = jax.ShapeDtypeStruct((N_PHASE, n_rows, d), BF16)
    weights = (g, w_in, w_conv, w_conv_out)
    weight_specs = [_const_spec(w.shape) for w in weights]
    if conv_prev is None:
        rows = min(MIX_IN_ROWS, rows_per_seq)
        assert rows_per_seq % rows == 0
        tiles = rows_per_seq // rows
        ph_spec = pl.BlockSpec((N_PHASE, rows, d), lambda b, i: (0, b * tiles + i, 0))
        last_spec = pl.BlockSpec((None, 2, d), lambda b, i: (b, 0, 0))
        rider_in, rider_out, rider_shapes = _rider_specs(riders, n_seq * tiles, lambda b, i: b * tiles + i)
        u8, a8, gs8, conv_state, *converted = pl.pallas_call(
            functools.partial(_mix_in_prompt_body, rows=rows, n_riders=len(riders)),
            grid=(n_seq, tiles),
            in_specs=[ph_spec] + weight_specs + rider_in,
            out_specs=[ph_spec, ph_spec, ph_spec, last_spec] + rider_out,
            out_shape=[ph_shape, ph_shape, ph_shape, jax.ShapeDtypeStruct((n_seq, 2, d), F32)] + rider_shapes,
            scratch_shapes=[pltpu.VMEM((8, d), F32)],
            name="mix_in_prompt",
            compiler_params=_params(("arbitrary", "arbitrary")),
        )(xp, *weights, *riders)
    else:
        assert not riders
        converted = []
        rows = n_rows
        start = jnp.repeat(jnp.swapaxes(conv_prev, 0, 1), rows_per_seq, axis=1)
        full = lambda shape: pl.BlockSpec(shape, lambda i: (0,) * len(shape))
        u8, a8, gs8, z6, z7 = pl.pallas_call(
            functools.partial(_mix_in_sample_body, rows=rows, rows_per_seq=rows_per_seq),
            grid=(1,),
            in_specs=[full((N_PHASE, rows, d)), full((2, rows, d))] + weight_specs,
            out_specs=[full((N_PHASE, rows, d))] * 3 + [full((n_seq, d))] * 2,
            out_shape=[ph_shape, ph_shape, ph_shape,
                       jax.ShapeDtypeStruct((n_seq, d), F32), jax.ShapeDtypeStruct((n_seq, d), F32)],
            scratch_shapes=[pltpu.VMEM((rows, LANE), F32)],
            name="mix_in_sample",
            compiler_params=_params(("arbitrary",)),
        )(xp, start, *weights)
        conv_state = jnp.stack([z6, z7], axis=1)
    return u8, a8, gs8, conv_state, tuple(converted)


def _split_bf16(a):
    hi = a.astype(BF16)
    return hi, (a - hi.astype(F32)).astype(BF16)


def _ssm_prep_body(*refs, n_riders):
    bbre_ref, bbim_ref, ctre_ref, ctim_ref, lre_ref, lim_ref, lstep_ref = refs[:7]
    wb_ref, kt_ref, wc_ref, pw_ref = refs[7 + n_riders:11 + n_riders]
    _convert_riders(refs[7:7 + n_riders], refs[11 + n_riders:])
    ns = SLAB_STATES
    row_group = lax.broadcasted_iota(jnp.int32, (SLAB, LANE), 0) // (SLAB // GROUPS_PER_SLAB)
    lane_half = lax.broadcasted_iota(jnp.int32, (SLAB, LANE), 1) // STATE_DIM

    def block_diag(ref):
        tiles = [jnp.where(row_group == 2 * k + lane_half, ref[...], 0.0) for k in range(GROUPS_PER_SLAB // 2)]
        return jnp.concatenate(tiles, axis=1)

    bbre, bbim = block_diag(bbre_ref), block_diag(bbim_ref)
    ctre, ctim = block_diag(ctre_ref), block_diag(ctim_ref)
    lre, lim = lre_ref[...], lim_ref[...]
    step = jnp.exp(lstep_ref[...])

    def lam_pow(n):
        mag = jnp.exp((n * lre) * step)
        ang = (n * lim) * step
        return mag * jnp.cos(ang), mag * jnp.sin(ang)

    l1re, l1im = lam_pow(1)
    den = lre * lre + lim * lim
    fre = ((l1re - 1.0) * lre + l1im * lim) / den
    fim = (l1im * lre - (l1re - 1.0) * lim) / den

    def dot_nt(a, b_split):
        nt = lambda p, q: lax.dot_general(p, q, (((1,), (1,)), ((), ())), preferred_element_type=F32)
        a_hi, a_lo = _split_bf16(a)
        b_hi, b_lo = b_split
        return nt(a_hi, b_hi) + nt(a_hi, b_lo) + nt(a_lo, b_hi)

    ctre_split = _split_bf16(ctre)
    ctim_split = _split_bf16(ctim)
    kt_ref[...] = jnp.zeros_like(kt_ref)
    for k in range(N_PHASE):
        pre, pim = lam_pow(k)
        gre = fre * pre - fim * pim
        gim = fre * pim + fim * pre
        are = bbre * gre - bbim * gim
        aim = bbre * gim + bbim * gre
        j = N_PHASE - 1 - k
        wb_ref[j * SLAB:(j + 1) * SLAB, 0:ns] = are.astype(BF16)
        wb_ref[j * SLAB:(j + 1) * SLAB, ns:2 * ns] = aim.astype(BF16)
        kk = (dot_nt(are, ctre_split) - dot_nt(aim, ctim_split)).astype(BF16)
        for jj in range(N_PHASE - k):
            t = jj + k
            kt_ref[jj * SLAB:(jj + 1) * SLAB, t * SLAB:(t + 1) * SLAB] = kk
    for t in range(N_PHASE):
        pre, pim = lam_pow(t + 1)
        cre = ctre * pre - ctim * pim
        cim = ctre * pim + ctim * pre
        wc_ref[0:ns, t * SLAB:(t + 1) * SLAB] = cre.T.astype(BF16)
        wc_ref[ns:2 * ns, t * SLAB:(t + 1) * SLAB] = (-cim).T.astype(BF16)
    pw_ref[...] = jnp.zeros_like(pw_ref)
    for r, n in enumerate((N_PHASE, N_PHASE * SEG)):
        pre, pim = lam_pow(n)
        pw_ref[r:r + 1, 0:ns] = pre
        pw_ref[r:r + 1, ns:2 * ns] = pim


def _ssm_prep(lam_re, lam_im, log_step, b_re, b_im, c_re, c_im, riders=()):
    ns = SLAB_STATES

    def twice(m):
        m = m.reshape(N_SLAB, SLAB, STATE_DIM)
        return jnp.concatenate([m, m], axis=-1)

    def vec(v):
        return v.reshape(N_SLAB, 1, ns)

    mats = [twice(jnp.transpose(b_re, (0, 2, 1))), twice(jnp.transpose(b_im, (0, 2, 1))), twice(c_re), twice(c_im)]
    vecs = [vec(lam_re), vec(lam_im), vec(jnp.broadcast_to(log_step[:, None], lam_re.shape))]
    w_shape = jax.ShapeDtypeStruct((N_SLAB, 2 * ns, 2 * ns), BF16)
    w_spec = pl.BlockSpec((None, 2 * ns, 2 * ns), lambda q: (q, 0, 0))
    rider_in, rider_out, rider_shapes = _rider_specs(riders, N_SLAB)
    out = pl.pallas_call(
        functools.partial(_ssm_prep_body, n_riders=len(riders)),
        grid=(N_SLAB,),
        in_specs=[pl.BlockSpec((None, SLAB, 2 * STATE_DIM), lambda q: (q, 0, 0))] * 4
        + [pl.BlockSpec((None, 1, ns), lambda q: (q, 0, 0))] * 3 + rider_in,
        out_specs=[w_spec, w_spec, w_spec, pl.BlockSpec((None, 8, 2 * ns), lambda q: (q, 0, 0))] + rider_out,
        out_shape=[w_shape, w_shape, w_shape, jax.ShapeDtypeStruct((N_SLAB, 8, 2 * ns), F32)] + rider_shapes,
        name="ssm_prep",
        compiler_params=_params(("arbitrary",)),
    )(*mats, *vecs, *riders)
    return tuple(out[:4]), tuple(out[4:])


def _cmul(are, aim, bre, bim):
    return are * bre - aim * bim, are * bim + aim * bre


HALF = SLAB_STATES // LANE
GROUP_ROWS = 8 * SEG


def _mult(pw_ref, r, c):
    return (pw_ref[r:r + 1, c * LANE:(c + 1) * LANE],
            pw_ref[r:r + 1, (HALF + c) * LANE:(HALF + c + 1) * LANE])


def _state_increments(lhs, wb_ref, e_ref):
    e = _dot(lhs, wb_ref[...])
    for lt in range(2 * HALF):
        e_ref[lt, :e.shape[0]] = e[:, lt * LANE:(lt + 1) * LANE]


def _scan_local(e_ref, p_ref, pw_ref, c, base):
    l8re, l8im = _mult(pw_ref, 0, c)
    idx = pl.ds(base, 8, stride=SEG)
    lre = e_ref[c, idx, :]
    lim = e_ref[HALF + c, idx, :]
    for i in range(1, SEG):
        idx = pl.ds(base + i, 8, stride=SEG)
        p_ref[c, idx, :] = lre
        p_ref[HALF + c, idx, :] = lim
        mre, mim = _cmul(l8re, l8im, lre, lim)
        lre = mre + e_ref[c, idx, :]
        lim = mim + e_ref[HALF + c, idx, :]
    return lre, lim


def _scan_fixup(p_ref, pw_ref, c, base, cre, cim):
    l8re, l8im = _mult(pw_ref, 0, c)
    idx = pl.ds(base, 8, stride=SEG)
    p_ref[c, idx, :] = cre
    p_ref[HALF + c, idx, :] = cim
    for i in range(1, SEG):
        idx = pl.ds(base + i, 8, stride=SEG)
        cre, cim = _cmul(l8re, l8im, cre, cim)
        p_ref[c, idx, :] = p_ref[c, idx, :] + cre
        p_ref[HALF + c, idx, :] = p_ref[HALF + c, idx, :] + cim


def _readout(lhs, p_ref, kt_ref, wc_ref, d_ref):
    prev = jnp.concatenate([p_ref[lt, :lhs.shape[0]] for lt in range(2 * HALF)], axis=1)
    toep = jnp.concatenate(
        [_dot(lhs[:, :(k + 1) * MXU_TILE], kt_ref[:(k + 1) * MXU_TILE, k * MXU_TILE:(k + 1) * MXU_TILE])
         for k in range(N_PHASE * LANE // MXU_TILE)], axis=1)
    y = toep + _dot(prev.astype(BF16), wc_ref[...]) + d_ref[...] * lhs.astype(F32)
    return y.astype(BF16)


def _ssm_prompt_body(u_ref, wb_ref, kt_ref, wc_ref, pw_ref, d_ref, y_ref, sre_ref, sim_ref,
                     e_ref, p_ref, end_ref, cin_ref, carry_ref, *, n_seq, rows, tile):
    total = n_seq * rows
    lhs = jnp.concatenate([u_ref[t].reshape(total, LANE) for t in range(N_PHASE)], axis=1)
    _state_increments(lhs, wb_ref, e_ref)

    @pl.when((pl.program_id(0) == 0) & (tile == 0))
    def _():
        end_ref[...] = jnp.zeros_like(end_ref)

    @pl.when(tile == 0)
    def _():
        carry_ref[...] = jnp.zeros_like(carry_ref)

    groups = rows // GROUP_ROWS
    for c in range(HALF):
        for b in range(n_seq):
            for j in range(groups):
                lre, lim = _scan_local(e_ref, p_ref, pw_ref, c, b * rows + j * GROUP_ROWS)
                idx = pl.ds(j * 64 + b, 8, stride=8)
                end_ref[c, idx, :] = lre
                end_ref[HALF + c, idx, :] = lim
    for c in range(HALF):
        lsre, lsim = _mult(pw_ref, 1, c)
        cre = carry_ref[c]
        cim = carry_ref[HALF + c]
        for s in range(rows // SEG):
            cin_ref[c, 8 * s:8 * s + 8, :] = cre
            cin_ref[HALF + c, 8 * s:8 * s + 8, :] = cim
            mre, mim = _cmul(lsre, lsim, cre, cim)
            cre = mre + end_ref[c, 8 * s:8 * s + 8, :]
            cim = mim + end_ref[HALF + c, 8 * s:8 * s + 8, :]
        carry_ref[c] = cre
        carry_ref[HALF + c] = cim
        sre_ref[:, c * LANE:(c + 1) * LANE] = cre[:n_seq]
        sim_ref[:, c * LANE:(c + 1) * LANE] = cim[:n_seq]
    for c in range(HALF):
        for b in range(n_seq):
            for j in range(groups):
                idx = pl.ds(j * 64 + b, 8, stride=8)
                _scan_fixup(p_ref, pw_ref, c, b * rows + j * GROUP_ROWS,
                            cin_ref[c, idx, :], cin_ref[HALF + c, idx, :])
    y = _readout(lhs, p_ref, kt_ref, wc_ref, d_ref)
    for t in range(N_PHASE):
        y_ref[t] = y[:, t * LANE:(t + 1) * LANE].reshape(n_seq, rows, LANE)


def _ssm_sample_body(u_ref, wb_ref, kt_ref, wc_ref, pw_ref, d_ref, h0re_ref, h0im_ref,
                     y_ref, sre_ref, sim_ref, e_ref, p_ref, *, rows):
    lhs = jnp.concatenate([u_ref[t] for t in range(N_PHASE)], axis=1)
    _state_increments(lhs, wb_ref, e_ref)
    for c in range(HALF):
        lsre, lsim = _mult(pw_ref, 1, c)
        for j in range(rows // GROUP_ROWS):
            lre, lim = _scan_local(e_ref, p_ref, pw_ref, c, j * GROUP_ROWS)
            cre = h0re_ref[8 * j:8 * j + 8, c * LANE:(c + 1) * LANE]
            cim = h0im_ref[8 * j:8 * j + 8, c * LANE:(c + 1) * LANE]
            mre, mim = _cmul(lsre, lsim, cre, cim)
            sre_ref[8 * j:8 * j + 8, c * LANE:(c + 1) * LANE] = mre + lre
            sim_ref[8 * j:8 * j + 8, c * LANE:(c + 1) * LANE] = mim + lim
            _scan_fixup(p_ref, pw_ref, c, j * GROUP_ROWS, cre, cim)
    y = _readout(lhs, p_ref, kt_ref, wc_ref, d_ref)
    for t in range(N_PHASE):
        y_ref[t] = y[:, t * LANE:(t + 1) * LANE]


def _ssm_body(u_ref, wb_ref, kt_ref, wc_ref, pw_ref, d_ref, us_ref, h0re_ref, h0im_ref,
              y_ref, sre_ref, sim_ref, ys_ref, sres_ref, sims_ref,
              e_ref, p_ref, end_ref, cin_ref, carry_ref, *, n_seq, rows, rows_s):
    step = pl.program_id(1)

    @pl.when(step == 0)
    def _():
        _ssm_sample_body(us_ref, wb_ref, kt_ref, wc_ref, pw_ref, d_ref, h0re_ref, h0im_ref,
                         ys_ref, sres_ref, sims_ref, e_ref, p_ref, rows=rows_s)

    @pl.when(step > 0)
    def _():
        _ssm_prompt_body(u_ref, wb_ref, kt_ref, wc_ref, pw_ref, d_ref, y_ref, sre_ref, sim_ref,
                         e_ref, p_ref, end_ref, cin_ref, carry_ref, n_seq=n_seq, rows=rows, tile=step - 1)


def _ssm(u8_p, n_p, u8_s, n_s, h0, wb, kt, wc, pw, d8):
    d = D_MODEL
    ns = SLAB_STATES
    rows_p, rows_s = u8_p.shape[1], u8_s.shape[1]
    per_seq = rows_p // n_p
    assert rows_s // n_s == SEG, "carried-state path scans one segment per sequence"
    assert n_p <= 8, "sequences ride the sublanes of the segment chain"
    rows = min(SSM_ROWS, per_seq)
    assert per_seq % rows == 0 and rows % GROUP_ROWS == 0 and rows_s % GROUP_ROWS == 0
    assert rows_s <= n_p * rows, "the carried-state step reuses the scan scratch"
    tiles = per_seq // rows
    slab = lambda q, i: (q, 0, 0)
    p_spec = pl.BlockSpec((N_PHASE, n_p, rows, LANE), lambda q, i: (0, 0, jnp.maximum(i - 1, 0), q))
    s_spec = pl.BlockSpec((N_PHASE, rows_s, LANE), lambda q, i: (0, 0, q))
    w_spec = pl.BlockSpec((None, 2 * ns, 2 * ns), slab)
    state_p = pl.BlockSpec((n_p, ns), lambda q, i: (0, q))
    state_s = pl.BlockSpec((n_s, ns), lambda q, i: (0, q))
    seg_rows = 8 * (rows // SEG)
    dims_p = (N_PHASE, n_p, per_seq, d)
    y_p, re_p, im_p, y_s, re_s, im_s = pl.pallas_call(
        functools.partial(_ssm_body, n_seq=n_p, rows=rows, rows_s=rows_s),
        grid=(N_SLAB, tiles + 1),
        in_specs=[p_spec, w_spec, w_spec, w_spec, pl.BlockSpec((None, 8, 2 * ns), slab),
                  pl.BlockSpec((None, 1, d), slab), s_spec, state_s, state_s],
        out_specs=[p_spec, state_p, state_p, s_spec, state_s, state_s],
        out_shape=[jax.ShapeDtypeStruct(dims_p, BF16),
                   jax.ShapeDtypeStruct((n_p, N_SLAB * ns), F32), jax.ShapeDtypeStruct((n_p, N_SLAB * ns), F32),
                   jax.ShapeDtypeStruct((N_PHASE, rows_s, d), BF16),
                   jax.ShapeDtypeStruct((n_s, N_SLAB * ns), F32), jax.ShapeDtypeStruct((n_s, N_SLAB * ns), F32)],
        scratch_shapes=[pltpu.VMEM((2 * HALF, n_p * rows, LANE), F32),
                        pltpu.VMEM((2 * HALF, n_p * rows, LANE), F32),
                        pltpu.VMEM((2 * HALF, seg_rows, LANE), F32),
                        pltpu.VMEM((2 * HALF, seg_rows, LANE), F32),
                        pltpu.VMEM((2 * HALF, 8, LANE), F32)],
        name="ssm",
        compiler_params=_params(("arbitrary", "arbitrary")),
    )(u8_p.reshape(dims_p), wb, kt, wc, pw, d8, u8_s, h0[0].reshape(n_s, -1), h0[1].reshape(n_s, -1))
    return (y_p.reshape(N_PHASE, rows_p, d), re_p, im_p), (y_s, re_s, im_s)


def _mix_out_ffn_body(y_ref, a_ref, gs_ref, x_ref, wglu_ref, wo_ref, g_ref, wg_ref, wu_ref, wd_ref, gf_ref,
                      o_hbm, ybuf_ref, sem, *, rows, n_steps, final_norm):
    d = D_MODEL
    step = pl.program_id(0)
    slot = step % 2

    def copies(of_step, of_slot):
        return [pltpu.make_async_copy(ybuf_ref.at[of_slot, t], o_hbm.at[pl.ds(of_step * rows, rows), t, :],
                                      sem.at[of_slot]) for t in range(N_PHASE)]

    @pl.when(step >= 2)
    def _():
        for c in copies(step - 2, slot):
            c.wait()

    glu = _dot(jax.nn.gelu(_stack_phases(y_ref).astype(F32)).astype(BF16), wglu_ref[...])
    y_ssm = glu[:, :d] * jax.nn.sigmoid(glu[:, d:])
    merged = _stack_phases(a_ref).astype(F32) + _stack_phases(gs_ref).astype(F32) * y_ssm
    x2 = _stack_phases(x_ref) + _dot(merged.astype(BF16), wo_ref[...])
    y = _ffn_math(x2, g_ref, wg_ref, wu_ref, wd_ref)
    if final_norm:
        y = _rms(y, gf_ref[...])
    for t in range(N_PHASE):
        ybuf_ref[slot, t] = y[t * rows:(t + 1) * rows]
    for c in copies(step, slot):
        c.start()

    @pl.when(step == n_steps - 1)
    def _():
        if n_steps > 1:
            for c in copies(step - 1, 1 - slot):
                c.wait()
        for c in copies(step, slot):
            c.wait()


def _mix_out_ffn(y8, a8, gs8, xp, w_glu, w_o, g, wg, wu, wd, gf, final_norm):
    n_rows = xp.shape[1]
    d = D_MODEL
    rows = min(MIX_OUT_ROWS, n_rows)
    assert n_rows % rows == 0
    ph_spec = pl.BlockSpec((N_PHASE, rows, d), lambda i: (0, i, 0))
    weights = (w_glu, w_o, g, wg, wu, wd, gf)
    n_steps = n_rows // rows
    return pl.pallas_call(
        functools.partial(_mix_out_ffn_body, rows=rows, n_steps=n_steps, final_norm=final_norm),
        grid=(n_steps,),
        in_specs=[ph_spec, ph_spec, ph_spec, ph_spec] + [_const_spec(w.shape) for w in weights],
        out_specs=pl.BlockSpec(memory_space=pl.ANY),
        out_shape=jax.ShapeDtypeStruct((n_rows, N_PHASE, d), F32),
        scratch_shapes=[pltpu.VMEM((2, N_PHASE, rows, d), F32), pltpu.SemaphoreType.DMA((2,))],
        name="mix_out_ffn",
        compiler_params=_params(("arbitrary",)),
    )(y8, a8, gs8, xp, *weights).reshape(N_PHASE * n_rows, d)


def kernel(x_prompt, x_sample, state_conv, state_ssm_re, state_ssm_im, norm_ffn1, w_ffn1_gate, w_ffn1_up, w_ffn1_down, norm_mix, w_in, w_conv, w_conv_out, ssm_lambda_re, ssm_lambda_im, ssm_log_step, ssm_b_re, ssm_b_im, ssm_c_re, ssm_c_im, ssm_d, w_glu, w_o, norm_ffn2, w_ffn2_gate, w_ffn2_up, w_ffn2_down, norm_final):
    depth, d = w_in.shape[0], w_in.shape[1]
    row = lambda v: v.reshape(1, -1)
    nf = row(norm_final)
    xt_p = x_prompt.reshape(-1, d)
    xt_s = x_sample.reshape(-1, d)
    n_p, n_s = x_prompt.shape[0], x_sample.shape[0]
    outs_p, outs_s = [], []
    for l in range(depth):
        last = l == depth - 1
        ssm_w, (f1g, f1u, f1d) = _ssm_prep(
            ssm_lambda_re[l], ssm_lambda_im[l], ssm_log_step[l], ssm_b_re[l], ssm_b_im[l], ssm_c_re[l],
            ssm_c_im[l], riders=(w_ffn1_gate[l], w_ffn1_up[l], w_ffn1_down[l]))
        ffn1 = (row(norm_ffn1[l]), f1g, f1u, f1d)
        xp_p, (win, wco, wgl, wo) = _ffn_to_phase(xt_p, *ffn1, riders=(w_in[l], w_conv_out[l], w_glu[l], w_o[l]))
        xp_s, _ = _ffn_to_phase(xt_s, *ffn1)
        d8 = jnp.tile(ssm_d[l].reshape(N_SLAB, 1, SLAB), (1, 1, N_PHASE))
        mix_w = (row(norm_mix[l]), win, w_conv[l], wco)
        u_p, a_p, gs_p, conv_p, ffn2_w = _mix_in(xp_p, None, *mix_w, n_p,
                                                 riders=(w_ffn2_gate[l], w_ffn2_up[l], w_ffn2_down[l]))
        u_s, a_s, gs_s, conv_s, _ = _mix_in(xp_s, state_conv[l], *mix_w, n_s)
        (y_p, re_p, im_p), (y_s, re_s, im_s) = _ssm(u_p, n_p, u_s, n_s, (state_ssm_re[l], state_ssm_im[l]),
                                                    *ssm_w, d8)
        tail = (wgl, wo, row(norm_ffn2[l]), *ffn2_w, nf, last)
        xt_p = _mix_out_ffn(y_p, a_p, gs_p, xp_p, *tail)
        xt_s = _mix_out_ffn(y_s, a_s, gs_s, xp_s, *tail)
        outs_p.append([conv_p, re_p.reshape(n_p, -1, STATE_DIM), im_p.reshape(n_p, -1, STATE_DIM)])
        outs_s.append([conv_s, re_s.reshape(n_s, -1, STATE_DIM), im_s.reshape(n_s, -1, STATE_DIM)])
    stack = lambda outs: tuple(jnp.stack(leaf) for leaf in zip(*outs))
    return (xt_p.reshape(x_prompt.shape), xt_s.reshape(x_sample.shape), *stack(outs_p), *stack(outs_s))
```

```python
import functools

import jax
import jax.numpy as jnp
from jax import lax
from jax.experimental import pallas as pl
from jax.experimental.pallas import tpu as pltpu

F32 = jnp.float32
BF16 = jnp.bfloat16

D_MODEL = 1024
N_PHASE = 8
LANE = 128
MXU_TILE = 256
SLAB = 128
N_SLAB = D_MODEL // SLAB
GROUPS_PER_SLAB = 8
STATE_DIM = 64
SLAB_STATES = GROUPS_PER_SLAB * STATE_DIM
SEG = 4
RMS_EPS = 1e-6
VMEM_LIMIT = 60 * 1024 * 1024

FFN_ROWS = 1024
MIX_IN_ROWS = 128
MIX_OUT_ROWS = 64
SSM_ROWS = 256


def _rms(x, g):
    return x * lax.rsqrt(jnp.mean(x * x, axis=-1, keepdims=True) + RMS_EPS) * g


def _dot(a, b):
    return jnp.dot(a, b, preferred_element_type=F32)


def _const_spec(shape):
    zeros = (0,) * len(shape)
    return pl.BlockSpec(shape, lambda *_: zeros, pipeline_mode=pl.Buffered(1))


def _params(semantics):
    return pltpu.CompilerParams(dimension_semantics=semantics, vmem_limit_bytes=VMEM_LIMIT)


def _stack_phases(ref):
    return jnp.concatenate([ref[t] for t in range(N_PHASE)], axis=0)


def _ffn_math(x, g_ref, wg_ref, wu_ref, wd_ref):
    h = _rms(x, g_ref[...]).astype(BF16)
    act = (jax.nn.silu(_dot(h, wg_ref[...])) * _dot(h, wu_ref[...])).astype(BF16)
    return x + 0.5 * _dot(act, wd_ref[...])


def _rider_specs(weights, n_steps, step=lambda i: i):
    in_specs, out_specs, out_shapes = [], [], []
    for w in weights:
        n_blocks = n_steps
        while w.shape[0] % (16 * n_blocks):
            assert n_blocks % 2 == 0
            n_blocks //= 2
        spec = pl.BlockSpec((w.shape[0] // n_blocks, w.shape[1]),
                            lambda *idx, rep=n_steps // n_blocks: (step(*idx) // rep, 0))
        in_specs.append(spec)
        out_specs.append(spec)
        out_shapes.append(jax.ShapeDtypeStruct(w.shape, BF16))
    return in_specs, out_specs, out_shapes


def _convert_riders(in_refs, out_refs):
    for w_ref, o_ref in zip(in_refs, out_refs):
        o_ref[...] = w_ref[...].astype(BF16)


def _ffn_to_phase_body(*refs, rows, steps_a, n_steps, n_riders):
    xa_hbm, xb_hbm, g_ref, wg_ref, wu_ref, wd_ref = refs[:6]
    o_ref, xbuf_ref, sem = refs[6 + n_riders], refs[-2], refs[-1]
    step = pl.program_id(0)
    slot = step % 2

    def copies(src, row0, of_slot):
        return [pltpu.make_async_copy(src.at[pl.ds(row0, rows), t, :], xbuf_ref.at[of_slot, t], sem.at[of_slot])
                for t in range(N_PHASE)]

    def start(of_step, of_slot):
        @pl.when(of_step < steps_a)
        def _():
            for c in copies(xa_hbm, of_step * rows, of_slot):
                c.start()

        @pl.when(of_step >= steps_a)
        def _():
            for c in copies(xb_hbm, (of_step - steps_a) * rows, of_slot):
                c.start()

    @pl.when(step == 0)
    def _():
        start(step, slot)

    @pl.when(step + 1 < n_steps)
    def _():
        start(step + 1, 1 - slot)

    for c in copies(xa_hbm, 0, slot):
        c.wait()
    _convert_riders(refs[6:6 + n_riders], refs[7 + n_riders:7 + 2 * n_riders])
    x = jnp.concatenate([xbuf_ref[slot, t] for t in range(N_PHASE)], axis=0)
    y = _ffn_math(x, g_ref, wg_ref, wu_ref, wd_ref)
    for t in range(N_PHASE):
        o_ref[t] = y[t * rows:(t + 1) * rows]


def _ffn_to_phase(x_a, x_b, g, wg, wu, wd, riders=()):
    d = x_a.shape[1]
    rows_a, rows_b = x_a.shape[0] // N_PHASE, x_b.shape[0] // N_PHASE
    n_rows = rows_a + rows_b
    d_ff = wg.shape[1]
    rows = FFN_ROWS // N_PHASE
    assert rows_a % rows == 0 and rows_b % rows == 0
    steps_a, steps_b = rows_a // rows, rows_b // rows
    n_steps = steps_a + steps_b
    rider_in, rider_out, rider_shapes = _rider_specs(riders, steps_a, lambda i: jnp.minimum(i, steps_a - 1))
    out = pl.pallas_call(
        functools.partial(_ffn_to_phase_body, rows=rows, steps_a=steps_a, n_steps=n_steps, n_riders=len(riders)),
        grid=(n_steps,),
        in_specs=[
            pl.BlockSpec(memory_space=pl.ANY),
            pl.BlockSpec(memory_space=pl.ANY),
            _const_spec((1, d)),
            _const_spec((d, d_ff)),
            _const_spec((d, d_ff)),
            _const_spec((d_ff, d)),
        ] + rider_in,
        out_specs=[pl.BlockSpec((N_PHASE, rows, d), lambda i: (0, i, 0))] + rider_out,
        out_shape=[jax.ShapeDtypeStruct((N_PHASE, n_rows, d), F32)] + rider_shapes,
        scratch_shapes=[pltpu.VMEM((2, N_PHASE, rows, d), F32), pltpu.SemaphoreType.DMA((2,))],
        name="ffn_to_phase",
        compiler_params=_params(("arbitrary",)),
    )(x_a.reshape(rows_a, N_PHASE, d), x_b.reshape(rows_b, N_PHASE, d), g, wg, wu, wd, *riders)
    return out[0], tuple(out[1:])


def _mix_in_compute(x_ref, g, win_ref, wcv, wco_ref, shift, rows):
    d = D_MODEL
    h = _rms(_stack_phases(x_ref), g).astype(BF16)

    def proj(k):
        return _dot(h, win_ref[:, k * d:(k + 1) * d])

    z = proj(1) * proj(0)
    z6 = z[6 * rows:7 * rows]
    z7 = z[7 * rows:8 * rows]
    s6 = shift(z6, 0)
    s7 = shift(z7, 1)
    z1 = jnp.concatenate([s7, z[:7 * rows]], axis=0)
    z2 = jnp.concatenate([s6, s7, z[:6 * rows]], axis=0)
    conv = wcv[2:3] * z + wcv[1:2] * z1 + wcv[0:1] * z2
    y_conv = _dot((proj(2) * conv).astype(BF16), wco_ref[...])
    u = proj(3)
    a = jax.nn.sigmoid(proj(4)) * y_conv
    gs = jax.nn.sigmoid(proj(5))
    return u, a, gs, z6, z7


def _store_phases(ref, val, rows):
    val = val.astype(ref.dtype)
    for t in range(N_PHASE):
        ref[t] = val[t * rows:(t + 1) * rows]


def _mix_in_prompt_body(*refs, rows, n_riders):
    x_ref, g_ref, win_ref, wcv_ref, wco_ref = refs[:5]
    u_ref, a_ref, gs_ref, z_ref = refs[5 + n_riders:9 + n_riders]
    carry_ref = refs[-1]
    _convert_riders(refs[5:5 + n_riders], refs[9 + n_riders:9 + 2 * n_riders])

    @pl.when(pl.program_id(1) == 0)
    def _():
        carry_ref[...] = jnp.zeros_like(carry_ref)

    row_id = lax.broadcasted_iota(jnp.int32, (rows, D_MODEL), 0)

    def shift(z, k):
        return jnp.where(row_id == 0, carry_ref[k:k + 1, :], pltpu.roll(z, 1, 0))

    u, a, gs, z6, z7 = _mix_in_compute(x_ref, g_ref[...], win_ref, wcv_ref[...], wco_ref, shift, rows)
    last6 = z6[rows - 1:rows]
    last7 = z7[rows - 1:rows]
    carry_ref[0:1, :] = last6
    carry_ref[1:2, :] = last7
    z_ref[0:1, :] = last6
    z_ref[1:2, :] = last7
    _store_phases(u_ref, u, rows)
    _store_phases(a_ref, a, rows)
    _store_phases(gs_ref, gs, rows)


def _mix_in_sample_body(x_ref, start_ref, g_ref, win_ref, wcv_ref, wco_ref,
                        u_ref, a_ref, gs_ref, z6_ref, z7_ref, slab_ref, *, rows, rows_per_seq):
    row_id = lax.broadcasted_iota(jnp.int32, (rows, D_MODEL), 0)

    def shift(z, k):
        return jnp.where(row_id % rows_per_seq == 0, start_ref[k], pltpu.roll(z, 1, 0))

    u, a, gs, z6, z7 = _mix_in_compute(x_ref, g_ref[...], win_ref, wcv_ref[...], wco_ref, shift, rows)
    for z, z_ref in ((z6, z6_ref), (z7, z7_ref)):
        for lt in range(D_MODEL // LANE):
            slab_ref[...] = z[:, lt * LANE:(lt + 1) * LANE]
            z_ref[:, lt * LANE:(lt + 1) * LANE] = slab_ref[pl.ds(rows_per_seq - 1, rows // rows_per_seq,
                                                                stride=rows_per_seq), :]
    _store_phases(u_ref, u, rows)
    _store_phases(a_ref, a, rows)
    _store_phases(gs_ref, gs, rows)


def _mix_in(xp, row0, n_rows, conv_prev, g, w_in, w_conv, w_conv_out, n_seq, riders=()):
    d = D_MODEL
    rows_per_seq = n_rows // n_seq
    ph_shape = jax.ShapeDtypeStruct((N_PHASE, n_rows, d), BF16)
    weights = (g, w_in, w_conv, w_conv_out)
    weight_specs = [_const_spec(w.shape) for w in weights]
    if conv_prev is None:
        rows = min(MIX_IN_ROWS, rows_per_seq)
        assert rows_per_seq % rows == 0
        tiles = rows_per_seq // rows
        assert row0 % rows == 0
        ph_spec = pl.BlockSpec((N_PHASE, rows, d), lambda b, i: (0, b * tiles + i, 0))
        x_spec = pl.BlockSpec((N_PHASE, rows, d), lambda b, i: (0, row0 // rows + b * tiles + i, 0))
        last_spec = pl.BlockSpec((None, 2, d), lambda b, i: (b, 0, 0))
        rider_in, rider_out, rider_shapes = _rider_specs(riders, n_seq * tiles, lambda b, i: b * tiles + i)
        u8, a8, gs8, conv_state, *converted = pl.pallas_call(
            functools.partial(_mix_in_prompt_body, rows=rows, n_riders=len(riders)),
            grid=(n_seq, tiles),
            in_specs=[x_spec] + weight_specs + rider_in,
            out_specs=[ph_spec, ph_spec, ph_spec, last_spec] + rider_out,
            out_shape=[ph_shape, ph_shape, ph_shape, jax.ShapeDtypeStruct((n_seq, 2, d), F32)] + rider_shapes,
            scratch_shapes=[pltpu.VMEM((8, d), F32)],
            name="mix_in_prompt",
            compiler_params=_params(("arbitrary", "arbitrary")),
        )(xp, *weights, *riders)
    else:
        assert not riders
        converted = []
        rows = n_rows
        assert row0 % rows == 0
        start = jnp.repeat(jnp.swapaxes(conv_prev, 0, 1), rows_per_seq, axis=1)
        full = lambda shape: pl.BlockSpec(shape, lambda i: (0,) * len(shape))
        x_spec = pl.BlockSpec((N_PHASE, rows, d), lambda i: (0, row0 // rows, 0))
        u8, a8, gs8, z6, z7 = pl.pallas_call(
            functools.partial(_mix_in_sample_body, rows=rows, rows_per_seq=rows_per_seq),
            grid=(1,),
            in_specs=[x_spec, full((2, rows, d))] + weight_specs,
            out_specs=[full((N_PHASE, rows, d))] * 3 + [full((n_seq, d))] * 2,
            out_shape=[ph_shape, ph_shape, ph_shape,
                       jax.ShapeDtypeStruct((n_seq, d), F32), jax.ShapeDtypeStruct((n_seq, d), F32)],
            scratch_shapes=[pltpu.VMEM((rows, LANE), F32)],
            name="mix_in_sample",
            compiler_params=_params(("arbitrary",)),
        )(xp, start, *weights)
        conv_state = jnp.stack([z6, z7], axis=1)
    return u8, a8, gs8, conv_state, tuple(converted)


def _split_bf16(a):
    hi = a.astype(BF16)
    return hi, (a - hi.astype(F32)).astype(BF16)


def _ssm_prep_body(*refs, n_riders):
    bbre_ref, bbim_ref, ctre_ref, ctim_ref, lre_ref, lim_ref, lstep_ref = refs[:7]
    wb_ref, kt_ref, wc_ref, pw_ref = refs[7 + n_riders:11 + n_riders]
    _convert_riders(refs[7:7 + n_riders], refs[11 + n_riders:])
    ns = SLAB_STATES
    row_group = lax.broadcasted_iota(jnp.int32, (SLAB, LANE), 0) // (SLAB // GROUPS_PER_SLAB)
    lane_half = lax.broadcasted_iota(jnp.int32, (SLAB, LANE), 1) // STATE_DIM

    def block_diag(ref):
        tiles = [jnp.where(row_group == 2 * k + lane_half, ref[...], 0.0) for k in range(GROUPS_PER_SLAB // 2)]
        return jnp.concatenate(tiles, axis=1)

    bbre, bbim = block_diag(bbre_ref), block_diag(bbim_ref)
    ctre, ctim = block_diag(ctre_ref), block_diag(ctim_ref)
    lre, lim = lre_ref[...], lim_ref[...]
    step = jnp.exp(lstep_ref[...])

    def lam_pow(n):
        mag = jnp.exp((n * lre) * step)
        ang = (n * lim) * step
        return mag * jnp.cos(ang), mag * jnp.sin(ang)

    l1re, l1im = lam_pow(1)
    den = lre * lre + lim * lim
    fre = ((l1re - 1.0) * lre + l1im * lim) / den
    fim = (l1im * lre - (l1re - 1.0) * lim) / den

    def dot_nt(a, b_split):
        nt = lambda p, q: lax.dot_general(p, q, (((1,), (1,)), ((), ())), preferred_element_type=F32)
        a_hi, a_lo = _split_bf16(a)
        b_hi, b_lo = b_split
        return nt(a_hi, b_hi) + nt(a_hi, b_lo) + nt(a_lo, b_hi)

    ctre_split = _split_bf16(ctre)
    ctim_split = _split_bf16(ctim)
    kt_ref[...] = jnp.zeros_like(kt_ref)
    for k in range(N_PHASE):
        pre, pim = lam_pow(k)
        gre = fre * pre - fim * pim
        gim = fre * pim + fim * pre
        are = bbre * gre - bbim * gim
        aim = bbre * gim + bbim * gre
        j = N_PHASE - 1 - k
        wb_ref[j * SLAB:(j + 1) * SLAB, 0:ns] = are.astype(BF16)
        wb_ref[j * SLAB:(j + 1) * SLAB, ns:2 * ns] = aim.astype(BF16)
        kk = (dot_nt(are, ctre_split) - dot_nt(aim, ctim_split)).astype(BF16)
        for jj in range(N_PHASE - k):
            t = jj + k
            kt_ref[jj * SLAB:(jj + 1) * SLAB, t * SLAB:(t + 1) * SLAB] = kk
    for t in range(N_PHASE):
        pre, pim = lam_pow(t + 1)
        cre = ctre * pre - ctim * pim
        cim = ctre * pim + ctim * pre
        wc_ref[0:ns, t * SLAB:(t + 1) * SLAB] = cre.T.astype(BF16)
        wc_ref[ns:2 * ns, t * SLAB:(t + 1) * SLAB] = (-cim).T.astype(BF16)
    pw_ref[...] = jnp.zeros_like(pw_ref)
    for r, n in enumerate((N_PHASE, N_PHASE * SEG)):
        pre, pim = lam_pow(n)
        pw_ref[r:r + 1, 0:ns] = pre
        pw_ref[r:r + 1, ns:2 * ns] = pim


def _ssm_prep(lam_re, lam_im, log_step, b_re, b_im, c_re, c_im, riders=()):
    ns = SLAB_STATES

    def twice(m):
        m = m.reshape(N_SLAB, SLAB, STATE_DIM)
        return jnp.concatenate([m, m], axis=-1)

    def vec(v):
        return v.reshape(N_SLAB, 1, ns)

    mats = [twice(jnp.transpose(b_re, (0, 2, 1))), twice(jnp.transpose(b_im, (0, 2, 1))), twice(c_re), twice(c_im)]
    vecs = [vec(lam_re), vec(lam_im), vec(jnp.broadcast_to(log_step[:, None], lam_re.shape))]
    w_shape = jax.ShapeDtypeStruct((N_SLAB, 2 * ns, 2 * ns), BF16)
    w_spec = pl.BlockSpec((None, 2 * ns, 2 * ns), lambda q: (q, 0, 0))
    rider_in, rider_out, rider_shapes = _rider_specs(riders, N_SLAB)
    out = pl.pallas_call(
        functools.partial(_ssm_prep_body, n_riders=len(riders)),
        grid=(N_SLAB,),
        in_specs=[pl.BlockSpec((None, SLAB, 2 * STATE_DIM), lambda q: (q, 0, 0))] * 4
        + [pl.BlockSpec((None, 1, ns), lambda q: (q, 0, 0))] * 3 + rider_in,
        out_specs=[w_spec, w_spec, w_spec, pl.BlockSpec((None, 8, 2 * ns), lambda q: (q, 0, 0))] + rider_out,
        out_shape=[w_shape, w_shape, w_shape, jax.ShapeDtypeStruct((N_SLAB, 8, 2 * ns), F32)] + rider_shapes,
        name="ssm_prep",
        compiler_params=_params(("arbitrary",)),
    )(*mats, *vecs, *riders)
    return tuple(out[:4]), tuple(out[4:])


def _cmul(are, aim, bre, bim):
    return are * bre - aim * bim, are * bim + aim * bre


HALF = SLAB_STATES // LANE
GROUP_ROWS = 8 * SEG


def _mult(pw_ref, r, c):
    return (pw_ref[r:r + 1, c * LANE:(c + 1) * LANE],
            pw_ref[r:r + 1, (HALF + c) * LANE:(HALF + c + 1) * LANE])


def _state_increments(lhs, wb_ref, e_ref):
    e = _dot(lhs, wb_ref[...])
    for lt in range(2 * HALF):
        e_ref[lt, :e.shape[0]] = e[:, lt * LANE:(lt + 1) * LANE]


def _scan_local(e_ref, p_ref, pw_ref, c, base):
    l8re, l8im = _mult(pw_ref, 0, c)
    idx = pl.ds(base, 8, stride=SEG)
    lre = e_ref[c, idx, :]
    lim = e_ref[HALF + c, idx, :]
    for i in range(1, SEG):
        idx = pl.ds(base + i, 8, stride=SEG)
        p_ref[c, idx, :] = lre
        p_ref[HALF + c, idx, :] = lim
        mre, mim = _cmul(l8re, l8im, lre, lim)
        lre = mre + e_ref[c, idx, :]
        lim = mim + e_ref[HALF + c, idx, :]
    return lre, lim


def _scan_fixup(p_ref, pw_ref, c, base, cre, cim):
    l8re, l8im = _mult(pw_ref, 0, c)
    idx = pl.ds(base, 8, stride=SEG)
    p_ref[c, idx, :] = cre
    p_ref[HALF + c, idx, :] = cim
    for i in range(1, SEG):
        idx = pl.ds(base + i, 8, stride=SEG)
        cre, cim = _cmul(l8re, l8im, cre, cim)
        p_ref[c, idx, :] = p_ref[c, idx, :] + cre
        p_ref[HALF + c, idx, :] = p_ref[HALF + c, idx, :] + cim


def _readout(lhs, p_ref, kt_ref, wc_ref, d_ref):
    prev = jnp.concatenate([p_ref[lt, :lhs.shape[0]] for lt in range(2 * HALF)], axis=1)
    toep = jnp.concatenate(
        [_dot(lhs[:, :(k + 1) * MXU_TILE], kt_ref[:(k + 1) * MXU_TILE, k * MXU_TILE:(k + 1) * MXU_TILE])
         for k in range(N_PHASE * LANE // MXU_TILE)], axis=1)
    y = toep + _dot(prev.astype(BF16), wc_ref[...]) + d_ref[...] * lhs.astype(F32)
    return y.astype(BF16)


def _ssm_prompt_body(u_ref, wb_ref, kt_ref, wc_ref, pw_ref, d_ref, y_ref, sre_ref, sim_ref,
                     e_ref, p_ref, end_ref, cin_ref, carry_ref, *, n_seq, rows, tile):
    total = n_seq * rows
    lhs = jnp.concatenate([u_ref[t].reshape(total, LANE) for t in range(N_PHASE)], axis=1)
    _state_increments(lhs, wb_ref, e_ref)

    @pl.when((pl.program_id(0) == 0) & (tile == 0))
    def _():
        end_ref[...] = jnp.zeros_like(end_ref)

    @pl.when(tile == 0)
    def _():
        carry_ref[...] = jnp.zeros_like(carry_ref)

    groups = rows // GROUP_ROWS
    for c in range(HALF):
        for b in range(n_seq):
            for j in range(groups):
                lre, lim = _scan_local(e_ref, p_ref, pw_ref, c, b * rows + j * GROUP_ROWS)
                idx = pl.ds(j * 64 + b, 8, stride=8)
                end_ref[c, idx, :] = lre
                end_ref[HALF + c, idx, :] = lim
    for c in range(HALF):
        lsre, lsim = _mult(pw_ref, 1, c)
        cre = carry_ref[c]
        cim = carry_ref[HALF + c]
        for s in range(rows // SEG):
            cin_ref[c, 8 * s:8 * s + 8, :] = cre
            cin_ref[HALF + c, 8 * s:8 * s + 8, :] = cim
            mre, mim = _cmul(lsre, lsim, cre, cim)
            cre = mre + end_ref[c, 8 * s:8 * s + 8, :]
            cim = mim + end_ref[HALF + c, 8 * s:8 * s + 8, :]
        carry_ref[c] = cre
        carry_ref[HALF + c] = cim
        sre_ref[:, c * LANE:(c + 1) * LANE] = cre[:n_seq]
        sim_ref[:, c * LANE:(c + 1) * LANE] = cim[:n_seq]
    for c in range(HALF):
        for b in range(n_seq):
            for j in range(groups):
                idx = pl.ds(j * 64 + b, 8, stride=8)
                _scan_fixup(p_ref, pw_ref, c, b * rows + j * GROUP_ROWS,
                            cin_ref[c, idx, :], cin_ref[HALF + c, idx, :])
    y = _readout(lhs, p_ref, kt_ref, wc_ref, d_ref)
    for t in range(N_PHASE):
        y_ref[t] = y[:, t * LANE:(t + 1) * LANE].reshape(n_seq, rows, LANE)


def _ssm_sample_body(u_ref, wb_ref, kt_ref, wc_ref, pw_ref, d_ref, h0re_ref, h0im_ref,
                     y_ref, sre_ref, sim_ref, e_ref, p_ref, *, rows):
    lhs = jnp.concatenate([u_ref[t] for t in range(N_PHASE)], axis=1)
    _state_increments(lhs, wb_ref, e_ref)
    for c in range(HALF):
        lsre, lsim = _mult(pw_ref, 1, c)
        for j in range(rows // GROUP_ROWS):
            lre, lim = _scan_local(e_ref, p_ref, pw_ref, c, j * GROUP_ROWS)
            cre = h0re_ref[8 * j:8 * j + 8, c * LANE:(c + 1) * LANE]
            cim = h0im_ref[8 * j:8 * j + 8, c * LANE:(c + 1) * LANE]
            mre, mim = _cmul(lsre, lsim, cre, cim)
            sre_ref[8 * j:8 * j + 8, c * LANE:(c + 1) * LANE] = mre + lre
            sim_ref[8 * j:8 * j + 8, c * LANE:(c + 1) * LANE] = mim + lim
            _scan_fixup(p_ref, pw_ref, c, j * GROUP_ROWS, cre, cim)
    y = _readout(lhs, p_ref, kt_ref, wc_ref, d_ref)
    for t in range(N_PHASE):
        y_ref[t] = y[:, t * LANE:(t + 1) * LANE]


def _ssm_body(u_ref, wb_ref, kt_ref, wc_ref, pw_ref, d_ref, us_ref, h0re_ref, h0im_ref,
              y_ref, sre_ref, sim_ref, ys_ref, sres_ref, sims_ref,
              e_ref, p_ref, end_ref, cin_ref, carry_ref, *, n_seq, rows, rows_s):
    step = pl.program_id(1)

    @pl.when(step == 0)
    def _():
        _ssm_sample_body(us_ref, wb_ref, kt_ref, wc_ref, pw_ref, d_ref, h0re_ref, h0im_ref,
                         ys_ref, sres_ref, sims_ref, e_ref, p_ref, rows=rows_s)

    @pl.when(step > 0)
    def _():
        _ssm_prompt_body(u_ref, wb_ref, kt_ref, wc_ref, pw_ref, d_ref, y_ref, sre_ref, sim_ref,
                         e_ref, p_ref, end_ref, cin_ref, carry_ref, n_seq=n_seq, rows=rows, tile=step - 1)


def _ssm(u8_p, n_p, u8_s, n_s, h0, wb, kt, wc, pw, d8):
    d = D_MODEL
    ns = SLAB_STATES
    rows_p, rows_s = u8_p.shape[1], u8_s.shape[1]
    per_seq = rows_p // n_p
    assert rows_s // n_s == SEG, "carried-state path scans one segment per sequence"
    assert n_p <= 8, "sequences ride the sublanes of the segment chain"
    rows = min(SSM_ROWS, per_seq)
    assert per_seq % rows == 0 and rows % GROUP_ROWS == 0 and rows_s % GROUP_ROWS == 0
    assert rows_s <= n_p * rows, "the carried-state step reuses the scan scratch"
    tiles = per_seq // rows
    slab = lambda q, i: (q, 0, 0)
    p_spec = pl.BlockSpec((N_PHASE, n_p, rows, LANE), lambda q, i: (0, 0, jnp.maximum(i - 1, 0), q))
    s_spec = pl.BlockSpec((N_PHASE, rows_s, LANE), lambda q, i: (0, 0, q))
    w_spec = pl.BlockSpec((None, 2 * ns, 2 * ns), slab)
    state_p = pl.BlockSpec((n_p, ns), lambda q, i: (0, q))
    state_s = pl.BlockSpec((n_s, ns), lambda q, i: (0, q))
    seg_rows = 8 * (rows // SEG)
    dims_p = (N_PHASE, n_p, per_seq, d)
    y_p, re_p, im_p, y_s, re_s, im_s = pl.pallas_call(
        functools.partial(_ssm_body, n_seq=n_p, rows=rows, rows_s=rows_s),
        grid=(N_SLAB, tiles + 1),
        in_specs=[p_spec, w_spec, w_spec, w_spec, pl.BlockSpec((None, 8, 2 * ns), slab),
                  pl.BlockSpec((None, 1, d), slab), s_spec, state_s, state_s],
        out_specs=[p_spec, state_p, state_p, s_spec, state_s, state_s],
        out_shape=[jax.ShapeDtypeStruct(dims_p, BF16),
                   jax.ShapeDtypeStruct((n_p, N_SLAB * ns), F32), jax.ShapeDtypeStruct((n_p, N_SLAB * ns), F32),
                   jax.ShapeDtypeStruct((N_PHASE, rows_s, d), BF16),
                   jax.ShapeDtypeStruct((n_s, N_SLAB * ns), F32), jax.ShapeDtypeStruct((n_s, N_SLAB * ns), F32)],
        scratch_shapes=[pltpu.VMEM((2 * HALF, n_p * rows, LANE), F32),
                        pltpu.VMEM((2 * HALF, n_p * rows, LANE), F32),
                        pltpu.VMEM((2 * HALF, seg_rows, LANE), F32),
                        pltpu.VMEM((2 * HALF, seg_rows, LANE), F32),
                        pltpu.VMEM((2 * HALF, 8, LANE), F32)],
        name="ssm",
        compiler_params=_params(("arbitrary", "arbitrary")),
    )(u8_p.reshape(dims_p), wb, kt, wc, pw, d8, u8_s, h0[0].reshape(n_s, -1), h0[1].reshape(n_s, -1))
    return (y_p.reshape(N_PHASE, rows_p, d), re_p, im_p), (y_s, re_s, im_s)


def _mix_out_ffn_body(y_ref, a_ref, gs_ref, x_ref, wglu_ref, wo_ref, g_ref, wg_ref, wu_ref, wd_ref, gf_ref,
                      o_hbm, ybuf_ref, sem, *, rows, n_steps, final_norm):
    d = D_MODEL
    step = pl.program_id(0)
    slot = step % 2

    def copies(of_step, of_slot):
        return [pltpu.make_async_copy(ybuf_ref.at[of_slot, t], o_hbm.at[pl.ds(of_step * rows, rows), t, :],
                                      sem.at[of_slot]) for t in range(N_PHASE)]

    @pl.when(step >= 2)
    def _():
        for c in copies(step - 2, slot):
            c.wait()

    glu = _dot(jax.nn.gelu(_stack_phases(y_ref).astype(F32)).astype(BF16), wglu_ref[...])
    y_ssm = glu[:, :d] * jax.nn.sigmoid(glu[:, d:])
    merged = _stack_phases(a_ref).astype(F32) + _stack_phases(gs_ref).astype(F32) * y_ssm
    x2 = _stack_phases(x_ref) + _dot(merged.astype(BF16), wo_ref[...])
    y = _ffn_math(x2, g_ref, wg_ref, wu_ref, wd_ref)
    if final_norm:
        y = _rms(y, gf_ref[...])
    for t in range(N_PHASE):
        ybuf_ref[slot, t] = y[t * rows:(t + 1) * rows]
    for c in copies(step, slot):
        c.start()

    @pl.when(step == n_steps - 1)
    def _():
        if n_steps > 1:
            for c in copies(step - 1, 1 - slot):
                c.wait()
        for c in copies(step, slot):
            c.wait()


def _mix_out_ffn(y8, a8, gs8, xp, row0, w_glu, w_o, g, wg, wu, wd, gf, final_norm):
    n_rows = y8.shape[1]
    d = D_MODEL
    rows = min(MIX_OUT_ROWS, n_rows)
    assert n_rows % rows == 0 and row0 % rows == 0
    ph_spec = pl.BlockSpec((N_PHASE, rows, d), lambda i: (0, i, 0))
    x_spec = pl.BlockSpec((N_PHASE, rows, d), lambda i: (0, row0 // rows + i, 0))
    weights = (w_glu, w_o, g, wg, wu, wd, gf)
    n_steps = n_rows // rows
    return pl.pallas_call(
        functools.partial(_mix_out_ffn_body, rows=rows, n_steps=n_steps, final_norm=final_norm),
        grid=(n_steps,),
        in_specs=[ph_spec, ph_spec, ph_spec, x_spec] + [_const_spec(w.shape) for w in weights],
        out_specs=pl.BlockSpec(memory_space=pl.ANY),
        out_shape=jax.ShapeDtypeStruct((n_rows, N_PHASE, d), F32),
        scratch_shapes=[pltpu.VMEM((2, N_PHASE, rows, d), F32), pltpu.SemaphoreType.DMA((2,))],
        name="mix_out_ffn",
        compiler_params=_params(("arbitrary",)),
    )(y8, a8, gs8, xp, *weights).reshape(N_PHASE * n_rows, d)


def kernel(x_prompt, x_sample, state_conv, state_ssm_re, state_ssm_im, norm_ffn1, w_ffn1_gate, w_ffn1_up, w_ffn1_down, norm_mix, w_in, w_conv, w_conv_out, ssm_lambda_re, ssm_lambda_im, ssm_log_step, ssm_b_re, ssm_b_im, ssm_c_re, ssm_c_im, ssm_d, w_glu, w_o, norm_ffn2, w_ffn2_gate, w_ffn2_up, w_ffn2_down, norm_final):
    depth, d = w_in.shape[0], w_in.shape[1]
    row = lambda v: v.reshape(1, -1)
    nf = row(norm_final)
    xt_p = x_prompt.reshape(-1, d)
    xt_s = x_sample.reshape(-1, d)
    n_p, n_s = x_prompt.shape[0], x_sample.shape[0]
    outs_p, outs_s = [], []
    for l in range(depth):
        last = l == depth - 1
        ssm_w, (f1g, f1u, f1d) = _ssm_prep(
            ssm_lambda_re[l], ssm_lambda_im[l], ssm_log_step[l], ssm_b_re[l], ssm_b_im[l], ssm_c_re[l],
            ssm_c_im[l], riders=(w_ffn1_gate[l], w_ffn1_up[l], w_ffn1_down[l]))
        ffn1 = (row(norm_ffn1[l]), f1g, f1u, f1d)
        xp, (win, wco, wgl, wo) = _ffn_to_phase(xt_p, xt_s, *ffn1, riders=(w_in[l], w_conv_out[l], w_glu[l], w_o[l]))
        rows_p, rows_s = xt_p.shape[0] // N_PHASE, xt_s.shape[0] // N_PHASE
        d8 = jnp.tile(ssm_d[l].reshape(N_SLAB, 1, SLAB), (1, 1, N_PHASE))
        mix_w = (row(norm_mix[l]), win, w_conv[l], wco)
        u_p, a_p, gs_p, conv_p, ffn2_w = _mix_in(xp, 0, rows_p, None, *mix_w, n_p,
                                                 riders=(w_ffn2_gate[l], w_ffn2_up[l], w_ffn2_down[l]))
        u_s, a_s, gs_s, conv_s, _ = _mix_in(xp, rows_p, rows_s, state_conv[l], *mix_w, n_s)
        (y_p, re_p, im_p), (y_s, re_s, im_s) = _ssm(u_p, n_p, u_s, n_s, (state_ssm_re[l], state_ssm_im[l]),
                                                    *ssm_w, d8)
        tail = (wgl, wo, row(norm_ffn2[l]), *ffn2_w, nf, last)
        xt_p = _mix_out_ffn(y_p, a_p, gs_p, xp, 0, *tail)
        xt_s = _mix_out_ffn(y_s, a_s, gs_s, xp, rows_p, *tail)
        outs_p.append([conv_p, re_p.reshape(n_p, -1, STATE_DIM), im_p.reshape(n_p, -1, STATE_DIM)])
        outs_s.append([conv_s, re_s.reshape(n_s, -1, STATE_DIM), im_s.reshape(n_s, -1, STATE_DIM)])
    stack = lambda outs: tuple(jnp.stack(leaf) for leaf in zip(*outs))
    return (xt_p.reshape(x_prompt.shape), xt_s.reshape(x_sample.shape), *stack(outs_p), *stack(outs_s))
```

```python
import functools

import jax
import jax.numpy as jnp
from jax import lax
from jax.experimental import pallas as pl
from jax.experimental.pallas import tpu as pltpu

F32 = jnp.float32
BF16 = jnp.bfloat16

D_MODEL = 1024
N_PHASE = 8
LANE = 128
MXU_TILE = 256
SLAB = 128
N_SLAB = D_MODEL // SLAB
GROUPS_PER_SLAB = 8
STATE_DIM = 64
SLAB_STATES = GROUPS_PER_SLAB * STATE_DIM
SEG = 4
RMS_EPS = 1e-6
VMEM_LIMIT = 60 * 1024 * 1024

FFN_ROWS = 1024
MIX_IN_ROWS = 128
MIX_OUT_ROWS = 64
SSM_ROWS = 256


def _rms(x, g):
    return x * lax.rsqrt(jnp.mean(x * x, axis=-1, keepdims=True) + RMS_EPS) * g


def _dot(a, b):
    return jnp.dot(a, b, preferred_element_type=F32)


def _const_spec(shape):
    zeros = (0,) * len(shape)
    return pl.BlockSpec(shape, lambda *_: zeros, pipeline_mode=pl.Buffered(1))


def _params(semantics):
    return pltpu.CompilerParams(dimension_semantics=semantics, vmem_limit_bytes=VMEM_LIMIT)


def _stack_phases(ref):
    return jnp.concatenate([ref[t] for t in range(N_PHASE)], axis=0)


def _ffn_math(x, g_ref, wg_ref, wu_ref, wd_ref):
    h = _rms(x, g_ref[...]).astype(BF16)
    act = (jax.nn.silu(_dot(h, wg_ref[...])) * _dot(h, wu_ref[...])).astype(BF16)
    return x + 0.5 * _dot(act, wd_ref[...])


def _rider_specs(weights, n_steps, step=lambda i: i):
    in_specs, out_specs, out_shapes = [], [], []
    for w in weights:
        n_blocks = n_steps
        while w.shape[0] % (16 * n_blocks):
            assert n_blocks % 2 == 0
            n_blocks //= 2
        spec = pl.BlockSpec((w.shape[0] // n_blocks, w.shape[1]),
                            lambda *idx, rep=n_steps // n_blocks: (step(*idx) // rep, 0))
        in_specs.append(spec)
        out_specs.append(spec)
        out_shapes.append(jax.ShapeDtypeStruct(w.shape, BF16))
    return in_specs, out_specs, out_shapes


def _convert_riders(in_refs, out_refs):
    for w_ref, o_ref in zip(in_refs, out_refs):
        o_ref[...] = w_ref[...].astype(BF16)


def _ffn_to_phase_body(*refs, rows, steps_a, n_steps, n_riders):
    xa_hbm, xb_hbm, g_ref, wg_ref, wu_ref, wd_ref = refs[:6]
    o_ref, xbuf_ref, sem = refs[6 + n_riders], refs[-2], refs[-1]
    step = pl.program_id(0)
    slot = step % 2

    def copies(src, row0, of_slot):
        return [pltpu.make_async_copy(src.at[pl.ds(row0, rows), t, :], xbuf_ref.at[of_slot, t], sem.at[of_slot])
                for t in range(N_PHASE)]

    def start(of_step, of_slot):
        @pl.when(of_step < steps_a)
        def _():
            for c in copies(xa_hbm, of_step * rows, of_slot):
                c.start()

        @pl.when(of_step >= steps_a)
        def _():
            for c in copies(xb_hbm, (of_step - steps_a) * rows, of_slot):
                c.start()

    @pl.when(step == 0)
    def _():
        start(step, slot)

    @pl.when(step + 1 < n_steps)
    def _():
        start(step + 1, 1 - slot)

    for c in copies(xa_hbm, 0, slot):
        c.wait()
    _convert_riders(refs[6:6 + n_riders], refs[7 + n_riders:7 + 2 * n_riders])
    x = jnp.concatenate([xbuf_ref[slot, t] for t in range(N_PHASE)], axis=0)
    y = _ffn_math(x, g_ref, wg_ref, wu_ref, wd_ref)
    for t in range(N_PHASE):
        o_ref[t] = y[t * rows:(t + 1) * rows]


def _ffn_to_phase(x_a, x_b, g, wg, wu, wd, riders=()):
    d = x_a.shape[1]
    rows_a, rows_b = x_a.shape[0] // N_PHASE, x_b.shape[0] // N_PHASE
    n_rows = rows_a + rows_b
    d_ff = wg.shape[1]
    rows = FFN_ROWS // N_PHASE
    assert rows_a % rows == 0 and rows_b % rows == 0
    steps_a, steps_b = rows_a // rows, rows_b // rows
    n_steps = steps_a + steps_b
    rider_in, rider_out, rider_shapes = _rider_specs(riders, steps_a, lambda i: jnp.minimum(i, steps_a - 1))
    out = pl.pallas_call(
        functools.partial(_ffn_to_phase_body, rows=rows, steps_a=steps_a, n_steps=n_steps, n_riders=len(riders)),
        grid=(n_steps,),
        in_specs=[
            pl.BlockSpec(memory_space=pl.ANY),
            pl.BlockSpec(memory_space=pl.ANY),
            _const_spec((1, d)),
            _const_spec((d, d_ff)),
            _const_spec((d, d_ff)),
            _const_spec((d_ff, d)),
        ] + rider_in,
        out_specs=[pl.BlockSpec((N_PHASE, rows, d), lambda i: (0, i, 0))] + rider_out,
        out_shape=[jax.ShapeDtypeStruct((N_PHASE, n_rows, d), F32)] + rider_shapes,
        scratch_shapes=[pltpu.VMEM((2, N_PHASE, rows, d), F32), pltpu.SemaphoreType.DMA((2,))],
        name="ffn_to_phase",
        compiler_params=_params(("arbitrary",)),
    )(x_a.reshape(rows_a, N_PHASE, d), x_b.reshape(rows_b, N_PHASE, d), g, wg, wu, wd, *riders)
    return out[0], tuple(out[1:])


def _mix_in_compute(x_ref, g, win_ref, wcv, wco_ref, shift, rows):
    d = D_MODEL
    h = _rms(_stack_phases(x_ref), g).astype(BF16)

    def proj(k):
        return _dot(h, win_ref[:, k * d:(k + 1) * d])

    z = proj(1) * proj(0)
    z6 = z[6 * rows:7 * rows]
    z7 = z[7 * rows:8 * rows]
    s6 = shift(z6, 0)
    s7 = shift(z7, 1)
    z1 = jnp.concatenate([s7, z[:7 * rows]], axis=0)
    z2 = jnp.concatenate([s6, s7, z[:6 * rows]], axis=0)
    conv = wcv[2:3] * z + wcv[1:2] * z1 + wcv[0:1] * z2
    y_conv = _dot((proj(2) * conv).astype(BF16), wco_ref[...])
    u = proj(3)
    a = jax.nn.sigmoid(proj(4)) * y_conv
    gs = jax.nn.sigmoid(proj(5))
    return u, a, gs, z6, z7


def _store_phases(ref, val, rows):
    val = val.astype(ref.dtype)
    for t in range(N_PHASE):
        ref[t] = val[t * rows:(t + 1) * rows]


def _mix_in_prompt_body(*refs, rows, n_riders):
    x_ref, g_ref, win_ref, wcv_ref, wco_ref = refs[:5]
    u_ref, a_ref, gs_ref, z_ref = refs[5 + n_riders:9 + n_riders]
    carry_ref = refs[-1]
    _convert_riders(refs[5:5 + n_riders], refs[9 + n_riders:9 + 2 * n_riders])

    @pl.when(pl.program_id(1) == 0)
    def _():
        carry_ref[...] = jnp.zeros_like(carry_ref)

    row_id = lax.broadcasted_iota(jnp.int32, (rows, D_MODEL), 0)

    def shift(z, k):
        return jnp.where(row_id == 0, carry_ref[k:k + 1, :], pltpu.roll(z, 1, 0))

    u, a, gs, z6, z7 = _mix_in_compute(x_ref, g_ref[...], win_ref, wcv_ref[...], wco_ref, shift, rows)
    last6 = z6[rows - 1:rows]
    last7 = z7[rows - 1:rows]
    carry_ref[0:1, :] = last6
    carry_ref[1:2, :] = last7
    z_ref[0:1, :] = last6
    z_ref[1:2, :] = last7
    _store_phases(u_ref, u, rows)
    _store_phases(a_ref, a, rows)
    _store_phases(gs_ref, gs, rows)


def _mix_in_sample_body(x_ref, start_ref, g_ref, win_ref, wcv_ref, wco_ref,
                        u_ref, a_ref, gs_ref, z6_ref, z7_ref, slab_ref, *, rows, rows_per_seq):
    row_id = lax.broadcasted_iota(jnp.int32, (rows, D_MODEL), 0)

    def shift(z, k):
        return jnp.where(row_id % rows_per_seq == 0, start_ref[k], pltpu.roll(z, 1, 0))

    u, a, gs, z6, z7 = _mix_in_compute(x_ref, g_ref[...], win_ref, wcv_ref[...], wco_ref, shift, rows)
    for z, z_ref in ((z6, z6_ref), (z7, z7_ref)):
        for lt in range(D_MODEL // LANE):
            slab_ref[...] = z[:, lt * LANE:(lt + 1) * LANE]
            z_ref[:, lt * LANE:(lt + 1) * LANE] = slab_ref[pl.ds(rows_per_seq - 1, rows // rows_per_seq,
                                                                stride=rows_per_seq), :]
    _store_phases(u_ref, u, rows)
    _store_phases(a_ref, a, rows)
    _store_phases(gs_ref, gs, rows)


def _mix_in(xp, row0, n_rows, conv_prev, g, w_in, w_conv, w_conv_out, n_seq, riders=()):
    d = D_MODEL
    rows_per_seq = n_rows // n_seq
    ph_shape = jax.ShapeDtypeStruct((N_PHASE, n_rows, d), BF16)
    weights = (g, w_in, w_conv, w_conv_out)
    weight_specs = [_const_spec(w.shape) for w in weights]
    if conv_prev is None:
        rows = min(MIX_IN_ROWS, rows_per_seq)
        assert rows_per_seq % rows == 0
        tiles = rows_per_seq // rows
        assert row0 % rows == 0
        ph_spec = pl.BlockSpec((N_PHASE, rows, d), lambda b, i: (0, b * tiles + i, 0))
        x_spec = pl.BlockSpec((N_PHASE, rows, d), lambda b, i: (0, row0 // rows + b * tiles + i, 0))
        last_spec = pl.BlockSpec((None, 2, d), lambda b, i: (b, 0, 0))
        rider_in, rider_out, rider_shapes = _rider_specs(riders, n_seq * tiles, lambda b, i: b * tiles + i)
        u8, a8, gs8, conv_state, *converted = pl.pallas_call(
            functools.partial(_mix_in_prompt_body, rows=rows, n_riders=len(riders)),
            grid=(n_seq, tiles),
            in_specs=[x_spec] + weight_specs + rider_in,
            out_specs=[ph_spec, ph_spec, ph_spec, last_spec] + rider_out,
            out_shape=[ph_shape, ph_shape, ph_shape, jax.ShapeDtypeStruct((n_seq, 2, d), F32)] + rider_shapes,
            scratch_shapes=[pltpu.VMEM((8, d), F32)],
            name="mix_in_prompt",
            compiler_params=_params(("arbitrary", "arbitrary")),
        )(xp, *weights, *riders)
    else:
        assert not riders
        converted = []
        rows = n_rows
        assert row0 % rows == 0
        start = jnp.repeat(jnp.swapaxes(conv_prev, 0, 1), rows_per_seq, axis=1)
        full = lambda shape: pl.BlockSpec(shape, lambda i: (0,) * len(shape))
        x_spec = pl.BlockSpec((N_PHASE, rows, d), lambda i: (0, row0 // rows, 0))
        u8, a8, gs8, z6, z7 = pl.pallas_call(
            functools.partial(_mix_in_sample_body, rows=rows, rows_per_seq=rows_per_seq),
            grid=(1,),
            in_specs=[x_spec, full((2, rows, d))] + weight_specs,
            out_specs=[full((N_PHASE, rows, d))] * 3 + [full((n_seq, d))] * 2,
            out_shape=[ph_shape, ph_shape, ph_shape,
                       jax.ShapeDtypeStruct((n_seq, d), F32), jax.ShapeDtypeStruct((n_seq, d), F32)],
            scratch_shapes=[pltpu.VMEM((rows, LANE), F32)],
            name="mix_in_sample",
            compiler_params=_params(("arbitrary",)),
        )(xp, start, *weights)
        conv_state = jnp.stack([z6, z7], axis=1)
    return u8, a8, gs8, conv_state, tuple(converted)


def _split_bf16(a):
    hi = a.astype(BF16)
    return hi, (a - hi.astype(F32)).astype(BF16)


def _ssm_prep_body(*refs, n_riders):
    bbre_ref, bbim_ref, ctre_ref, ctim_ref, lre_ref, lim_ref, lstep_ref = refs[:7]
    wb_ref, kt_ref, wc_ref, pw_ref = refs[7 + n_riders:11 + n_riders]
    _convert_riders(refs[7:7 + n_riders], refs[11 + n_riders:])
    ns = SLAB_STATES
    row_group = lax.broadcasted_iota(jnp.int32, (SLAB, LANE), 0) // (SLAB // GROUPS_PER_SLAB)
    lane_half = lax.broadcasted_iota(jnp.int32, (SLAB, LANE), 1) // STATE_DIM

    def block_diag(ref):
        tiles = [jnp.where(row_group == 2 * k + lane_half, ref[...], 0.0) for k in range(GROUPS_PER_SLAB // 2)]
        return jnp.concatenate(tiles, axis=1)

    bbre, bbim = block_diag(bbre_ref), block_diag(bbim_ref)
    ctre, ctim = block_diag(ctre_ref), block_diag(ctim_ref)
    lre, lim = lre_ref[...], lim_ref[...]
    step = jnp.exp(lstep_ref[...])

    def lam_pow(n):
        mag = jnp.exp((n * lre) * step)
        ang = (n * lim) * step
        return mag * jnp.cos(ang), mag * jnp.sin(ang)

    l1re, l1im = lam_pow(1)
    den = lre * lre + lim * lim
    fre = ((l1re - 1.0) * lre + l1im * lim) / den
    fim = (l1im * lre - (l1re - 1.0) * lim) / den

    def dot_nt(a, b_split):
        nt = lambda p, q: lax.dot_general(p, q, (((1,), (1,)), ((), ())), preferred_element_type=F32)
        a_hi, a_lo = _split_bf16(a)
        b_hi, b_lo = b_split
        return nt(a_hi, b_hi) + nt(a_hi, b_lo) + nt(a_lo, b_hi)

    ctre_split = _split_bf16(ctre)
    ctim_split = _split_bf16(ctim)
    kt_ref[...] = jnp.zeros_like(kt_ref)
    for k in range(N_PHASE):
        pre, pim = lam_pow(k)
        gre = fre * pre - fim * pim
        gim = fre * pim + fim * pre
        are = bbre * gre - bbim * gim
        aim = bbre * gim + bbim * gre
        j = N_PHASE - 1 - k
        wb_ref[j * SLAB:(j + 1) * SLAB, 0:ns] = are.astype(BF16)
        wb_ref[j * SLAB:(j + 1) * SLAB, ns:2 * ns] = aim.astype(BF16)
        kk = (dot_nt(are, ctre_split) - dot_nt(aim, ctim_split)).astype(BF16)
        for jj in range(N_PHASE - k):
            t = jj + k
            kt_ref[jj * SLAB:(jj + 1) * SLAB, t * SLAB:(t + 1) * SLAB] = kk
    for t in range(N_PHASE):
        pre, pim = lam_pow(t + 1)
        cre = ctre * pre - ctim * pim
        cim = ctre * pim + ctim * pre
        wc_ref[0:ns, t * SLAB:(t + 1) * SLAB] = cre.T.astype(BF16)
        wc_ref[ns:2 * ns, t * SLAB:(t + 1) * SLAB] = (-cim).T.astype(BF16)
    pw_ref[...] = jnp.zeros_like(pw_ref)
    for r, n in enumerate((N_PHASE, N_PHASE * SEG)):
        pre, pim = lam_pow(n)
        pw_ref[r:r + 1, 0:ns] = pre
        pw_ref[r:r + 1, ns:2 * ns] = pim


def _ssm_prep(lam_re, lam_im, log_step, b_re, b_im, c_re, c_im, riders=()):
    ns = SLAB_STATES

    def twice(m):
        m = m.reshape(N_SLAB, SLAB, STATE_DIM)
        return jnp.concatenate([m, m], axis=-1)

    def vec(v):
        return v.reshape(N_SLAB, 1, ns)

    mats = [twice(jnp.transpose(b_re, (0, 2, 1))), twice(jnp.transpose(b_im, (0, 2, 1))), twice(c_re), twice(c_im)]
    vecs = [vec(lam_re), vec(lam_im), vec(jnp.broadcast_to(log_step[:, None], lam_re.shape))]
    w_shape = jax.ShapeDtypeStruct((N_SLAB, 2 * ns, 2 * ns), BF16)
    w_spec = pl.BlockSpec((None, 2 * ns, 2 * ns), lambda q: (q, 0, 0))
    rider_in, rider_out, rider_shapes = _rider_specs(riders, N_SLAB)
    out = pl.pallas_call(
        functools.partial(_ssm_prep_body, n_riders=len(riders)),
        grid=(N_SLAB,),
        in_specs=[pl.BlockSpec((None, SLAB, 2 * STATE_DIM), lambda q: (q, 0, 0))] * 4
        + [pl.BlockSpec((None, 1, ns), lambda q: (q, 0, 0))] * 3 + rider_in,
        out_specs=[w_spec, w_spec, w_spec, pl.BlockSpec((None, 8, 2 * ns), lambda q: (q, 0, 0))] + rider_out,
        out_shape=[w_shape, w_shape, w_shape, jax.ShapeDtypeStruct((N_SLAB, 8, 2 * ns), F32)] + rider_shapes,
        name="ssm_prep",
        compiler_params=_params(("arbitrary",)),
    )(*mats, *vecs, *riders)
    return tuple(out[:4]), tuple(out[4:])


def _cmul(are, aim, bre, bim):
    return are * bre - aim * bim, are * bim + aim * bre


HALF = SLAB_STATES // LANE
GROUP_ROWS = 8 * SEG


def _mult(pw_ref, r, c):
    return (pw_ref[r:r + 1, c * LANE:(c + 1) * LANE],
            pw_ref[r:r + 1, (HALF + c) * LANE:(HALF + c + 1) * LANE])


def _state_increments(lhs, wb_ref, e_ref):
    e = _dot(lhs, wb_ref[...])
    for lt in range(2 * HALF):
        e_ref[lt, :e.shape[0]] = e[:, lt * LANE:(lt + 1) * LANE]


def _scan_local(e_ref, p_ref, pw_ref, c, base):
    l8re, l8im = _mult(pw_ref, 0, c)
    idx = pl.ds(base, 8, stride=SEG)
    lre = e_ref[c, idx, :]
    lim = e_ref[HALF + c, idx, :]
    for i in range(1, SEG):
        idx = pl.ds(base + i, 8, stride=SEG)
        p_ref[c, idx, :] = lre
        p_ref[HALF + c, idx, :] = lim
        mre, mim = _cmul(l8re, l8im, lre, lim)
        lre = mre + e_ref[c, idx, :]
        lim = mim + e_ref[HALF + c, idx, :]
    return lre, lim


def _scan_fixup(p_ref, pw_ref, c, base, cre, cim):
    l8re, l8im = _mult(pw_ref, 0, c)
    idx = pl.ds(base, 8, stride=SEG)
    p_ref[c, idx, :] = cre
    p_ref[HALF + c, idx, :] = cim
    for i in range(1, SEG):
        idx = pl.ds(base + i, 8, stride=SEG)
        cre, cim = _cmul(l8re, l8im, cre, cim)
        p_ref[c, idx, :] = p_ref[c, idx, :] + cre
        p_ref[HALF + c, idx, :] = p_ref[HALF + c, idx, :] + cim


def _readout(lhs, p_ref, kt_ref, wc_ref, d_ref):
    prev = jnp.concatenate([p_ref[lt, :lhs.shape[0]] for lt in range(2 * HALF)], axis=1)
    toep = jnp.concatenate(
        [_dot(lhs[:, :(k + 1) * MXU_TILE], kt_ref[:(k + 1) * MXU_TILE, k * MXU_TILE:(k + 1) * MXU_TILE])
         for k in range(N_PHASE * LANE // MXU_TILE)], axis=1)
    y = toep + _dot(prev.astype(BF16), wc_ref[...]) + d_ref[...] * lhs.astype(F32)
    return y.astype(BF16)


def _ssm_prompt_body(u_ref, wb_ref, kt_ref, wc_ref, pw_ref, d_ref, y_ref, sre_ref, sim_ref,
                     e_ref, p_ref, end_ref, cin_ref, carry_ref, *, n_seq, rows, tile):
    total = n_seq * rows
    lhs = jnp.concatenate([u_ref[t].reshape(total, LANE) for t in range(N_PHASE)], axis=1)
    _state_increments(lhs, wb_ref, e_ref)

    @pl.when((pl.program_id(0) == 0) & (tile == 0))
    def _():
        end_ref[...] = jnp.zeros_like(end_ref)

    @pl.when(tile == 0)
    def _():
        carry_ref[...] = jnp.zeros_like(carry_ref)

    groups = rows // GROUP_ROWS
    for c in range(HALF):
        for b in range(n_seq):
            for j in range(groups):
                lre, lim = _scan_local(e_ref, p_ref, pw_ref, c, b * rows + j * GROUP_ROWS)
                idx = pl.ds(j * 64 + b, 8, stride=8)
                end_ref[c, idx, :] = lre
                end_ref[HALF + c, idx, :] = lim
    for c in range(HALF):
        lsre, lsim = _mult(pw_ref, 1, c)
        cre = carry_ref[c]
        cim = carry_ref[HALF + c]
        for s in range(rows // SEG):
            cin_ref[c, 8 * s:8 * s + 8, :] = cre
            cin_ref[HALF + c, 8 * s:8 * s + 8, :] = cim
            mre, mim = _cmul(lsre, lsim, cre, cim)
            cre = mre + end_ref[c, 8 * s:8 * s + 8, :]
            cim = mim + end_ref[HALF + c, 8 * s:8 * s + 8, :]
        carry_ref[c] = cre
        carry_ref[HALF + c] = cim
        sre_ref[:, c * LANE:(c + 1) * LANE] = cre[:n_seq]
        sim_ref[:, c * LANE:(c + 1) * LANE] = cim[:n_seq]
    for c in range(HALF):
        for b in range(n_seq):
            for j in range(groups):
                idx = pl.ds(j * 64 + b, 8, stride=8)
                _scan_fixup(p_ref, pw_ref, c, b * rows + j * GROUP_ROWS,
                            cin_ref[c, idx, :], cin_ref[HALF + c, idx, :])
    y = _readout(lhs, p_ref, kt_ref, wc_ref, d_ref)
    for t in range(N_PHASE):
        y_ref[t] = y[:, t * LANE:(t + 1) * LANE].reshape(n_seq, rows, LANE)


def _ssm_sample_body(u_ref, wb_ref, kt_ref, wc_ref, pw_ref, d_ref, h0re_ref, h0im_ref,
                     y_ref, sre_ref, sim_ref, e_ref, p_ref, *, rows):
    lhs = jnp.concatenate([u_ref[t] for t in range(N_PHASE)], axis=1)
    _state_increments(lhs, wb_ref, e_ref)
    for c in range(HALF):
        lsre, lsim = _mult(pw_ref, 1, c)
        for j in range(rows // GROUP_ROWS):
            lre, lim = _scan_local(e_ref, p_ref, pw_ref, c, j * GROUP_ROWS)
            cre = h0re_ref[8 * j:8 * j + 8, c * LANE:(c + 1) * LANE]
            cim = h0im_ref[8 * j:8 * j + 8, c * LANE:(c + 1) * LANE]
            mre, mim = _cmul(lsre, lsim, cre, cim)
            sre_ref[8 * j:8 * j + 8, c * LANE:(c + 1) * LANE] = mre + lre
            sim_ref[8 * j:8 * j + 8, c * LANE:(c + 1) * LANE] = mim + lim
            _scan_fixup(p_ref, pw_ref, c, j * GROUP_ROWS, cre, cim)
    y = _readout(lhs, p_ref, kt_ref, wc_ref, d_ref)
    for t in range(N_PHASE):
        y_ref[t] = y[:, t * LANE:(t + 1) * LANE]


def _ssm_body(u_ref, wb_ref, kt_ref, wc_ref, pw_ref, d_ref, us_ref, h0re_ref, h0im_ref,
              y_ref, sre_ref, sim_ref, ys_ref, sres_ref, sims_ref,
              e_ref, p_ref, end_ref, cin_ref, carry_ref, *, n_seq, rows, rows_s):
    step = pl.program_id(1)

    @pl.when(step == 0)
    def _():
        _ssm_sample_body(us_ref, wb_ref, kt_ref, wc_ref, pw_ref, d_ref, h0re_ref, h0im_ref,
                         ys_ref, sres_ref, sims_ref, e_ref, p_ref, rows=rows_s)

    @pl.when(step > 0)
    def _():
        _ssm_prompt_body(u_ref, wb_ref, kt_ref, wc_ref, pw_ref, d_ref, y_ref, sre_ref, sim_ref,
                         e_ref, p_ref, end_ref, cin_ref, carry_ref, n_seq=n_seq, rows=rows, tile=step - 1)


def _ssm(u8_p, n_p, u8_s, n_s, h0, wb, kt, wc, pw, d8):
    d = D_MODEL
    ns = SLAB_STATES
    rows_p, rows_s = u8_p.shape[1], u8_s.shape[1]
    per_seq = rows_p // n_p
    assert rows_s // n_s == SEG, "carried-state path scans one segment per sequence"
    assert n_p <= 8, "sequences ride the sublanes of the segment chain"
    rows = min(SSM_ROWS, per_seq)
    assert per_seq % rows == 0 and rows % GROUP_ROWS == 0 and rows_s % GROUP_ROWS == 0
    assert rows_s <= n_p * rows, "the carried-state step reuses the scan scratch"
    tiles = per_seq // rows
    slab = lambda q, i: (q, 0, 0)
    p_spec = pl.BlockSpec((N_PHASE, n_p, rows, LANE), lambda q, i: (0, 0, jnp.maximum(i - 1, 0), q))
    s_spec = pl.BlockSpec((N_PHASE, rows_s, LANE), lambda q, i: (0, 0, q))
    w_spec = pl.BlockSpec((None, 2 * ns, 2 * ns), slab)
    state_p = pl.BlockSpec((n_p, ns), lambda q, i: (0, q))
    state_s = pl.BlockSpec((n_s, ns), lambda q, i: (0, q))
    seg_rows = 8 * (rows // SEG)
    dims_p = (N_PHASE, n_p, per_seq, d)
    y_p, re_p, im_p, y_s, re_s, im_s = pl.pallas_call(
        functools.partial(_ssm_body, n_seq=n_p, rows=rows, rows_s=rows_s),
        grid=(N_SLAB, tiles + 1),
        in_specs=[p_spec, w_spec, w_spec, w_spec, pl.BlockSpec((None, 8, 2 * ns), slab),
                  pl.BlockSpec((None, 1, d), slab), s_spec, state_s, state_s],
        out_specs=[p_spec, state_p, state_p, s_spec, state_s, state_s],
        out_shape=[jax.ShapeDtypeStruct(dims_p, BF16),
                   jax.ShapeDtypeStruct((n_p, N_SLAB * ns), F32), jax.ShapeDtypeStruct((n_p, N_SLAB * ns), F32),
                   jax.ShapeDtypeStruct((N_PHASE, rows_s, d), BF16),
                   jax.ShapeDtypeStruct((n_s, N_SLAB * ns), F32), jax.ShapeDtypeStruct((n_s, N_SLAB * ns), F32)],
        scratch_shapes=[pltpu.VMEM((2 * HALF, n_p * rows, LANE), F32),
                        pltpu.VMEM((2 * HALF, n_p * rows, LANE), F32),
                        pltpu.VMEM((2 * HALF, seg_rows, LANE), F32),
                        pltpu.VMEM((2 * HALF, seg_rows, LANE), F32),
                        pltpu.VMEM((2 * HALF, 8, LANE), F32)],
        name="ssm",
        compiler_params=_params(("arbitrary", "arbitrary")),
    )(u8_p.reshape(dims_p), wb, kt, wc, pw, d8, u8_s, h0[0].reshape(n_s, -1), h0[1].reshape(n_s, -1))
    return (y_p.reshape(N_PHASE, rows_p, d), re_p, im_p), (y_s, re_s, im_s)


def _mix_out_ffn_body(ya_hbm, aa_hbm, gsa_hbm, yb_hbm, ab_hbm, gsb_hbm, x_ref,
                      wglu_ref, wo_ref, g_ref, wg_ref, wu_ref, wd_ref, gf_ref,
                      oa_hbm, ob_hbm, inbuf_ref, ybuf_ref, sem_in, sem_out, *, rows, steps_a, n_steps, final_norm):
    d = D_MODEL
    step = pl.program_id(0)
    slot = step % 2

    def in_copies(srcs, row0, of_slot):
        return [pltpu.make_async_copy(src.at[:, pl.ds(row0, rows), :], inbuf_ref.at[of_slot, k], sem_in.at[of_slot])
                for k, src in enumerate(srcs)]

    def out_copies(dst, row0, of_slot):
        return [pltpu.make_async_copy(ybuf_ref.at[of_slot, t], dst.at[pl.ds(row0, rows), t, :], sem_out.at[of_slot])
                for t in range(N_PHASE)]

    def start_for_group(of_step, copies_a, copies_b):
        @pl.when(of_step < steps_a)
        def _():
            for c in copies_a(of_step * rows):
                c.start()

        @pl.when(of_step >= steps_a)
        def _():
            for c in copies_b((of_step - steps_a) * rows):
                c.start()

    def start_in(of_step, of_slot):
        start_for_group(of_step, lambda r: in_copies((ya_hbm, aa_hbm, gsa_hbm), r, of_slot),
                        lambda r: in_copies((yb_hbm, ab_hbm, gsb_hbm), r, of_slot))

    @pl.when(step == 0)
    def _():
        start_in(step, slot)

    @pl.when(step + 1 < n_steps)
    def _():
        start_in(step + 1, 1 - slot)

    @pl.when(step >= 2)
    def _():
        for c in out_copies(oa_hbm, 0, slot):
            c.wait()

    for c in in_copies((ya_hbm, aa_hbm, gsa_hbm), 0, slot):
        c.wait()
    y_ref, a_ref, gs_ref = (inbuf_ref.at[slot, k] for k in range(3))
    glu = _dot(jax.nn.gelu(_stack_phases(y_ref).astype(F32)).astype(BF16), wglu_ref[...])
    y_ssm = glu[:, :d] * jax.nn.sigmoid(glu[:, d:])
    merged = _stack_phases(a_ref).astype(F32) + _stack_phases(gs_ref).astype(F32) * y_ssm
    x2 = _stack_phases(x_ref) + _dot(merged.astype(BF16), wo_ref[...])
    y = _ffn_math(x2, g_ref, wg_ref, wu_ref, wd_ref)
    if final_norm:
        y = _rms(y, gf_ref[...])
    for t in range(N_PHASE):
        ybuf_ref[slot, t] = y[t * rows:(t + 1) * rows]
    start_for_group(step, lambda r: out_copies(oa_hbm, r, slot), lambda r: out_copies(ob_hbm, r, slot))

    @pl.when(step == n_steps - 1)
    def _():
        if n_steps > 1:
            for c in out_copies(oa_hbm, 0, 1 - slot):
                c.wait()
        for c in out_copies(oa_hbm, 0, slot):
            c.wait()


def _mix_out_ffn(acts_a, acts_b, xp, w_glu, w_o, g, wg, wu, wd, gf, final_norm):
    rows_a, rows_b = acts_a[0].shape[1], acts_b[0].shape[1]
    d = D_MODEL
    rows = MIX_OUT_ROWS
    assert rows_a % rows == 0 and rows_b % rows == 0 and xp.shape[1] == rows_a + rows_b
    steps_a, n_steps = rows_a // rows, (rows_a + rows_b) // rows
    weights = (w_glu, w_o, g, wg, wu, wd, gf)
    any_spec = pl.BlockSpec(memory_space=pl.ANY)
    out_a, out_b = pl.pallas_call(
        functools.partial(_mix_out_ffn_body, rows=rows, steps_a=steps_a, n_steps=n_steps, final_norm=final_norm),
        grid=(n_steps,),
        in_specs=[any_spec] * 6 + [pl.BlockSpec((N_PHASE, rows, d), lambda i: (0, i, 0))]
        + [_const_spec(w.shape) for w in weights],
        out_specs=[any_spec, any_spec],
        out_shape=[jax.ShapeDtypeStruct((rows_a, N_PHASE, d), F32), jax.ShapeDtypeStruct((rows_b, N_PHASE, d), F32)],
        scratch_shapes=[pltpu.VMEM((2, 3, N_PHASE, rows, d), BF16),
                        pltpu.VMEM((2, N_PHASE, rows, d), F32),
                        pltpu.SemaphoreType.DMA((2,)), pltpu.SemaphoreType.DMA((2,))],
        name="mix_out_ffn",
        compiler_params=_params(("arbitrary",)),
    )(*acts_a, *acts_b, xp, *weights)
    return out_a.reshape(N_PHASE * rows_a, d), out_b.reshape(N_PHASE * rows_b, d)


def kernel(x_prompt, x_sample, state_conv, state_ssm_re, state_ssm_im, norm_ffn1, w_ffn1_gate, w_ffn1_up, w_ffn1_down, norm_mix, w_in, w_conv, w_conv_out, ssm_lambda_re, ssm_lambda_im, ssm_log_step, ssm_b_re, ssm_b_im, ssm_c_re, ssm_c_im, ssm_d, w_glu, w_o, norm_ffn2, w_ffn2_gate, w_ffn2_up, w_ffn2_down, norm_final):
    depth, d = w_in.shape[0], w_in.shape[1]
    row = lambda v: v.reshape(1, -1)
    nf = row(norm_final)
    xt_p = x_prompt.reshape(-1, d)
    xt_s = x_sample.reshape(-1, d)
    n_p, n_s = x_prompt.shape[0], x_sample.shape[0]
    outs_p, outs_s = [], []
    for l in range(depth):
        last = l == depth - 1
        ssm_w, (f1g, f1u, f1d) = _ssm_prep(
            ssm_lambda_re[l], ssm_lambda_im[l], ssm_log_step[l], ssm_b_re[l], ssm_b_im[l], ssm_c_re[l],
            ssm_c_im[l], riders=(w_ffn1_gate[l], w_ffn1_up[l], w_ffn1_down[l]))
        ffn1 = (row(norm_ffn1[l]), f1g, f1u, f1d)
        xp, (win, wco, wgl, wo) = _ffn_to_phase(xt_p, xt_s, *ffn1, riders=(w_in[l], w_conv_out[l], w_glu[l], w_o[l]))
        rows_p, rows_s = xt_p.shape[0] // N_PHASE, xt_s.shape[0] // N_PHASE
        d8 = jnp.tile(ssm_d[l].reshape(N_SLAB, 1, SLAB), (1, 1, N_PHASE))
        mix_w = (row(norm_mix[l]), win, w_conv[l], wco)
        u_p, a_p, gs_p, conv_p, ffn2_w = _mix_in(xp, 0, rows_p, None, *mix_w, n_p,
                                                 riders=(w_ffn2_gate[l], w_ffn2_up[l], w_ffn2_down[l]))
        u_s, a_s, gs_s, conv_s, _ = _mix_in(xp, rows_p, rows_s, state_conv[l], *mix_w, n_s)
        (y_p, re_p, im_p), (y_s, re_s, im_s) = _ssm(u_p, n_p, u_s, n_s, (state_ssm_re[l], state_ssm_im[l]),
                                                    *ssm_w, d8)
        tail = (wgl, wo, row(norm_ffn2[l]), *ffn2_w, nf, last)
        xt_p, xt_s = _mix_out_ffn((y_p, a_p, gs_p), (y_s, a_s, gs_s), xp, *tail)
        outs_p.append([conv_p, re_p.reshape(n_p, -1, STATE_DIM), im_p.reshape(n_p, -1, STATE_DIM)])
        outs_s.append([conv_s, re_s.reshape(n_s, -1, STATE_DIM), im_s.reshape(n_s, -1, STATE_DIM)])
    stack = lambda outs: tuple(jnp.stack(leaf) for leaf in zip(*outs))
    return (xt_p.reshape(x_prompt.shape), xt_s.reshape(x_sample.shape), *stack(outs_p), *stack(outs_s))
```

```python
import functools

import jax
import jax.numpy as jnp
from jax import lax
from jax.experimental import pallas as pl
from jax.experimental.pallas import tpu as pltpu

F32 = jnp.float32
BF16 = jnp.bfloat16

D_MODEL = 1024
N_PHASE = 8
LANE = 128
MXU_TILE = 256
SLAB = 128
N_SLAB = D_MODEL // SLAB
SUB = 64
SUBS = SLAB // SUB
GROUP_SIZE = 16
STATE_DIM = 64
SUB_STATES = (SUB // GROUP_SIZE) * STATE_DIM
SUB_K = N_PHASE * SUB
SEG = 4
RMS_EPS = 1e-6
VMEM_LIMIT = 60 * 1024 * 1024

FFN_ROWS = 1024
MIX_IN_ROWS = 128
MIX_OUT_ROWS = 64
SSM_ROWS = 256


def _rms(x, g):
    return x * lax.rsqrt(jnp.mean(x * x, axis=-1, keepdims=True) + RMS_EPS) * g


def _dot(a, b):
    return jnp.dot(a, b, preferred_element_type=F32)


def _const_spec(shape):
    zeros = (0,) * len(shape)
    return pl.BlockSpec(shape, lambda *_: zeros, pipeline_mode=pl.Buffered(1))


def _params(semantics):
    return pltpu.CompilerParams(dimension_semantics=semantics, vmem_limit_bytes=VMEM_LIMIT)


def _stack_phases(ref):
    return jnp.concatenate([ref[t] for t in range(N_PHASE)], axis=0)


def _ffn_math(x, g_ref, wg_ref, wu_ref, wd_ref):
    h = _rms(x, g_ref[...]).astype(BF16)
    act = (jax.nn.silu(_dot(h, wg_ref[...])) * _dot(h, wu_ref[...])).astype(BF16)
    return x + 0.5 * _dot(act, wd_ref[...])


def _rider_specs(weights, n_steps, step=lambda i: i):
    in_specs, out_specs, out_shapes = [], [], []
    for w in weights:
        n_blocks = n_steps
        while w.shape[0] % (16 * n_blocks):
            assert n_blocks % 2 == 0
            n_blocks //= 2
        spec = pl.BlockSpec((w.shape[0] // n_blocks, w.shape[1]),
                            lambda *idx, rep=n_steps // n_blocks: (step(*idx) // rep, 0))
        in_specs.append(spec)
        out_specs.append(spec)
        out_shapes.append(jax.ShapeDtypeStruct(w.shape, BF16))
    return in_specs, out_specs, out_shapes


def _convert_riders(in_refs, out_refs):
    for w_ref, o_ref in zip(in_refs, out_refs):
        o_ref[...] = w_ref[...].astype(BF16)


def _ffn_to_phase_body(*refs, rows, steps_a, n_steps, n_riders):
    xa_hbm, xb_hbm, g_ref, wg_ref, wu_ref, wd_ref = refs[:6]
    o_ref, xbuf_ref, sem = refs[6 + n_riders], refs[-2], refs[-1]
    step = pl.program_id(0)
    slot = step % 2

    def copies(src, row0, of_slot):
        return [pltpu.make_async_copy(src.at[pl.ds(row0, rows), t, :], xbuf_ref.at[of_slot, t], sem.at[of_slot])
                for t in range(N_PHASE)]

    def start(of_step, of_slot):
        @pl.when(of_step < steps_a)
        def _():
            for c in copies(xa_hbm, of_step * rows, of_slot):
                c.start()

        @pl.when(of_step >= steps_a)
        def _():
            for c in copies(xb_hbm, (of_step - steps_a) * rows, of_slot):
                c.start()

    @pl.when(step == 0)
    def _():
        start(step, slot)

    @pl.when(step + 1 < n_steps)
    def _():
        start(step + 1, 1 - slot)

    for c in copies(xa_hbm, 0, slot):
        c.wait()
    _convert_riders(refs[6:6 + n_riders], refs[7 + n_riders:7 + 2 * n_riders])
    x = jnp.concatenate([xbuf_ref[slot, t] for t in range(N_PHASE)], axis=0)
    y = _ffn_math(x, g_ref, wg_ref, wu_ref, wd_ref)
    for t in range(N_PHASE):
        o_ref[t] = y[t * rows:(t + 1) * rows]


def _ffn_to_phase(x_a, x_b, g, wg, wu, wd, riders=()):
    d = x_a.shape[1]
    rows_a, rows_b = x_a.shape[0] // N_PHASE, x_b.shape[0] // N_PHASE
    n_rows = rows_a + rows_b
    d_ff = wg.shape[1]
    rows = FFN_ROWS // N_PHASE
    assert rows_a % rows == 0 and rows_b % rows == 0
    steps_a, steps_b = rows_a // rows, rows_b // rows
    n_steps = steps_a + steps_b
    rider_in, rider_out, rider_shapes = _rider_specs(riders, steps_a, lambda i: jnp.minimum(i, steps_a - 1))
    out = pl.pallas_call(
        functools.partial(_ffn_to_phase_body, rows=rows, steps_a=steps_a, n_steps=n_steps, n_riders=len(riders)),
        grid=(n_steps,),
        in_specs=[
            pl.BlockSpec(memory_space=pl.ANY),
            pl.BlockSpec(memory_space=pl.ANY),
            _const_spec((1, d)),
            _const_spec((d, d_ff)),
            _const_spec((d, d_ff)),
            _const_spec((d_ff, d)),
        ] + rider_in,
        out_specs=[pl.BlockSpec((N_PHASE, rows, d), lambda i: (0, i, 0))] + rider_out,
        out_shape=[jax.ShapeDtypeStruct((N_PHASE, n_rows, d), F32)] + rider_shapes,
        scratch_shapes=[pltpu.VMEM((2, N_PHASE, rows, d), F32), pltpu.SemaphoreType.DMA((2,))],
        name="ffn_to_phase",
        compiler_params=_params(("arbitrary",)),
    )(x_a.reshape(rows_a, N_PHASE, d), x_b.reshape(rows_b, N_PHASE, d), g, wg, wu, wd, *riders)
    return out[0], tuple(out[1:])


def _mix_in_compute(x_ref, g, win_ref, wcv, wco_ref, shift, rows):
    d = D_MODEL
    h = _rms(_stack_phases(x_ref), g).astype(BF16)

    def proj(k):
        return _dot(h, win_ref[:, k * d:(k + 1) * d])

    z = proj(1) * proj(0)
    z6 = z[6 * rows:7 * rows]
    z7 = z[7 * rows:8 * rows]
    s6 = shift(z6, 0)
    s7 = shift(z7, 1)
    z1 = jnp.concatenate([s7, z[:7 * rows]], axis=0)
    z2 = jnp.concatenate([s6, s7, z[:6 * rows]], axis=0)
    conv = wcv[2:3] * z + wcv[1:2] * z1 + wcv[0:1] * z2
    y_conv = _dot((proj(2) * conv).astype(BF16), wco_ref[...])
    u = proj(3)
    a = jax.nn.sigmoid(proj(4)) * y_conv
    gs = jax.nn.sigmoid(proj(5))
    return u, a, gs, z6, z7


def _store_phases(ref, val, rows):
    val = val.astype(ref.dtype)
    for t in range(N_PHASE):
        ref[t] = val[t * rows:(t + 1) * rows]


def _store_sub_slab_pairs(ref, val, rows):
    for k in range(N_PHASE // 2):
        a = val[2 * k * rows:(2 * k + 1) * rows]
        b = val[(2 * k + 1) * rows:(2 * k + 2) * rows]
        for q in range(D_MODEL // LANE):
            lanes = slice(q * LANE, (q + 1) * LANE)
            for h, tile in enumerate(_pair_halves(a[:, lanes], b[:, lanes])):
                ref[2 * k + h, :, lanes] = tile.astype(ref.dtype)


def _mix_in_prompt_body(*refs, rows, n_riders):
    x_ref, g_ref, win_ref, wcv_ref, wco_ref = refs[:5]
    u_ref, a_ref, gs_ref, z_ref = refs[5 + n_riders:9 + n_riders]
    carry_ref = refs[-1]
    _convert_riders(refs[5:5 + n_riders], refs[9 + n_riders:9 + 2 * n_riders])

    @pl.when(pl.program_id(1) == 0)
    def _():
        carry_ref[...] = jnp.zeros_like(carry_ref)

    row_id = lax.broadcasted_iota(jnp.int32, (rows, D_MODEL), 0)

    def shift(z, k):
        return jnp.where(row_id == 0, carry_ref[k:k + 1, :], pltpu.roll(z, 1, 0))

    u, a, gs, z6, z7 = _mix_in_compute(x_ref, g_ref[...], win_ref, wcv_ref[...], wco_ref, shift, rows)
    last6 = z6[rows - 1:rows]
    last7 = z7[rows - 1:rows]
    carry_ref[0:1, :] = last6
    carry_ref[1:2, :] = last7
    z_ref[0:1, :] = last6
    z_ref[1:2, :] = last7
    _store_sub_slab_pairs(u_ref, u, rows)
    _store_phases(a_ref, a, rows)
    _store_phases(gs_ref, gs, rows)


def _mix_in_sample_body(x_ref, start_ref, g_ref, win_ref, wcv_ref, wco_ref,
                        u_ref, a_ref, gs_ref, z6_ref, z7_ref, slab_ref, *, rows, rows_per_seq):
    row_id = lax.broadcasted_iota(jnp.int32, (rows, D_MODEL), 0)

    def shift(z, k):
        return jnp.where(row_id % rows_per_seq == 0, start_ref[k], pltpu.roll(z, 1, 0))

    u, a, gs, z6, z7 = _mix_in_compute(x_ref, g_ref[...], win_ref, wcv_ref[...], wco_ref, shift, rows)
    for z, z_ref in ((z6, z6_ref), (z7, z7_ref)):
        for lt in range(D_MODEL // LANE):
            slab_ref[...] = z[:, lt * LANE:(lt + 1) * LANE]
            z_ref[:, lt * LANE:(lt + 1) * LANE] = slab_ref[pl.ds(rows_per_seq - 1, rows // rows_per_seq,
                                                                stride=rows_per_seq), :]
    _store_sub_slab_pairs(u_ref, u, rows)
    _store_phases(a_ref, a, rows)
    _store_phases(gs_ref, gs, rows)


def _mix_in(xp, row0, n_rows, conv_prev, g, w_in, w_conv, w_conv_out, n_seq, riders=()):
    d = D_MODEL
    rows_per_seq = n_rows // n_seq
    ph_shape = jax.ShapeDtypeStruct((N_PHASE, n_rows, d), BF16)
    weights = (g, w_in, w_conv, w_conv_out)
    weight_specs = [_const_spec(w.shape) for w in weights]
    if conv_prev is None:
        rows = min(MIX_IN_ROWS, rows_per_seq)
        assert rows_per_seq % rows == 0
        tiles = rows_per_seq // rows
        assert row0 % rows == 0
        ph_spec = pl.BlockSpec((N_PHASE, rows, d), lambda b, i: (0, b * tiles + i, 0))
        x_spec = pl.BlockSpec((N_PHASE, rows, d), lambda b, i: (0, row0 // rows + b * tiles + i, 0))
        last_spec = pl.BlockSpec((None, 2, d), lambda b, i: (b, 0, 0))
        rider_in, rider_out, rider_shapes = _rider_specs(riders, n_seq * tiles, lambda b, i: b * tiles + i)
        u8, a8, gs8, conv_state, *converted = pl.pallas_call(
            functools.partial(_mix_in_prompt_body, rows=rows, n_riders=len(riders)),
            grid=(n_seq, tiles),
            in_specs=[x_spec] + weight_specs + rider_in,
            out_specs=[ph_spec, ph_spec, ph_spec, last_spec] + rider_out,
            out_shape=[ph_shape, ph_shape, ph_shape, jax.ShapeDtypeStruct((n_seq, 2, d), F32)] + rider_shapes,
            scratch_shapes=[pltpu.VMEM((8, d), F32)],
            name="mix_in_prompt",
            compiler_params=_params(("arbitrary", "arbitrary")),
        )(xp, *weights, *riders)
    else:
        assert not riders
        converted = []
        rows = n_rows
        assert row0 % rows == 0
        start = jnp.repeat(jnp.swapaxes(conv_prev, 0, 1), rows_per_seq, axis=1)
        full = lambda shape: pl.BlockSpec(shape, lambda i: (0,) * len(shape))
        x_spec = pl.BlockSpec((N_PHASE, rows, d), lambda i: (0, row0 // rows, 0))
        u8, a8, gs8, z6, z7 = pl.pallas_call(
            functools.partial(_mix_in_sample_body, rows=rows, rows_per_seq=rows_per_seq),
            grid=(1,),
            in_specs=[x_spec, full((2, rows, d))] + weight_specs,
            out_specs=[full((N_PHASE, rows, d))] * 3 + [full((n_seq, d))] * 2,
            out_shape=[ph_shape, ph_shape, ph_shape,
                       jax.ShapeDtypeStruct((n_seq, d), F32), jax.ShapeDtypeStruct((n_seq, d), F32)],
            scratch_shapes=[pltpu.VMEM((rows, LANE), F32)],
            name="mix_in_sample",
            compiler_params=_params(("arbitrary",)),
        )(xp, start, *weights)
        conv_state = jnp.stack([z6, z7], axis=1)
    return u8, a8, gs8, conv_state, tuple(converted)


def _split_bf16(a):
    hi = a.astype(BF16)
    return hi, (a - hi.astype(F32)).astype(BF16)


def _ssm_prep_body(*refs, n_riders):
    bbre_ref, bbim_ref, ctre_ref, ctim_ref, lre_ref, lim_ref, lstep_ref = refs[:7]
    wb_ref, kt_ref, wc_ref, pw_ref = refs[7 + n_riders:11 + n_riders]
    _convert_riders(refs[7:7 + n_riders], refs[11 + n_riders:])
    ns = SUB_STATES
    row_group = lax.broadcasted_iota(jnp.int32, (SUB, LANE), 0) // GROUP_SIZE
    lane_half = lax.broadcasted_iota(jnp.int32, (SUB, LANE), 1) // STATE_DIM
    low_half = lax.broadcasted_iota(jnp.int32, (SUB, LANE), 1) < SUB

    def block_diag(ref):
        tiles = [jnp.where(row_group == 2 * k + lane_half, ref[...], 0.0) for k in range(ns // LANE)]
        return jnp.concatenate(tiles, axis=1)

    bbre, bbim = block_diag(bbre_ref), block_diag(bbim_ref)
    ctre, ctim = block_diag(ctre_ref), block_diag(ctim_ref)
    lre, lim = lre_ref[...], lim_ref[...]
    step = jnp.exp(lstep_ref[...])

    def lam_pow(n):
        mag = jnp.exp((n * lre) * step)
        ang = (n * lim) * step
        return mag * jnp.cos(ang), mag * jnp.sin(ang)

    l1re, l1im = lam_pow(1)
    den = lre * lre + lim * lim
    fre = ((l1re - 1.0) * lre + l1im * lim) / den
    fim = (l1im * lre - (l1re - 1.0) * lim) / den

    def dot_nt(a, b_split):
        nt = lambda p, q: lax.dot_general(p, q, (((1,), (1,)), ((), ())), preferred_element_type=F32)
        a_hi, a_lo = _split_bf16(a)
        b_hi, b_lo = b_split
        return nt(a_hi, b_hi) + nt(a_hi, b_lo) + nt(a_lo, b_hi)

    def c_pair(n):
        (p0re, p0im), (p1re, p1im) = lam_pow(n), lam_pow(n + 1)
        return (jnp.concatenate([ctre * p0re - ctim * p0im, ctre * p1re - ctim * p1im], axis=0),
                jnp.concatenate([ctre * p0im + ctim * p0re, ctre * p1im + ctim * p1re], axis=0))

    c01re, c01im = c_pair(0)
    c01re_split = _split_bf16(c01re)
    c01im_split = _split_bf16(c01im)
    lag_pairs = {}
    for k in range(N_PHASE):
        pre, pim = lam_pow(k)
        gre = fre * pre - fim * pim
        gim = fre * pim + fim * pre
        are = bbre * gre - bbim * gim
        aim = bbre * gim + bbim * gre
        j = N_PHASE - 1 - k
        wb_ref[j * SUB:(j + 1) * SUB, 0:ns] = are.astype(BF16)
        wb_ref[j * SUB:(j + 1) * SUB, ns:2 * ns] = aim.astype(BF16)
        if k < N_PHASE - 1:
            lag_pairs[k] = dot_nt(are, c01re_split) - dot_nt(aim, c01im_split)
    kt_ref[...] = jnp.zeros_like(kt_ref)
    first_odd = jnp.where(low_half, 0.0, pltpu.roll(lag_pairs[0], SUB, 1))
    for j in range(N_PHASE):
        for tile in range(j // 2, N_PHASE // 2):
            lag = 2 * tile - j
            pair = first_odd if lag < 0 else lag_pairs[lag]
            kt_ref[j * SUB:(j + 1) * SUB, tile * LANE:(tile + 1) * LANE] = pair.astype(BF16)
    for tile in range(N_PHASE // 2):
        cre, cim = c_pair(2 * tile + 1)
        wc_ref[0:ns, tile * LANE:(tile + 1) * LANE] = cre.T.astype(BF16)
        wc_ref[ns:2 * ns, tile * LANE:(tile + 1) * LANE] = (-cim).T.astype(BF16)
    pw_ref[...] = jnp.zeros_like(pw_ref)
    for r, n in enumerate((N_PHASE, N_PHASE * SEG)):
        pre, pim = lam_pow(n)
        pw_ref[r:r + 1, 0:ns] = pre
        pw_ref[r:r + 1, ns:2 * ns] = pim


def _ssm_prep(lam_re, lam_im, log_step, b_re, b_im, c_re, c_im, riders=()):
    ns = SUB_STATES
    n_sub = N_SLAB * SUBS
    assert 2 * STATE_DIM == LANE and SUB_K == 2 * ns

    def twice(m):
        m = m.reshape(n_sub, SUB, STATE_DIM)
        return jnp.concatenate([m, m], axis=-1)

    def vec(v):
        return v.reshape(n_sub, 1, ns)

    mats = [twice(jnp.transpose(b_re, (0, 2, 1))), twice(jnp.transpose(b_im, (0, 2, 1))), twice(c_re), twice(c_im)]
    vecs = [vec(lam_re), vec(lam_im), vec(jnp.broadcast_to(log_step[:, None], lam_re.shape))]
    w_shape = jax.ShapeDtypeStruct((n_sub, SUB_K, 2 * ns), BF16)
    w_spec = pl.BlockSpec((None, SUB_K, 2 * ns), lambda q: (q, 0, 0))
    rider_in, rider_out, rider_shapes = _rider_specs(riders, n_sub)
    out = pl.pallas_call(
        functools.partial(_ssm_prep_body, n_riders=len(riders)),
        grid=(n_sub,),
        in_specs=[pl.BlockSpec((None, SUB, LANE), lambda q: (q, 0, 0))] * 4
        + [pl.BlockSpec((None, 1, ns), lambda q: (q, 0, 0))] * 3 + rider_in,
        out_specs=[w_spec, w_spec, w_spec, pl.BlockSpec((None, 8, 2 * ns), lambda q: (q, 0, 0))] + rider_out,
        out_shape=[w_shape, w_shape, w_shape, jax.ShapeDtypeStruct((n_sub, 8, 2 * ns), F32)] + rider_shapes,
        name="ssm_prep",
        compiler_params=_params(("arbitrary",)),
    )(*mats, *vecs, *riders)
    return tuple(w.reshape(N_SLAB, SUBS, *w.shape[1:]) for w in out[:4]), tuple(out[4:])


def _cmul(are, aim, bre, bim):
    return are * bre - aim * bim, are * bim + aim * bre


HALF = SUB_STATES // LANE
GROUP_ROWS = 8 * SEG


def _pair_halves(a, b):
    low = lax.broadcasted_iota(jnp.int32, a.shape, 1) < SUB
    return jnp.where(low, a, pltpu.roll(b, SUB, 1)), jnp.where(low, pltpu.roll(a, SUB, 1), b)


def _sub_slab_operands(u_ref):
    return [jnp.concatenate([u_ref[2 * k + h].reshape(-1, LANE) for k in range(N_PHASE // 2)], axis=1)
            for h in range(SUBS)]


def _slab_phase_tiles(subs):
    tiles = []
    for k in range(N_PHASE // 2):
        tiles += _pair_halves(subs[0][:, k * LANE:(k + 1) * LANE], subs[1][:, k * LANE:(k + 1) * LANE])
    return [y.astype(BF16) for y in tiles]


def _mult(pw_ref, r, c):
    return (pw_ref[r:r + 1, c * LANE:(c + 1) * LANE],
            pw_ref[r:r + 1, (HALF + c) * LANE:(HALF + c + 1) * LANE])


def _state_increments(lhs, wb_ref, e_ref):
    e = _dot(lhs, wb_ref[...])
    for lt in range(2 * HALF):
        e_ref[lt, :e.shape[0]] = e[:, lt * LANE:(lt + 1) * LANE]


def _scan_local(e_ref, p_ref, pw_ref, c, base):
    l8re, l8im = _mult(pw_ref, 0, c)
    idx = pl.ds(base, 8, stride=SEG)
    lre = e_ref[c, idx, :]
    lim = e_ref[HALF + c, idx, :]
    for i in range(1, SEG):
        idx = pl.ds(base + i, 8, stride=SEG)
        p_ref[c, idx, :] = lre
        p_ref[HALF + c, idx, :] = lim
        mre, mim = _cmul(l8re, l8im, lre, lim)
        lre = mre + e_ref[c, idx, :]
        lim = mim + e_ref[HALF + c, idx, :]
    return lre, lim


def _scan_fixup(p_ref, pw_ref, c, base, cre, cim):
    l8re, l8im = _mult(pw_ref, 0, c)
    idx = pl.ds(base, 8, stride=SEG)
    p_ref[c, idx, :] = cre
    p_ref[HALF + c, idx, :] = cim
    for i in range(1, SEG):
        idx = pl.ds(base + i, 8, stride=SEG)
        cre, cim = _cmul(l8re, l8im, cre, cim)
        p_ref[c, idx, :] = p_ref[c, idx, :] + cre
        p_ref[HALF + c, idx, :] = p_ref[HALF + c, idx, :] + cim


def _readout(lhs, p_ref, kt_ref, wc_ref, d_ref):
    prev = jnp.concatenate([p_ref[lt, :lhs.shape[0]] for lt in range(2 * HALF)], axis=1)
    toep = jnp.concatenate(
        [_dot(lhs[:, :(k + 1) * MXU_TILE], kt_ref[:(k + 1) * MXU_TILE, k * MXU_TILE:(k + 1) * MXU_TILE])
         for k in range(SUB_K // MXU_TILE)], axis=1)
    return toep + _dot(prev.astype(BF16), wc_ref[...]) + d_ref[...] * lhs.astype(F32)


def _ssm_prompt_body(u_ref, wb_ref, kt_ref, wc_ref, pw_ref, d_ref, y_ref, sre_ref, sim_ref,
                     e_ref, p_ref, end_ref, cin_ref, carry_ref, *, n_seq, rows, tile):
    @pl.when((pl.program_id(0) == 0) & (tile == 0))
    def _():
        end_ref[...] = jnp.zeros_like(end_ref)

    @pl.when(tile == 0)
    def _():
        carry_ref[...] = jnp.zeros_like(carry_ref)

    operands = _sub_slab_operands(u_ref)
    groups = rows // GROUP_ROWS
    outs = []
    for h, lhs in enumerate(operands):
        e, p, ends, cins, carry, pw = e_ref.at[h], p_ref.at[h], end_ref.at[h], cin_ref.at[h], carry_ref.at[h], pw_ref.at[h]
        lane0 = h * SUB_STATES
        _state_increments(lhs, wb_ref.at[h], e)
        for c in range(HALF):
            for b in range(n_seq):
                for j in range(groups):
                    lre, lim = _scan_local(e, p, pw, c, b * rows + j * GROUP_ROWS)
                    idx = pl.ds(j * 64 + b, 8, stride=8)
                    ends[c, idx, :] = lre
                    ends[HALF + c, idx, :] = lim
        for c in range(HALF):
            lsre, lsim = _mult(pw, 1, c)
            cre = carry[c]
            cim = carry[HALF + c]
            for s in range(rows // SEG):
                cins[c, 8 * s:8 * s + 8, :] = cre
                cins[HALF + c, 8 * s:8 * s + 8, :] = cim
                mre, mim = _cmul(lsre, lsim, cre, cim)
                cre = mre + ends[c, 8 * s:8 * s + 8, :]
                cim = mim + ends[HALF + c, 8 * s:8 * s + 8, :]
            carry[c] = cre
            carry[HALF + c] = cim
            sre_ref[:, lane0 + c * LANE:lane0 + (c + 1) * LANE] = cre[:n_seq]
            sim_ref[:, lane0 + c * LANE:lane0 + (c + 1) * LANE] = cim[:n_seq]
        for c in range(HALF):
            for b in range(n_seq):
                for j in range(groups):
                    idx = pl.ds(j * 64 + b, 8, stride=8)
                    _scan_fixup(p, pw, c, b * rows + j * GROUP_ROWS, cins[c, idx, :], cins[HALF + c, idx, :])
        outs.append(_readout(lhs, p, kt_ref.at[h], wc_ref.at[h], d_ref.at[h]))
    for t, y in enumerate(_slab_phase_tiles(outs)):
        y_ref[t] = y.reshape(n_seq, rows, LANE)


def _ssm_sample_body(u_ref, wb_ref, kt_ref, wc_ref, pw_ref, d_ref, h0re_ref, h0im_ref,
                     y_ref, sre_ref, sim_ref, e_ref, p_ref, *, rows):
    outs = []
    for h, lhs in enumerate(_sub_slab_operands(u_ref)):
        e, p, pw = e_ref.at[h], p_ref.at[h], pw_ref.at[h]
        _state_increments(lhs, wb_ref.at[h], e)
        for c in range(HALF):
            lanes = slice(h * SUB_STATES + c * LANE, h * SUB_STATES + (c + 1) * LANE)
            lsre, lsim = _mult(pw, 1, c)
            for j in range(rows // GROUP_ROWS):
                lre, lim = _scan_local(e, p, pw, c, j * GROUP_ROWS)
                cre = h0re_ref[8 * j:8 * j + 8, lanes]
                cim = h0im_ref[8 * j:8 * j + 8, lanes]
                mre, mim = _cmul(lsre, lsim, cre, cim)
                sre_ref[8 * j:8 * j + 8, lanes] = mre + lre
                sim_ref[8 * j:8 * j + 8, lanes] = mim + lim
                _scan_fixup(p, pw, c, j * GROUP_ROWS, cre, cim)
        outs.append(_readout(lhs, p, kt_ref.at[h], wc_ref.at[h], d_ref.at[h]))
    for t, y in enumerate(_slab_phase_tiles(outs)):
        y_ref[t] = y


def _ssm_body(u_ref, wb_ref, kt_ref, wc_ref, pw_ref, d_ref, us_ref, h0re_ref, h0im_ref,
              y_ref, sre_ref, sim_ref, ys_ref, sres_ref, sims_ref,
              e_ref, p_ref, end_ref, cin_ref, carry_ref, *, n_seq, rows, rows_s):
    step = pl.program_id(1)

    @pl.when(step == 0)
    def _():
        _ssm_sample_body(us_ref, wb_ref, kt_ref, wc_ref, pw_ref, d_ref, h0re_ref, h0im_ref,
                         ys_ref, sres_ref, sims_ref, e_ref, p_ref, rows=rows_s)

    @pl.when(step > 0)
    def _():
        _ssm_prompt_body(u_ref, wb_ref, kt_ref, wc_ref, pw_ref, d_ref, y_ref, sre_ref, sim_ref,
                         e_ref, p_ref, end_ref, cin_ref, carry_ref, n_seq=n_seq, rows=rows, tile=step - 1)


def _ssm(u8_p, n_p, u8_s, n_s, h0, wb, kt, wc, pw, d8):
    d = D_MODEL
    ns = SUBS * SUB_STATES
    rows_p, rows_s = u8_p.shape[1], u8_s.shape[1]
    per_seq = rows_p // n_p
    assert rows_s // n_s == SEG, "carried-state path scans one segment per sequence"
    assert n_p <= 8, "sequences ride the sublanes of the segment chain"
    rows = min(SSM_ROWS, per_seq)
    assert per_seq % rows == 0 and rows % GROUP_ROWS == 0 and rows_s % GROUP_ROWS == 0
    assert rows_s <= n_p * rows, "the carried-state step reuses the scan scratch"
    tiles = per_seq // rows
    slab = lambda q, i: (q, 0, 0, 0)
    p_spec = pl.BlockSpec((N_PHASE, n_p, rows, LANE), lambda q, i: (0, 0, jnp.maximum(i - 1, 0), q))
    s_spec = pl.BlockSpec((N_PHASE, rows_s, LANE), lambda q, i: (0, 0, q))
    w_spec = pl.BlockSpec((None, SUBS, SUB_K, 2 * SUB_STATES), slab)
    state_p = pl.BlockSpec((n_p, ns), lambda q, i: (0, q))
    state_s = pl.BlockSpec((n_s, ns), lambda q, i: (0, q))
    seg_rows = 8 * (rows // SEG)
    dims_p = (N_PHASE, n_p, per_seq, d)
    y_p, re_p, im_p, y_s, re_s, im_s = pl.pallas_call(
        functools.partial(_ssm_body, n_seq=n_p, rows=rows, rows_s=rows_s),
        grid=(N_SLAB, tiles + 1),
        in_specs=[p_spec, w_spec, w_spec, w_spec, pl.BlockSpec((None, SUBS, 8, 2 * SUB_STATES), slab),
                  pl.BlockSpec((None, SUBS, 1, SUB_K), slab), s_spec, state_s, state_s],
        out_specs=[p_spec, state_p, state_p, s_spec, state_s, state_s],
        out_shape=[jax.ShapeDtypeStruct(dims_p, BF16),
                   jax.ShapeDtypeStruct((n_p, N_SLAB * ns), F32), jax.ShapeDtypeStruct((n_p, N_SLAB * ns), F32),
                   jax.ShapeDtypeStruct((N_PHASE, rows_s, d), BF16),
                   jax.ShapeDtypeStruct((n_s, N_SLAB * ns), F32), jax.ShapeDtypeStruct((n_s, N_SLAB * ns), F32)],
        scratch_shapes=[pltpu.VMEM((SUBS, 2 * HALF, n_p * rows, LANE), F32),
                        pltpu.VMEM((SUBS, 2 * HALF, n_p * rows, LANE), F32),
                        pltpu.VMEM((SUBS, 2 * HALF, seg_rows, LANE), F32),
                        pltpu.VMEM((SUBS, 2 * HALF, seg_rows, LANE), F32),
                        pltpu.VMEM((SUBS, 2 * HALF, 8, LANE), F32)],
        name="ssm",
        compiler_params=_params(("arbitrary", "arbitrary")),
    )(u8_p.reshape(dims_p), wb, kt, wc, pw, d8, u8_s, h0[0].reshape(n_s, -1), h0[1].reshape(n_s, -1))
    return (y_p.reshape(N_PHASE, rows_p, d), re_p, im_p), (y_s, re_s, im_s)


def _mix_out_ffn_body(ya_hbm, aa_hbm, gsa_hbm, yb_hbm, ab_hbm, gsb_hbm, x_ref,
                      wglu_ref, wo_ref, g_ref, wg_ref, wu_ref, wd_ref, gf_ref,
                      oa_hbm, ob_hbm, inbuf_ref, ybuf_ref, sem_in, sem_out, *, rows, steps_a, n_steps, final_norm):
    d = D_MODEL
    step = pl.program_id(0)
    slot = step % 2

    def in_copies(srcs, row0, of_slot):
        return [pltpu.make_async_copy(src.at[:, pl.ds(row0, rows), :], inbuf_ref.at[of_slot, k], sem_in.at[of_slot])
                for k, src in enumerate(srcs)]

    def out_copies(dst, row0, of_slot):
        return [pltpu.make_async_copy(ybuf_ref.at[of_slot, t], dst.at[pl.ds(row0, rows), t, :], sem_out.at[of_slot])
                for t in range(N_PHASE)]

    def start_for_group(of_step, copies_a, copies_b):
        @pl.when(of_step < steps_a)
        def _():
            for c in copies_a(of_step * rows):
                c.start()

        @pl.when(of_step >= steps_a)
        def _():
            for c in copies_b((of_step - steps_a) * rows):
                c.start()

    def start_in(of_step, of_slot):
        start_for_group(of_step, lambda r: in_copies((ya_hbm, aa_hbm, gsa_hbm), r, of_slot),
                        lambda r: in_copies((yb_hbm, ab_hbm, gsb_hbm), r, of_slot))

    @pl.when(step == 0)
    def _():
        start_in(step, slot)

    @pl.when(step + 1 < n_steps)
    def _():
        start_in(step + 1, 1 - slot)

    @pl.when(step >= 2)
    def _():
        for c in out_copies(oa_hbm, 0, slot):
            c.wait()

    for c in in_copies((ya_hbm, aa_hbm, gsa_hbm), 0, slot):
        c.wait()
    y_ref, a_ref, gs_ref = (inbuf_ref.at[slot, k] for k in range(3))
    glu = _dot(jax.nn.gelu(_stack_phases(y_ref).astype(F32)).astype(BF16), wglu_ref[...])
    y_ssm = glu[:, :d] * jax.nn.sigmoid(glu[:, d:])
    merged = _stack_phases(a_ref).astype(F32) + _stack_phases(gs_ref).astype(F32) * y_ssm
    x2 = _stack_phases(x_ref) + _dot(merged.astype(BF16), wo_ref[...])
    y = _ffn_math(x2, g_ref, wg_ref, wu_ref, wd_ref)
    if final_norm:
        y = _rms(y, gf_ref[...])
    for t in range(N_PHASE):
        ybuf_ref[slot, t] = y[t * rows:(t + 1) * rows]
    start_for_group(step, lambda r: out_copies(oa_hbm, r, slot), lambda r: out_copies(ob_hbm, r, slot))

    @pl.when(step == n_steps - 1)
    def _():
        if n_steps > 1:
            for c in out_copies(oa_hbm, 0, 1 - slot):
                c.wait()
        for c in out_copies(oa_hbm, 0, slot):
            c.wait()


def _mix_out_ffn(acts_a, acts_b, xp, w_glu, w_o, g, wg, wu, wd, gf, final_norm):
    rows_a, rows_b = acts_a[0].shape[1], acts_b[0].shape[1]
    d = D_MODEL
    rows = MIX_OUT_ROWS
    assert rows_a % rows == 0 and rows_b % rows == 0 and xp.shape[1] == rows_a + rows_b
    steps_a, n_steps = rows_a // rows, (rows_a + rows_b) // rows
    weights = (w_glu, w_o, g, wg, wu, wd, gf)
    any_spec = pl.BlockSpec(memory_space=pl.ANY)
    out_a, out_b = pl.pallas_call(
        functools.partial(_mix_out_ffn_body, rows=rows, steps_a=steps_a, n_steps=n_steps, final_norm=final_norm),
        grid=(n_steps,),
        in_specs=[any_spec] * 6 + [pl.BlockSpec((N_PHASE, rows, d), lambda i: (0, i, 0))]
        + [_const_spec(w.shape) for w in weights],
        out_specs=[any_spec, any_spec],
        out_shape=[jax.ShapeDtypeStruct((rows_a, N_PHASE, d), F32), jax.ShapeDtypeStruct((rows_b, N_PHASE, d), F32)],
        scratch_shapes=[pltpu.VMEM((2, 3, N_PHASE, rows, d), BF16),
                        pltpu.VMEM((2, N_PHASE, rows, d), F32),
                        pltpu.SemaphoreType.DMA((2,)), pltpu.SemaphoreType.DMA((2,))],
        name="mix_out_ffn",
        compiler_params=_params(("arbitrary",)),
    )(*acts_a, *acts_b, xp, *weights)
    return out_a.reshape(N_PHASE * rows_a, d), out_b.reshape(N_PHASE * rows_b, d)


def kernel(x_prompt, x_sample, state_conv, state_ssm_re, state_ssm_im, norm_ffn1, w_ffn1_gate, w_ffn1_up, w_ffn1_down, norm_mix, w_in, w_conv, w_conv_out, ssm_lambda_re, ssm_lambda_im, ssm_log_step, ssm_b_re, ssm_b_im, ssm_c_re, ssm_c_im, ssm_d, w_glu, w_o, norm_ffn2, w_ffn2_gate, w_ffn2_up, w_ffn2_down, norm_final):
    depth, d = w_in.shape[0], w_in.shape[1]
    row = lambda v: v.reshape(1, -1)
    nf = row(norm_final)
    xt_p = x_prompt.reshape(-1, d)
    xt_s = x_sample.reshape(-1, d)
    n_p, n_s = x_prompt.shape[0], x_sample.shape[0]
    outs_p, outs_s = [], []
    for l in range(depth):
        last = l == depth - 1
        ssm_w, (f1g, f1u, f1d) = _ssm_prep(
            ssm_lambda_re[l], ssm_lambda_im[l], ssm_log_step[l], ssm_b_re[l], ssm_b_im[l], ssm_c_re[l],
            ssm_c_im[l], riders=(w_ffn1_gate[l], w_ffn1_up[l], w_ffn1_down[l]))
        ffn1 = (row(norm_ffn1[l]), f1g, f1u, f1d)
        xp, (win, wco, wgl, wo) = _ffn_to_phase(xt_p, xt_s, *ffn1, riders=(w_in[l], w_conv_out[l], w_glu[l], w_o[l]))
        rows_p, rows_s = xt_p.shape[0] // N_PHASE, xt_s.shape[0] // N_PHASE
        d8 = jnp.tile(ssm_d[l].reshape(N_SLAB, SUBS, 1, SUB), (1, 1, 1, N_PHASE))
        mix_w = (row(norm_mix[l]), win, w_conv[l], wco)
        u_p, a_p, gs_p, conv_p, ffn2_w = _mix_in(xp, 0, rows_p, None, *mix_w, n_p,
                                                 riders=(w_ffn2_gate[l], w_ffn2_up[l], w_ffn2_down[l]))
        u_s, a_s, gs_s, conv_s, _ = _mix_in(xp, rows_p, rows_s, state_conv[l], *mix_w, n_s)
        (y_p, re_p, im_p), (y_s, re_s, im_s) = _ssm(u_p, n_p, u_s, n_s, (state_ssm_re[l], state_ssm_im[l]),
                                                    *ssm_w, d8)
        tail = (wgl, wo, row(norm_ffn2[l]), *ffn2_w, nf, last)
        xt_p, xt_s = _mix_out_ffn((y_p, a_p, gs_p), (y_s, a_s, gs_s), xp, *tail)
        outs_p.append([conv_p, re_p.reshape(n_p, -1, STATE_DIM), im_p.reshape(n_p, -1, STATE_DIM)])
        outs_s.append([conv_s, re_s.reshape(n_s, -1, STATE_DIM), im_s.reshape(n_s, -1, STATE_DIM)])
    stack = lambda outs: tuple(jnp.stack(leaf) for leaf in zip(*outs))
    return (xt_p.reshape(x_prompt.shape), xt_s.reshape(x_sample.shape), *stack(outs_p), *stack(outs_s))
```

```python
import functools

import jax
import jax.numpy as jnp
from jax import lax
from jax.experimental import pallas as pl
from jax.experimental.pallas import tpu as pltpu

F32 = jnp.float32
BF16 = jnp.bfloat16

D_MODEL = 1024
N_PHASE = 8
LANE = 128
MXU_TILE = 256
SLAB = 128
N_SLAB = D_MODEL // SLAB
SUB = 64
SUBS = SLAB // SUB
GROUP_SIZE = 16
STATE_DIM = 64
SUB_STATES = (SUB // GROUP_SIZE) * STATE_DIM
SUB_K = N_PHASE * SUB
SEG = 4
RMS_EPS = 1e-6
VMEM_LIMIT = 60 * 1024 * 1024

FFN_ROWS = 1024
MIX_IN_ROWS = 128
MIX_OUT_ROWS = 64
SSM_ROWS = 256


def _rms(x, g):
    return x * lax.rsqrt(jnp.mean(x * x, axis=-1, keepdims=True) + RMS_EPS) * g


def _dot(a, b):
    return jnp.dot(a, b, preferred_element_type=F32)


def _const_spec(shape):
    zeros = (0,) * len(shape)
    return pl.BlockSpec(shape, lambda *_: zeros, pipeline_mode=pl.Buffered(1))


def _params(semantics):
    return pltpu.CompilerParams(dimension_semantics=semantics, vmem_limit_bytes=VMEM_LIMIT)


def _stack_phases(ref):
    return jnp.concatenate([ref[t] for t in range(N_PHASE)], axis=0)


def _ffn_math(x, g_ref, wg_ref, wu_ref, wd_ref):
    h = _rms(x, g_ref[...]).astype(BF16)
    act = (jax.nn.silu(_dot(h, wg_ref[...])) * _dot(h, wu_ref[...])).astype(BF16)
    return x + 0.5 * _dot(act, wd_ref[...])


def _rider_specs(weights, n_steps, step=lambda i: i):
    in_specs, out_specs, out_shapes = [], [], []
    for w in weights:
        n_blocks = n_steps
        while w.shape[0] % (16 * n_blocks):
            assert n_blocks % 2 == 0
            n_blocks //= 2
        spec = pl.BlockSpec((w.shape[0] // n_blocks, w.shape[1]),
                            lambda *idx, rep=n_steps // n_blocks: (step(*idx) // rep, 0))
        in_specs.append(spec)
        out_specs.append(spec)
        out_shapes.append(jax.ShapeDtypeStruct(w.shape, BF16))
    return in_specs, out_specs, out_shapes


def _convert_riders(in_refs, out_refs):
    for w_ref, o_ref in zip(in_refs, out_refs):
        o_ref[...] = w_ref[...].astype(BF16)


def _ffn_to_phase_body(*refs, rows, steps_a, n_steps, n_riders):
    xa_hbm, xb_hbm, g_ref, wg_ref, wu_ref, wd_ref = refs[:6]
    o_ref, xbuf_ref, sem = refs[6 + n_riders], refs[-2], refs[-1]
    step = pl.program_id(0)
    slot = step % 2

    def copies(src, row0, of_slot):
        return [pltpu.make_async_copy(src.at[pl.ds(row0, rows), t, :], xbuf_ref.at[of_slot, t], sem.at[of_slot])
                for t in range(N_PHASE)]

    def start(of_step, of_slot):
        @pl.when(of_step < steps_a)
        def _():
            for c in copies(xa_hbm, of_step * rows, of_slot):
                c.start()

        @pl.when(of_step >= steps_a)
        def _():
            for c in copies(xb_hbm, (of_step - steps_a) * rows, of_slot):
                c.start()

    @pl.when(step == 0)
    def _():
        start(step, slot)

    @pl.when(step + 1 < n_steps)
    def _():
        start(step + 1, 1 - slot)

    for c in copies(xa_hbm, 0, slot):
        c.wait()
    _convert_riders(refs[6:6 + n_riders], refs[7 + n_riders:7 + 2 * n_riders])
    x = jnp.concatenate([xbuf_ref[slot, t] for t in range(N_PHASE)], axis=0)
    y = _ffn_math(x, g_ref, wg_ref, wu_ref, wd_ref)
    for t in range(N_PHASE):
        o_ref[t] = y[t * rows:(t + 1) * rows]


def _ffn_to_phase(x_a, x_b, g, wg, wu, wd, riders=()):
    d = x_a.shape[1]
    rows_a, rows_b = x_a.shape[0] // N_PHASE, x_b.shape[0] // N_PHASE
    n_rows = rows_a + rows_b
    d_ff = wg.shape[1]
    rows = FFN_ROWS // N_PHASE
    assert rows_a % rows == 0 and rows_b % rows == 0
    steps_a, steps_b = rows_a // rows, rows_b // rows
    n_steps = steps_a + steps_b
    rider_in, rider_out, rider_shapes = _rider_specs(riders, steps_a, lambda i: jnp.minimum(i, steps_a - 1))
    out = pl.pallas_call(
        functools.partial(_ffn_to_phase_body, rows=rows, steps_a=steps_a, n_steps=n_steps, n_riders=len(riders)),
        grid=(n_steps,),
        in_specs=[
            pl.BlockSpec(memory_space=pl.ANY),
            pl.BlockSpec(memory_space=pl.ANY),
            _const_spec((1, d)),
            _const_spec((d, d_ff)),
            _const_spec((d, d_ff)),
            _const_spec((d_ff, d)),
        ] + rider_in,
        out_specs=[pl.BlockSpec((N_PHASE, rows, d), lambda i: (0, i, 0))] + rider_out,
        out_shape=[jax.ShapeDtypeStruct((N_PHASE, n_rows, d), F32)] + rider_shapes,
        scratch_shapes=[pltpu.VMEM((2, N_PHASE, rows, d), F32), pltpu.SemaphoreType.DMA((2,))],
        name="ffn_to_phase",
        compiler_params=_params(("arbitrary",)),
    )(x_a.reshape(rows_a, N_PHASE, d), x_b.reshape(rows_b, N_PHASE, d), g, wg, wu, wd, *riders)
    return out[0], tuple(out[1:])


def _mix_in_compute(x_ref, g, win_ref, wcv, wco_ref, shift, rows):
    d = D_MODEL
    h = _rms(_stack_phases(x_ref), g).astype(BF16)

    def proj(k):
        return _dot(h, win_ref[:, k * d:(k + 1) * d])

    z = proj(1) * proj(0)
    b_gate = proj(2)
    u = proj(3)
    g_conv = jax.nn.sigmoid(proj(4))
    gs = jax.nn.sigmoid(proj(5))
    z6 = z[6 * rows:7 * rows]
    z7 = z[7 * rows:8 * rows]
    s6 = shift(z6, 0)
    s7 = shift(z7, 1)
    z1 = jnp.concatenate([s7, z[:7 * rows]], axis=0)
    z2 = jnp.concatenate([s6, s7, z[:6 * rows]], axis=0)
    conv = wcv[2:3] * z + wcv[1:2] * z1 + wcv[0:1] * z2
    a = g_conv * _dot((b_gate * conv).astype(BF16), wco_ref[...])
    return u, a, gs, z6, z7


def _store_phases(ref, val, rows):
    val = val.astype(ref.dtype)
    for t in range(N_PHASE):
        ref[t] = val[t * rows:(t + 1) * rows]


def _store_sub_slab_pairs(ref, val, rows):
    for k in range(N_PHASE // 2):
        a = val[2 * k * rows:(2 * k + 1) * rows]
        b = val[(2 * k + 1) * rows:(2 * k + 2) * rows]
        for q in range(D_MODEL // LANE):
            lanes = slice(q * LANE, (q + 1) * LANE)
            for h, tile in enumerate(_pair_halves(a[:, lanes], b[:, lanes])):
                ref[2 * k + h, :, lanes] = tile.astype(ref.dtype)


def _mix_in_prompt_body(*refs, rows, n_riders):
    x_ref, g_ref, win_ref, wcv_ref, wco_ref = refs[:5]
    u_ref, a_ref, gs_ref, z_ref = refs[5 + n_riders:9 + n_riders]
    carry_ref = refs[-1]
    _convert_riders(refs[5:5 + n_riders], refs[9 + n_riders:9 + 2 * n_riders])

    @pl.when(pl.program_id(1) == 0)
    def _():
        carry_ref[...] = jnp.zeros_like(carry_ref)

    row_id = lax.broadcasted_iota(jnp.int32, (rows, D_MODEL), 0)

    def shift(z, k):
        return jnp.where(row_id == 0, carry_ref[k:k + 1, :], pltpu.roll(z, 1, 0))

    u, a, gs, z6, z7 = _mix_in_compute(x_ref, g_ref[...], win_ref, wcv_ref[...], wco_ref, shift, rows)
    last6 = z6[rows - 1:rows]
    last7 = z7[rows - 1:rows]
    carry_ref[0:1, :] = last6
    carry_ref[1:2, :] = last7
    z_ref[0:1, :] = last6
    z_ref[1:2, :] = last7
    _store_sub_slab_pairs(u_ref, u, rows)
    _store_phases(a_ref, a, rows)
    _store_phases(gs_ref, gs, rows)


def _mix_in_sample_body(x_ref, start_ref, g_ref, win_ref, wcv_ref, wco_ref,
                        u_ref, a_ref, gs_ref, z6_ref, z7_ref, slab_ref, *, rows, rows_per_seq):
    row_id = lax.broadcasted_iota(jnp.int32, (rows, D_MODEL), 0)

    def shift(z, k):
        return jnp.where(row_id % rows_per_seq == 0, start_ref[k], pltpu.roll(z, 1, 0))

    u, a, gs, z6, z7 = _mix_in_compute(x_ref, g_ref[...], win_ref, wcv_ref[...], wco_ref, shift, rows)
    for z, z_ref in ((z6, z6_ref), (z7, z7_ref)):
        for lt in range(D_MODEL // LANE):
            slab_ref[...] = z[:, lt * LANE:(lt + 1) * LANE]
            z_ref[:, lt * LANE:(lt + 1) * LANE] = slab_ref[pl.ds(rows_per_seq - 1, rows // rows_per_seq,
                                                                stride=rows_per_seq), :]
    _store_sub_slab_pairs(u_ref, u, rows)
    _store_phases(a_ref, a, rows)
    _store_phases(gs_ref, gs, rows)


def _mix_in(xp, row0, n_rows, conv_prev, g, w_in, w_conv, w_conv_out, n_seq, riders=()):
    d = D_MODEL
    rows_per_seq = n_rows // n_seq
    ph_shape = jax.ShapeDtypeStruct((N_PHASE, n_rows, d), BF16)
    weights = (g, w_in, w_conv, w_conv_out)
    weight_specs = [_const_spec(w.shape) for w in weights]
    if conv_prev is None:
        rows = min(MIX_IN_ROWS, rows_per_seq)
        assert rows_per_seq % rows == 0
        tiles = rows_per_seq // rows
        assert row0 % rows == 0
        ph_spec = pl.BlockSpec((N_PHASE, rows, d), lambda b, i: (0, b * tiles + i, 0))
        x_spec = pl.BlockSpec((N_PHASE, rows, d), lambda b, i: (0, row0 // rows + b * tiles + i, 0))
        last_spec = pl.BlockSpec((None, 2, d), lambda b, i: (b, 0, 0))
        rider_in, rider_out, rider_shapes = _rider_specs(riders, n_seq * tiles, lambda b, i: b * tiles + i)
        u8, a8, gs8, conv_state, *converted = pl.pallas_call(
            functools.partial(_mix_in_prompt_body, rows=rows, n_riders=len(riders)),
            grid=(n_seq, tiles),
            in_specs=[x_spec] + weight_specs + rider_in,
            out_specs=[ph_spec, ph_spec, ph_spec, last_spec] + rider_out,
            out_shape=[ph_shape, ph_shape, ph_shape, jax.ShapeDtypeStruct((n_seq, 2, d), F32)] + rider_shapes,
            scratch_shapes=[pltpu.VMEM((8, d), F32)],
            name="mix_in_prompt",
            compiler_params=_params(("arbitrary", "arbitrary")),
        )(xp, *weights, *riders)
    else:
        assert not riders
        converted = []
        rows = n_rows
        assert row0 % rows == 0
        start = jnp.repeat(jnp.swapaxes(conv_prev, 0, 1), rows_per_seq, axis=1)
        full = lambda shape: pl.BlockSpec(shape, lambda i: (0,) * len(shape))
        x_spec = pl.BlockSpec((N_PHASE, rows, d), lambda i: (0, row0 // rows, 0))
        u8, a8, gs8, z6, z7 = pl.pallas_call(
            functools.partial(_mix_in_sample_body, rows=rows, rows_per_seq=rows_per_seq),
            grid=(1,),
            in_specs=[x_spec, full((2, rows, d))] + weight_specs,
            out_specs=[full((N_PHASE, rows, d))] * 3 + [full((n_seq, d))] * 2,
            out_shape=[ph_shape, ph_shape, ph_shape,
                       jax.ShapeDtypeStruct((n_seq, d), F32), jax.ShapeDtypeStruct((n_seq, d), F32)],
            scratch_shapes=[pltpu.VMEM((rows, LANE), F32)],
            name="mix_in_sample",
            compiler_params=_params(("arbitrary",)),
        )(xp, start, *weights)
        conv_state = jnp.stack([z6, z7], axis=1)
    return u8, a8, gs8, conv_state, tuple(converted)


def _split_bf16(a):
    hi = a.astype(BF16)
    return hi, (a - hi.astype(F32)).astype(BF16)


def _ssm_prep_body(*refs, n_riders):
    bbre_ref, bbim_ref, ctre_ref, ctim_ref, lre_ref, lim_ref, lstep_ref = refs[:7]
    wb_ref, kt_ref, wc_ref, pw_ref = refs[7 + n_riders:11 + n_riders]
    _convert_riders(refs[7:7 + n_riders], refs[11 + n_riders:])
    ns = SUB_STATES
    row_group = lax.broadcasted_iota(jnp.int32, (SUB, LANE), 0) // GROUP_SIZE
    lane_half = lax.broadcasted_iota(jnp.int32, (SUB, LANE), 1) // STATE_DIM
    low_half = lax.broadcasted_iota(jnp.int32, (SUB, LANE), 1) < SUB

    def block_diag(ref):
        tiles = [jnp.where(row_group == 2 * k + lane_half, ref[...], 0.0) for k in range(ns // LANE)]
        return jnp.concatenate(tiles, axis=1)

    bbre, bbim = block_diag(bbre_ref), block_diag(bbim_ref)
    ctre, ctim = block_diag(ctre_ref), block_diag(ctim_ref)
    lre, lim = lre_ref[...], lim_ref[...]
    step = jnp.exp(lstep_ref[...])

    def lam_pow(n):
        mag = jnp.exp((n * lre) * step)
        ang = (n * lim) * step
        return mag * jnp.cos(ang), mag * jnp.sin(ang)

    l1re, l1im = lam_pow(1)
    den = lre * lre + lim * lim
    fre = ((l1re - 1.0) * lre + l1im * lim) / den
    fim = (l1im * lre - (l1re - 1.0) * lim) / den

    def dot_nt(a, b_split):
        nt = lambda p, q: lax.dot_general(p, q, (((1,), (1,)), ((), ())), preferred_element_type=F32)
        a_hi, a_lo = _split_bf16(a)
        b_hi, b_lo = b_split
        return nt(a_hi, b_hi) + nt(a_hi, b_lo) + nt(a_lo, b_hi)

    def c_pair(n):
        (p0re, p0im), (p1re, p1im) = lam_pow(n), lam_pow(n + 1)
        return (jnp.concatenate([ctre * p0re - ctim * p0im, ctre * p1re - ctim * p1im], axis=0),
                jnp.concatenate([ctre * p0im + ctim * p0re, ctre * p1im + ctim * p1re], axis=0))

    c01re, c01im = c_pair(0)
    c01re_split = _split_bf16(c01re)
    c01im_split = _split_bf16(c01im)
    lag_pairs = {}
    for k in range(N_PHASE):
        pre, pim = lam_pow(k)
        gre = fre * pre - fim * pim
        gim = fre * pim + fim * pre
        are = bbre * gre - bbim * gim
        aim = bbre * gim + bbim * gre
        j = N_PHASE - 1 - k
        wb_ref[j * SUB:(j + 1) * SUB, 0:ns] = are.astype(BF16)
        wb_ref[j * SUB:(j + 1) * SUB, ns:2 * ns] = aim.astype(BF16)
        if k < N_PHASE - 1:
            lag_pairs[k] = dot_nt(are, c01re_split) - dot_nt(aim, c01im_split)
    kt_ref[...] = jnp.zeros_like(kt_ref)
    first_odd = jnp.where(low_half, 0.0, pltpu.roll(lag_pairs[0], SUB, 1))
    for j in range(N_PHASE):
        for tile in range(j // 2, N_PHASE // 2):
            lag = 2 * tile - j
            pair = first_odd if lag < 0 else lag_pairs[lag]
            kt_ref[j * SUB:(j + 1) * SUB, tile * LANE:(tile + 1) * LANE] = pair.astype(BF16)
    for tile in range(N_PHASE // 2):
        cre, cim = c_pair(2 * tile + 1)
        wc_ref[0:ns, tile * LANE:(tile + 1) * LANE] = cre.T.astype(BF16)
        wc_ref[ns:2 * ns, tile * LANE:(tile + 1) * LANE] = (-cim).T.astype(BF16)
    pw_ref[...] = jnp.zeros_like(pw_ref)
    for r, n in enumerate((N_PHASE, N_PHASE * SEG)):
        pre, pim = lam_pow(n)
        pw_ref[r:r + 1, 0:ns] = pre
        pw_ref[r:r + 1, ns:2 * ns] = pim


def _ssm_prep(lam_re, lam_im, log_step, b_re, b_im, c_re, c_im, riders=()):
    ns = SUB_STATES
    n_sub = N_SLAB * SUBS
    assert 2 * STATE_DIM == LANE and SUB_K == 2 * ns

    def twice(m):
        m = m.reshape(n_sub, SUB, STATE_DIM)
        return jnp.concatenate([m, m], axis=-1)

    def vec(v):
        return v.reshape(n_sub, 1, ns)

    mats = [twice(jnp.transpose(b_re, (0, 2, 1))), twice(jnp.transpose(b_im, (0, 2, 1))), twice(c_re), twice(c_im)]
    vecs = [vec(lam_re), vec(lam_im), vec(jnp.broadcast_to(log_step[:, None], lam_re.shape))]
    w_shape = jax.ShapeDtypeStruct((n_sub, SUB_K, 2 * ns), BF16)
    w_spec = pl.BlockSpec((None, SUB_K, 2 * ns), lambda q: (q, 0, 0))
    rider_in, rider_out, rider_shapes = _rider_specs(riders, n_sub)
    out = pl.pallas_call(
        functools.partial(_ssm_prep_body, n_riders=len(riders)),
        grid=(n_sub,),
        in_specs=[pl.BlockSpec((None, SUB, LANE), lambda q: (q, 0, 0))] * 4
        + [pl.BlockSpec((None, 1, ns), lambda q: (q, 0, 0))] * 3 + rider_in,
        out_specs=[w_spec, w_spec, w_spec, pl.BlockSpec((None, 8, 2 * ns), lambda q: (q, 0, 0))] + rider_out,
        out_shape=[w_shape, w_shape, w_shape, jax.ShapeDtypeStruct((n_sub, 8, 2 * ns), F32)] + rider_shapes,
        name="ssm_prep",
        compiler_params=_params(("arbitrary",)),
    )(*mats, *vecs, *riders)
    return tuple(w.reshape(N_SLAB, SUBS, *w.shape[1:]) for w in out[:4]), tuple(out[4:])


def _cmul(are, aim, bre, bim):
    return are * bre - aim * bim, are * bim + aim * bre


HALF = SUB_STATES // LANE
GROUP_ROWS = 8 * SEG


def _pair_halves(a, b):
    low = lax.broadcasted_iota(jnp.int32, a.shape, 1) < SUB
    return jnp.where(low, a, pltpu.roll(b, SUB, 1)), jnp.where(low, pltpu.roll(a, SUB, 1), b)


def _sub_slab_operands(u_ref):
    return [jnp.concatenate([u_ref[2 * k + h].reshape(-1, LANE) for k in range(N_PHASE // 2)], axis=1)
            for h in range(SUBS)]


def _slab_phase_tiles(subs):
    tiles = []
    for k in range(N_PHASE // 2):
        tiles += _pair_halves(subs[0][:, k * LANE:(k + 1) * LANE], subs[1][:, k * LANE:(k + 1) * LANE])
    return [y.astype(BF16) for y in tiles]


def _mult(pw_ref, r, c):
    return (pw_ref[r:r + 1, c * LANE:(c + 1) * LANE],
            pw_ref[r:r + 1, (HALF + c) * LANE:(HALF + c + 1) * LANE])


def _state_increments(lhs, wb_ref, e_ref):
    e = _dot(lhs, wb_ref[...])
    for lt in range(2 * HALF):
        e_ref[lt, :e.shape[0]] = e[:, lt * LANE:(lt + 1) * LANE]


def _scan_local(e_ref, p_ref, pw_ref, c, base):
    l8re, l8im = _mult(pw_ref, 0, c)
    idx = pl.ds(base, 8, stride=SEG)
    lre = e_ref[c, idx, :]
    lim = e_ref[HALF + c, idx, :]
    for i in range(1, SEG):
        idx = pl.ds(base + i, 8, stride=SEG)
        p_ref[c, idx, :] = lre
        p_ref[HALF + c, idx, :] = lim
        mre, mim = _cmul(l8re, l8im, lre, lim)
        lre = mre + e_ref[c, idx, :]
        lim = mim + e_ref[HALF + c, idx, :]
    return lre, lim


def _scan_fixup(p_ref, pw_ref, c, base, cre, cim):
    l8re, l8im = _mult(pw_ref, 0, c)
    idx = pl.ds(base, 8, stride=SEG)
    p_ref[c, idx, :] = cre
    p_ref[HALF + c, idx, :] = cim
    for i in range(1, SEG):
        idx = pl.ds(base + i, 8, stride=SEG)
        cre, cim = _cmul(l8re, l8im, cre, cim)
        p_ref[c, idx, :] = p_ref[c, idx, :] + cre
        p_ref[HALF + c, idx, :] = p_ref[HALF + c, idx, :] + cim


def _direct_terms(lhs, kt_ref, d_ref):
    return _dot(lhs, kt_ref[...]) + d_ref[...] * lhs.astype(F32)


def _state_term(p_ref, n_rows, wc_ref):
    prev = jnp.concatenate([p_ref[lt, :n_rows] for lt in range(2 * HALF)], axis=1)
    return _dot(prev.astype(BF16), wc_ref[...])


def _ssm_prompt_body(u_ref, wb_ref, kt_ref, wc_ref, pw_ref, d_ref, y_ref, sre_ref, sim_ref,
                     e_ref, p_ref, end_ref, cin_ref, carry_ref, *, n_seq, rows, tile):
    @pl.when((pl.program_id(0) == 0) & (tile == 0))
    def _():
        end_ref[...] = jnp.zeros_like(end_ref)

    @pl.when(tile == 0)
    def _():
        carry_ref[...] = jnp.zeros_like(carry_ref)

    operands = _sub_slab_operands(u_ref)
    groups = rows // GROUP_ROWS
    for h, lhs in enumerate(operands):
        _state_increments(lhs, wb_ref.at[h], e_ref.at[h])
    outs = [_direct_terms(lhs, kt_ref.at[h], d_ref.at[h]) for h, lhs in enumerate(operands)]
    for h in range(SUBS):
        e, p, ends, cins, carry, pw = e_ref.at[h], p_ref.at[h], end_ref.at[h], cin_ref.at[h], carry_ref.at[h], pw_ref.at[h]
        lane0 = h * SUB_STATES
        for c in range(HALF):
            for b in range(n_seq):
                for j in range(groups):
                    lre, lim = _scan_local(e, p, pw, c, b * rows + j * GROUP_ROWS)
                    idx = pl.ds(j * 64 + b, 8, stride=8)
                    ends[c, idx, :] = lre
                    ends[HALF + c, idx, :] = lim
        for c in range(HALF):
            lsre, lsim = _mult(pw, 1, c)
            cre = carry[c]
            cim = carry[HALF + c]
            for s in range(rows // SEG):
                cins[c, 8 * s:8 * s + 8, :] = cre
                cins[HALF + c, 8 * s:8 * s + 8, :] = cim
                mre, mim = _cmul(lsre, lsim, cre, cim)
                cre = mre + ends[c, 8 * s:8 * s + 8, :]
                cim = mim + ends[HALF + c, 8 * s:8 * s + 8, :]
            carry[c] = cre
            carry[HALF + c] = cim
            sre_ref[:, lane0 + c * LANE:lane0 + (c + 1) * LANE] = cre[:n_seq]
            sim_ref[:, lane0 + c * LANE:lane0 + (c + 1) * LANE] = cim[:n_seq]
        for c in range(HALF):
            for b in range(n_seq):
                for j in range(groups):
                    idx = pl.ds(j * 64 + b, 8, stride=8)
                    _scan_fixup(p, pw, c, b * rows + j * GROUP_ROWS, cins[c, idx, :], cins[HALF + c, idx, :])
        outs[h] = outs[h] + _state_term(p, n_seq * rows, wc_ref.at[h])
    for t, y in enumerate(_slab_phase_tiles(outs)):
        y_ref[t] = y.reshape(n_seq, rows, LANE)


def _ssm_sample_body(u_ref, wb_ref, kt_ref, wc_ref, pw_ref, d_ref, h0re_ref, h0im_ref,
                     y_ref, sre_ref, sim_ref, e_ref, p_ref, *, rows):
    outs = []
    for h, lhs in enumerate(_sub_slab_operands(u_ref)):
        e, p, pw = e_ref.at[h], p_ref.at[h], pw_ref.at[h]
        _state_increments(lhs, wb_ref.at[h], e)
        for c in range(HALF):
            lanes = slice(h * SUB_STATES + c * LANE, h * SUB_STATES + (c + 1) * LANE)
            lsre, lsim = _mult(pw, 1, c)
            for j in range(rows // GROUP_ROWS):
                lre, lim = _scan_local(e, p, pw, c, j * GROUP_ROWS)
                cre = h0re_ref[8 * j:8 * j + 8, lanes]
                cim = h0im_ref[8 * j:8 * j + 8, lanes]
                mre, mim = _cmul(lsre, lsim, cre, cim)
                sre_ref[8 * j:8 * j + 8, lanes] = mre + lre
                sim_ref[8 * j:8 * j + 8, lanes] = mim + lim
                _scan_fixup(p, pw, c, j * GROUP_ROWS, cre, cim)
        outs.append(_direct_terms(lhs, kt_ref.at[h], d_ref.at[h]) + _state_term(p, rows, wc_ref.at[h]))
    for t, y in enumerate(_slab_phase_tiles(outs)):
        y_ref[t] = y


def _ssm_body(u_ref, wb_ref, kt_ref, wc_ref, pw_ref, d_ref, us_ref, h0re_ref, h0im_ref,
              y_ref, sre_ref, sim_ref, ys_ref, sres_ref, sims_ref,
              e_ref, p_ref, end_ref, cin_ref, carry_ref, *, n_seq, rows, rows_s):
    step = pl.program_id(1)

    @pl.when(step == 0)
    def _():
        _ssm_sample_body(us_ref, wb_ref, kt_ref, wc_ref, pw_ref, d_ref, h0re_ref, h0im_ref,
                         ys_ref, sres_ref, sims_ref, e_ref, p_ref, rows=rows_s)

    @pl.when(step > 0)
    def _():
        _ssm_prompt_body(u_ref, wb_ref, kt_ref, wc_ref, pw_ref, d_ref, y_ref, sre_ref, sim_ref,
                         e_ref, p_ref, end_ref, cin_ref, carry_ref, n_seq=n_seq, rows=rows, tile=step - 1)


def _ssm(u8_p, n_p, u8_s, n_s, h0, wb, kt, wc, pw, d8):
    d = D_MODEL
    ns = SUBS * SUB_STATES
    rows_p, rows_s = u8_p.shape[1], u8_s.shape[1]
    per_seq = rows_p // n_p
    assert rows_s // n_s == SEG, "carried-state path scans one segment per sequence"
    assert n_p <= 8, "sequences ride the sublanes of the segment chain"
    rows = min(SSM_ROWS, per_seq)
    assert per_seq % rows == 0 and rows % GROUP_ROWS == 0 and rows_s % GROUP_ROWS == 0
    assert rows_s <= n_p * rows, "the carried-state step reuses the scan scratch"
    tiles = per_seq // rows
    slab = lambda q, i: (q, 0, 0, 0)
    p_spec = pl.BlockSpec((N_PHASE, n_p, rows, LANE), lambda q, i: (0, 0, jnp.maximum(i - 1, 0), q))
    s_spec = pl.BlockSpec((N_PHASE, rows_s, LANE), lambda q, i: (0, 0, q))
    w_spec = pl.BlockSpec((None, SUBS, SUB_K, 2 * SUB_STATES), slab)
    state_p = pl.BlockSpec((n_p, ns), lambda q, i: (0, q))
    state_s = pl.BlockSpec((n_s, ns), lambda q, i: (0, q))
    seg_rows = 8 * (rows // SEG)
    dims_p = (N_PHASE, n_p, per_seq, d)
    y_p, re_p, im_p, y_s, re_s, im_s = pl.pallas_call(
        functools.partial(_ssm_body, n_seq=n_p, rows=rows, rows_s=rows_s),
        grid=(N_SLAB, tiles + 1),
        in_specs=[p_spec, w_spec, w_spec, w_spec, pl.BlockSpec((None, SUBS, 8, 2 * SUB_STATES), slab),
                  pl.BlockSpec((None, SUBS, 1, SUB_K), slab), s_spec, state_s, state_s],
        out_specs=[p_spec, state_p, state_p, s_spec, state_s, state_s],
        out_shape=[jax.ShapeDtypeStruct(dims_p, BF16),
                   jax.ShapeDtypeStruct((n_p, N_SLAB * ns), F32), jax.ShapeDtypeStruct((n_p, N_SLAB * ns), F32),
                   jax.ShapeDtypeStruct((N_PHASE, rows_s, d), BF16),
                   jax.ShapeDtypeStruct((n_s, N_SLAB * ns), F32), jax.ShapeDtypeStruct((n_s, N_SLAB * ns), F32)],
        scratch_shapes=[pltpu.VMEM((SUBS, 2 * HALF, n_p * rows, LANE), F32),
                        pltpu.VMEM((SUBS, 2 * HALF, n_p * rows, LANE), F32),
                        pltpu.VMEM((SUBS, 2 * HALF, seg_rows, LANE), F32),
                        pltpu.VMEM((SUBS, 2 * HALF, seg_rows, LANE), F32),
                        pltpu.VMEM((SUBS, 2 * HALF, 8, LANE), F32)],
        name="ssm",
        compiler_params=_params(("arbitrary", "arbitrary")),
    )(u8_p.reshape(dims_p), wb, kt, wc, pw, d8, u8_s, h0[0].reshape(n_s, -1), h0[1].reshape(n_s, -1))
    return (y_p.reshape(N_PHASE, rows_p, d), re_p, im_p), (y_s, re_s, im_s)


def _mix_out_ffn_body(ya_hbm, aa_hbm, gsa_hbm, yb_hbm, ab_hbm, gsb_hbm, x_ref,
                      wglu_ref, wo_ref, g_ref, wg_ref, wu_ref, wd_ref, gf_ref,
                      oa_hbm, ob_hbm, inbuf_ref, ybuf_ref, sem_in, sem_out, *, rows, steps_a, n_steps, final_norm):
    d = D_MODEL
    step = pl.program_id(0)
    slot = step % 2

    def in_copies(srcs, row0, of_slot):
        return [pltpu.make_async_copy(src.at[:, pl.ds(row0, rows), :], inbuf_ref.at[of_slot, k], sem_in.at[of_slot])
                for k, src in enumerate(srcs)]

    def out_copies(dst, row0, of_slot):
        return [pltpu.make_async_copy(ybuf_ref.at[of_slot, t], dst.at[pl.ds(row0, rows), t, :], sem_out.at[of_slot])
                for t in range(N_PHASE)]

    def start_for_group(of_step, copies_a, copies_b):
        @pl.when(of_step < steps_a)
        def _():
            for c in copies_a(of_step * rows):
                c.start()

        @pl.when(of_step >= steps_a)
        def _():
            for c in copies_b((of_step - steps_a) * rows):
                c.start()

    def start_in(of_step, of_slot):
        start_for_group(of_step, lambda r: in_copies((ya_hbm, aa_hbm, gsa_hbm), r, of_slot),
                        lambda r: in_copies((yb_hbm, ab_hbm, gsb_hbm), r, of_slot))

    @pl.when(step == 0)
    def _():
        start_in(step, slot)

    @pl.when(step + 1 < n_steps)
    def _():
        start_in(step + 1, 1 - slot)

    @pl.when(step >= 2)
    def _():
        for c in out_copies(oa_hbm, 0, slot):
            c.wait()

    for c in in_copies((ya_hbm, aa_hbm, gsa_hbm), 0, slot):
        c.wait()
    y_ref, a_ref, gs_ref = (inbuf_ref.at[slot, k] for k in range(3))
    glu = _dot(jax.nn.gelu(_stack_phases(y_ref).astype(F32)).astype(BF16), wglu_ref[...])
    y_ssm = glu[:, :d] * jax.nn.sigmoid(glu[:, d:])
    merged = _stack_phases(a_ref).astype(F32) + _stack_phases(gs_ref).astype(F32) * y_ssm
    x2 = _stack_phases(x_ref) + _dot(merged.astype(BF16), wo_ref[...])
    y = _ffn_math(x2, g_ref, wg_ref, wu_ref, wd_ref)
    if final_norm:
        y = _rms(y, gf_ref[...])
    for t in range(N_PHASE):
        ybuf_ref[slot, t] = y[t * rows:(t + 1) * rows]
    start_for_group(step, lambda r: out_copies(oa_hbm, r, slot), lambda r: out_copies(ob_hbm, r, slot))

    @pl.when(step == n_steps - 1)
    def _():
        if n_steps > 1:
            for c in out_copies(oa_hbm, 0, 1 - slot):
                c.wait()
        for c in out_copies(oa_hbm, 0, slot):
            c.wait()


def _mix_out_ffn(acts_a, acts_b, xp, w_glu, w_o, g, wg, wu, wd, gf, final_norm):
    rows_a, rows_b = acts_a[0].shape[1], acts_b[0].shape[1]
    d = D_MODEL
    rows = MIX_OUT_ROWS
    assert rows_a % rows == 0 and rows_b % rows == 0 and xp.shape[1] == rows_a + rows_b
    steps_a, n_steps = rows_a // rows, (rows_a + rows_b) // rows
    weights = (w_glu, w_o, g, wg, wu, wd, gf)
    any_spec = pl.BlockSpec(memory_space=pl.ANY)
    out_a, out_b = pl.pallas_call(
        functools.partial(_mix_out_ffn_body, rows=rows, steps_a=steps_a, n_steps=n_steps, final_norm=final_norm),
        grid=(n_steps,),
        in_specs=[any_spec] * 6 + [pl.BlockSpec((N_PHASE, rows, d), lambda i: (0, i, 0))]
        + [_const_spec(w.shape) for w in weights],
        out_specs=[any_spec, any_spec],
        out_shape=[jax.ShapeDtypeStruct((rows_a, N_PHASE, d), F32), jax.ShapeDtypeStruct((rows_b, N_PHASE, d), F32)],
        scratch_shapes=[pltpu.VMEM((2, 3, N_PHASE, rows, d), BF16),
                        pltpu.VMEM((2, N_PHASE, rows, d), F32),
                        pltpu.SemaphoreType.DMA((2,)), pltpu.SemaphoreType.DMA((2,))],
        name="mix_out_ffn",
        compiler_params=_params(("arbitrary",)),
    )(*acts_a, *acts_b, xp, *weights)
    return out_a.reshape(N_PHASE * rows_a, d), out_b.reshape(N_PHASE * rows_b, d)


def kernel(x_prompt, x_sample, state_conv, state_ssm_re, state_ssm_im, norm_ffn1, w_ffn1_gate, w_ffn1_up, w_ffn1_down, norm_mix, w_in, w_conv, w_conv_out, ssm_lambda_re, ssm_lambda_im, ssm_log_step, ssm_b_re, ssm_b_im, ssm_c_re, ssm_c_im, ssm_d, w_glu, w_o, norm_ffn2, w_ffn2_gate, w_ffn2_up, w_ffn2_down, norm_final):
    depth, d = w_in.shape[0], w_in.shape[1]
    row = lambda v: v.reshape(1, -1)
    nf = row(norm_final)
    xt_p = x_prompt.reshape(-1, d)
    xt_s = x_sample.reshape(-1, d)
    n_p, n_s = x_prompt.shape[0], x_sample.shape[0]
    outs_p, outs_s = [], []
    for l in range(depth):
        last = l == depth - 1
        ssm_w, (f1g, f1u, f1d) = _ssm_prep(
            ssm_lambda_re[l], ssm_lambda_im[l], ssm_log_step[l], ssm_b_re[l], ssm_b_im[l], ssm_c_re[l],
            ssm_c_im[l], riders=(w_ffn1_gate[l], w_ffn1_up[l], w_ffn1_down[l]))
        ffn1 = (row(norm_ffn1[l]), f1g, f1u, f1d)
        xp, (win, wco, wgl, wo) = _ffn_to_phase(xt_p, xt_s, *ffn1, riders=(w_in[l], w_conv_out[l], w_glu[l], w_o[l]))
        rows_p, rows_s = xt_p.shape[0] // N_PHASE, xt_s.shape[0] // N_PHASE
        d8 = jnp.tile(ssm_d[l].reshape(N_SLAB, SUBS, 1, SUB), (1, 1, 1, N_PHASE))
        mix_w = (row(norm_mix[l]), win, w_conv[l], wco)
        u_p, a_p, gs_p, conv_p, ffn2_w = _mix_in(xp, 0, rows_p, None, *mix_w, n_p,
                                                 riders=(w_ffn2_gate[l], w_ffn2_up[l], w_ffn2_down[l]))
        u_s, a_s, gs_s, conv_s, _ = _mix_in(xp, rows_p, rows_s, state_conv[l], *mix_w, n_s)
        (y_p, re_p, im_p), (y_s, re_s, im_s) = _ssm(u_p, n_p, u_s, n_s, (state_ssm_re[l], state_ssm_im[l]),
                                                    *ssm_w, d8)
        tail = (wgl, wo, row(norm_ffn2[l]), *ffn2_w, nf, last)
        xt_p, xt_s = _mix_out_ffn((y_p, a_p, gs_p), (y_s, a_s, gs_s), xp, *tail)
        outs_p.append([conv_p, re_p.reshape(n_p, -1, STATE_DIM), im_p.reshape(n_p, -1, STATE_DIM)])
        outs_s.append([conv_s, re_s.reshape(n_s, -1, STATE_DIM), im_s.reshape(n_s, -1, STATE_DIM)])
    stack = lambda outs: tuple(jnp.stack(leaf) for leaf in zip(*outs))
    return (xt_p.reshape(x_prompt.shape), xt_s.reshape(x_sample.shape), *stack(outs_p), *stack(outs_s))
```

```python
import functools

import jax
import jax.numpy as jnp
from jax import lax
from jax.experimental import pallas as pl
from jax.experimental.pallas import tpu as pltpu

F32 = jnp.float32
BF16 = jnp.bfloat16

D_MODEL = 1024
N_PHASE = 8
LANE = 128
MXU_TILE = 256
SLAB = 128
N_SLAB = D_MODEL // SLAB
SUB = 64
SUBS = SLAB // SUB
GROUP_SIZE = 16
STATE_DIM = 64
SUB_STATES = (SUB // GROUP_SIZE) * STATE_DIM
SUB_K = N_PHASE * SUB
SEG = 4
RMS_EPS = 1e-6
VMEM_LIMIT = 60 * 1024 * 1024

FFN_ROWS = 1024
MIX_OUT_PARTS = 2
MIX_IN_ROWS = 128
MIX_OUT_ROWS = 64
SSM_ROWS = 256


def _rms(x, g):
    return x * lax.rsqrt(jnp.mean(x * x, axis=-1, keepdims=True) + RMS_EPS) * g


def _dot(a, b):
    return jnp.dot(a, b, preferred_element_type=F32)


def _const_spec(shape):
    zeros = (0,) * len(shape)
    return pl.BlockSpec(shape, lambda *_: zeros, pipeline_mode=pl.Buffered(1))


def _params(semantics):
    return pltpu.CompilerParams(dimension_semantics=semantics, vmem_limit_bytes=VMEM_LIMIT)


def _stack_phases(ref):
    return jnp.concatenate([ref[t] for t in range(N_PHASE)], axis=0)


def _ffn_math(xs, g_ref, wg_ref, wu_ref, wd_ref):
    hs = [_rms(x, g_ref[...]).astype(BF16) for x in xs]
    gates = [_dot(h, wg_ref[...]) for h in hs]
    ups = [_dot(h, wu_ref[...]) for h in hs]
    acts = [(jax.nn.silu(gate) * up).astype(BF16) for gate, up in zip(gates, ups)]
    return [x + 0.5 * _dot(act, wd_ref[...]) for x, act in zip(xs, acts)]


def _rider_specs(weights, n_steps, step=lambda i: i):
    in_specs, out_specs, out_shapes = [], [], []
    for w in weights:
        n_blocks = n_steps
        while w.shape[0] % (16 * n_blocks):
            assert n_blocks % 2 == 0
            n_blocks //= 2
        spec = pl.BlockSpec((w.shape[0] // n_blocks, w.shape[1]),
                            lambda *idx, rep=n_steps // n_blocks: (step(*idx) // rep, 0))
        in_specs.append(spec)
        out_specs.append(spec)
        out_shapes.append(jax.ShapeDtypeStruct(w.shape, BF16))
    return in_specs, out_specs, out_shapes


def _convert_riders(in_refs, out_refs):
    for w_ref, o_ref in zip(in_refs, out_refs):
        o_ref[...] = w_ref[...].astype(BF16)


def _ffn_to_phase_body(*refs, rows, steps_a, n_steps, n_riders):
    xa_hbm, xb_hbm, g_ref, wg_ref, wu_ref, wd_ref = refs[:6]
    o_ref, xbuf_ref, sem = refs[6 + n_riders], refs[-2], refs[-1]
    step = pl.program_id(0)
    slot = step % 2

    def copies(src, row0, of_slot):
        return [pltpu.make_async_copy(src.at[pl.ds(row0, rows), t, :], xbuf_ref.at[of_slot, t], sem.at[of_slot])
                for t in range(N_PHASE)]

    def start(of_step, of_slot):
        @pl.when(of_step < steps_a)
        def _():
            for c in copies(xa_hbm, of_step * rows, of_slot):
                c.start()

        @pl.when(of_step >= steps_a)
        def _():
            for c in copies(xb_hbm, (of_step - steps_a) * rows, of_slot):
                c.start()

    @pl.when(step == 0)
    def _():
        start(step, slot)

    @pl.when(step + 1 < n_steps)
    def _():
        start(step + 1, 1 - slot)

    for c in copies(xa_hbm, 0, slot):
        c.wait()
    _convert_riders(refs[6:6 + n_riders], refs[7 + n_riders:7 + 2 * n_riders])
    x = jnp.concatenate([xbuf_ref[slot, t] for t in range(N_PHASE)], axis=0)
    (y,) = _ffn_math([x], g_ref, wg_ref, wu_ref, wd_ref)
    for t in range(N_PHASE):
        o_ref[t] = y[t * rows:(t + 1) * rows]


def _ffn_to_phase(x_a, x_b, g, wg, wu, wd, riders=()):
    d = x_a.shape[1]
    rows_a, rows_b = x_a.shape[0] // N_PHASE, x_b.shape[0] // N_PHASE
    n_rows = rows_a + rows_b
    d_ff = wg.shape[1]
    rows = FFN_ROWS // N_PHASE
    assert rows_a % rows == 0 and rows_b % rows == 0
    steps_a, steps_b = rows_a // rows, rows_b // rows
    n_steps = steps_a + steps_b
    rider_in, rider_out, rider_shapes = _rider_specs(riders, steps_a, lambda i: jnp.minimum(i, steps_a - 1))
    out = pl.pallas_call(
        functools.partial(_ffn_to_phase_body, rows=rows, steps_a=steps_a, n_steps=n_steps, n_riders=len(riders)),
        grid=(n_steps,),
        in_specs=[
            pl.BlockSpec(memory_space=pl.ANY),
            pl.BlockSpec(memory_space=pl.ANY),
            _const_spec((1, d)),
            _const_spec((d, d_ff)),
            _const_spec((d, d_ff)),
            _const_spec((d_ff, d)),
        ] + rider_in,
        out_specs=[pl.BlockSpec((N_PHASE, rows, d), lambda i: (0, i, 0))] + rider_out,
        out_shape=[jax.ShapeDtypeStruct((N_PHASE, n_rows, d), F32)] + rider_shapes,
        scratch_shapes=[pltpu.VMEM((2, N_PHASE, rows, d), F32), pltpu.SemaphoreType.DMA((2,))],
        name="ffn_to_phase",
        compiler_params=_params(("arbitrary",)),
    )(x_a.reshape(rows_a, N_PHASE, d), x_b.reshape(rows_b, N_PHASE, d), g, wg, wu, wd, *riders)
    return out[0], tuple(out[1:])


def _mix_in_compute(x_ref, g, win_ref, wcv, wco_ref, shift, rows):
    d = D_MODEL
    h = _rms(_stack_phases(x_ref), g).astype(BF16)

    def proj(k):
        return _dot(h, win_ref[:, k * d:(k + 1) * d])

    z = proj(1) * proj(0)
    b_gate = proj(2)
    u = proj(3)
    g_conv = jax.nn.sigmoid(proj(4))
    gs = jax.nn.sigmoid(proj(5))
    z6 = z[6 * rows:7 * rows]
    z7 = z[7 * rows:8 * rows]
    s6 = shift(z6, 0)
    s7 = shift(z7, 1)
    z1 = jnp.concatenate([s7, z[:7 * rows]], axis=0)
    z2 = jnp.concatenate([s6, s7, z[:6 * rows]], axis=0)
    conv = wcv[2:3] * z + wcv[1:2] * z1 + wcv[0:1] * z2
    a = g_conv * _dot((b_gate * conv).astype(BF16), wco_ref[...])
    return u, a, gs, z6, z7


def _store_phases(ref, val, rows):
    val = val.astype(ref.dtype)
    for t in range(N_PHASE):
        ref[t] = val[t * rows:(t + 1) * rows]


def _store_sub_slab_pairs(ref, val, rows):
    for k in range(N_PHASE // 2):
        a = val[2 * k * rows:(2 * k + 1) * rows]
        b = val[(2 * k + 1) * rows:(2 * k + 2) * rows]
        for q in range(D_MODEL // LANE):
            lanes = slice(q * LANE, (q + 1) * LANE)
            for h, tile in enumerate(_pair_halves(a[:, lanes], b[:, lanes])):
                ref[2 * k + h, :, lanes] = tile.astype(ref.dtype)


def _mix_in_prompt_body(*refs, rows, n_riders):
    x_ref, g_ref, win_ref, wcv_ref, wco_ref = refs[:5]
    u_ref, a_ref, gs_ref, z_ref = refs[5 + n_riders:9 + n_riders]
    carry_ref = refs[-1]
    _convert_riders(refs[5:5 + n_riders], refs[9 + n_riders:9 + 2 * n_riders])

    @pl.when(pl.program_id(1) == 0)
    def _():
        carry_ref[...] = jnp.zeros_like(carry_ref)

    row_id = lax.broadcasted_iota(jnp.int32, (rows, D_MODEL), 0)

    def shift(z, k):
        return jnp.where(row_id == 0, carry_ref[k:k + 1, :], pltpu.roll(z, 1, 0))

    u, a, gs, z6, z7 = _mix_in_compute(x_ref, g_ref[...], win_ref, wcv_ref[...], wco_ref, shift, rows)
    last6 = z6[rows - 1:rows]
    last7 = z7[rows - 1:rows]
    carry_ref[0:1, :] = last6
    carry_ref[1:2, :] = last7
    z_ref[0:1, :] = last6
    z_ref[1:2, :] = last7
    _store_sub_slab_pairs(u_ref, u, rows)
    _store_phases(a_ref, a, rows)
    _store_phases(gs_ref, gs, rows)


def _mix_in_sample_body(x_ref, start_ref, g_ref, win_ref, wcv_ref, wco_ref,
                        u_ref, a_ref, gs_ref, z6_ref, z7_ref, slab_ref, *, rows, rows_per_seq):
    row_id = lax.broadcasted_iota(jnp.int32, (rows, D_MODEL), 0)

    def shift(z, k):
        return jnp.where(row_id % rows_per_seq == 0, start_ref[k], pltpu.roll(z, 1, 0))

    u, a, gs, z6, z7 = _mix_in_compute(x_ref, g_ref[...], win_ref, wcv_ref[...], wco_ref, shift, rows)
    for z, z_ref in ((z6, z6_ref), (z7, z7_ref)):
        for lt in range(D_MODEL // LANE):
            slab_ref[...] = z[:, lt * LANE:(lt + 1) * LANE]
            z_ref[:, lt * LANE:(lt + 1) * LANE] = slab_ref[pl.ds(rows_per_seq - 1, rows // rows_per_seq,
                                                                stride=rows_per_seq), :]
    _store_sub_slab_pairs(u_ref, u, rows)
    _store_phases(a_ref, a, rows)
    _store_phases(gs_ref, gs, rows)


def _mix_in(xp, row0, n_rows, conv_prev, g, w_in, w_conv, w_conv_out, n_seq, riders=()):
    d = D_MODEL
    rows_per_seq = n_rows // n_seq
    ph_shape = jax.ShapeDtypeStruct((N_PHASE, n_rows, d), BF16)
    weights = (g, w_in, w_conv, w_conv_out)
    weight_specs = [_const_spec(w.shape) for w in weights]
    if conv_prev is None:
        rows = min(MIX_IN_ROWS, rows_per_seq)
        assert rows_per_seq % rows == 0
        tiles = rows_per_seq // rows
        assert row0 % rows == 0
        ph_spec = pl.BlockSpec((N_PHASE, rows, d), lambda b, i: (0, b * tiles + i, 0))
        x_spec = pl.BlockSpec((N_PHASE, rows, d), lambda b, i: (0, row0 // rows + b * tiles + i, 0))
        last_spec = pl.BlockSpec((None, 2, d), lambda b, i: (b, 0, 0))
        rider_in, rider_out, rider_shapes = _rider_specs(riders, n_seq * tiles, lambda b, i: b * tiles + i)
        u8, a8, gs8, conv_state, *converted = pl.pallas_call(
            functools.partial(_mix_in_prompt_body, rows=rows, n_riders=len(riders)),
            grid=(n_seq, tiles),
            in_specs=[x_spec] + weight_specs + rider_in,
            out_specs=[ph_spec, ph_spec, ph_spec, last_spec] + rider_out,
            out_shape=[ph_shape, ph_shape, ph_shape, jax.ShapeDtypeStruct((n_seq, 2, d), F32)] + rider_shapes,
            scratch_shapes=[pltpu.VMEM((8, d), F32)],
            name="mix_in_prompt",
            compiler_params=_params(("arbitrary", "arbitrary")),
        )(xp, *weights, *riders)
    else:
        assert not riders
        converted = []
        rows = n_rows
        assert row0 % rows == 0
        start = jnp.repeat(jnp.swapaxes(conv_prev, 0, 1), rows_per_seq, axis=1)
        full = lambda shape: pl.BlockSpec(shape, lambda i: (0,) * len(shape))
        x_spec = pl.BlockSpec((N_PHASE, rows, d), lambda i: (0, row0 // rows, 0))
        u8, a8, gs8, z6, z7 = pl.pallas_call(
            functools.partial(_mix_in_sample_body, rows=rows, rows_per_seq=rows_per_seq),
            grid=(1,),
            in_specs=[x_spec, full((2, rows, d))] + weight_specs,
            out_specs=[full((N_PHASE, rows, d))] * 3 + [full((n_seq, d))] * 2,
            out_shape=[ph_shape, ph_shape, ph_shape,
                       jax.ShapeDtypeStruct((n_seq, d), F32), jax.ShapeDtypeStruct((n_seq, d), F32)],
            scratch_shapes=[pltpu.VMEM((rows, LANE), F32)],
            name="mix_in_sample",
            compiler_params=_params(("arbitrary",)),
        )(xp, start, *weights)
        conv_state = jnp.stack([z6, z7], axis=1)
    return u8, a8, gs8, conv_state, tuple(converted)


def _split_bf16(a):
    hi = a.astype(BF16)
    return hi, (a - hi.astype(F32)).astype(BF16)


def _ssm_prep_body(*refs, n_riders):
    bbre_ref, bbim_ref, ctre_ref, ctim_ref, lre_ref, lim_ref, lstep_ref = refs[:7]
    wb_ref, kt_ref, wc_ref, pw_ref = refs[7 + n_riders:11 + n_riders]
    _convert_riders(refs[7:7 + n_riders], refs[11 + n_riders:])
    ns = SUB_STATES
    row_group = lax.broadcasted_iota(jnp.int32, (SUB, LANE), 0) // GROUP_SIZE
    lane_half = lax.broadcasted_iota(jnp.int32, (SUB, LANE), 1) // STATE_DIM
    low_half = lax.broadcasted_iota(jnp.int32, (SUB, LANE), 1) < SUB

    def block_diag(ref):
        tiles = [jnp.where(row_group == 2 * k + lane_half, ref[...], 0.0) for k in range(ns // LANE)]
        return jnp.concatenate(tiles, axis=1)

    bbre, bbim = block_diag(bbre_ref), block_diag(bbim_ref)
    ctre, ctim = block_diag(ctre_ref), block_diag(ctim_ref)
    lre, lim = lre_ref[...], lim_ref[...]
    step = jnp.exp(lstep_ref[...])

    def lam_pow(n):
        mag = jnp.exp((n * lre) * step)
        ang = (n * lim) * step
        return mag * jnp.cos(ang), mag * jnp.sin(ang)

    l1re, l1im = lam_pow(1)
    den = lre * lre + lim * lim
    fre = ((l1re - 1.0) * lre + l1im * lim) / den
    fim = (l1im * lre - (l1re - 1.0) * lim) / den

    def dot_nt(a, b_split):
        nt = lambda p, q: lax.dot_general(p, q, (((1,), (1,)), ((), ())), preferred_element_type=F32)
        a_hi, a_lo = _split_bf16(a)
        b_hi, b_lo = b_split
        return nt(a_hi, b_hi) + nt(a_hi, b_lo) + nt(a_lo, b_hi)

    def c_pair(n):
        (p0re, p0im), (p1re, p1im) = lam_pow(n), lam_pow(n + 1)
        return (jnp.concatenate([ctre * p0re - ctim * p0im, ctre * p1re - ctim * p1im], axis=0),
                jnp.concatenate([ctre * p0im + ctim * p0re, ctre * p1im + ctim * p1re], axis=0))

    c01re, c01im = c_pair(0)
    c01re_split = _split_bf16(c01re)
    c01im_split = _split_bf16(c01im)
    lag_pairs = {}
    for k in range(N_PHASE):
        pre, pim = lam_pow(k)
        gre = fre * pre - fim * pim
        gim = fre * pim + fim * pre
        are = bbre * gre - bbim * gim
        aim = bbre * gim + bbim * gre
        j = N_PHASE - 1 - k
        wb_ref[j * SUB:(j + 1) * SUB, 0:ns] = are.astype(BF16)
        wb_ref[j * SUB:(j + 1) * SUB, ns:2 * ns] = aim.astype(BF16)
        if k < N_PHASE - 1:
            lag_pairs[k] = dot_nt(are, c01re_split) - dot_nt(aim, c01im_split)
    kt_ref[...] = jnp.zeros_like(kt_ref)
    first_odd = jnp.where(low_half, 0.0, pltpu.roll(lag_pairs[0], SUB, 1))
    for j in range(N_PHASE):
        for tile in range(j // 2, N_PHASE // 2):
            lag = 2 * tile - j
            pair = first_odd if lag < 0 else lag_pairs[lag]
            kt_ref[j * SUB:(j + 1) * SUB, tile * LANE:(tile + 1) * LANE] = pair.astype(BF16)
    for tile in range(N_PHASE // 2):
        cre, cim = c_pair(2 * tile + 1)
        wc_ref[0:ns, tile * LANE:(tile + 1) * LANE] = cre.T.astype(BF16)
        wc_ref[ns:2 * ns, tile * LANE:(tile + 1) * LANE] = (-cim).T.astype(BF16)
    pw_ref[...] = jnp.zeros_like(pw_ref)
    for r, n in enumerate((N_PHASE, N_PHASE * SEG)):
        pre, pim = lam_pow(n)
        pw_ref[r:r + 1, 0:ns] = pre
        pw_ref[r:r + 1, ns:2 * ns] = pim


def _ssm_prep(lam_re, lam_im, log_step, b_re, b_im, c_re, c_im, riders=()):
    ns = SUB_STATES
    n_sub = N_SLAB * SUBS
    assert 2 * STATE_DIM == LANE and SUB_K == 2 * ns

    def twice(m):
        m = m.reshape(n_sub, SUB, STATE_DIM)
        return jnp.concatenate([m, m], axis=-1)

    def vec(v):
        return v.reshape(n_sub, 1, ns)

    mats = [twice(jnp.transpose(b_re, (0, 2, 1))), twice(jnp.transpose(b_im, (0, 2, 1))), twice(c_re), twice(c_im)]
    vecs = [vec(lam_re), vec(lam_im), vec(jnp.broadcast_to(log_step[:, None], lam_re.shape))]
    w_shape = jax.ShapeDtypeStruct((n_sub, SUB_K, 2 * ns), BF16)
    w_spec = pl.BlockSpec((None, SUB_K, 2 * ns), lambda q: (q, 0, 0))
    rider_in, rider_out, rider_shapes = _rider_specs(riders, n_sub)
    out = pl.pallas_call(
        functools.partial(_ssm_prep_body, n_riders=len(riders)),
        grid=(n_sub,),
        in_specs=[pl.BlockSpec((None, SUB, LANE), lambda q: (q, 0, 0))] * 4
        + [pl.BlockSpec((None, 1, ns), lambda q: (q, 0, 0))] * 3 + rider_in,
        out_specs=[w_spec, w_spec, w_spec, pl.BlockSpec((None, 8, 2 * ns), lambda q: (q, 0, 0))] + rider_out,
        out_shape=[w_shape, w_shape, w_shape, jax.ShapeDtypeStruct((n_sub, 8, 2 * ns), F32)] + rider_shapes,
        name="ssm_prep",
        compiler_params=_params(("arbitrary",)),
    )(*mats, *vecs, *riders)
    return tuple(w.reshape(N_SLAB, SUBS, *w.shape[1:]) for w in out[:4]), tuple(out[4:])


def _cmul(are, aim, bre, bim):
    return are * bre - aim * bim, are * bim + aim * bre


HALF = SUB_STATES // LANE
GROUP_ROWS = 8 * SEG


def _pair_halves(a, b):
    low = lax.broadcasted_iota(jnp.int32, a.shape, 1) < SUB
    return jnp.where(low, a, pltpu.roll(b, SUB, 1)), jnp.where(low, pltpu.roll(a, SUB, 1), b)


def _sub_slab_operands(u_ref):
    return [jnp.concatenate([u_ref[2 * k + h].reshape(-1, LANE) for k in range(N_PHASE // 2)], axis=1)
            for h in range(SUBS)]


def _slab_phase_tiles(subs):
    tiles = []
    for k in range(N_PHASE // 2):
        tiles += _pair_halves(subs[0][:, k * LANE:(k + 1) * LANE], subs[1][:, k * LANE:(k + 1) * LANE])
    return [y.astype(BF16) for y in tiles]


def _mult(pw_ref, r, c):
    return (pw_ref[r:r + 1, c * LANE:(c + 1) * LANE],
            pw_ref[r:r + 1, (HALF + c) * LANE:(HALF + c + 1) * LANE])


def _state_increments(lhs, wb_ref, e_ref):
    e = _dot(lhs, wb_ref[...])
    for lt in range(2 * HALF):
        e_ref[lt, :e.shape[0]] = e[:, lt * LANE:(lt + 1) * LANE]


def _scan_local(e_ref, p_ref, pw_ref, c, base):
    l8re, l8im = _mult(pw_ref, 0, c)
    idx = pl.ds(base, 8, stride=SEG)
    lre = e_ref[c, idx, :]
    lim = e_ref[HALF + c, idx, :]
    for i in range(1, SEG):
        idx = pl.ds(base + i, 8, stride=SEG)
        p_ref[c, idx, :] = lre
        p_ref[HALF + c, idx, :] = lim
        mre, mim = _cmul(l8re, l8im, lre, lim)
        lre = mre + e_ref[c, idx, :]
        lim = mim + e_ref[HALF + c, idx, :]
    return lre, lim


def _scan_fixup(p_ref, pw_ref, c, base, cre, cim):
    l8re, l8im = _mult(pw_ref, 0, c)
    idx = pl.ds(base, 8, stride=SEG)
    p_ref[c, idx, :] = cre
    p_ref[HALF + c, idx, :] = cim
    for i in range(1, SEG):
        idx = pl.ds(base + i, 8, stride=SEG)
        cre, cim = _cmul(l8re, l8im, cre, cim)
        p_ref[c, idx, :] = p_ref[c, idx, :] + cre
        p_ref[HALF + c, idx, :] = p_ref[HALF + c, idx, :] + cim


def _direct_terms(lhs, kt_ref, d_ref):
    return _dot(lhs, kt_ref[...]) + d_ref[...] * lhs.astype(F32)


def _state_term(p_ref, n_rows, wc_ref):
    prev = jnp.concatenate([p_ref[lt, :n_rows] for lt in range(2 * HALF)], axis=1)
    return _dot(prev.astype(BF16), wc_ref[...])


def _ssm_prompt_body(u_ref, wb_ref, kt_ref, wc_ref, pw_ref, d_ref, y_ref, sre_ref, sim_ref,
                     e_ref, p_ref, end_ref, cin_ref, carry_ref, *, n_seq, rows, tile):
    @pl.when((pl.program_id(0) == 0) & (tile == 0))
    def _():
        end_ref[...] = jnp.zeros_like(end_ref)

    @pl.when(tile == 0)
    def _():
        carry_ref[...] = jnp.zeros_like(carry_ref)

    operands = _sub_slab_operands(u_ref)
    groups = rows // GROUP_ROWS
    for h, lhs in enumerate(operands):
        _state_increments(lhs, wb_ref.at[h], e_ref.at[h])
    outs = [_direct_terms(lhs, kt_ref.at[h], d_ref.at[h]) for h, lhs in enumerate(operands)]
    for h in range(SUBS):
        e, p, ends, cins, carry, pw = e_ref.at[h], p_ref.at[h], end_ref.at[h], cin_ref.at[h], carry_ref.at[h], pw_ref.at[h]
        lane0 = h * SUB_STATES
        for c in range(HALF):
            for b in range(n_seq):
                for j in range(groups):
                    lre, lim = _scan_local(e, p, pw, c, b * rows + j * GROUP_ROWS)
                    idx = pl.ds(j * 64 + b, 8, stride=8)
                    ends[c, idx, :] = lre
                    ends[HALF + c, idx, :] = lim
        for c in range(HALF):
            lsre, lsim = _mult(pw, 1, c)
            cre = carry[c]
            cim = carry[HALF + c]
            for s in range(rows // SEG):
                cins[c, 8 * s:8 * s + 8, :] = cre
                cins[HALF + c, 8 * s:8 * s + 8, :] = cim
                mre, mim = _cmul(lsre, lsim, cre, cim)
                cre = mre + ends[c, 8 * s:8 * s + 8, :]
                cim = mim + ends[HALF + c, 8 * s:8 * s + 8, :]
            carry[c] = cre
            carry[HALF + c] = cim
            sre_ref[:, lane0 + c * LANE:lane0 + (c + 1) * LANE] = cre[:n_seq]
            sim_ref[:, lane0 + c * LANE:lane0 + (c + 1) * LANE] = cim[:n_seq]
        for c in range(HALF):
            for b in range(n_seq):
                for j in range(groups):
                    idx = pl.ds(j * 64 + b, 8, stride=8)
                    _scan_fixup(p, pw, c, b * rows + j * GROUP_ROWS, cins[c, idx, :], cins[HALF + c, idx, :])
        outs[h] = outs[h] + _state_term(p, n_seq * rows, wc_ref.at[h])
    for t, y in enumerate(_slab_phase_tiles(outs)):
        y_ref[t] = y.reshape(n_seq, rows, LANE)


def _ssm_sample_body(u_ref, wb_ref, kt_ref, wc_ref, pw_ref, d_ref, h0re_ref, h0im_ref,
                     y_ref, sre_ref, sim_ref, e_ref, p_ref, *, rows):
    outs = []
    for h, lhs in enumerate(_sub_slab_operands(u_ref)):
        e, p, pw = e_ref.at[h], p_ref.at[h], pw_ref.at[h]
        _state_increments(lhs, wb_ref.at[h], e)
        for c in range(HALF):
            lanes = slice(h * SUB_STATES + c * LANE, h * SUB_STATES + (c + 1) * LANE)
            lsre, lsim = _mult(pw, 1, c)
            for j in range(rows // GROUP_ROWS):
                lre, lim = _scan_local(e, p, pw, c, j * GROUP_ROWS)
                cre = h0re_ref[8 * j:8 * j + 8, lanes]
                cim = h0im_ref[8 * j:8 * j + 8, lanes]
                mre, mim = _cmul(lsre, lsim, cre, cim)
                sre_ref[8 * j:8 * j + 8, lanes] = mre + lre
                sim_ref[8 * j:8 * j + 8, lanes] = mim + lim
                _scan_fixup(p, pw, c, j * GROUP_ROWS, cre, cim)
        outs.append(_direct_terms(lhs, kt_ref.at[h], d_ref.at[h]) + _state_term(p, rows, wc_ref.at[h]))
    for t, y in enumerate(_slab_phase_tiles(outs)):
        y_ref[t] = y


def _ssm_body(u_ref, wb_ref, kt_ref, wc_ref, pw_ref, d_ref, us_ref, h0re_ref, h0im_ref,
              y_ref, sre_ref, sim_ref, ys_ref, sres_ref, sims_ref,
              e_ref, p_ref, end_ref, cin_ref, carry_ref, *, n_seq, rows, rows_s):
    step = pl.program_id(1)

    @pl.when(step == 0)
    def _():
        _ssm_sample_body(us_ref, wb_ref, kt_ref, wc_ref, pw_ref, d_ref, h0re_ref, h0im_ref,
                         ys_ref, sres_ref, sims_ref, e_ref, p_ref, rows=rows_s)

    @pl.when(step > 0)
    def _():
        _ssm_prompt_body(u_ref, wb_ref, kt_ref, wc_ref, pw_ref, d_ref, y_ref, sre_ref, sim_ref,
                         e_ref, p_ref, end_ref, cin_ref, carry_ref, n_seq=n_seq, rows=rows, tile=step - 1)


def _ssm(u8_p, n_p, u8_s, n_s, h0, wb, kt, wc, pw, d8):
    d = D_MODEL
    ns = SUBS * SUB_STATES
    rows_p, rows_s = u8_p.shape[1], u8_s.shape[1]
    per_seq = rows_p // n_p
    assert rows_s // n_s == SEG, "carried-state path scans one segment per sequence"
    assert n_p <= 8, "sequences ride the sublanes of the segment chain"
    rows = min(SSM_ROWS, per_seq)
    assert per_seq % rows == 0 and rows % GROUP_ROWS == 0 and rows_s % GROUP_ROWS == 0
    assert rows_s <= n_p * rows, "the carried-state step reuses the scan scratch"
    tiles = per_seq // rows
    slab = lambda q, i: (q, 0, 0, 0)
    p_spec = pl.BlockSpec((N_PHASE, n_p, rows, LANE), lambda q, i: (0, 0, jnp.maximum(i - 1, 0), q))
    s_spec = pl.BlockSpec((N_PHASE, rows_s, LANE), lambda q, i: (0, 0, q))
    w_spec = pl.BlockSpec((None, SUBS, SUB_K, 2 * SUB_STATES), slab)
    state_p = pl.BlockSpec((n_p, ns), lambda q, i: (0, q))
    state_s = pl.BlockSpec((n_s, ns), lambda q, i: (0, q))
    seg_rows = 8 * (rows // SEG)
    dims_p = (N_PHASE, n_p, per_seq, d)
    y_p, re_p, im_p, y_s, re_s, im_s = pl.pallas_call(
        functools.partial(_ssm_body, n_seq=n_p, rows=rows, rows_s=rows_s),
        grid=(N_SLAB, tiles + 1),
        in_specs=[p_spec, w_spec, w_spec, w_spec, pl.BlockSpec((None, SUBS, 8, 2 * SUB_STATES), slab),
                  pl.BlockSpec((None, SUBS, 1, SUB_K), slab), s_spec, state_s, state_s],
        out_specs=[p_spec, state_p, state_p, s_spec, state_s, state_s],
        out_shape=[jax.ShapeDtypeStruct(dims_p, BF16),
                   jax.ShapeDtypeStruct((n_p, N_SLAB * ns), F32), jax.ShapeDtypeStruct((n_p, N_SLAB * ns), F32),
                   jax.ShapeDtypeStruct((N_PHASE, rows_s, d), BF16),
                   jax.ShapeDtypeStruct((n_s, N_SLAB * ns), F32), jax.ShapeDtypeStruct((n_s, N_SLAB * ns), F32)],
        scratch_shapes=[pltpu.VMEM((SUBS, 2 * HALF, n_p * rows, LANE), F32),
                        pltpu.VMEM((SUBS, 2 * HALF, n_p * rows, LANE), F32),
                        pltpu.VMEM((SUBS, 2 * HALF, seg_rows, LANE), F32),
                        pltpu.VMEM((SUBS, 2 * HALF, seg_rows, LANE), F32),
                        pltpu.VMEM((SUBS, 2 * HALF, 8, LANE), F32)],
        name="ssm",
        compiler_params=_params(("arbitrary", "arbitrary")),
    )(u8_p.reshape(dims_p), wb, kt, wc, pw, d8, u8_s, h0[0].reshape(n_s, -1), h0[1].reshape(n_s, -1))
    return (y_p.reshape(N_PHASE, rows_p, d), re_p, im_p), (y_s, re_s, im_s)


def _mix_out_ffn_body(ya_hbm, aa_hbm, gsa_hbm, yb_hbm, ab_hbm, gsb_hbm, x_ref,
                      wglu_ref, wo_ref, g_ref, wg_ref, wu_ref, wd_ref, gf_ref,
                      oa_hbm, ob_hbm, inbuf_ref, ybuf_ref, sem_in, sem_out, *, rows, steps_a, n_steps, final_norm):
    d = D_MODEL
    step = pl.program_id(0)
    slot = step % 2

    def in_copies(srcs, row0, of_slot):
        return [pltpu.make_async_copy(src.at[:, pl.ds(row0, rows), :], inbuf_ref.at[of_slot, k], sem_in.at[of_slot])
                for k, src in enumerate(srcs)]

    def out_copies(dst, row0, of_slot):
        return [pltpu.make_async_copy(ybuf_ref.at[of_slot, t], dst.at[pl.ds(row0, rows), t, :], sem_out.at[of_slot])
                for t in range(N_PHASE)]

    def start_for_group(of_step, copies_a, copies_b):
        @pl.when(of_step < steps_a)
        def _():
            for c in copies_a(of_step * rows):
                c.start()

        @pl.when(of_step >= steps_a)
        def _():
            for c in copies_b((of_step - steps_a) * rows):
                c.start()

    def start_in(of_step, of_slot):
        start_for_group(of_step, lambda r: in_copies((ya_hbm, aa_hbm, gsa_hbm), r, of_slot),
                        lambda r: in_copies((yb_hbm, ab_hbm, gsb_hbm), r, of_slot))

    @pl.when(step == 0)
    def _():
        start_in(step, slot)

    @pl.when(step + 1 < n_steps)
    def _():
        start_in(step + 1, 1 - slot)

    @pl.when(step >= 2)
    def _():
        for c in out_copies(oa_hbm, 0, slot):
            c.wait()

    for c in in_copies((ya_hbm, aa_hbm, gsa_hbm), 0, slot):
        c.wait()
    y_ref, a_ref, gs_ref = (inbuf_ref.at[slot, k] for k in range(3))
    part = rows // MIX_OUT_PARTS
    parts = range(MIX_OUT_PARTS)

    def stack(ref, p):
        return jnp.concatenate([ref[t, p * part:(p + 1) * part] for t in range(N_PHASE)], axis=0)

    glus = [_dot(jax.nn.gelu(stack(y_ref, p).astype(F32)).astype(BF16), wglu_ref[...]) for p in parts]
    mergeds = [(stack(a_ref, p).astype(F32) + stack(gs_ref, p).astype(F32) * (glu[:, :d] * jax.nn.sigmoid(glu[:, d:])))
               .astype(BF16) for p, glu in zip(parts, glus)]
    x2s = [stack(x_ref, p) + _dot(merged, wo_ref[...]) for p, merged in zip(parts, mergeds)]
    for p, y in enumerate(_ffn_math(x2s, g_ref, wg_ref, wu_ref, wd_ref)):
        if final_norm:
            y = _rms(y, gf_ref[...])
        for t in range(N_PHASE):
            ybuf_ref[slot, t, p * part:(p + 1) * part] = y[t * part:(t + 1) * part]
    start_for_group(step, lambda r: out_copies(oa_hbm, r, slot), lambda r: out_copies(ob_hbm, r, slot))

    @pl.when(step == n_steps - 1)
    def _():
        if n_steps > 1:
            for c in out_copies(oa_hbm, 0, 1 - slot):
                c.wait()
        for c in out_copies(oa_hbm, 0, slot):
            c.wait()


def _mix_out_ffn(acts_a, acts_b, xp, w_glu, w_o, g, wg, wu, wd, gf, final_norm):
    rows_a, rows_b = acts_a[0].shape[1], acts_b[0].shape[1]
    d = D_MODEL
    rows = MIX_OUT_ROWS
    assert rows_a % rows == 0 and rows_b % rows == 0 and xp.shape[1] == rows_a + rows_b
    steps_a, n_steps = rows_a // rows, (rows_a + rows_b) // rows
    weights = (w_glu, w_o, g, wg, wu, wd, gf)
    any_spec = pl.BlockSpec(memory_space=pl.ANY)
    out_a, out_b = pl.pallas_call(
        functools.partial(_mix_out_ffn_body, rows=rows, steps_a=steps_a, n_steps=n_steps, final_norm=final_norm),
        grid=(n_steps,),
        in_specs=[any_spec] * 6 + [pl.BlockSpec((N_PHASE, rows, d), lambda i: (0, i, 0))]
        + [_const_spec(w.shape) for w in weights],
        out_specs=[any_spec, any_spec],
        out_shape=[jax.ShapeDtypeStruct((rows_a, N_PHASE, d), F32), jax.ShapeDtypeStruct((rows_b, N_PHASE, d), F32)],
        scratch_shapes=[pltpu.VMEM((2, 3, N_PHASE, rows, d), BF16),
                        pltpu.VMEM((2, N_PHASE, rows, d), F32),
                        pltpu.SemaphoreType.DMA((2,)), pltpu.SemaphoreType.DMA((2,))],
        name="mix_out_ffn",
        compiler_params=_params(("arbitrary",)),
    )(*acts_a, *acts_b, xp, *weights)
    return out_a.reshape(N_PHASE * rows_a, d), out_b.reshape(N_PHASE * rows_b, d)


def kernel(x_prompt, x_sample, state_conv, state_ssm_re, state_ssm_im, norm_ffn1, w_ffn1_gate, w_ffn1_up, w_ffn1_down, norm_mix, w_in, w_conv, w_conv_out, ssm_lambda_re, ssm_lambda_im, ssm_log_step, ssm_b_re, ssm_b_im, ssm_c_re, ssm_c_im, ssm_d, w_glu, w_o, norm_ffn2, w_ffn2_gate, w_ffn2_up, w_ffn2_down, norm_final):
    depth, d = w_in.shape[0], w_in.shape[1]
    row = lambda v: v.reshape(1, -1)
    nf = row(norm_final)
    xt_p = x_prompt.reshape(-1, d)
    xt_s = x_sample.reshape(-1, d)
    n_p, n_s = x_prompt.shape[0], x_sample.shape[0]
    outs_p, outs_s = [], []
    for l in range(depth):
        last = l == depth - 1
        ssm_w, (f1g, f1u, f1d) = _ssm_prep(
            ssm_lambda_re[l], ssm_lambda_im[l], ssm_log_step[l], ssm_b_re[l], ssm_b_im[l], ssm_c_re[l],
            ssm_c_im[l], riders=(w_ffn1_gate[l], w_ffn1_up[l], w_ffn1_down[l]))
        ffn1 = (row(norm_ffn1[l]), f1g, f1u, f1d)
        xp, (win, wco, wgl, wo) = _ffn_to_phase(xt_p, xt_s, *ffn1, riders=(w_in[l], w_conv_out[l], w_glu[l], w_o[l]))
        rows_p, rows_s = xt_p.shape[0] // N_PHASE, xt_s.shape[0] // N_PHASE
        d8 = jnp.tile(ssm_d[l].reshape(N_SLAB, SUBS, 1, SUB), (1, 1, 1, N_PHASE))
        mix_w = (row(norm_mix[l]), win, w_conv[l], wco)
        u_p, a_p, gs_p, conv_p, ffn2_w = _mix_in(xp, 0, rows_p, None, *mix_w, n_p,
                                                 riders=(w_ffn2_gate[l], w_ffn2_up[l], w_ffn2_down[l]))
        u_s, a_s, gs_s, conv_s, _ = _mix_in(xp, rows_p, rows_s, state_conv[l], *mix_w, n_s)
        (y_p, re_p, im_p), (y_s, re_s, im_s) = _ssm(u_p, n_p, u_s, n_s, (state_ssm_re[l], state_ssm_im[l]),
                                                    *ssm_w, d8)
        tail = (wgl, wo, row(norm_ffn2[l]), *ffn2_w, nf, last)
        xt_p, xt_s = _mix_out_ffn((y_p, a_p, gs_p), (y_s, a_s, gs_s), xp, *tail)
        outs_p.append([conv_p, re_p.reshape(n_p, -1, STATE_DIM), im_p.reshape(n_p, -1, STATE_DIM)])
        outs_s.append([conv_s, re_s.reshape(n_s, -1, STATE_DIM), im_s.reshape(n_s, -1, STATE_DIM)])
    stack = lambda outs: tuple(jnp.stack(leaf) for leaf in zip(*outs))
    return (xt_p.reshape(x_prompt.shape), xt_s.reshape(x_sample.shape), *stack(outs_p), *stack(outs_s))
```

```python
import functools

import jax
import jax.numpy as jnp
from jax import lax
from jax.experimental import pallas as pl
from jax.experimental.pallas import tpu as pltpu

F32 = jnp.float32
BF16 = jnp.bfloat16

D_MODEL = 1024
N_PHASE = 8
LANE = 128
MXU_TILE = 256
SLAB = 128
N_SLAB = D_MODEL // SLAB
SUB = 64
SUBS = SLAB // SUB
GROUP_SIZE = 16
STATE_DIM = 64
SUB_STATES = (SUB // GROUP_SIZE) * STATE_DIM
SUB_K = N_PHASE * SUB
SEG = 4
RMS_EPS = 1e-6
VMEM_LIMIT = 60 * 1024 * 1024

FFN_ROWS = 1024
MIX_OUT_PARTS = 2
MIX_IN_ROWS = 128
MIX_OUT_ROWS = 64
SSM_ROWS = 256


def _rms(x, g):
    return x * lax.rsqrt(jnp.mean(x * x, axis=-1, keepdims=True) + RMS_EPS) * g


def _dot(a, b):
    return jnp.dot(a, b, preferred_element_type=F32)


def _const_spec(shape):
    zeros = (0,) * len(shape)
    return pl.BlockSpec(shape, lambda *_: zeros, pipeline_mode=pl.Buffered(1))


def _params(semantics):
    return pltpu.CompilerParams(dimension_semantics=semantics, vmem_limit_bytes=VMEM_LIMIT)


def _stack_phases(ref):
    return jnp.concatenate([ref[t] for t in range(N_PHASE)], axis=0)


def _ffn_math(xs, g_ref, wg_ref, wu_ref, wd_ref):
    hs = [_rms(x, g_ref[...]).astype(BF16) for x in xs]
    gates = [_dot(h, wg_ref[...]) for h in hs]
    ups = [_dot(h, wu_ref[...]) for h in hs]
    acts = [(jax.nn.silu(gate) * up).astype(BF16) for gate, up in zip(gates, ups)]
    return [x + 0.5 * _dot(act, wd_ref[...]) for x, act in zip(xs, acts)]


def _rider_specs(weights, n_steps, step=lambda i: i):
    in_specs, out_specs, out_shapes = [], [], []
    for w in weights:
        n_blocks = n_steps
        while w.shape[0] % (16 * n_blocks):
            assert n_blocks % 2 == 0
            n_blocks //= 2
        spec = pl.BlockSpec((w.shape[0] // n_blocks, w.shape[1]),
                            lambda *idx, rep=n_steps // n_blocks: (step(*idx) // rep, 0))
        in_specs.append(spec)
        out_specs.append(spec)
        out_shapes.append(jax.ShapeDtypeStruct(w.shape, BF16))
    return in_specs, out_specs, out_shapes


def _convert_riders(in_refs, out_refs):
    for w_ref, o_ref in zip(in_refs, out_refs):
        o_ref[...] = w_ref[...].astype(BF16)


def _ffn_to_phase_body(*refs, rows, steps_a, n_steps, n_riders):
    xa_hbm, xb_hbm, g_ref, wg_ref, wu_ref, wd_ref = refs[:6]
    o_ref, xbuf_ref, sem = refs[6 + n_riders], refs[-2], refs[-1]
    step = pl.program_id(0)
    slot = step % 2

    def copies(src, row0, of_slot):
        return [pltpu.make_async_copy(src.at[pl.ds(row0, rows), t, :], xbuf_ref.at[of_slot, t], sem.at[of_slot])
                for t in range(N_PHASE)]

    def start(of_step, of_slot):
        @pl.when(of_step < steps_a)
        def _():
            for c in copies(xa_hbm, of_step * rows, of_slot):
                c.start()

        @pl.when(of_step >= steps_a)
        def _():
            for c in copies(xb_hbm, (of_step - steps_a) * rows, of_slot):
                c.start()

    @pl.when(step == 0)
    def _():
        start(step, slot)

    @pl.when(step + 1 < n_steps)
    def _():
        start(step + 1, 1 - slot)

    for c in copies(xa_hbm, 0, slot):
        c.wait()
    _convert_riders(refs[6:6 + n_riders], refs[7 + n_riders:7 + 2 * n_riders])
    x = jnp.concatenate([xbuf_ref[slot, t] for t in range(N_PHASE)], axis=0)
    (y,) = _ffn_math([x], g_ref, wg_ref, wu_ref, wd_ref)
    for t in range(N_PHASE):
        o_ref[t] = y[t * rows:(t + 1) * rows]


def _ffn_to_phase(x_a, x_b, g, wg, wu, wd, riders=()):
    d = x_a.shape[1]
    rows_a, rows_b = x_a.shape[0] // N_PHASE, x_b.shape[0] // N_PHASE
    n_rows = rows_a + rows_b
    d_ff = wg.shape[1]
    rows = FFN_ROWS // N_PHASE
    assert rows_a % rows == 0 and rows_b % rows == 0
    steps_a, steps_b = rows_a // rows, rows_b // rows
    n_steps = steps_a + steps_b
    rider_in, rider_out, rider_shapes = _rider_specs(riders, steps_a, lambda i: jnp.minimum(i, steps_a - 1))
    out = pl.pallas_call(
        functools.partial(_ffn_to_phase_body, rows=rows, steps_a=steps_a, n_steps=n_steps, n_riders=len(riders)),
        grid=(n_steps,),
        in_specs=[
            pl.BlockSpec(memory_space=pl.ANY),
            pl.BlockSpec(memory_space=pl.ANY),
            _const_spec((1, d)),
            _const_spec((d, d_ff)),
            _const_spec((d, d_ff)),
            _const_spec((d_ff, d)),
        ] + rider_in,
        out_specs=[pl.BlockSpec((N_PHASE, rows, d), lambda i: (0, i, 0))] + rider_out,
        out_shape=[jax.ShapeDtypeStruct((N_PHASE, n_rows, d), F32)] + rider_shapes,
        scratch_shapes=[pltpu.VMEM((2, N_PHASE, rows, d), F32), pltpu.SemaphoreType.DMA((2,))],
        name="ffn_to_phase",
        compiler_params=_params(("arbitrary",)),
    )(x_a.reshape(rows_a, N_PHASE, d), x_b.reshape(rows_b, N_PHASE, d), g, wg, wu, wd, *riders)
    return out[0], tuple(out[1:])


def _mix_in_compute(x_ref, g, win_ref, wcv, wco_ref, shift, rows):
    d = D_MODEL
    h = _rms(_stack_phases(x_ref), g).astype(BF16)

    def proj(k):
        return _dot(h, win_ref[:, k * d:(k + 1) * d])

    z = proj(1) * proj(0)
    b_gate = proj(2)
    u = proj(3)
    g_conv = jax.nn.sigmoid(proj(4))
    gs = jax.nn.sigmoid(proj(5))
    z6 = z[6 * rows:7 * rows]
    z7 = z[7 * rows:8 * rows]
    s6 = shift(z6, 0)
    s7 = shift(z7, 1)
    z1 = jnp.concatenate([s7, z[:7 * rows]], axis=0)
    z2 = jnp.concatenate([s6, s7, z[:6 * rows]], axis=0)
    conv = wcv[2:3] * z + wcv[1:2] * z1 + wcv[0:1] * z2
    a = g_conv * _dot((b_gate * conv).astype(BF16), wco_ref[...])
    return u, a, gs, z6, z7


def _store_phases(ref, val, rows):
    val = val.astype(ref.dtype)
    for t in range(N_PHASE):
        ref[t] = val[t * rows:(t + 1) * rows]


def _store_sub_slab_pairs(ref, val, rows):
    for k in range(N_PHASE // 2):
        a = val[2 * k * rows:(2 * k + 1) * rows]
        b = val[(2 * k + 1) * rows:(2 * k + 2) * rows]
        for q in range(D_MODEL // LANE):
            lanes = slice(q * LANE, (q + 1) * LANE)
            for h, tile in enumerate(_pair_halves(a[:, lanes], b[:, lanes])):
                ref[2 * k + h, :, lanes] = tile.astype(ref.dtype)


def _mix_in_prompt_body(*refs, rows, n_riders):
    x_ref, g_ref, win_ref, wcv_ref, wco_ref = refs[:5]
    u_ref, a_ref, gs_ref, z_ref = refs[5 + n_riders:9 + n_riders]
    carry_ref = refs[-1]
    _convert_riders(refs[5:5 + n_riders], refs[9 + n_riders:9 + 2 * n_riders])

    @pl.when(pl.program_id(1) == 0)
    def _():
        carry_ref[...] = jnp.zeros_like(carry_ref)

    row_id = lax.broadcasted_iota(jnp.int32, (rows, D_MODEL), 0)

    def shift(z, k):
        return jnp.where(row_id == 0, carry_ref[k:k + 1, :], pltpu.roll(z, 1, 0))

    u, a, gs, z6, z7 = _mix_in_compute(x_ref, g_ref[...], win_ref, wcv_ref[...], wco_ref, shift, rows)
    last6 = z6[rows - 1:rows]
    last7 = z7[rows - 1:rows]
    carry_ref[0:1, :] = last6
    carry_ref[1:2, :] = last7
    z_ref[0:1, :] = last6
    z_ref[1:2, :] = last7
    _store_sub_slab_pairs(u_ref, u, rows)
    _store_phases(a_ref, a, rows)
    _store_phases(gs_ref, gs, rows)


def _mix_in_sample_body(x_ref, start_ref, g_ref, win_ref, wcv_ref, wco_ref,
                        u_ref, a_ref, gs_ref, z6_ref, z7_ref, slab_ref, *, rows, rows_per_seq):
    row_id = lax.broadcasted_iota(jnp.int32, (rows, D_MODEL), 0)

    def shift(z, k):
        return jnp.where(row_id % rows_per_seq == 0, start_ref[k], pltpu.roll(z, 1, 0))

    u, a, gs, z6, z7 = _mix_in_compute(x_ref, g_ref[...], win_ref, wcv_ref[...], wco_ref, shift, rows)
    for z, z_ref in ((z6, z6_ref), (z7, z7_ref)):
        for lt in range(D_MODEL // LANE):
            slab_ref[...] = z[:, lt * LANE:(lt + 1) * LANE]
            z_ref[:, lt * LANE:(lt + 1) * LANE] = slab_ref[pl.ds(rows_per_seq - 1, rows // rows_per_seq,
                                                                stride=rows_per_seq), :]
    _store_sub_slab_pairs(u_ref, u, rows)
    _store_phases(a_ref, a, rows)
    _store_phases(gs_ref, gs, rows)


def _mix_in(xp, row0, n_rows, conv_prev, g, w_in, w_conv, w_conv_out, n_seq, riders=()):
    d = D_MODEL
    rows_per_seq = n_rows // n_seq
    ph_shape = jax.ShapeDtypeStruct((N_PHASE, n_rows, d), BF16)
    weights = (g, w_in, w_conv, w_conv_out)
    weight_specs = [_const_spec(w.shape) for w in weights]
    if conv_prev is None:
        rows = min(MIX_IN_ROWS, rows_per_seq)
        assert rows_per_seq % rows == 0
        tiles = rows_per_seq // rows
        assert row0 % rows == 0
        ph_spec = pl.BlockSpec((N_PHASE, rows, d), lambda b, i: (0, b * tiles + i, 0))
        x_spec = pl.BlockSpec((N_PHASE, rows, d), lambda b, i: (0, row0 // rows + b * tiles + i, 0))
        last_spec = pl.BlockSpec((None, 2, d), lambda b, i: (b, 0, 0))
        rider_in, rider_out, rider_shapes = _rider_specs(riders, n_seq * tiles, lambda b, i: b * tiles + i)
        u8, a8, gs8, conv_state, *converted = pl.pallas_call(
            functools.partial(_mix_in_prompt_body, rows=rows, n_riders=len(riders)),
            grid=(n_seq, tiles),
            in_specs=[x_spec] + weight_specs + rider_in,
            out_specs=[ph_spec, ph_spec, ph_spec, last_spec] + rider_out,
            out_shape=[ph_shape, ph_shape, ph_shape, jax.ShapeDtypeStruct((n_seq, 2, d), F32)] + rider_shapes,
            scratch_shapes=[pltpu.VMEM((8, d), F32)],
            name="mix_in_prompt",
            compiler_params=_params(("arbitrary", "arbitrary")),
        )(xp, *weights, *riders)
    else:
        assert not riders
        converted = []
        rows = n_rows
        assert row0 % rows == 0
        start = jnp.repeat(jnp.swapaxes(conv_prev, 0, 1), rows_per_seq, axis=1)
        full = lambda shape: pl.BlockSpec(shape, lambda i: (0,) * len(shape))
        x_spec = pl.BlockSpec((N_PHASE, rows, d), lambda i: (0, row0 // rows, 0))
        u8, a8, gs8, z6, z7 = pl.pallas_call(
            functools.partial(_mix_in_sample_body, rows=rows, rows_per_seq=rows_per_seq),
            grid=(1,),
            in_specs=[x_spec, full((2, rows, d))] + weight_specs,
            out_specs=[full((N_PHASE, rows, d))] * 3 + [full((n_seq, d))] * 2,
            out_shape=[ph_shape, ph_shape, ph_shape,
                       jax.ShapeDtypeStruct((n_seq, d), F32), jax.ShapeDtypeStruct((n_seq, d), F32)],
            scratch_shapes=[pltpu.VMEM((rows, LANE), F32)],
            name="mix_in_sample",
            compiler_params=_params(("arbitrary",)),
        )(xp, start, *weights)
        conv_state = jnp.stack([z6, z7], axis=1)
    return u8, a8, gs8, conv_state, tuple(converted)


def _split_bf16(a):
    hi = a.astype(BF16)
    return hi, (a - hi.astype(F32)).astype(BF16)


def _ssm_prep_body(*refs, n_riders):
    bbre_ref, bbim_ref, ctre_ref, ctim_ref, lre_ref, lim_ref, lstep_ref = refs[:7]
    wb_ref, kt_ref, wc_ref, pw_ref = refs[7 + n_riders:11 + n_riders]
    _convert_riders(refs[7:7 + n_riders], refs[11 + n_riders:])
    ns = SUB_STATES
    row_group = lax.broadcasted_iota(jnp.int32, (SUB, LANE), 0) // GROUP_SIZE
    lane_half = lax.broadcasted_iota(jnp.int32, (SUB, LANE), 1) // STATE_DIM
    low_half = lax.broadcasted_iota(jnp.int32, (SUB, LANE), 1) < SUB

    def block_diag(ref):
        tiles = [jnp.where(row_group == 2 * k + lane_half, ref[...], 0.0) for k in range(ns // LANE)]
        return jnp.concatenate(tiles, axis=1)

    bbre, bbim = block_diag(bbre_ref), block_diag(bbim_ref)
    ctre, ctim = block_diag(ctre_ref), block_diag(ctim_ref)
    lre, lim = lre_ref[...], lim_ref[...]
    step = jnp.exp(lstep_ref[...])

    def lam_pow(n):
        mag = jnp.exp((n * lre) * step)
        ang = (n * lim) * step
        return mag * jnp.cos(ang), mag * jnp.sin(ang)

    l1re, l1im = lam_pow(1)
    den = lre * lre + lim * lim
    fre = ((l1re - 1.0) * lre + l1im * lim) / den
    fim = (l1im * lre - (l1re - 1.0) * lim) / den

    def dot_nt(a, b_split):
        nt = lambda p, q: lax.dot_general(p, q, (((1,), (1,)), ((), ())), preferred_element_type=F32)
        a_hi, a_lo = _split_bf16(a)
        b_hi, b_lo = b_split
        return nt(a_hi, b_hi) + nt(a_hi, b_lo) + nt(a_lo, b_hi)

    def c_pair(n):
        (p0re, p0im), (p1re, p1im) = lam_pow(n), lam_pow(n + 1)
        return (jnp.concatenate([ctre * p0re - ctim * p0im, ctre * p1re - ctim * p1im], axis=0),
                jnp.concatenate([ctre * p0im + ctim * p0re, ctre * p1im + ctim * p1re], axis=0))

    c01re, c01im = c_pair(0)
    c01re_split = _split_bf16(c01re)
    c01im_split = _split_bf16(c01im)
    lag_pairs = {}
    for k in range(N_PHASE):
        pre, pim = lam_pow(k)
        gre = fre * pre - fim * pim
        gim = fre * pim + fim * pre
        are = bbre * gre - bbim * gim
        aim = bbre * gim + bbim * gre
        j = N_PHASE - 1 - k
        wb_ref[j * SUB:(j + 1) * SUB, 0:ns] = are.astype(BF16)
        wb_ref[j * SUB:(j + 1) * SUB, ns:2 * ns] = aim.astype(BF16)
        if k < N_PHASE - 1:
            lag_pairs[k] = dot_nt(are, c01re_split) - dot_nt(aim, c01im_split)
    kt_ref[...] = jnp.zeros_like(kt_ref)
    first_odd = jnp.where(low_half, 0.0, pltpu.roll(lag_pairs[0], SUB, 1))
    for j in range(N_PHASE):
        for tile in range(j // 2, N_PHASE // 2):
            lag = 2 * tile - j
            pair = first_odd if lag < 0 else lag_pairs[lag]
            kt_ref[j * SUB:(j + 1) * SUB, tile * LANE:(tile + 1) * LANE] = pair.astype(BF16)
    for tile in range(N_PHASE // 2):
        cre, cim = c_pair(2 * tile + 1)
        wc_ref[0:ns, tile * LANE:(tile + 1) * LANE] = cre.T.astype(BF16)
        wc_ref[ns:2 * ns, tile * LANE:(tile + 1) * LANE] = (-cim).T.astype(BF16)
    pw_ref[...] = jnp.zeros_like(pw_ref)
    for r, n in enumerate((N_PHASE, N_PHASE * SEG)):
        pre, pim = lam_pow(n)
        pw_ref[r:r + 1, 0:ns] = pre
        pw_ref[r:r + 1, ns:2 * ns] = pim


def _ssm_prep(lam_re, lam_im, log_step, b_re, b_im, c_re, c_im, riders=()):
    ns = SUB_STATES
    n_sub = N_SLAB * SUBS
    assert 2 * STATE_DIM == LANE and SUB_K == 2 * ns

    def twice(m):
        m = m.reshape(n_sub, SUB, STATE_DIM)
        return jnp.concatenate([m, m], axis=-1)

    def vec(v):
        return v.reshape(n_sub, 1, ns)

    mats = [twice(jnp.transpose(b_re, (0, 2, 1))), twice(jnp.transpose(b_im, (0, 2, 1))), twice(c_re), twice(c_im)]
    vecs = [vec(lam_re), vec(lam_im), vec(jnp.broadcast_to(log_step[:, None], lam_re.shape))]
    w_shape = jax.ShapeDtypeStruct((n_sub, SUB_K, 2 * ns), BF16)
    w_spec = pl.BlockSpec((None, SUB_K, 2 * ns), lambda q: (q, 0, 0))
    rider_in, rider_out, rider_shapes = _rider_specs(riders, n_sub)
    out = pl.pallas_call(
        functools.partial(_ssm_prep_body, n_riders=len(riders)),
        grid=(n_sub,),
        in_specs=[pl.BlockSpec((None, SUB, LANE), lambda q: (q, 0, 0))] * 4
        + [pl.BlockSpec((None, 1, ns), lambda q: (q, 0, 0))] * 3 + rider_in,
        out_specs=[w_spec, w_spec, w_spec, pl.BlockSpec((None, 8, 2 * ns), lambda q: (q, 0, 0))] + rider_out,
        out_shape=[w_shape, w_shape, w_shape, jax.ShapeDtypeStruct((n_sub, 8, 2 * ns), F32)] + rider_shapes,
        name="ssm_prep",
        compiler_params=_params(("arbitrary",)),
    )(*mats, *vecs, *riders)
    return tuple(w.reshape(N_SLAB, SUBS, *w.shape[1:]) for w in out[:4]), tuple(out[4:])


def _cmul(are, aim, bre, bim):
    return are * bre - aim * bim, are * bim + aim * bre


HALF = SUB_STATES // LANE
GROUP_ROWS = 8 * SEG


def _pair_halves(a, b):
    low = lax.broadcasted_iota(jnp.int32, a.shape, 1) < SUB
    return jnp.where(low, a, pltpu.roll(b, SUB, 1)), jnp.where(low, pltpu.roll(a, SUB, 1), b)


def _sub_slab_operands(u_ref):
    return [jnp.concatenate([u_ref[2 * k + h].reshape(-1, LANE) for k in range(N_PHASE // 2)], axis=1)
            for h in range(SUBS)]


def _slab_phase_tiles(subs):
    tiles = []
    for k in range(N_PHASE // 2):
        tiles += _pair_halves(subs[0][:, k * LANE:(k + 1) * LANE], subs[1][:, k * LANE:(k + 1) * LANE])
    return [y.astype(BF16) for y in tiles]


def _mult(pw_ref, r, c):
    return (pw_ref[r:r + 1, c * LANE:(c + 1) * LANE],
            pw_ref[r:r + 1, (HALF + c) * LANE:(HALF + c + 1) * LANE])


def _state_increments(lhs, wb_ref, e_ref):
    e = _dot(lhs, wb_ref[...])
    for lt in range(2 * HALF):
        e_ref[lt, :e.shape[0]] = e[:, lt * LANE:(lt + 1) * LANE]


def _scan_local(e_ref, p_ref, pw_ref, c, base):
    l8re, l8im = _mult(pw_ref, 0, c)
    idx = pl.ds(base, 8, stride=SEG)
    lre = e_ref[c, idx, :]
    lim = e_ref[HALF + c, idx, :]
    for i in range(1, SEG):
        idx = pl.ds(base + i, 8, stride=SEG)
        p_ref[c, idx, :] = lre
        p_ref[HALF + c, idx, :] = lim
        mre, mim = _cmul(l8re, l8im, lre, lim)
        lre = mre + e_ref[c, idx, :]
        lim = mim + e_ref[HALF + c, idx, :]
    return lre, lim


def _scan_fixup(p_ref, pw_ref, c, base, cre, cim):
    l8re, l8im = _mult(pw_ref, 0, c)
    idx = pl.ds(base, 8, stride=SEG)
    p_ref[c, idx, :] = cre
    p_ref[HALF + c, idx, :] = cim
    for i in range(1, SEG):
        idx = pl.ds(base + i, 8, stride=SEG)
        cre, cim = _cmul(l8re, l8im, cre, cim)
        p_ref[c, idx, :] = p_ref[c, idx, :] + cre
        p_ref[HALF + c, idx, :] = p_ref[HALF + c, idx, :] + cim


def _direct_terms(lhs, kt_ref, d_ref):
    return _dot(lhs, kt_ref[...]) + d_ref[...] * lhs.astype(F32)


def _state_term(p_ref, n_rows, wc_ref):
    prev = jnp.concatenate([p_ref[lt, :n_rows] for lt in range(2 * HALF)], axis=1)
    return _dot(prev.astype(BF16), wc_ref[...])


def _ssm_prompt_body(u_ref, wb_ref, kt_ref, wc_ref, pw_ref, d_ref, y_ref, sre_ref, sim_ref,
                     e_ref, p_ref, end_ref, cin_ref, carry_ref, *, n_seq, rows, tile):
    @pl.when((pl.program_id(0) == 0) & (tile == 0))
    def _():
        end_ref[...] = jnp.zeros_like(end_ref)

    @pl.when(tile == 0)
    def _():
        carry_ref[...] = jnp.zeros_like(carry_ref)

    operands = _sub_slab_operands(u_ref)
    groups = rows // GROUP_ROWS
    for h, lhs in enumerate(operands):
        _state_increments(lhs, wb_ref.at[h], e_ref.at[h])
    outs = [_direct_terms(lhs, kt_ref.at[h], d_ref.at[h]) for h, lhs in enumerate(operands)]
    for h in range(SUBS):
        e, p, ends, cins, carry, pw = e_ref.at[h], p_ref.at[h], end_ref.at[h], cin_ref.at[h], carry_ref.at[h], pw_ref.at[h]
        lane0 = h * SUB_STATES
        for c in range(HALF):
            for b in range(n_seq):
                for j in range(groups):
                    lre, lim = _scan_local(e, p, pw, c, b * rows + j * GROUP_ROWS)
                    idx = pl.ds(j * 64 + b, 8, stride=8)
                    ends[c, idx, :] = lre
                    ends[HALF + c, idx, :] = lim
        for c in range(HALF):
            lsre, lsim = _mult(pw, 1, c)
            cre = carry[c]
            cim = carry[HALF + c]
            for s in range(rows // SEG):
                cins[c, 8 * s:8 * s + 8, :] = cre
                cins[HALF + c, 8 * s:8 * s + 8, :] = cim
                mre, mim = _cmul(lsre, lsim, cre, cim)
                cre = mre + ends[c, 8 * s:8 * s + 8, :]
                cim = mim + ends[HALF + c, 8 * s:8 * s + 8, :]
            carry[c] = cre
            carry[HALF + c] = cim
            sre_ref[:, lane0 + c * LANE:lane0 + (c + 1) * LANE] = cre[:n_seq]
            sim_ref[:, lane0 + c * LANE:lane0 + (c + 1) * LANE] = cim[:n_seq]
        for c in range(HALF):
            for b in range(n_seq):
                for j in range(groups):
                    idx = pl.ds(j * 64 + b, 8, stride=8)
                    _scan_fixup(p, pw, c, b * rows + j * GROUP_ROWS, cins[c, idx, :], cins[HALF + c, idx, :])
        outs[h] = outs[h] + _state_term(p, n_seq * rows, wc_ref.at[h])
    for t, y in enumerate(_slab_phase_tiles(outs)):
        y_ref[t] = y.reshape(n_seq, rows, LANE)


def _ssm_sample_body(u_ref, wb_ref, kt_ref, wc_ref, pw_ref, d_ref, h0re_ref, h0im_ref,
                     y_ref, sre_ref, sim_ref, e_ref, p_ref, *, rows):
    operands = _sub_slab_operands(u_ref)
    for h, lhs in enumerate(operands):
        _state_increments(lhs, wb_ref.at[h], e_ref.at[h])
    outs = [_direct_terms(lhs, kt_ref.at[h], d_ref.at[h]) for h, lhs in enumerate(operands)]
    for h in range(SUBS):
        e, p, pw = e_ref.at[h], p_ref.at[h], pw_ref.at[h]
        for c in range(HALF):
            lanes = slice(h * SUB_STATES + c * LANE, h * SUB_STATES + (c + 1) * LANE)
            lsre, lsim = _mult(pw, 1, c)
            for j in range(rows // GROUP_ROWS):
                lre, lim = _scan_local(e, p, pw, c, j * GROUP_ROWS)
                cre = h0re_ref[8 * j:8 * j + 8, lanes]
                cim = h0im_ref[8 * j:8 * j + 8, lanes]
                mre, mim = _cmul(lsre, lsim, cre, cim)
                sre_ref[8 * j:8 * j + 8, lanes] = mre + lre
                sim_ref[8 * j:8 * j + 8, lanes] = mim + lim
                _scan_fixup(p, pw, c, j * GROUP_ROWS, cre, cim)
        outs[h] = outs[h] + _state_term(p, rows, wc_ref.at[h])
    for t, y in enumerate(_slab_phase_tiles(outs)):
        y_ref[t] = y


def _ssm_body(u_ref, wb_ref, kt_ref, wc_ref, pw_ref, d_ref, us_ref, h0re_ref, h0im_ref,
              y_ref, sre_ref, sim_ref, ys_ref, sres_ref, sims_ref,
              e_ref, p_ref, end_ref, cin_ref, carry_ref, *, n_seq, rows, rows_s):
    step = pl.program_id(1)

    @pl.when(step == 0)
    def _():
        _ssm_sample_body(us_ref, wb_ref, kt_ref, wc_ref, pw_ref, d_ref, h0re_ref, h0im_ref,
                         ys_ref, sres_ref, sims_ref, e_ref, p_ref, rows=rows_s)

    @pl.when(step > 0)
    def _():
        _ssm_prompt_body(u_ref, wb_ref, kt_ref, wc_ref, pw_ref, d_ref, y_ref, sre_ref, sim_ref,
                         e_ref, p_ref, end_ref, cin_ref, carry_ref, n_seq=n_seq, rows=rows, tile=step - 1)


def _ssm(u8_p, n_p, u8_s, n_s, h0, wb, kt, wc, pw, d8):
    d = D_MODEL
    ns = SUBS * SUB_STATES
    rows_p, rows_s = u8_p.shape[1], u8_s.shape[1]
    per_seq = rows_p // n_p
    assert rows_s // n_s == SEG, "carried-state path scans one segment per sequence"
    assert n_p <= 8, "sequences ride the sublanes of the segment chain"
    rows = min(SSM_ROWS, per_seq)
    assert per_seq % rows == 0 and rows % GROUP_ROWS == 0 and rows_s % GROUP_ROWS == 0
    assert rows_s <= n_p * rows, "the carried-state step reuses the scan scratch"
    tiles = per_seq // rows
    slab = lambda q, i: (q, 0, 0, 0)
    p_spec = pl.BlockSpec((N_PHASE, n_p, rows, LANE), lambda q, i: (0, 0, jnp.maximum(i - 1, 0), q))
    s_spec = pl.BlockSpec((N_PHASE, rows_s, LANE), lambda q, i: (0, 0, q))
    w_spec = pl.BlockSpec((None, SUBS, SUB_K, 2 * SUB_STATES), slab)
    state_p = pl.BlockSpec((n_p, ns), lambda q, i: (0, q))
    state_s = pl.BlockSpec((n_s, ns), lambda q, i: (0, q))
    seg_rows = 8 * (rows // SEG)
    dims_p = (N_PHASE, n_p, per_seq, d)
    y_p, re_p, im_p, y_s, re_s, im_s = pl.pallas_call(
        functools.partial(_ssm_body, n_seq=n_p, rows=rows, rows_s=rows_s),
        grid=(N_SLAB, tiles + 1),
        in_specs=[p_spec, w_spec, w_spec, w_spec, pl.BlockSpec((None, SUBS, 8, 2 * SUB_STATES), slab),
                  pl.BlockSpec((None, SUBS, 1, SUB_K), slab), s_spec, state_s, state_s],
        out_specs=[p_spec, state_p, state_p, s_spec, state_s, state_s],
        out_shape=[jax.ShapeDtypeStruct(dims_p, BF16),
                   jax.ShapeDtypeStruct((n_p, N_SLAB * ns), F32), jax.ShapeDtypeStruct((n_p, N_SLAB * ns), F32),
                   jax.ShapeDtypeStruct((N_PHASE, rows_s, d), BF16),
                   jax.ShapeDtypeStruct((n_s, N_SLAB * ns), F32), jax.ShapeDtypeStruct((n_s, N_SLAB * ns), F32)],
        scratch_shapes=[pltpu.VMEM((SUBS, 2 * HALF, n_p * rows, LANE), F32),
                        pltpu.VMEM((SUBS, 2 * HALF, n_p * rows, LANE), F32),
                        pltpu.VMEM((SUBS, 2 * HALF, seg_rows, LANE), F32),
                        pltpu.VMEM((SUBS, 2 * HALF, seg_rows, LANE), F32),
                        pltpu.VMEM((SUBS, 2 * HALF, 8, LANE), F32)],
        name="ssm",
        compiler_params=_params(("arbitrary", "arbitrary")),
    )(u8_p.reshape(dims_p), wb, kt, wc, pw, d8, u8_s, h0[0].reshape(n_s, -1), h0[1].reshape(n_s, -1))
    return (y_p.reshape(N_PHASE, rows_p, d), re_p, im_p), (y_s, re_s, im_s)


def _mix_out_ffn_body(ya_hbm, aa_hbm, gsa_hbm, yb_hbm, ab_hbm, gsb_hbm, x_ref,
                      wglu_ref, wo_ref, g_ref, wg_ref, wu_ref, wd_ref, gf_ref,
                      oa_hbm, ob_hbm, inbuf_ref, ybuf_ref, sem_in, sem_out, *, rows, steps_a, n_steps, final_norm):
    d = D_MODEL
    step = pl.program_id(0)
    slot = step % 2

    def in_copies(srcs, row0, of_slot):
        return [pltpu.make_async_copy(src.at[:, pl.ds(row0, rows), :], inbuf_ref.at[of_slot, k], sem_in.at[of_slot])
                for k, src in enumerate(srcs)]

    def out_copies(dst, row0, of_slot):
        return [pltpu.make_async_copy(ybuf_ref.at[of_slot, t], dst.at[pl.ds(row0, rows), t, :], sem_out.at[of_slot])
                for t in range(N_PHASE)]

    def start_for_group(of_step, copies_a, copies_b):
        @pl.when(of_step < steps_a)
        def _():
            for c in copies_a(of_step * rows):
                c.start()

        @pl.when(of_step >= steps_a)
        def _():
            for c in copies_b((of_step - steps_a) * rows):
                c.start()

    def start_in(of_step, of_slot):
        start_for_group(of_step, lambda r: in_copies((ya_hbm, aa_hbm, gsa_hbm), r, of_slot),
                        lambda r: in_copies((yb_hbm, ab_hbm, gsb_hbm), r, of_slot))

    @pl.when(step == 0)
    def _():
        start_in(step, slot)

    @pl.when(step + 1 < n_steps)
    def _():
        start_in(step + 1, 1 - slot)

    @pl.when(step >= 2)
    def _():
        for c in out_copies(oa_hbm, 0, slot):
            c.wait()

    for c in in_copies((ya_hbm, aa_hbm, gsa_hbm), 0, slot):
        c.wait()
    y_ref, a_ref, gs_ref = (inbuf_ref.at[slot, k] for k in range(3))
    part = rows // MIX_OUT_PARTS
    parts = range(MIX_OUT_PARTS)

    def stack(ref, p):
        return jnp.concatenate([ref[t, p * part:(p + 1) * part] for t in range(N_PHASE)], axis=0)

    glus = [_dot(jax.nn.gelu(stack(y_ref, p).astype(F32)).astype(BF16), wglu_ref[...]) for p in parts]
    mergeds = [(stack(a_ref, p).astype(F32) + stack(gs_ref, p).astype(F32) * (glu[:, :d] * jax.nn.sigmoid(glu[:, d:])))
               .astype(BF16) for p, glu in zip(parts, glus)]
    x2s = [stack(x_ref, p) + _dot(merged, wo_ref[...]) for p, merged in zip(parts, mergeds)]
    for p, y in enumerate(_ffn_math(x2s, g_ref, wg_ref, wu_ref, wd_ref)):
        if final_norm:
            y = _rms(y, gf_ref[...])
        for t in range(N_PHASE):
            ybuf_ref[slot, t, p * part:(p + 1) * part] = y[t * part:(t + 1) * part]
    start_for_group(step, lambda r: out_copies(oa_hbm, r, slot), lambda r: out_copies(ob_hbm, r, slot))

    @pl.when(step == n_steps - 1)
    def _():
        if n_steps > 1:
            for c in out_copies(oa_hbm, 0, 1 - slot):
                c.wait()
        for c in out_copies(oa_hbm, 0, slot):
            c.wait()


def _mix_out_ffn(acts_a, acts_b, xp, w_glu, w_o, g, wg, wu, wd, gf, final_norm):
    rows_a, rows_b = acts_a[0].shape[1], acts_b[0].shape[1]
    d = D_MODEL
    rows = MIX_OUT_ROWS
    assert rows_a % rows == 0 and rows_b % rows == 0 and xp.shape[1] == rows_a + rows_b
    steps_a, n_steps = rows_a // rows, (rows_a + rows_b) // rows
    weights = (w_glu, w_o, g, wg, wu, wd, gf)
    any_spec = pl.BlockSpec(memory_space=pl.ANY)
    out_a, out_b = pl.pallas_call(
        functools.partial(_mix_out_ffn_body, rows=rows, steps_a=steps_a, n_steps=n_steps, final_norm=final_norm),
        grid=(n_steps,),
        in_specs=[any_spec] * 6 + [pl.BlockSpec((N_PHASE, rows, d), lambda i: (0, i, 0))]
        + [_const_spec(w.shape) for w in weights],
        out_specs=[any_spec, any_spec],
        out_shape=[jax.ShapeDtypeStruct((rows_a, N_PHASE, d), F32), jax.ShapeDtypeStruct((rows_b, N_PHASE, d), F32)],
        scratch_shapes=[pltpu.VMEM((2, 3, N_PHASE, rows, d), BF16),
                        pltpu.VMEM((2, N_PHASE, rows, d), F32),
                        pltpu.SemaphoreType.DMA((2,)), pltpu.SemaphoreType.DMA((2,))],
        name="mix_out_ffn",
        compiler_params=_params(("arbitrary",)),
    )(*acts_a, *acts_b, xp, *weights)
    return out_a.reshape(N_PHASE * rows_a, d), out_b.reshape(N_PHASE * rows_b, d)


def kernel(x_prompt, x_sample, state_conv, state_ssm_re, state_ssm_im, norm_ffn1, w_ffn1_gate, w_ffn1_up, w_ffn1_down, norm_mix, w_in, w_conv, w_conv_out, ssm_lambda_re, ssm_lambda_im, ssm_log_step, ssm_b_re, ssm_b_im, ssm_c_re, ssm_c_im, ssm_d, w_glu, w_o, norm_ffn2, w_ffn2_gate, w_ffn2_up, w_ffn2_down, norm_final):
    depth, d = w_in.shape[0], w_in.shape[1]
    row = lambda v: v.reshape(1, -1)
    nf = row(norm_final)
    xt_p = x_prompt.reshape(-1, d)
    xt_s = x_sample.reshape(-1, d)
    n_p, n_s = x_prompt.shape[0], x_sample.shape[0]
    outs_p, outs_s = [], []
    for l in range(depth):
        last = l == depth - 1
        ssm_w, (f1g, f1u, f1d) = _ssm_prep(
            ssm_lambda_re[l], ssm_lambda_im[l], ssm_log_step[l], ssm_b_re[l], ssm_b_im[l], ssm_c_re[l],
            ssm_c_im[l], riders=(w_ffn1_gate[l], w_ffn1_up[l], w_ffn1_down[l]))
        ffn1 = (row(norm_ffn1[l]), f1g, f1u, f1d)
        xp, (win, wco, wgl, wo) = _ffn_to_phase(xt_p, xt_s, *ffn1, riders=(w_in[l], w_conv_out[l], w_glu[l], w_o[l]))
        rows_p, rows_s = xt_p.shape[0] // N_PHASE, xt_s.shape[0] // N_PHASE
        d8 = jnp.tile(ssm_d[l].reshape(N_SLAB, SUBS, 1, SUB), (1, 1, 1, N_PHASE))
        mix_w = (row(norm_mix[l]), win, w_conv[l], wco)
        u_p, a_p, gs_p, conv_p, ffn2_w = _mix_in(xp, 0, rows_p, None, *mix_w, n_p,
                                                 riders=(w_ffn2_gate[l], w_ffn2_up[l], w_ffn2_down[l]))
        u_s, a_s, gs_s, conv_s, _ = _mix_in(xp, rows_p, rows_s, state_conv[l], *mix_w, n_s)
        (y_p, re_p, im_p), (y_s, re_s, im_s) = _ssm(u_p, n_p, u_s, n_s, (state_ssm_re[l], state_ssm_im[l]),
                                                    *ssm_w, d8)
        tail = (wgl, wo, row(norm_ffn2[l]), *ffn2_w, nf, last)
        xt_p, xt_s = _mix_out_ffn((y_p, a_p, gs_p), (y_s, a_s, gs_s), xp, *tail)
        outs_p.append([conv_p, re_p.reshape(n_p, -1, STATE_DIM), im_p.reshape(n_p, -1, STATE_DIM)])
        outs_s.append([conv_s, re_s.reshape(n_s, -1, STATE_DIM), im_s.reshape(n_s, -1, STATE_DIM)])
    stack = lambda outs: tuple(jnp.stack(leaf) for leaf in zip(*outs))
    return (xt_p.reshape(x_prompt.shape), xt_s.reshape(x_sample.shape), *stack(outs_p), *stack(outs_s))
```

```python
import functools

import jax
import jax.numpy as jnp
from jax import lax
from jax.experimental import pallas as pl
from jax.experimental.pallas import tpu as pltpu

F32 = jnp.float32
BF16 = jnp.bfloat16

D_MODEL = 1024
N_PHASE = 8
LANE = 128
MXU_TILE = 256
SLAB = 128
N_SLAB = D_MODEL // SLAB
SUB = 64
SUBS = SLAB // SUB
GROUP_SIZE = 16
STATE_DIM = 64
SUB_STATES = (SUB // GROUP_SIZE) * STATE_DIM
SUB_K = N_PHASE * SUB
SEG = 4
RMS_EPS = 1e-6
VMEM_LIMIT = 60 * 1024 * 1024

FFN_ROWS = 1024
MIX_OUT_PARTS = 2
MIX_IN_ROWS = 128
MIX_OUT_ROWS = 64
SSM_ROWS = 256


def _rms(x, g):
    return x * lax.rsqrt(jnp.mean(x * x, axis=-1, keepdims=True) + RMS_EPS) * g


def _dot(a, b):
    return jnp.dot(a, b, preferred_element_type=F32)


def _const_spec(shape):
    zeros = (0,) * len(shape)
    return pl.BlockSpec(shape, lambda *_: zeros, pipeline_mode=pl.Buffered(1))


def _params(semantics):
    return pltpu.CompilerParams(dimension_semantics=semantics, vmem_limit_bytes=VMEM_LIMIT)


def _stack_phases(ref):
    return jnp.concatenate([ref[t] for t in range(N_PHASE)], axis=0)


def _ffn_math(xs, g_ref, wg_ref, wu_ref, wd_ref):
    hs = [_rms(x, g_ref[...]).astype(BF16) for x in xs]
    gates = [_dot(h, wg_ref[...]) for h in hs]
    ups = [_dot(h, wu_ref[...]) for h in hs]
    acts = [(jax.nn.silu(gate) * up).astype(BF16) for gate, up in zip(gates, ups)]
    return [x + 0.5 * _dot(act, wd_ref[...]) for x, act in zip(xs, acts)]


def _rider_specs(weights, n_steps, step=lambda i: i):
    in_specs, out_specs, out_shapes = [], [], []
    for w in weights:
        n_blocks = n_steps
        while w.shape[0] % (16 * n_blocks):
            assert n_blocks % 2 == 0
            n_blocks //= 2
        spec = pl.BlockSpec((w.shape[0] // n_blocks, w.shape[1]),
                            lambda *idx, rep=n_steps // n_blocks: (step(*idx) // rep, 0))
        in_specs.append(spec)
        out_specs.append(spec)
        out_shapes.append(jax.ShapeDtypeStruct(w.shape, BF16))
    return in_specs, out_specs, out_shapes


def _convert_riders(in_refs, out_refs):
    for w_ref, o_ref in zip(in_refs, out_refs):
        o_ref[...] = w_ref[...].astype(BF16)


def _ffn_to_phase_body(*refs, rows, steps_a, n_steps, n_riders):
    xa_hbm, xb_hbm, g_ref, wg_ref, wu_ref, wd_ref = refs[:6]
    o_ref, xbuf_ref, sem = refs[6 + n_riders], refs[-2], refs[-1]
    step = pl.program_id(0)
    slot = step % 2

    def copies(src, row0, of_slot):
        return [pltpu.make_async_copy(src.at[pl.ds(row0, rows), t, :], xbuf_ref.at[of_slot, t], sem.at[of_slot])
                for t in range(N_PHASE)]

    def start(of_step, of_slot):
        @pl.when(of_step < steps_a)
        def _():
            for c in copies(xa_hbm, of_step * rows, of_slot):
                c.start()

        @pl.when(of_step >= steps_a)
        def _():
            for c in copies(xb_hbm, (of_step - steps_a) * rows, of_slot):
                c.start()

    @pl.when(step == 0)
    def _():
        start(step, slot)

    @pl.when(step + 1 < n_steps)
    def _():
        start(step + 1, 1 - slot)

    for c in copies(xa_hbm, 0, slot):
        c.wait()
    _convert_riders(refs[6:6 + n_riders], refs[7 + n_riders:7 + 2 * n_riders])
    x = jnp.concatenate([xbuf_ref[slot, t] for t in range(N_PHASE)], axis=0)
    (y,) = _ffn_math([x], g_ref, wg_ref, wu_ref, wd_ref)
    for t in range(N_PHASE):
        o_ref[t] = y[t * rows:(t + 1) * rows]


def _ffn_to_phase(x_a, x_b, g, wg, wu, wd, riders=()):
    d = x_a.shape[1]
    rows_a, rows_b = x_a.shape[0] // N_PHASE, x_b.shape[0] // N_PHASE
    n_rows = rows_a + rows_b
    d_ff = wg.shape[1]
    rows = FFN_ROWS // N_PHASE
    assert rows_a % rows == 0 and rows_b % rows == 0
    steps_a, steps_b = rows_a // rows, rows_b // rows
    n_steps = steps_a + steps_b
    rider_in, rider_out, rider_shapes = _rider_specs(riders, steps_a, lambda i: jnp.minimum(i, steps_a - 1))
    out = pl.pallas_call(
        functools.partial(_ffn_to_phase_body, rows=rows, steps_a=steps_a, n_steps=n_steps, n_riders=len(riders)),
        grid=(n_steps,),
        in_specs=[
            pl.BlockSpec(memory_space=pl.ANY),
            pl.BlockSpec(memory_space=pl.ANY),
            _const_spec((1, d)),
            _const_spec((d, d_ff)),
            _const_spec((d, d_ff)),
            _const_spec((d_ff, d)),
        ] + rider_in,
        out_specs=[pl.BlockSpec((N_PHASE, rows, d), lambda i: (0, i, 0))] + rider_out,
        out_shape=[jax.ShapeDtypeStruct((N_PHASE, n_rows, d), F32)] + rider_shapes,
        scratch_shapes=[pltpu.VMEM((2, N_PHASE, rows, d), F32), pltpu.SemaphoreType.DMA((2,))],
        name="ffn_to_phase",
        compiler_params=_params(("arbitrary",)),
    )(x_a.reshape(rows_a, N_PHASE, d), x_b.reshape(rows_b, N_PHASE, d), g, wg, wu, wd, *riders)
    return out[0], tuple(out[1:])


def _mix_in_compute(x_ref, g, win_ref, wcv, wco_ref, shift, rows):
    d = D_MODEL
    h = _rms(_stack_phases(x_ref), g).astype(BF16)

    def proj(k):
        return _dot(h, win_ref[:, k * d:(k + 1) * d])

    z = proj(1) * proj(0)
    b_gate = proj(2)
    u = proj(3)
    g_conv = jax.nn.sigmoid(proj(4))
    gs = jax.nn.sigmoid(proj(5))
    z6 = z[6 * rows:7 * rows]
    z7 = z[7 * rows:8 * rows]
    s6 = shift(z6, 0)
    s7 = shift(z7, 1)
    z1 = jnp.concatenate([s7, z[:7 * rows]], axis=0)
    z2 = jnp.concatenate([s6, s7, z[:6 * rows]], axis=0)
    conv = wcv[2:3] * z + wcv[1:2] * z1 + wcv[0:1] * z2
    a = g_conv * _dot((b_gate * conv).astype(BF16), wco_ref[...])
    return u, a, gs, z6, z7


def _store_phases(ref, val, rows):
    val = val.astype(ref.dtype)
    for t in range(N_PHASE):
        ref[t] = val[t * rows:(t + 1) * rows]


def _store_sub_slab_pairs(ref, val, rows):
    for k in range(N_PHASE // 2):
        a = val[2 * k * rows:(2 * k + 1) * rows]
        b = val[(2 * k + 1) * rows:(2 * k + 2) * rows]
        for q in range(D_MODEL // LANE):
            lanes = slice(q * LANE, (q + 1) * LANE)
            for h, tile in enumerate(_pair_halves(a[:, lanes], b[:, lanes])):
                ref[2 * k + h, :, lanes] = tile.astype(ref.dtype)


def _mix_in_prompt_body(*refs, rows, n_riders):
    x_ref, g_ref, win_ref, wcv_ref, wco_ref = refs[:5]
    u_ref, a_ref, gs_ref, z_ref = refs[5 + n_riders:9 + n_riders]
    carry_ref = refs[-1]
    _convert_riders(refs[5:5 + n_riders], refs[9 + n_riders:9 + 2 * n_riders])

    @pl.when(pl.program_id(1) == 0)
    def _():
        carry_ref[...] = jnp.zeros_like(carry_ref)

    row_id = lax.broadcasted_iota(jnp.int32, (rows, D_MODEL), 0)

    def shift(z, k):
        return jnp.where(row_id == 0, carry_ref[k:k + 1, :], pltpu.roll(z, 1, 0))

    u, a, gs, z6, z7 = _mix_in_compute(x_ref, g_ref[...], win_ref, wcv_ref[...], wco_ref, shift, rows)
    last6 = z6[rows - 1:rows]
    last7 = z7[rows - 1:rows]
    carry_ref[0:1, :] = last6
    carry_ref[1:2, :] = last7
    z_ref[0:1, :] = last6
    z_ref[1:2, :] = last7
    _store_sub_slab_pairs(u_ref, u, rows)
    _store_phases(a_ref, a, rows)
    _store_phases(gs_ref, gs, rows)


def _mix_in_sample_body(x_ref, start_ref, g_ref, win_ref, wcv_ref, wco_ref,
                        u_ref, a_ref, gs_ref, z6_ref, z7_ref, slab_ref, *, rows, rows_per_seq):
    row_id = lax.broadcasted_iota(jnp.int32, (rows, D_MODEL), 0)

    def shift(z, k):
        return jnp.where(row_id % rows_per_seq == 0, start_ref[k], pltpu.roll(z, 1, 0))

    u, a, gs, z6, z7 = _mix_in_compute(x_ref, g_ref[...], win_ref, wcv_ref[...], wco_ref, shift, rows)
    for z, z_ref in ((z6, z6_ref), (z7, z7_ref)):
        for lt in range(D_MODEL // LANE):
            slab_ref[...] = z[:, lt * LANE:(lt + 1) * LANE]
            z_ref[:, lt * LANE:(lt + 1) * LANE] = slab_ref[pl.ds(rows_per_seq - 1, rows // rows_per_seq,
                                                                stride=rows_per_seq), :]
    _store_sub_slab_pairs(u_ref, u, rows)
    _store_phases(a_ref, a, rows)
    _store_phases(gs_ref, gs, rows)


def _mix_in(xp, row0, n_rows, conv_prev, g, w_in, w_conv, w_conv_out, n_seq, riders=()):
    d = D_MODEL
    rows_per_seq = n_rows // n_seq
    ph_shape = jax.ShapeDtypeStruct((N_PHASE, n_rows, d), BF16)
    weights = (g, w_in, w_conv, w_conv_out)
    weight_specs = [_const_spec(w.shape) for w in weights]
    if conv_prev is None:
        rows = min(MIX_IN_ROWS, rows_per_seq)
        assert rows_per_seq % rows == 0
        tiles = rows_per_seq // rows
        assert row0 % rows == 0
        ph_spec = pl.BlockSpec((N_PHASE, rows, d), lambda b, i: (0, b * tiles + i, 0))
        x_spec = pl.BlockSpec((N_PHASE, rows, d), lambda b, i: (0, row0 // rows + b * tiles + i, 0))
        last_spec = pl.BlockSpec((None, 2, d), lambda b, i: (b, 0, 0))
        rider_in, rider_out, rider_shapes = _rider_specs(riders, n_seq * tiles, lambda b, i: b * tiles + i)
        u8, a8, gs8, conv_state, *converted = pl.pallas_call(
            functools.partial(_mix_in_prompt_body, rows=rows, n_riders=len(riders)),
            grid=(n_seq, tiles),
            in_specs=[x_spec] + weight_specs + rider_in,
            out_specs=[ph_spec, ph_spec, ph_spec, last_spec] + rider_out,
            out_shape=[ph_shape, ph_shape, ph_shape, jax.ShapeDtypeStruct((n_seq, 2, d), F32)] + rider_shapes,
            scratch_shapes=[pltpu.VMEM((8, d), F32)],
            name="mix_in_prompt",
            compiler_params=_params(("arbitrary", "arbitrary")),
        )(xp, *weights, *riders)
    else:
        assert not riders
        converted = []
        rows = n_rows
        assert row0 % rows == 0
        start = jnp.repeat(jnp.swapaxes(conv_prev, 0, 1), rows_per_seq, axis=1)
        full = lambda shape: pl.BlockSpec(shape, lambda i: (0,) * len(shape))
        x_spec = pl.BlockSpec((N_PHASE, rows, d), lambda i: (0, row0 // rows, 0))
        u8, a8, gs8, z6, z7 = pl.pallas_call(
            functools.partial(_mix_in_sample_body, rows=rows, rows_per_seq=rows_per_seq),
            grid=(1,),
            in_specs=[x_spec, full((2, rows, d))] + weight_specs,
            out_specs=[full((N_PHASE, rows, d))] * 3 + [full((n_seq, d))] * 2,
            out_shape=[ph_shape, ph_shape, ph_shape,
                       jax.ShapeDtypeStruct((n_seq, d), F32), jax.ShapeDtypeStruct((n_seq, d), F32)],
            scratch_shapes=[pltpu.VMEM((rows, LANE), F32)],
            name="mix_in_sample",
            compiler_params=_params(("arbitrary",)),
        )(xp, start, *weights)
        conv_state = jnp.stack([z6, z7], axis=1)
    return u8, a8, gs8, conv_state, tuple(converted)


def _split_bf16(a):
    hi = a.astype(BF16)
    return hi, (a - hi.astype(F32)).astype(BF16)


def _ssm_prep_body(*refs, n_riders):
    bbre_ref, bbim_ref, ctre_ref, ctim_ref, d_ref, lre_ref, lim_ref, lstep_ref = refs[:8]
    wb_ref, kt_ref, wc_ref, pw_ref = refs[8 + n_riders:12 + n_riders]
    _convert_riders(refs[8:8 + n_riders], refs[12 + n_riders:])
    ns = SUB_STATES
    row_group = lax.broadcasted_iota(jnp.int32, (SUB, LANE), 0) // GROUP_SIZE
    lane_half = lax.broadcasted_iota(jnp.int32, (SUB, LANE), 1) // STATE_DIM
    low_half = lax.broadcasted_iota(jnp.int32, (SUB, LANE), 1) < SUB

    def block_diag(ref):
        tiles = [jnp.where(row_group == 2 * k + lane_half, ref[...], 0.0) for k in range(ns // LANE)]
        return jnp.concatenate(tiles, axis=1)

    bbre, bbim = block_diag(bbre_ref), block_diag(bbim_ref)
    ctre, ctim = block_diag(ctre_ref), block_diag(ctim_ref)
    lre, lim = lre_ref[...], lim_ref[...]
    step = jnp.exp(lstep_ref[...])

    def lam_pow(n):
        mag = jnp.exp((n * lre) * step)
        ang = (n * lim) * step
        return mag * jnp.cos(ang), mag * jnp.sin(ang)

    l1re, l1im = lam_pow(1)
    den = lre * lre + lim * lim
    fre = ((l1re - 1.0) * lre + l1im * lim) / den
    fim = (l1im * lre - (l1re - 1.0) * lim) / den

    def dot_nt(a, b_split):
        nt = lambda p, q: lax.dot_general(p, q, (((1,), (1,)), ((), ())), preferred_element_type=F32)
        a_hi, a_lo = _split_bf16(a)
        b_hi, b_lo = b_split
        return nt(a_hi, b_hi) + nt(a_hi, b_lo) + nt(a_lo, b_hi)

    def c_pair(n):
        (p0re, p0im), (p1re, p1im) = lam_pow(n), lam_pow(n + 1)
        return (jnp.concatenate([ctre * p0re - ctim * p0im, ctre * p1re - ctim * p1im], axis=0),
                jnp.concatenate([ctre * p0im + ctim * p0re, ctre * p1im + ctim * p1re], axis=0))

    c01re, c01im = c_pair(0)
    c01re_split = _split_bf16(c01re)
    c01im_split = _split_bf16(c01im)
    lag_pairs = {}
    for k in range(N_PHASE):
        pre, pim = lam_pow(k)
        gre = fre * pre - fim * pim
        gim = fre * pim + fim * pre
        are = bbre * gre - bbim * gim
        aim = bbre * gim + bbim * gre
        j = N_PHASE - 1 - k
        wb_ref[j * SUB:(j + 1) * SUB, 0:ns] = are.astype(BF16)
        wb_ref[j * SUB:(j + 1) * SUB, ns:2 * ns] = aim.astype(BF16)
        if k < N_PHASE - 1:
            lag_pairs[k] = dot_nt(are, c01re_split) - dot_nt(aim, c01im_split)
    on_diagonal = (lax.broadcasted_iota(jnp.int32, (SUB, LANE), 0) == lax.broadcasted_iota(jnp.int32, (SUB, LANE), 1))
    lag_pairs[0] = lag_pairs[0] + jnp.where(on_diagonal, d_ref[...], 0.0)
    kt_ref[...] = jnp.zeros_like(kt_ref)
    first_odd = jnp.where(low_half, 0.0, pltpu.roll(lag_pairs[0], SUB, 1))
    for j in range(N_PHASE):
        for tile in range(j // 2, N_PHASE // 2):
            lag = 2 * tile - j
            pair = first_odd if lag < 0 else lag_pairs[lag]
            kt_ref[j * SUB:(j + 1) * SUB, tile * LANE:(tile + 1) * LANE] = pair.astype(BF16)
    for tile in range(N_PHASE // 2):
        cre, cim = c_pair(2 * tile + 1)
        wc_ref[0:ns, tile * LANE:(tile + 1) * LANE] = cre.T.astype(BF16)
        wc_ref[ns:2 * ns, tile * LANE:(tile + 1) * LANE] = (-cim).T.astype(BF16)
    pw_ref[...] = jnp.zeros_like(pw_ref)
    for r, n in enumerate((N_PHASE, N_PHASE * SEG)):
        pre, pim = lam_pow(n)
        pw_ref[r:r + 1, 0:ns] = pre
        pw_ref[r:r + 1, ns:2 * ns] = pim


def _ssm_prep(lam_re, lam_im, log_step, b_re, b_im, c_re, c_im, d_skip, riders=()):
    ns = SUB_STATES
    n_sub = N_SLAB * SUBS
    assert 2 * STATE_DIM == LANE and SUB_K == 2 * ns

    def twice(m):
        m = m.reshape(n_sub, SUB, STATE_DIM)
        return jnp.concatenate([m, m], axis=-1)

    def vec(v):
        return v.reshape(n_sub, 1, ns)

    mats = [twice(jnp.transpose(b_re, (0, 2, 1))), twice(jnp.transpose(b_im, (0, 2, 1))), twice(c_re), twice(c_im)]
    d_twice = jnp.tile(d_skip.reshape(n_sub, 1, SUB), (1, 1, 2))
    vecs = [vec(lam_re), vec(lam_im), vec(jnp.broadcast_to(log_step[:, None], lam_re.shape))]
    w_shape = jax.ShapeDtypeStruct((n_sub, SUB_K, 2 * ns), BF16)
    w_spec = pl.BlockSpec((None, SUB_K, 2 * ns), lambda q: (q, 0, 0))
    rider_in, rider_out, rider_shapes = _rider_specs(riders, n_sub)
    out = pl.pallas_call(
        functools.partial(_ssm_prep_body, n_riders=len(riders)),
        grid=(n_sub,),
        in_specs=[pl.BlockSpec((None, SUB, LANE), lambda q: (q, 0, 0))] * 4
        + [pl.BlockSpec((None, 1, LANE), lambda q: (q, 0, 0))]
        + [pl.BlockSpec((None, 1, ns), lambda q: (q, 0, 0))] * 3 + rider_in,
        out_specs=[w_spec, w_spec, w_spec, pl.BlockSpec((None, 8, 2 * ns), lambda q: (q, 0, 0))] + rider_out,
        out_shape=[w_shape, w_shape, w_shape, jax.ShapeDtypeStruct((n_sub, 8, 2 * ns), F32)] + rider_shapes,
        name="ssm_prep",
        compiler_params=_params(("arbitrary",)),
    )(*mats, d_twice, *vecs, *riders)
    return tuple(w.reshape(N_SLAB, SUBS, *w.shape[1:]) for w in out[:4]), tuple(out[4:])


def _cmul(are, aim, bre, bim):
    return are * bre - aim * bim, are * bim + aim * bre


HALF = SUB_STATES // LANE
GROUP_ROWS = 8 * SEG


def _pair_halves(a, b):
    low = lax.broadcasted_iota(jnp.int32, a.shape, 1) < SUB
    return jnp.where(low, a, pltpu.roll(b, SUB, 1)), jnp.where(low, pltpu.roll(a, SUB, 1), b)


def _sub_slab_operands(u_ref):
    return [jnp.concatenate([u_ref[2 * k + h].reshape(-1, LANE) for k in range(N_PHASE // 2)], axis=1)
            for h in range(SUBS)]


def _slab_phase_tiles(subs):
    tiles = []
    for k in range(N_PHASE // 2):
        tiles += _pair_halves(subs[0][:, k * LANE:(k + 1) * LANE], subs[1][:, k * LANE:(k + 1) * LANE])
    return [y.astype(BF16) for y in tiles]


def _mult(pw_ref, r, c):
    return (pw_ref[r:r + 1, c * LANE:(c + 1) * LANE],
            pw_ref[r:r + 1, (HALF + c) * LANE:(HALF + c + 1) * LANE])


def _state_increments(lhs, wb_ref, e_ref):
    e = _dot(lhs, wb_ref[...])
    for lt in range(2 * HALF):
        e_ref[lt, :e.shape[0]] = e[:, lt * LANE:(lt + 1) * LANE]


def _scan_local(e_ref, p_ref, pw_ref, c, base):
    l8re, l8im = _mult(pw_ref, 0, c)
    idx = pl.ds(base, 8, stride=SEG)
    lre = e_ref[c, idx, :]
    lim = e_ref[HALF + c, idx, :]
    for i in range(1, SEG):
        idx = pl.ds(base + i, 8, stride=SEG)
        p_ref[c, idx, :] = lre
        p_ref[HALF + c, idx, :] = lim
        mre, mim = _cmul(l8re, l8im, lre, lim)
        lre = mre + e_ref[c, idx, :]
        lim = mim + e_ref[HALF + c, idx, :]
    return lre, lim


def _scan_fixup(p_ref, pw_ref, c, base, cre, cim):
    l8re, l8im = _mult(pw_ref, 0, c)
    idx = pl.ds(base, 8, stride=SEG)
    p_ref[c, idx, :] = cre
    p_ref[HALF + c, idx, :] = cim
    for i in range(1, SEG):
        idx = pl.ds(base + i, 8, stride=SEG)
        cre, cim = _cmul(l8re, l8im, cre, cim)
        p_ref[c, idx, :] = p_ref[c, idx, :] + cre
        p_ref[HALF + c, idx, :] = p_ref[HALF + c, idx, :] + cim


def _direct_terms(lhs, kt_ref):
    return _dot(lhs, kt_ref[...])


def _state_term(p_ref, n_rows, wc_ref):
    prev = jnp.concatenate([p_ref[lt, :n_rows] for lt in range(2 * HALF)], axis=1)
    return _dot(prev.astype(BF16), wc_ref[...])


def _ssm_prompt_body(u_ref, wb_ref, kt_ref, wc_ref, pw_ref, y_ref, sre_ref, sim_ref,
                     e_ref, p_ref, end_ref, cin_ref, carry_ref, *, n_seq, rows, tile):
    @pl.when((pl.program_id(0) == 0) & (tile == 0))
    def _():
        end_ref[...] = jnp.zeros_like(end_ref)

    @pl.when(tile == 0)
    def _():
        carry_ref[...] = jnp.zeros_like(carry_ref)

    operands = _sub_slab_operands(u_ref)
    groups = rows // GROUP_ROWS
    for h, lhs in enumerate(operands):
        _state_increments(lhs, wb_ref.at[h], e_ref.at[h])
    outs = [_direct_terms(lhs, kt_ref.at[h]) for h, lhs in enumerate(operands)]
    for h in range(SUBS):
        e, p, ends, cins, carry, pw = e_ref.at[h], p_ref.at[h], end_ref.at[h], cin_ref.at[h], carry_ref.at[h], pw_ref.at[h]
        lane0 = h * SUB_STATES
        for c in range(HALF):
            for b in range(n_seq):
                for j in range(groups):
                    lre, lim = _scan_local(e, p, pw, c, b * rows + j * GROUP_ROWS)
                    idx = pl.ds(j * 64 + b, 8, stride=8)
                    ends[c, idx, :] = lre
                    ends[HALF + c, idx, :] = lim
        for c in range(HALF):
            lsre, lsim = _mult(pw, 1, c)
            cre = carry[c]
            cim = carry[HALF + c]
            for s in range(rows // SEG):
                cins[c, 8 * s:8 * s + 8, :] = cre
                cins[HALF + c, 8 * s:8 * s + 8, :] = cim
                mre, mim = _cmul(lsre, lsim, cre, cim)
                cre = mre + ends[c, 8 * s:8 * s + 8, :]
                cim = mim + ends[HALF + c, 8 * s:8 * s + 8, :]
            carry[c] = cre
            carry[HALF + c] = cim
            sre_ref[:, lane0 + c * LANE:lane0 + (c + 1) * LANE] = cre[:n_seq]
            sim_ref[:, lane0 + c * LANE:lane0 + (c + 1) * LANE] = cim[:n_seq]
        for c in range(HALF):
            for b in range(n_seq):
                for j in range(groups):
                    idx = pl.ds(j * 64 + b, 8, stride=8)
                    _scan_fixup(p, pw, c, b * rows + j * GROUP_ROWS, cins[c, idx, :], cins[HALF + c, idx, :])
        outs[h] = outs[h] + _state_term(p, n_seq * rows, wc_ref.at[h])
    for t, y in enumerate(_slab_phase_tiles(outs)):
        y_ref[t] = y.reshape(n_seq, rows, LANE)


def _ssm_sample_body(u_ref, wb_ref, kt_ref, wc_ref, pw_ref, h0re_ref, h0im_ref,
                     y_ref, sre_ref, sim_ref, e_ref, p_ref, *, rows):
    operands = _sub_slab_operands(u_ref)
    for h, lhs in enumerate(operands):
        _state_increments(lhs, wb_ref.at[h], e_ref.at[h])
    outs = [_direct_terms(lhs, kt_ref.at[h]) for h, lhs in enumerate(operands)]
    for h in range(SUBS):
        e, p, pw = e_ref.at[h], p_ref.at[h], pw_ref.at[h]
        for c in range(HALF):
            lanes = slice(h * SUB_STATES + c * LANE, h * SUB_STATES + (c + 1) * LANE)
            lsre, lsim = _mult(pw, 1, c)
            for j in range(rows // GROUP_ROWS):
                lre, lim = _scan_local(e, p, pw, c, j * GROUP_ROWS)
                cre = h0re_ref[8 * j:8 * j + 8, lanes]
                cim = h0im_ref[8 * j:8 * j + 8, lanes]
                mre, mim = _cmul(lsre, lsim, cre, cim)
                sre_ref[8 * j:8 * j + 8, lanes] = mre + lre
                sim_ref[8 * j:8 * j + 8, lanes] = mim + lim
                _scan_fixup(p, pw, c, j * GROUP_ROWS, cre, cim)
        outs[h] = outs[h] + _state_term(p, rows, wc_ref.at[h])
    for t, y in enumerate(_slab_phase_tiles(outs)):
        y_ref[t] = y


def _ssm_body(u_ref, wb_ref, kt_ref, wc_ref, pw_ref, us_ref, h0re_ref, h0im_ref,
              y_ref, sre_ref, sim_ref, ys_ref, sres_ref, sims_ref,
              e_ref, p_ref, end_ref, cin_ref, carry_ref, *, n_seq, rows, rows_s):
    step = pl.program_id(1)

    @pl.when(step == 0)
    def _():
        _ssm_sample_body(us_ref, wb_ref, kt_ref, wc_ref, pw_ref, h0re_ref, h0im_ref,
                         ys_ref, sres_ref, sims_ref, e_ref, p_ref, rows=rows_s)

    @pl.when(step > 0)
    def _():
        _ssm_prompt_body(u_ref, wb_ref, kt_ref, wc_ref, pw_ref, y_ref, sre_ref, sim_ref,
                         e_ref, p_ref, end_ref, cin_ref, carry_ref, n_seq=n_seq, rows=rows, tile=step - 1)


def _ssm(u8_p, n_p, u8_s, n_s, h0, wb, kt, wc, pw):
    d = D_MODEL
    ns = SUBS * SUB_STATES
    rows_p, rows_s = u8_p.shape[1], u8_s.shape[1]
    per_seq = rows_p // n_p
    assert rows_s // n_s == SEG, "carried-state path scans one segment per sequence"
    assert n_p <= 8, "sequences ride the sublanes of the segment chain"
    rows = min(SSM_ROWS, per_seq)
    assert per_seq % rows == 0 and rows % GROUP_ROWS == 0 and rows_s % GROUP_ROWS == 0
    assert rows_s <= n_p * rows, "the carried-state step reuses the scan scratch"
    tiles = per_seq // rows
    slab = lambda q, i: (q, 0, 0, 0)
    p_spec = pl.BlockSpec((N_PHASE, n_p, rows, LANE), lambda q, i: (0, 0, jnp.maximum(i - 1, 0), q))
    s_spec = pl.BlockSpec((N_PHASE, rows_s, LANE), lambda q, i: (0, 0, q))
    w_spec = pl.BlockSpec((None, SUBS, SUB_K, 2 * SUB_STATES), slab)
    state_p = pl.BlockSpec((n_p, ns), lambda q, i: (0, q))
    state_s = pl.BlockSpec((n_s, ns), lambda q, i: (0, q))
    seg_rows = 8 * (rows // SEG)
    dims_p = (N_PHASE, n_p, per_seq, d)
    y_p, re_p, im_p, y_s, re_s, im_s = pl.pallas_call(
        functools.partial(_ssm_body, n_seq=n_p, rows=rows, rows_s=rows_s),
        grid=(N_SLAB, tiles + 1),
        in_specs=[p_spec, w_spec, w_spec, w_spec, pl.BlockSpec((None, SUBS, 8, 2 * SUB_STATES), slab),
                  s_spec, state_s, state_s],
        out_specs=[p_spec, state_p, state_p, s_spec, state_s, state_s],
        out_shape=[jax.ShapeDtypeStruct(dims_p, BF16),
                   jax.ShapeDtypeStruct((n_p, N_SLAB * ns), F32), jax.ShapeDtypeStruct((n_p, N_SLAB * ns), F32),
                   jax.ShapeDtypeStruct((N_PHASE, rows_s, d), BF16),
                   jax.ShapeDtypeStruct((n_s, N_SLAB * ns), F32), jax.ShapeDtypeStruct((n_s, N_SLAB * ns), F32)],
        scratch_shapes=[pltpu.VMEM((SUBS, 2 * HALF, n_p * rows, LANE), F32),
                        pltpu.VMEM((SUBS, 2 * HALF, n_p * rows, LANE), F32),
                        pltpu.VMEM((SUBS, 2 * HALF, seg_rows, LANE), F32),
                        pltpu.VMEM((SUBS, 2 * HALF, seg_rows, LANE), F32),
                        pltpu.VMEM((SUBS, 2 * HALF, 8, LANE), F32)],
        name="ssm",
        compiler_params=_params(("arbitrary", "arbitrary")),
    )(u8_p.reshape(dims_p), wb, kt, wc, pw, u8_s, h0[0].reshape(n_s, -1), h0[1].reshape(n_s, -1))
    return (y_p.reshape(N_PHASE, rows_p, d), re_p, im_p), (y_s, re_s, im_s)


def _mix_out_ffn_body(ya_hbm, aa_hbm, gsa_hbm, yb_hbm, ab_hbm, gsb_hbm, x_ref,
                      wglu_ref, wo_ref, g_ref, wg_ref, wu_ref, wd_ref, gf_ref,
                      oa_hbm, ob_hbm, inbuf_ref, ybuf_ref, sem_in, sem_out, *, rows, steps_a, n_steps, final_norm):
    d = D_MODEL
    step = pl.program_id(0)
    slot = step % 2

    def in_copies(srcs, row0, of_slot):
        return [pltpu.make_async_copy(src.at[:, pl.ds(row0, rows), :], inbuf_ref.at[of_slot, k], sem_in.at[of_slot])
                for k, src in enumerate(srcs)]

    def out_copies(dst, row0, of_slot):
        return [pltpu.make_async_copy(ybuf_ref.at[of_slot, t], dst.at[pl.ds(row0, rows), t, :], sem_out.at[of_slot])
                for t in range(N_PHASE)]

    def start_for_group(of_step, copies_a, copies_b):
        @pl.when(of_step < steps_a)
        def _():
            for c in copies_a(of_step * rows):
                c.start()

        @pl.when(of_step >= steps_a)
        def _():
            for c in copies_b((of_step - steps_a) * rows):
                c.start()

    def start_in(of_step, of_slot):
        start_for_group(of_step, lambda r: in_copies((ya_hbm, aa_hbm, gsa_hbm), r, of_slot),
                        lambda r: in_copies((yb_hbm, ab_hbm, gsb_hbm), r, of_slot))

    @pl.when(step == 0)
    def _():
        start_in(step, slot)

    @pl.when(step + 1 < n_steps)
    def _():
        start_in(step + 1, 1 - slot)

    @pl.when(step >= 2)
    def _():
        for c in out_copies(oa_hbm, 0, slot):
            c.wait()

    for c in in_copies((ya_hbm, aa_hbm, gsa_hbm), 0, slot):
        c.wait()
    y_ref, a_ref, gs_ref = (inbuf_ref.at[slot, k] for k in range(3))
    part = rows // MIX_OUT_PARTS
    parts = range(MIX_OUT_PARTS)

    def stack(ref, p):
        return jnp.concatenate([ref[t, p * part:(p + 1) * part] for t in range(N_PHASE)], axis=0)

    glus = [_dot(jax.nn.gelu(stack(y_ref, p).astype(F32)).astype(BF16), wglu_ref[...]) for p in parts]
    mergeds = [(stack(a_ref, p).astype(F32) + stack(gs_ref, p).astype(F32) * (glu[:, :d] * jax.nn.sigmoid(glu[:, d:])))
               .astype(BF16) for p, glu in zip(parts, glus)]
    x2s = [stack(x_ref, p) + _dot(merged, wo_ref[...]) for p, merged in zip(parts, mergeds)]
    for p, y in enumerate(_ffn_math(x2s, g_ref, wg_ref, wu_ref, wd_ref)):
        if final_norm:
            y = _rms(y, gf_ref[...])
        for t in range(N_PHASE):
            ybuf_ref[slot, t, p * part:(p + 1) * part] = y[t * part:(t + 1) * part]
    start_for_group(step, lambda r: out_copies(oa_hbm, r, slot), lambda r: out_copies(ob_hbm, r, slot))

    @pl.when(step == n_steps - 1)
    def _():
        if n_steps > 1:
            for c in out_copies(oa_hbm, 0, 1 - slot):
                c.wait()
        for c in out_copies(oa_hbm, 0, slot):
            c.wait()


def _mix_out_ffn(acts_a, acts_b, xp, w_glu, w_o, g, wg, wu, wd, gf, final_norm):
    rows_a, rows_b = acts_a[0].shape[1], acts_b[0].shape[1]
    d = D_MODEL
    rows = MIX_OUT_ROWS
    assert rows_a % rows == 0 and rows_b % rows == 0 and xp.shape[1] == rows_a + rows_b
    steps_a, n_steps = rows_a // rows, (rows_a + rows_b) // rows
    weights = (w_glu, w_o, g, wg, wu, wd, gf)
    any_spec = pl.BlockSpec(memory_space=pl.ANY)
    out_a, out_b = pl.pallas_call(
        functools.partial(_mix_out_ffn_body, rows=rows, steps_a=steps_a, n_steps=n_steps, final_norm=final_norm),
        grid=(n_steps,),
        in_specs=[any_spec] * 6 + [pl.BlockSpec((N_PHASE, rows, d), lambda i: (0, i, 0))]
        + [_const_spec(w.shape) for w in weights],
        out_specs=[any_spec, any_spec],
        out_shape=[jax.ShapeDtypeStruct((rows_a, N_PHASE, d), F32), jax.ShapeDtypeStruct((rows_b, N_PHASE, d), F32)],
        scratch_shapes=[pltpu.VMEM((2, 3, N_PHASE, rows, d), BF16),
                        pltpu.VMEM((2, N_PHASE, rows, d), F32),
                        pltpu.SemaphoreType.DMA((2,)), pltpu.SemaphoreType.DMA((2,))],
        name="mix_out_ffn",
        compiler_params=_params(("arbitrary",)),
    )(*acts_a, *acts_b, xp, *weights)
    return out_a.reshape(N_PHASE * rows_a, d), out_b.reshape(N_PHASE * rows_b, d)


def kernel(x_prompt, x_sample, state_conv, state_ssm_re, state_ssm_im, norm_ffn1, w_ffn1_gate, w_ffn1_up, w_ffn1_down, norm_mix, w_in, w_conv, w_conv_out, ssm_lambda_re, ssm_lambda_im, ssm_log_step, ssm_b_re, ssm_b_im, ssm_c_re, ssm_c_im, ssm_d, w_glu, w_o, norm_ffn2, w_ffn2_gate, w_ffn2_up, w_ffn2_down, norm_final):
    depth, d = w_in.shape[0], w_in.shape[1]
    row = lambda v: v.reshape(1, -1)
    nf = row(norm_final)
    xt_p = x_prompt.reshape(-1, d)
    xt_s = x_sample.reshape(-1, d)
    n_p, n_s = x_prompt.shape[0], x_sample.shape[0]
    outs_p, outs_s = [], []
    for l in range(depth):
        last = l == depth - 1
        ssm_w, (f1g, f1u, f1d) = _ssm_prep(
            ssm_lambda_re[l], ssm_lambda_im[l], ssm_log_step[l], ssm_b_re[l], ssm_b_im[l], ssm_c_re[l],
            ssm_c_im[l], ssm_d[l], riders=(w_ffn1_gate[l], w_ffn1_up[l], w_ffn1_down[l]))
        ffn1 = (row(norm_ffn1[l]), f1g, f1u, f1d)
        xp, (win, wco, wgl, wo) = _ffn_to_phase(xt_p, xt_s, *ffn1, riders=(w_in[l], w_conv_out[l], w_glu[l], w_o[l]))
        rows_p, rows_s = xt_p.shape[0] // N_PHASE, xt_s.shape[0] // N_PHASE
        mix_w = (row(norm_mix[l]), win, w_conv[l], wco)
        u_p, a_p, gs_p, conv_p, ffn2_w = _mix_in(xp, 0, rows_p, None, *mix_w, n_p,
                                                 riders=(w_ffn2_gate[l], w_ffn2_up[l], w_ffn2_down[l]))
        u_s, a_s, gs_s, conv_s, _ = _mix_in(xp, rows_p, rows_s, state_conv[l], *mix_w, n_s)
        (y_p, re_p, im_p), (y_s, re_s, im_s) = _ssm(u_p, n_p, u_s, n_s, (state_ssm_re[l], state_ssm_im[l]),
                                                    *ssm_w)
        tail = (wgl, wo, row(norm_ffn2[l]), *ffn2_w, nf, last)
        xt_p, xt_s = _mix_out_ffn((y_p, a_p, gs_p), (y_s, a_s, gs_s), xp, *tail)
        outs_p.append([conv_p, re_p.reshape(n_p, -1, STATE_DIM), im_p.reshape(n_p, -1, STATE_DIM)])
        outs_s.append([conv_s, re_s.reshape(n_s, -1, STATE_DIM), im_s.reshape(n_s, -1, STATE_DIM)])
    stack = lambda outs: tuple(jnp.stack(leaf) for leaf in zip(*outs))
    return (xt_p.reshape(x_prompt.shape), xt_s.reshape(x_sample.shape), *stack(outs_p), *stack(outs_s))
```

```python
import functools

import jax
import jax.numpy as jnp
from jax import lax
from jax.experimental import pallas as pl
from jax.experimental.pallas import tpu as pltpu

F32 = jnp.float32
BF16 = jnp.bfloat16

D_MODEL = 1024
N_PHASE = 8
LANE = 128
MXU_TILE = 256
SLAB = 128
N_SLAB = D_MODEL // SLAB
SUB = 64
SUBS = SLAB // SUB
GROUP_SIZE = 16
STATE_DIM = 64
SUB_STATES = (SUB // GROUP_SIZE) * STATE_DIM
SUB_K = N_PHASE * SUB
SEG = 4
RMS_EPS = 1e-6
VMEM_LIMIT = 60 * 1024 * 1024

FFN_ROWS = 1024
MIX_OUT_PARTS = 2
MIX_IN_ROWS = 128
MIX_OUT_ROWS = 64
SSM_ROWS = 256


def _rms(x, g):
    return x * lax.rsqrt(jnp.mean(x * x, axis=-1, keepdims=True) + RMS_EPS) * g


def _dot(a, b):
    return jnp.dot(a, b, preferred_element_type=F32)


def _const_spec(shape):
    zeros = (0,) * len(shape)
    return pl.BlockSpec(shape, lambda *_: zeros, pipeline_mode=pl.Buffered(1))


def _params(semantics):
    return pltpu.CompilerParams(dimension_semantics=semantics, vmem_limit_bytes=VMEM_LIMIT)


def _stack_phases(ref):
    return jnp.concatenate([ref[t] for t in range(N_PHASE)], axis=0)


def _ffn_math(xs, g_ref, wg_ref, wu_ref, wd_ref):
    hs = [_rms(x, g_ref[...]).astype(BF16) for x in xs]
    acts = [(jax.nn.silu(_dot(h, wg_ref[...])) * _dot(h, wu_ref[...])).astype(BF16) for h in hs]
    return [x + 0.5 * _dot(act, wd_ref[...]) for x, act in zip(xs, acts)]


def _rider_specs(weights, n_steps, step=lambda i: i):
    in_specs, out_specs, out_shapes = [], [], []
    for w in weights:
        n_blocks = n_steps
        while w.shape[0] % (16 * n_blocks):
            assert n_blocks % 2 == 0
            n_blocks //= 2
        spec = pl.BlockSpec((w.shape[0] // n_blocks, w.shape[1]),
                            lambda *idx, rep=n_steps // n_blocks: (step(*idx) // rep, 0))
        in_specs.append(spec)
        out_specs.append(spec)
        out_shapes.append(jax.ShapeDtypeStruct(w.shape, BF16))
    return in_specs, out_specs, out_shapes


def _convert_riders(in_refs, out_refs):
    for w_ref, o_ref in zip(in_refs, out_refs):
        o_ref[...] = w_ref[...].astype(BF16)


def _ffn_to_phase_body(*refs, rows, steps_a, n_steps, n_riders):
    xa_hbm, xb_hbm, g_ref, wg_ref, wu_ref, wd_ref = refs[:6]
    o_ref, xbuf_ref, sem = refs[6 + n_riders], refs[-2], refs[-1]
    step = pl.program_id(0)
    slot = step % 2

    def copies(src, row0, of_slot):
        return [pltpu.make_async_copy(src.at[pl.ds(row0, rows), t, :], xbuf_ref.at[of_slot, t], sem.at[of_slot])
                for t in range(N_PHASE)]

    def start(of_step, of_slot):
        @pl.when(of_step < steps_a)
        def _():
            for c in copies(xa_hbm, of_step * rows, of_slot):
                c.start()

        @pl.when(of_step >= steps_a)
        def _():
            for c in copies(xb_hbm, (of_step - steps_a) * rows, of_slot):
                c.start()

    @pl.when(step == 0)
    def _():
        start(step, slot)

    @pl.when(step + 1 < n_steps)
    def _():
        start(step + 1, 1 - slot)

    for c in copies(xa_hbm, 0, slot):
        c.wait()
    _convert_riders(refs[6:6 + n_riders], refs[7 + n_riders:7 + 2 * n_riders])
    x = jnp.concatenate([xbuf_ref[slot, t] for t in range(N_PHASE)], axis=0)
    (y,) = _ffn_math([x], g_ref, wg_ref, wu_ref, wd_ref)
    for t in range(N_PHASE):
        o_ref[t] = y[t * rows:(t + 1) * rows]


def _ffn_to_phase(x_a, x_b, g, wg, wu, wd, riders=()):
    d = x_a.shape[1]
    rows_a, rows_b = x_a.shape[0] // N_PHASE, x_b.shape[0] // N_PHASE
    n_rows = rows_a + rows_b
    d_ff = wg.shape[1]
    rows = FFN_ROWS // N_PHASE
    assert rows_a % rows == 0 and rows_b % rows == 0
    steps_a, steps_b = rows_a // rows, rows_b // rows
    n_steps = steps_a + steps_b
    rider_in, rider_out, rider_shapes = _rider_specs(riders, steps_a, lambda i: jnp.minimum(i, steps_a - 1))
    out = pl.pallas_call(
        functools.partial(_ffn_to_phase_body, rows=rows, steps_a=steps_a, n_steps=n_steps, n_riders=len(riders)),
        grid=(n_steps,),
        in_specs=[
            pl.BlockSpec(memory_space=pl.ANY),
            pl.BlockSpec(memory_space=pl.ANY),
            _const_spec((1, d)),
            _const_spec((d, d_ff)),
            _const_spec((d, d_ff)),
            _const_spec((d_ff, d)),
        ] + rider_in,
        out_specs=[pl.BlockSpec((N_PHASE, rows, d), lambda i: (0, i, 0))] + rider_out,
        out_shape=[jax.ShapeDtypeStruct((N_PHASE, n_rows, d), F32)] + rider_shapes,
        scratch_shapes=[pltpu.VMEM((2, N_PHASE, rows, d), F32), pltpu.SemaphoreType.DMA((2,))],
        name="ffn_to_phase",
        compiler_params=_params(("arbitrary",)),
    )(x_a.reshape(rows_a, N_PHASE, d), x_b.reshape(rows_b, N_PHASE, d), g, wg, wu, wd, *riders)
    return out[0], tuple(out[1:])


def _mix_in_compute(x_ref, g, win_ref, wcv, wco_ref, shift, rows):
    d = D_MODEL
    h = _rms(_stack_phases(x_ref), g).astype(BF16)

    def proj(k):
        return _dot(h, win_ref[:, k * d:(k + 1) * d])

    z = proj(1) * proj(0)
    b_gate = proj(2)
    u = proj(3)
    g_conv = jax.nn.sigmoid(proj(4))
    gs = jax.nn.sigmoid(proj(5))
    z6 = z[6 * rows:7 * rows]
    z7 = z[7 * rows:8 * rows]
    s6 = shift(z6, 0)
    s7 = shift(z7, 1)
    z1 = jnp.concatenate([s7, z[:7 * rows]], axis=0)
    z2 = jnp.concatenate([s6, s7, z[:6 * rows]], axis=0)
    conv = wcv[2:3] * z + wcv[1:2] * z1 + wcv[0:1] * z2
    a = g_conv * _dot((b_gate * conv).astype(BF16), wco_ref[...])
    return u, a, gs, z6, z7


def _store_phases(ref, val, rows):
    val = val.astype(ref.dtype)
    for t in range(N_PHASE):
        ref[t] = val[t * rows:(t + 1) * rows]


def _store_sub_slab_pairs(ref, val, rows):
    for k in range(N_PHASE // 2):
        a = val[2 * k * rows:(2 * k + 1) * rows]
        b = val[(2 * k + 1) * rows:(2 * k + 2) * rows]
        for q in range(D_MODEL // LANE):
            lanes = slice(q * LANE, (q + 1) * LANE)
            for h, tile in enumerate(_pair_halves(a[:, lanes], b[:, lanes])):
                ref[2 * k + h, :, lanes] = tile.astype(ref.dtype)


def _mix_in_prompt_body(x_ref, g_ref, win_ref, wcv_ref, wco_ref, u_ref, a_ref, gs_ref, z_ref, carry_ref, *,
                        rows, first_tile):
    @pl.when(first_tile)
    def _():
        carry_ref[...] = jnp.zeros_like(carry_ref)

    row_id = lax.broadcasted_iota(jnp.int32, (rows, D_MODEL), 0)

    def shift(z, k):
        return jnp.where(row_id == 0, carry_ref[k:k + 1, :], pltpu.roll(z, 1, 0))

    u, a, gs, z6, z7 = _mix_in_compute(x_ref, g_ref[...], win_ref, wcv_ref[...], wco_ref, shift, rows)
    last6 = z6[rows - 1:rows]
    last7 = z7[rows - 1:rows]
    carry_ref[0:1, :] = last6
    carry_ref[1:2, :] = last7
    z_ref[0:1, :] = last6
    z_ref[1:2, :] = last7
    _store_sub_slab_pairs(u_ref, u, rows)
    _store_phases(a_ref, a, rows)
    _store_phases(gs_ref, gs, rows)


def _mix_in_sample_body(x_ref, start_ref, g_ref, win_ref, wcv_ref, wco_ref,
                        u_ref, a_ref, gs_ref, z6_ref, z7_ref, slab_ref, *, rows, rows_per_seq):
    row_id = lax.broadcasted_iota(jnp.int32, (rows, D_MODEL), 0)

    def shift(z, k):
        return jnp.where(row_id % rows_per_seq == 0, start_ref[k], pltpu.roll(z, 1, 0))

    u, a, gs, z6, z7 = _mix_in_compute(x_ref, g_ref[...], win_ref, wcv_ref[...], wco_ref, shift, rows)
    for z, z_ref in ((z6, z6_ref), (z7, z7_ref)):
        for lt in range(D_MODEL // LANE):
            slab_ref[...] = z[:, lt * LANE:(lt + 1) * LANE]
            z_ref[:, lt * LANE:(lt + 1) * LANE] = slab_ref[pl.ds(rows_per_seq - 1, rows // rows_per_seq,
                                                                stride=rows_per_seq), :]
    _store_sub_slab_pairs(u_ref, u, rows)
    _store_phases(a_ref, a, rows)
    _store_phases(gs_ref, gs, rows)


def _mix_in_body(*refs, rows, steps_a, tiles, rows_per_seq_b, n_riders):
    x_ref, start_ref, g_ref, win_ref, wcv_ref, wco_ref = refs[:6]
    outs = refs[6 + n_riders:15 + n_riders]
    ua_ref, aa_ref, gsa_ref, za_ref, ub_ref, ab_ref, gsb_ref, z6b_ref, z7b_ref = outs
    carry_ref, slab_ref = refs[-2:]
    _convert_riders(refs[6:6 + n_riders], refs[15 + n_riders:15 + 2 * n_riders])
    step = pl.program_id(0)

    @pl.when(step < steps_a)
    def _():
        _mix_in_prompt_body(x_ref, g_ref, win_ref, wcv_ref, wco_ref, ua_ref, aa_ref, gsa_ref, za_ref, carry_ref,
                            rows=rows, first_tile=step % tiles == 0)

    @pl.when(step == steps_a)
    def _():
        _mix_in_sample_body(x_ref, start_ref, g_ref, win_ref, wcv_ref, wco_ref, ub_ref, ab_ref, gsb_ref,
                            z6b_ref, z7b_ref, slab_ref, rows=rows, rows_per_seq=rows_per_seq_b)


def _mix_in(xp, rows_a, n_a, rows_b, n_b, conv_prev_b, g, w_in, w_conv, w_conv_out, riders=()):
    d = D_MODEL
    rows = MIX_IN_ROWS
    per_seq_a, per_seq_b = rows_a // n_a, rows_b // n_b
    assert per_seq_a % rows == 0 and rows_b == rows, "group b is taken in one step"
    tiles, steps_a = per_seq_a // rows, rows_a // rows
    last_a = steps_a - 1
    weights = (g, w_in, w_conv, w_conv_out)
    start = jnp.repeat(jnp.swapaxes(conv_prev_b, 0, 1), per_seq_b, axis=1)
    a_spec = pl.BlockSpec((N_PHASE, rows, d), lambda i: (0, jnp.minimum(i, last_a), 0))
    b_spec = _const_spec((N_PHASE, rows, d))
    rider_in, rider_out, rider_shapes = _rider_specs(riders, steps_a, lambda i: jnp.minimum(i, last_a))
    shape = lambda n_rows: jax.ShapeDtypeStruct((N_PHASE, n_rows, d), BF16)
    u_a, a_a, gs_a, conv_a, u_b, a_b, gs_b, z6_b, z7_b, *converted = pl.pallas_call(
        functools.partial(_mix_in_body, rows=rows, steps_a=steps_a, tiles=tiles, rows_per_seq_b=per_seq_b,
                          n_riders=len(riders)),
        grid=(steps_a + 1,),
        in_specs=[pl.BlockSpec((N_PHASE, rows, d), lambda i: (0, i, 0)), _const_spec((2, rows, d))]
        + [_const_spec(w.shape) for w in weights] + rider_in,
        out_specs=[a_spec, a_spec, a_spec,
                   pl.BlockSpec((None, 2, d), lambda i: (jnp.minimum(i // tiles, n_a - 1), 0, 0)),
                   b_spec, b_spec, b_spec, _const_spec((n_b, d)), _const_spec((n_b, d))] + rider_out,
        out_shape=[shape(rows_a), shape(rows_a), shape(rows_a), jax.ShapeDtypeStruct((n_a, 2, d), F32),
                   shape(rows_b), shape(rows_b), shape(rows_b),
                   jax.ShapeDtypeStruct((n_b, d), F32), jax.ShapeDtypeStruct((n_b, d), F32)] + rider_shapes,
        scratch_shapes=[pltpu.VMEM((8, d), F32), pltpu.VMEM((rows, LANE), F32)],
        name="mix_in",
        compiler_params=_params(("arbitrary",)),
    )(xp, start, *weights, *riders)
    return (u_a, a_a, gs_a, conv_a), (u_b, a_b, gs_b, jnp.stack([z6_b, z7_b], axis=1)), tuple(converted)


def _split_bf16(a):
    hi = a.astype(BF16)
    return hi, (a - hi.astype(F32)).astype(BF16)


def _ssm_prep_body(*refs, n_riders):
    mats_ref, vecs_ref = refs[:2]
    wb_ref, kt_ref, wc_ref, pw_ref = refs[2 + n_riders:6 + n_riders]
    _convert_riders(refs[2:2 + n_riders], refs[6 + n_riders:])
    bbre_ref, bbim_ref, ctre_ref, ctim_ref = (mats_ref.at[i] for i in range(4))
    lre_ref, lim_ref, lstep_ref, d_ref = (vecs_ref.at[i] for i in range(4))
    ns = SUB_STATES
    row_group = lax.broadcasted_iota(jnp.int32, (SUB, LANE), 0) // GROUP_SIZE
    lane_half = lax.broadcasted_iota(jnp.int32, (SUB, LANE), 1) // STATE_DIM
    low_half = lax.broadcasted_iota(jnp.int32, (SUB, LANE), 1) < SUB

    def block_diag(ref):
        tiles = [jnp.where(row_group == 2 * k + lane_half, ref[...], 0.0) for k in range(ns // LANE)]
        return jnp.concatenate(tiles, axis=1)

    bbre, bbim = block_diag(bbre_ref), block_diag(bbim_ref)
    ctre, ctim = block_diag(ctre_ref), block_diag(ctim_ref)
    lre, lim = lre_ref[...], lim_ref[...]
    step = jnp.exp(lstep_ref[...])

    def lam_pow(n):
        mag = jnp.exp((n * lre) * step)
        ang = (n * lim) * step
        return mag * jnp.cos(ang), mag * jnp.sin(ang)

    l1re, l1im = lam_pow(1)
    den = lre * lre + lim * lim
    fre = ((l1re - 1.0) * lre + l1im * lim) / den
    fim = (l1im * lre - (l1re - 1.0) * lim) / den

    def dot_nt(a, b_split):
        nt = lambda p, q: lax.dot_general(p, q, (((1,), (1,)), ((), ())), preferred_element_type=F32)
        a_hi, a_lo = _split_bf16(a)
        b_hi, b_lo = b_split
        return nt(a_hi, b_hi) + nt(a_hi, b_lo) + nt(a_lo, b_hi)

    def c_pair(n):
        (p0re, p0im), (p1re, p1im) = lam_pow(n), lam_pow(n + 1)
        return (jnp.concatenate([ctre * p0re - ctim * p0im, ctre * p1re - ctim * p1im], axis=0),
                jnp.concatenate([ctre * p0im + ctim * p0re, ctre * p1im + ctim * p1re], axis=0))

    c01re, c01im = c_pair(0)
    c01re_split = _split_bf16(c01re)
    c01im_split = _split_bf16(c01im)
    lag_pairs = {}
    for k in range(N_PHASE):
        pre, pim = lam_pow(k)
        gre = fre * pre - fim * pim
        gim = fre * pim + fim * pre
        are = bbre * gre - bbim * gim
        aim = bbre * gim + bbim * gre
        j = N_PHASE - 1 - k
        wb_ref[j * SUB:(j + 1) * SUB, 0:ns] = are.astype(BF16)
        wb_ref[j * SUB:(j + 1) * SUB, ns:2 * ns] = aim.astype(BF16)
        if k < N_PHASE - 1:
            lag_pairs[k] = dot_nt(are, c01re_split) - dot_nt(aim, c01im_split)
    on_diagonal = (lax.broadcasted_iota(jnp.int32, (SUB, LANE), 0) == lax.broadcasted_iota(jnp.int32, (SUB, LANE), 1))
    lag_pairs[0] = lag_pairs[0] + jnp.where(on_diagonal, d_ref[:, :LANE], 0.0)
    kt_ref[...] = jnp.zeros_like(kt_ref)
    first_odd = jnp.where(low_half, 0.0, pltpu.roll(lag_pairs[0], SUB, 1))
    for j in range(N_PHASE):
        for tile in range(j // 2, N_PHASE // 2):
            lag = 2 * tile - j
            pair = first_odd if lag < 0 else lag_pairs[lag]
            kt_ref[j * SUB:(j + 1) * SUB, tile * LANE:(tile + 1) * LANE] = pair.astype(BF16)
    for tile in range(N_PHASE // 2):
        cre, cim = c_pair(2 * tile + 1)
        wc_ref[0:ns, tile * LANE:(tile + 1) * LANE] = cre.T.astype(BF16)
        wc_ref[ns:2 * ns, tile * LANE:(tile + 1) * LANE] = (-cim).T.astype(BF16)
    pw_ref[...] = jnp.zeros_like(pw_ref)
    for r, n in enumerate((N_PHASE, N_PHASE * SEG)):
        pre, pim = lam_pow(n)
        pw_ref[r:r + 1, 0:ns] = pre
        pw_ref[r:r + 1, ns:2 * ns] = pim


def _ssm_prep(lam_re, lam_im, log_step, b_re, b_im, c_re, c_im, d_skip, riders=()):
    ns = SUB_STATES
    n_sub = N_SLAB * SUBS
    assert 2 * STATE_DIM == LANE and SUB_K == 2 * ns

    def twice(m):
        m = m.reshape(n_sub, SUB, STATE_DIM)
        return jnp.concatenate([m, m], axis=-1)

    def vec(v):
        return v.reshape(n_sub, 1, ns)

    mats = jnp.stack([twice(jnp.transpose(b_re, (0, 2, 1))), twice(jnp.transpose(b_im, (0, 2, 1))),
                      twice(c_re), twice(c_im)], axis=1)
    vecs = jnp.stack([vec(lam_re), vec(lam_im), vec(jnp.broadcast_to(log_step[:, None], lam_re.shape)),
                      jnp.tile(d_skip.reshape(n_sub, 1, SUB), (1, 1, ns // SUB))], axis=1)
    w_shape = jax.ShapeDtypeStruct((n_sub, SUB_K, 2 * ns), BF16)
    w_spec = pl.BlockSpec((None, SUB_K, 2 * ns), lambda q: (q, 0, 0))
    rider_in, rider_out, rider_shapes = _rider_specs(riders, n_sub)
    out = pl.pallas_call(
        functools.partial(_ssm_prep_body, n_riders=len(riders)),
        grid=(n_sub,),
        in_specs=[pl.BlockSpec((None, 4, SUB, LANE), lambda q: (q, 0, 0, 0)),
                  pl.BlockSpec((None, 4, 1, ns), lambda q: (q, 0, 0, 0))] + rider_in,
        out_specs=[w_spec, w_spec, w_spec, pl.BlockSpec((None, 8, 2 * ns), lambda q: (q, 0, 0))] + rider_out,
        out_shape=[w_shape, w_shape, w_shape, jax.ShapeDtypeStruct((n_sub, 8, 2 * ns), F32)] + rider_shapes,
        name="ssm_prep",
        compiler_params=_params(("arbitrary",)),
    )(mats, vecs, *riders)
    return tuple(w.reshape(N_SLAB, SUBS, *w.shape[1:]) for w in out[:4]), tuple(out[4:])


def _cmul(are, aim, bre, bim):
    return are * bre - aim * bim, are * bim + aim * bre


HALF = SUB_STATES // LANE
GROUP_ROWS = 8 * SEG


def _pair_halves(a, b):
    low = lax.broadcasted_iota(jnp.int32, a.shape, 1) < SUB
    return jnp.where(low, a, pltpu.roll(b, SUB, 1)), jnp.where(low, pltpu.roll(a, SUB, 1), b)


def _sub_slab_operands(u_ref):
    return [jnp.concatenate([u_ref[2 * k + h].reshape(-1, LANE) for k in range(N_PHASE // 2)], axis=1)
            for h in range(SUBS)]


def _slab_phase_tiles(subs):
    tiles = []
    for k in range(N_PHASE // 2):
        tiles += _pair_halves(subs[0][:, k * LANE:(k + 1) * LANE], subs[1][:, k * LANE:(k + 1) * LANE])
    return [y.astype(BF16) for y in tiles]


def _mult(pw_ref, r, c):
    return (pw_ref[r:r + 1, c * LANE:(c + 1) * LANE],
            pw_ref[r:r + 1, (HALF + c) * LANE:(HALF + c + 1) * LANE])


def _state_increments(lhs, wb_ref, e_ref):
    e = _dot(lhs, wb_ref[...])
    for lt in range(2 * HALF):
        e_ref[lt, :e.shape[0]] = e[:, lt * LANE:(lt + 1) * LANE]


def _scan_local(e_ref, p_ref, pw_ref, c, base):
    l8re, l8im = _mult(pw_ref, 0, c)
    idx = pl.ds(base, 8, stride=SEG)
    lre = e_ref[c, idx, :]
    lim = e_ref[HALF + c, idx, :]
    for i in range(1, SEG):
        idx = pl.ds(base + i, 8, stride=SEG)
        p_ref[c, idx, :] = lre
        p_ref[HALF + c, idx, :] = lim
        mre, mim = _cmul(l8re, l8im, lre, lim)
        lre = mre + e_ref[c, idx, :]
        lim = mim + e_ref[HALF + c, idx, :]
    return lre, lim


def _scan_fixup(p_ref, pw_ref, c, base, cre, cim):
    l8re, l8im = _mult(pw_ref, 0, c)
    idx = pl.ds(base, 8, stride=SEG)
    p_ref[c, idx, :] = cre
    p_ref[HALF + c, idx, :] = cim
    for i in range(1, SEG):
        idx = pl.ds(base + i, 8, stride=SEG)
        cre, cim = _cmul(l8re, l8im, cre, cim)
        p_ref[c, idx, :] = p_ref[c, idx, :] + cre
        p_ref[HALF + c, idx, :] = p_ref[HALF + c, idx, :] + cim


def _direct_terms(lhs, kt_ref):
    return _dot(lhs, kt_ref[...])


def _state_term(p_ref, n_rows, wc_ref):
    prev = jnp.concatenate([p_ref[lt, :n_rows] for lt in range(2 * HALF)], axis=1)
    return _dot(prev.astype(BF16), wc_ref[...])


def _ssm_prompt_body(u_ref, wb_ref, kt_ref, wc_ref, pw_ref, y_ref, sre_ref, sim_ref,
                     e_ref, p_ref, end_ref, cin_ref, carry_ref, *, n_seq, rows, tile):
    @pl.when((pl.program_id(0) == 0) & (tile == 0))
    def _():
        end_ref[...] = jnp.zeros_like(end_ref)

    @pl.when(tile == 0)
    def _():
        carry_ref[...] = jnp.zeros_like(carry_ref)

    operands = _sub_slab_operands(u_ref)
    groups = rows // GROUP_ROWS
    for h, lhs in enumerate(operands):
        _state_increments(lhs, wb_ref.at[h], e_ref.at[h])
    outs = [_direct_terms(lhs, kt_ref.at[h]) for h, lhs in enumerate(operands)]
    for h in range(SUBS):
        e, p, ends, cins, carry, pw = e_ref.at[h], p_ref.at[h], end_ref.at[h], cin_ref.at[h], carry_ref.at[h], pw_ref.at[h]
        lane0 = h * SUB_STATES
        for c in range(HALF):
            for b in range(n_seq):
                for j in range(groups):
                    lre, lim = _scan_local(e, p, pw, c, b * rows + j * GROUP_ROWS)
                    idx = pl.ds(j * 64 + b, 8, stride=8)
                    ends[c, idx, :] = lre
                    ends[HALF + c, idx, :] = lim
        for c in range(HALF):
            lsre, lsim = _mult(pw, 1, c)
            cre = carry[c]
            cim = carry[HALF + c]
            for s in range(rows // SEG):
                cins[c, 8 * s:8 * s + 8, :] = cre
                cins[HALF + c, 8 * s:8 * s + 8, :] = cim
                mre, mim = _cmul(lsre, lsim, cre, cim)
                cre = mre + ends[c, 8 * s:8 * s + 8, :]
                cim = mim + ends[HALF + c, 8 * s:8 * s + 8, :]
            carry[c] = cre
            carry[HALF + c] = cim
            sre_ref[:, lane0 + c * LANE:lane0 + (c + 1) * LANE] = cre[:n_seq]
            sim_ref[:, lane0 + c * LANE:lane0 + (c + 1) * LANE] = cim[:n_seq]
        for c in range(HALF):
            for b in range(n_seq):
                for j in range(groups):
                    idx = pl.ds(j * 64 + b, 8, stride=8)
                    _scan_fixup(p, pw, c, b * rows + j * GROUP_ROWS, cins[c, idx, :], cins[HALF + c, idx, :])
        outs[h] = outs[h] + _state_term(p, n_seq * rows, wc_ref.at[h])
    for t, y in enumerate(_slab_phase_tiles(outs)):
        y_ref[t] = y.reshape(n_seq, rows, LANE)


def _ssm_sample_body(u_ref, wb_ref, kt_ref, wc_ref, pw_ref, h0re_ref, h0im_ref,
                     y_ref, sre_ref, sim_ref, e_ref, p_ref, *, rows):
    operands = _sub_slab_operands(u_ref)
    for h, lhs in enumerate(operands):
        _state_increments(lhs, wb_ref.at[h], e_ref.at[h])
    outs = [_direct_terms(lhs, kt_ref.at[h]) for h, lhs in enumerate(operands)]
    for h in range(SUBS):
        e, p, pw = e_ref.at[h], p_ref.at[h], pw_ref.at[h]
        for c in range(HALF):
            lanes = slice(h * SUB_STATES + c * LANE, h * SUB_STATES + (c + 1) * LANE)
            lsre, lsim = _mult(pw, 1, c)
            for j in range(rows // GROUP_ROWS):
                lre, lim = _scan_local(e, p, pw, c, j * GROUP_ROWS)
                cre = h0re_ref[8 * j:8 * j + 8, lanes]
                cim = h0im_ref[8 * j:8 * j + 8, lanes]
                mre, mim = _cmul(lsre, lsim, cre, cim)
                sre_ref[8 * j:8 * j + 8, lanes] = mre + lre
                sim_ref[8 * j:8 * j + 8, lanes] = mim + lim
                _scan_fixup(p, pw, c, j * GROUP_ROWS, cre, cim)
        outs[h] = outs[h] + _state_term(p, rows, wc_ref.at[h])
    for t, y in enumerate(_slab_phase_tiles(outs)):
        y_ref[t] = y


def _ssm_body(u_ref, wb_ref, kt_ref, wc_ref, pw_ref, us_ref, h0re_ref, h0im_ref,
              y_ref, sre_ref, sim_ref, ys_ref, sres_ref, sims_ref,
              e_ref, p_ref, end_ref, cin_ref, carry_ref, *, n_seq, rows, rows_s):
    step = pl.program_id(1)

    @pl.when(step == 0)
    def _():
        _ssm_sample_body(us_ref, wb_ref, kt_ref, wc_ref, pw_ref, h0re_ref, h0im_ref,
                         ys_ref, sres_ref, sims_ref, e_ref, p_ref, rows=rows_s)

    @pl.when(step > 0)
    def _():
        _ssm_prompt_body(u_ref, wb_ref, kt_ref, wc_ref, pw_ref, y_ref, sre_ref, sim_ref,
                         e_ref, p_ref, end_ref, cin_ref, carry_ref, n_seq=n_seq, rows=rows, tile=step - 1)


def _ssm(u8_p, n_p, u8_s, n_s, h0, wb, kt, wc, pw):
    d = D_MODEL
    ns = SUBS * SUB_STATES
    rows_p, rows_s = u8_p.shape[1], u8_s.shape[1]
    per_seq = rows_p // n_p
    assert rows_s // n_s == SEG, "carried-state path scans one segment per sequence"
    assert n_p <= 8, "sequences ride the sublanes of the segment chain"
    rows = min(SSM_ROWS, per_seq)
    assert per_seq % rows == 0 and rows % GROUP_ROWS == 0 and rows_s % GROUP_ROWS == 0
    assert rows_s <= n_p * rows, "the carried-state step reuses the scan scratch"
    tiles = per_seq // rows
    slab = lambda q, i: (q, 0, 0, 0)
    p_spec = pl.BlockSpec((N_PHASE, n_p, rows, LANE), lambda q, i: (0, 0, jnp.maximum(i - 1, 0), q))
    s_spec = pl.BlockSpec((N_PHASE, rows_s, LANE), lambda q, i: (0, 0, q))
    w_spec = pl.BlockSpec((None, SUBS, SUB_K, 2 * SUB_STATES), slab)
    state_p = pl.BlockSpec((n_p, ns), lambda q, i: (0, q))
    state_s = pl.BlockSpec((n_s, ns), lambda q, i: (0, q))
    seg_rows = 8 * (rows // SEG)
    dims_p = (N_PHASE, n_p, per_seq, d)
    y_p, re_p, im_p, y_s, re_s, im_s = pl.pallas_call(
        functools.partial(_ssm_body, n_seq=n_p, rows=rows, rows_s=rows_s),
        grid=(N_SLAB, tiles + 1),
        in_specs=[p_spec, w_spec, w_spec, w_spec, pl.BlockSpec((None, SUBS, 8, 2 * SUB_STATES), slab),
                  s_spec, state_s, state_s],
        out_specs=[p_spec, state_p, state_p, s_spec, state_s, state_s],
        out_shape=[jax.ShapeDtypeStruct(dims_p, BF16),
                   jax.ShapeDtypeStruct((n_p, N_SLAB * ns), F32), jax.ShapeDtypeStruct((n_p, N_SLAB * ns), F32),
                   jax.ShapeDtypeStruct((N_PHASE, rows_s, d), BF16),
                   jax.ShapeDtypeStruct((n_s, N_SLAB * ns), F32), jax.ShapeDtypeStruct((n_s, N_SLAB * ns), F32)],
        scratch_shapes=[pltpu.VMEM((SUBS, 2 * HALF, n_p * rows, LANE), F32),
                        pltpu.VMEM((SUBS, 2 * HALF, n_p * rows, LANE), F32),
                        pltpu.VMEM((SUBS, 2 * HALF, seg_rows, LANE), F32),
                        pltpu.VMEM((SUBS, 2 * HALF, seg_rows, LANE), F32),
                        pltpu.VMEM((SUBS, 2 * HALF, 8, LANE), F32)],
        name="ssm",
        compiler_params=_params(("arbitrary", "arbitrary")),
    )(u8_p.reshape(dims_p), wb, kt, wc, pw, u8_s, h0[0].reshape(n_s, -1), h0[1].reshape(n_s, -1))
    return (y_p.reshape(N_PHASE, rows_p, d), re_p, im_p), (y_s, re_s, im_s)


def _mix_out_ffn_body(ya_hbm, aa_hbm, gsa_hbm, yb_hbm, ab_hbm, gsb_hbm, x_ref,
                      wglu_ref, wo_ref, g_ref, wg_ref, wu_ref, wd_ref, gf_ref,
                      oa_hbm, ob_hbm, inbuf_ref, ybuf_ref, sem_in, sem_out, *, rows, steps_a, n_steps, final_norm):
    d = D_MODEL
    step = pl.program_id(0)
    slot = step % 2

    def in_copies(srcs, row0, of_slot):
        return [pltpu.make_async_copy(src.at[:, pl.ds(row0, rows), :], inbuf_ref.at[of_slot, k], sem_in.at[of_slot])
                for k, src in enumerate(srcs)]

    def out_copies(dst, row0, of_slot):
        return [pltpu.make_async_copy(ybuf_ref.at[of_slot, t], dst.at[pl.ds(row0, rows), t, :], sem_out.at[of_slot])
                for t in range(N_PHASE)]

    def start_for_group(of_step, copies_a, copies_b):
        @pl.when(of_step < steps_a)
        def _():
            for c in copies_a(of_step * rows):
                c.start()

        @pl.when(of_step >= steps_a)
        def _():
            for c in copies_b((of_step - steps_a) * rows):
                c.start()

    def start_in(of_step, of_slot):
        start_for_group(of_step, lambda r: in_copies((ya_hbm, aa_hbm, gsa_hbm), r, of_slot),
                        lambda r: in_copies((yb_hbm, ab_hbm, gsb_hbm), r, of_slot))

    @pl.when(step == 0)
    def _():
        start_in(step, slot)

    @pl.when(step + 1 < n_steps)
    def _():
        start_in(step + 1, 1 - slot)

    @pl.when(step >= 2)
    def _():
        for c in out_copies(oa_hbm, 0, slot):
            c.wait()

    for c in in_copies((ya_hbm, aa_hbm, gsa_hbm), 0, slot):
        c.wait()
    y_ref, a_ref, gs_ref = (inbuf_ref.at[slot, k] for k in range(3))
    part = rows // MIX_OUT_PARTS
    parts = range(MIX_OUT_PARTS)

    def stack(ref, p):
        return jnp.concatenate([ref[t, p * part:(p + 1) * part] for t in range(N_PHASE)], axis=0)

    glus = [_dot(jax.nn.gelu(stack(y_ref, p).astype(F32)).astype(BF16), wglu_ref[...]) for p in parts]
    mergeds = [(stack(a_ref, p).astype(F32) + stack(gs_ref, p).astype(F32) * (glu[:, :d] * jax.nn.sigmoid(glu[:, d:])))
               .astype(BF16) for p, glu in zip(parts, glus)]
    x2s = [stack(x_ref, p) + _dot(merged, wo_ref[...]) for p, merged in zip(parts, mergeds)]
    for p, y in enumerate(_ffn_math(x2s, g_ref, wg_ref, wu_ref, wd_ref)):
        if final_norm:
            y = _rms(y, gf_ref[...])
        for t in range(N_PHASE):
            ybuf_ref[slot, t, p * part:(p + 1) * part] = y[t * part:(t + 1) * part]
    start_for_group(step, lambda r: out_copies(oa_hbm, r, slot), lambda r: out_copies(ob_hbm, r, slot))

    @pl.when(step == n_steps - 1)
    def _():
        if n_steps > 1:
            for c in out_copies(oa_hbm, 0, 1 - slot):
                c.wait()
        for c in out_copies(oa_hbm, 0, slot):
            c.wait()


def _mix_out_ffn(acts_a, acts_b, xp, w_glu, w_o, g, wg, wu, wd, gf, final_norm):
    rows_a, rows_b = acts_a[0].shape[1], acts_b[0].shape[1]
    d = D_MODEL
    rows = MIX_OUT_ROWS
    assert rows_a % rows == 0 and rows_b % rows == 0 and xp.shape[1] == rows_a + rows_b
    steps_a, n_steps = rows_a // rows, (rows_a + rows_b) // rows
    weights = (w_glu, w_o, g, wg, wu, wd, gf)
    any_spec = pl.BlockSpec(memory_space=pl.ANY)
    out_a, out_b = pl.pallas_call(
        functools.partial(_mix_out_ffn_body, rows=rows, steps_a=steps_a, n_steps=n_steps, final_norm=final_norm),
        grid=(n_steps,),
        in_specs=[any_spec] * 6 + [pl.BlockSpec((N_PHASE, rows, d), lambda i: (0, i, 0))]
        + [_const_spec(w.shape) for w in weights],
        out_specs=[any_spec, any_spec],
        out_shape=[jax.ShapeDtypeStruct((rows_a, N_PHASE, d), F32), jax.ShapeDtypeStruct((rows_b, N_PHASE, d), F32)],
        scratch_shapes=[pltpu.VMEM((2, 3, N_PHASE, rows, d), BF16),
                        pltpu.VMEM((2, N_PHASE, rows, d), F32),
                        pltpu.SemaphoreType.DMA((2,)), pltpu.SemaphoreType.DMA((2,))],
        name="mix_out_ffn",
        compiler_params=_params(("arbitrary",)),
    )(*acts_a, *acts_b, xp, *weights)
    return out_a.reshape(N_PHASE * rows_a, d), out_b.reshape(N_PHASE * rows_b, d)


def kernel(x_prompt, x_sample, state_conv, state_ssm_re, state_ssm_im, norm_ffn1, w_ffn1_gate, w_ffn1_up, w_ffn1_down, norm_mix, w_in, w_conv, w_conv_out, ssm_lambda_re, ssm_lambda_im, ssm_log_step, ssm_b_re, ssm_b_im, ssm_c_re, ssm_c_im, ssm_d, w_glu, w_o, norm_ffn2, w_ffn2_gate, w_ffn2_up, w_ffn2_down, norm_final):
    depth, d = w_in.shape[0], w_in.shape[1]
    row = lambda v: v.reshape(1, -1)
    nf = row(norm_final)
    xt_p = x_prompt.reshape(-1, d)
    xt_s = x_sample.reshape(-1, d)
    n_p, n_s = x_prompt.shape[0], x_sample.shape[0]
    outs_p, outs_s = [], []
    for l in range(depth):
        last = l == depth - 1
        ssm_w, (f1g, f1u, f1d) = _ssm_prep(
            ssm_lambda_re[l], ssm_lambda_im[l], ssm_log_step[l], ssm_b_re[l], ssm_b_im[l], ssm_c_re[l],
            ssm_c_im[l], ssm_d[l], riders=(w_ffn1_gate[l], w_ffn1_up[l], w_ffn1_down[l]))
        ffn1 = (row(norm_ffn1[l]), f1g, f1u, f1d)
        xp, (win, wco, wgl, wo) = _ffn_to_phase(xt_p, xt_s, *ffn1, riders=(w_in[l], w_conv_out[l], w_glu[l], w_o[l]))
        rows_p, rows_s = xt_p.shape[0] // N_PHASE, xt_s.shape[0] // N_PHASE
        (u_p, a_p, gs_p, conv_p), (u_s, a_s, gs_s, conv_s), ffn2_w = _mix_in(
            xp, rows_p, n_p, rows_s, n_s, state_conv[l], row(norm_mix[l]), win, w_conv[l], wco,
            riders=(w_ffn2_gate[l], w_ffn2_up[l], w_ffn2_down[l]))
        (y_p, re_p, im_p), (y_s, re_s, im_s) = _ssm(u_p, n_p, u_s, n_s, (state_ssm_re[l], state_ssm_im[l]),
                                                    *ssm_w)
        tail = (wgl, wo, row(norm_ffn2[l]), *ffn2_w, nf, last)
        xt_p, xt_s = _mix_out_ffn((y_p, a_p, gs_p), (y_s, a_s, gs_s), xp, *tail)
        outs_p.append([conv_p, re_p.reshape(n_p, -1, STATE_DIM), im_p.reshape(n_p, -1, STATE_DIM)])
        outs_s.append([conv_s, re_s.reshape(n_s, -1, STATE_DIM), im_s.reshape(n_s, -1, STATE_DIM)])
    stack = lambda outs: tuple(jnp.stack(leaf) for leaf in zip(*outs))
    return (xt_p.reshape(x_prompt.shape), xt_s.reshape(x_sample.shape), *stack(outs_p), *stack(outs_s))
```

```python
import functools

import jax
import jax.numpy as jnp
from jax import lax
from jax.experimental import pallas as pl
from jax.experimental.pallas import tpu as pltpu

F32 = jnp.float32
BF16 = jnp.bfloat16

D_MODEL = 1024
N_PHASE = 8
LANE = 128
MXU_TILE = 256
SLAB = 128
N_SLAB = D_MODEL // SLAB
SUB = 64
SUBS = SLAB // SUB
GROUP_SIZE = 16
STATE_DIM = 64
SUB_STATES = (SUB // GROUP_SIZE) * STATE_DIM
SUB_K = N_PHASE * SUB
SEG = 4
RMS_EPS = 1e-6
VMEM_LIMIT = 60 * 1024 * 1024

FFN_ROWS = 1024
MIX_OUT_PARTS = 2
MIX_IN_ROWS = 128
MIX_OUT_ROWS = 64
SSM_ROWS = 256


def _rms(x, g):
    return x * lax.rsqrt(jnp.mean(x * x, axis=-1, keepdims=True) + RMS_EPS) * g


def _dot(a, b):
    return jnp.dot(a, b, preferred_element_type=F32)


def _const_spec(shape):
    zeros = (0,) * len(shape)
    return pl.BlockSpec(shape, lambda *_: zeros, pipeline_mode=pl.Buffered(1))


def _params(semantics):
    return pltpu.CompilerParams(dimension_semantics=semantics, vmem_limit_bytes=VMEM_LIMIT)


def _stack_phases(ref):
    return jnp.concatenate([ref[t] for t in range(N_PHASE)], axis=0)


def _ffn_math(xs, g_ref, wg_ref, wu_ref, wd_ref, side_work=None):
    hs = [_rms(x, g_ref[...]).astype(BF16) for x in xs]
    if side_work is not None:
        side_work()
    acts = [(jax.nn.silu(_dot(h, wg_ref[...])) * _dot(h, wu_ref[...])).astype(BF16) for h in hs]
    return [x + 0.5 * _dot(act, wd_ref[...]) for x, act in zip(xs, acts)]


def _rider_specs(weights, n_steps, step=lambda i: i):
    in_specs, out_specs, out_shapes = [], [], []
    for w in weights:
        n_blocks = n_steps
        while w.shape[0] % (16 * n_blocks):
            assert n_blocks % 2 == 0
            n_blocks //= 2
        spec = pl.BlockSpec((w.shape[0] // n_blocks, w.shape[1]),
                            lambda *idx, rep=n_steps // n_blocks: (step(*idx) // rep, 0))
        in_specs.append(spec)
        out_specs.append(spec)
        out_shapes.append(jax.ShapeDtypeStruct(w.shape, BF16))
    return in_specs, out_specs, out_shapes


def _convert_riders(in_refs, out_refs):
    for w_ref, o_ref in zip(in_refs, out_refs):
        o_ref[...] = w_ref[...].astype(BF16)


def _ffn_to_phase_body(*refs, rows, steps_a, n_steps, n_riders):
    xa_hbm, xb_hbm, g_ref, wg_ref, wu_ref, wd_ref = refs[:6]
    o_ref, xbuf_ref, sem = refs[6 + n_riders], refs[-2], refs[-1]
    step = pl.program_id(0)
    slot = step % 2

    def copies(src, row0, of_slot):
        return [pltpu.make_async_copy(src.at[pl.ds(row0, rows), t, :], xbuf_ref.at[of_slot, t], sem.at[of_slot])
                for t in range(N_PHASE)]

    def start(of_step, of_slot):
        @pl.when(of_step < steps_a)
        def _():
            for c in copies(xa_hbm, of_step * rows, of_slot):
                c.start()

        @pl.when(of_step >= steps_a)
        def _():
            for c in copies(xb_hbm, (of_step - steps_a) * rows, of_slot):
                c.start()

    @pl.when(step == 0)
    def _():
        start(step, slot)

    @pl.when(step + 1 < n_steps)
    def _():
        start(step + 1, 1 - slot)

    for c in copies(xa_hbm, 0, slot):
        c.wait()
    _convert_riders(refs[6:6 + n_riders], refs[7 + n_riders:7 + 2 * n_riders])
    x = jnp.concatenate([xbuf_ref[slot, t] for t in range(N_PHASE)], axis=0)
    (y,) = _ffn_math([x], g_ref, wg_ref, wu_ref, wd_ref)
    for t in range(N_PHASE):
        o_ref[t] = y[t * rows:(t + 1) * rows]


def _ffn_to_phase(x_a, x_b, g, wg, wu, wd, riders=()):
    d = x_a.shape[1]
    rows_a, rows_b = x_a.shape[0] // N_PHASE, x_b.shape[0] // N_PHASE
    n_rows = rows_a + rows_b
    d_ff = wg.shape[1]
    rows = FFN_ROWS // N_PHASE
    assert rows_a % rows == 0 and rows_b % rows == 0
    steps_a, steps_b = rows_a // rows, rows_b // rows
    n_steps = steps_a + steps_b
    rider_in, rider_out, rider_shapes = _rider_specs(riders, steps_a, lambda i: jnp.minimum(i, steps_a - 1))
    out = pl.pallas_call(
        functools.partial(_ffn_to_phase_body, rows=rows, steps_a=steps_a, n_steps=n_steps, n_riders=len(riders)),
        grid=(n_steps,),
        in_specs=[
            pl.BlockSpec(memory_space=pl.ANY),
            pl.BlockSpec(memory_space=pl.ANY),
            _const_spec((1, d)),
            _const_spec((d, d_ff)),
            _const_spec((d, d_ff)),
            _const_spec((d_ff, d)),
        ] + rider_in,
        out_specs=[pl.BlockSpec((N_PHASE, rows, d), lambda i: (0, i, 0))] + rider_out,
        out_shape=[jax.ShapeDtypeStruct((N_PHASE, n_rows, d), F32)] + rider_shapes,
        scratch_shapes=[pltpu.VMEM((2, N_PHASE, rows, d), F32), pltpu.SemaphoreType.DMA((2,))],
        name="ffn_to_phase",
        compiler_params=_params(("arbitrary",)),
    )(x_a.reshape(rows_a, N_PHASE, d), x_b.reshape(rows_b, N_PHASE, d), g, wg, wu, wd, *riders)
    return out[0], tuple(out[1:])


def _mix_in_compute(x_ref, g, win_ref, wcv, wco_ref, shift, rows):
    d = D_MODEL
    h = _rms(_stack_phases(x_ref), g).astype(BF16)

    def proj(k):
        return _dot(h, win_ref[:, k * d:(k + 1) * d])

    z = proj(1) * proj(0)
    b_gate = proj(2)
    u = proj(3)
    g_conv = jax.nn.sigmoid(proj(4))
    gs = jax.nn.sigmoid(proj(5))
    z6 = z[6 * rows:7 * rows]
    z7 = z[7 * rows:8 * rows]
    s6 = shift(z6, 0)
    s7 = shift(z7, 1)
    z1 = jnp.concatenate([s7, z[:7 * rows]], axis=0)
    z2 = jnp.concatenate([s6, s7, z[:6 * rows]], axis=0)
    conv = wcv[2:3] * z + wcv[1:2] * z1 + wcv[0:1] * z2
    a = g_conv * _dot((b_gate * conv).astype(BF16), wco_ref[...])
    return u, a, gs, z6, z7


def _store_phases(ref, val, rows):
    val = val.astype(ref.dtype)
    for t in range(N_PHASE):
        ref[t] = val[t * rows:(t + 1) * rows]


def _store_sub_slab_pairs(ref, val, rows):
    for k in range(N_PHASE // 2):
        a = val[2 * k * rows:(2 * k + 1) * rows]
        b = val[(2 * k + 1) * rows:(2 * k + 2) * rows]
        for q in range(D_MODEL // LANE):
            lanes = slice(q * LANE, (q + 1) * LANE)
            for h, tile in enumerate(_pair_halves(a[:, lanes], b[:, lanes])):
                ref[2 * k + h, :, lanes] = tile.astype(ref.dtype)


def _mix_in_prompt_body(*refs, rows, n_riders):
    x_ref, g_ref, win_ref, wcv_ref, wco_ref = refs[:5]
    u_ref, a_ref, gs_ref, z_ref = refs[5 + n_riders:9 + n_riders]
    carry_ref = refs[-1]
    _convert_riders(refs[5:5 + n_riders], refs[9 + n_riders:9 + 2 * n_riders])

    @pl.when(pl.program_id(1) == 0)
    def _():
        carry_ref[...] = jnp.zeros_like(carry_ref)

    row_id = lax.broadcasted_iota(jnp.int32, (rows, D_MODEL), 0)

    def shift(z, k):
        return jnp.where(row_id == 0, carry_ref[k:k + 1, :], pltpu.roll(z, 1, 0))

    u, a, gs, z6, z7 = _mix_in_compute(x_ref, g_ref[...], win_ref, wcv_ref[...], wco_ref, shift, rows)
    last6 = z6[rows - 1:rows]
    last7 = z7[rows - 1:rows]
    carry_ref[0:1, :] = last6
    carry_ref[1:2, :] = last7
    z_ref[0:1, :] = last6
    z_ref[1:2, :] = last7
    _store_sub_slab_pairs(u_ref, u, rows)
    _store_phases(a_ref, a, rows)
    _store_phases(gs_ref, gs, rows)


def _mix_in_sample_body(x_ref, start_ref, g_ref, win_ref, wcv_ref, wco_ref,
                        u_ref, a_ref, gs_ref, z6_ref, z7_ref, slab_ref, *, rows, rows_per_seq):
    row_id = lax.broadcasted_iota(jnp.int32, (rows, D_MODEL), 0)

    def shift(z, k):
        return jnp.where(row_id % rows_per_seq == 0, start_ref[k], pltpu.roll(z, 1, 0))

    u, a, gs, z6, z7 = _mix_in_compute(x_ref, g_ref[...], win_ref, wcv_ref[...], wco_ref, shift, rows)
    for z, z_ref in ((z6, z6_ref), (z7, z7_ref)):
        for lt in range(D_MODEL // LANE):
            slab_ref[...] = z[:, lt * LANE:(lt + 1) * LANE]
            z_ref[:, lt * LANE:(lt + 1) * LANE] = slab_ref[pl.ds(rows_per_seq - 1, rows // rows_per_seq,
                                                                stride=rows_per_seq), :]
    _store_sub_slab_pairs(u_ref, u, rows)
    _store_phases(a_ref, a, rows)
    _store_phases(gs_ref, gs, rows)


def _mix_in(xp, row0, n_rows, conv_prev, g, w_in, w_conv, w_conv_out, n_seq, riders=()):
    d = D_MODEL
    rows_per_seq = n_rows // n_seq
    ph_shape = jax.ShapeDtypeStruct((N_PHASE, n_rows, d), BF16)
    weights = (g, w_in, w_conv, w_conv_out)
    weight_specs = [_const_spec(w.shape) for w in weights]
    if conv_prev is None:
        rows = min(MIX_IN_ROWS, rows_per_seq)
        assert rows_per_seq % rows == 0
        tiles = rows_per_seq // rows
        assert row0 % rows == 0
        ph_spec = pl.BlockSpec((N_PHASE, rows, d), lambda b, i: (0, b * tiles + i, 0))
        x_spec = pl.BlockSpec((N_PHASE, rows, d), lambda b, i: (0, row0 // rows + b * tiles + i, 0))
        last_spec = pl.BlockSpec((None, 2, d), lambda b, i: (b, 0, 0))
        rider_in, rider_out, rider_shapes = _rider_specs(riders, n_seq * tiles, lambda b, i: b * tiles + i)
        u8, a8, gs8, conv_state, *converted = pl.pallas_call(
            functools.partial(_mix_in_prompt_body, rows=rows, n_riders=len(riders)),
            grid=(n_seq, tiles),
            in_specs=[x_spec] + weight_specs + rider_in,
            out_specs=[ph_spec, ph_spec, ph_spec, last_spec] + rider_out,
            out_shape=[ph_shape, ph_shape, ph_shape, jax.ShapeDtypeStruct((n_seq, 2, d), F32)] + rider_shapes,
            scratch_shapes=[pltpu.VMEM((8, d), F32)],
            name="mix_in_prompt",
            compiler_params=_params(("arbitrary", "arbitrary")),
        )(xp, *weights, *riders)
    else:
        assert not riders
        converted = []
        rows = n_rows
        assert row0 % rows == 0
        start = jnp.repeat(jnp.swapaxes(conv_prev, 0, 1), rows_per_seq, axis=1)
        full = lambda shape: pl.BlockSpec(shape, lambda i: (0,) * len(shape))
        x_spec = pl.BlockSpec((N_PHASE, rows, d), lambda i: (0, row0 // rows, 0))
        u8, a8, gs8, z6, z7 = pl.pallas_call(
            functools.partial(_mix_in_sample_body, rows=rows, rows_per_seq=rows_per_seq),
            grid=(1,),
            in_specs=[x_spec, full((2, rows, d))] + weight_specs,
            out_specs=[full((N_PHASE, rows, d))] * 3 + [full((n_seq, d))] * 2,
            out_shape=[ph_shape, ph_shape, ph_shape,
                       jax.ShapeDtypeStruct((n_seq, d), F32), jax.ShapeDtypeStruct((n_seq, d), F32)],
            scratch_shapes=[pltpu.VMEM((rows, LANE), F32)],
            name="mix_in_sample",
            compiler_params=_params(("arbitrary",)),
        )(xp, start, *weights)
        conv_state = jnp.stack([z6, z7], axis=1)
    return u8, a8, gs8, conv_state, tuple(converted)


def _split_bf16(a):
    hi = a.astype(BF16)
    return hi, (a - hi.astype(F32)).astype(BF16)


def _ssm_prep_body(*refs, n_riders):
    mats_ref, vecs_ref = refs[:2]
    wb_ref, kt_ref, wc_ref, pw_ref = refs[2 + n_riders:6 + n_riders]
    _convert_riders(refs[2:2 + n_riders], refs[6 + n_riders:])
    bbre_ref, bbim_ref, ctre_ref, ctim_ref = (mats_ref.at[i] for i in range(4))
    lre_ref, lim_ref, lstep_ref, d_ref = (vecs_ref.at[i] for i in range(4))
    ns = SUB_STATES
    row_group = lax.broadcasted_iota(jnp.int32, (SUB, LANE), 0) // GROUP_SIZE
    lane_half = lax.broadcasted_iota(jnp.int32, (SUB, LANE), 1) // STATE_DIM
    low_half = lax.broadcasted_iota(jnp.int32, (SUB, LANE), 1) < SUB

    def block_diag(ref):
        tiles = [jnp.where(row_group == 2 * k + lane_half, ref[...], 0.0) for k in range(ns // LANE)]
        return jnp.concatenate(tiles, axis=1)

    bbre, bbim = block_diag(bbre_ref), block_diag(bbim_ref)
    ctre, ctim = block_diag(ctre_ref), block_diag(ctim_ref)
    lre, lim = lre_ref[...], lim_ref[...]
    step = jnp.exp(lstep_ref[...])

    def lam_pow(n):
        mag = jnp.exp((n * lre) * step)
        ang = (n * lim) * step
        return mag * jnp.cos(ang), mag * jnp.sin(ang)

    l1re, l1im = lam_pow(1)
    den = lre * lre + lim * lim
    fre = ((l1re - 1.0) * lre + l1im * lim) / den
    fim = (l1im * lre - (l1re - 1.0) * lim) / den

    def dot_nt(a, b_split):
        nt = lambda p, q: lax.dot_general(p, q, (((1,), (1,)), ((), ())), preferred_element_type=F32)
        a_hi, a_lo = _split_bf16(a)
        b_hi, b_lo = b_split
        return nt(a_hi, b_hi) + nt(a_hi, b_lo) + nt(a_lo, b_hi)

    def c_pair(n):
        (p0re, p0im), (p1re, p1im) = lam_pow(n), lam_pow(n + 1)
        return (jnp.concatenate([ctre * p0re - ctim * p0im, ctre * p1re - ctim * p1im], axis=0),
                jnp.concatenate([ctre * p0im + ctim * p0re, ctre * p1im + ctim * p1re], axis=0))

    c01re, c01im = c_pair(0)
    c01re_split = _split_bf16(c01re)
    c01im_split = _split_bf16(c01im)
    lag_pairs = {}
    for k in range(N_PHASE):
        pre, pim = lam_pow(k)
        gre = fre * pre - fim * pim
        gim = fre * pim + fim * pre
        are = bbre * gre - bbim * gim
        aim = bbre * gim + bbim * gre
        j = N_PHASE - 1 - k
        wb_ref[j * SUB:(j + 1) * SUB, 0:ns] = are.astype(BF16)
        wb_ref[j * SUB:(j + 1) * SUB, ns:2 * ns] = aim.astype(BF16)
        if k < N_PHASE - 1:
            lag_pairs[k] = dot_nt(are, c01re_split) - dot_nt(aim, c01im_split)
    on_diagonal = (lax.broadcasted_iota(jnp.int32, (SUB, LANE), 0) == lax.broadcasted_iota(jnp.int32, (SUB, LANE), 1))
    lag_pairs[0] = lag_pairs[0] + jnp.where(on_diagonal, d_ref[:, :LANE], 0.0)
    kt_ref[...] = jnp.zeros_like(kt_ref)
    first_odd = jnp.where(low_half, 0.0, pltpu.roll(lag_pairs[0], SUB, 1))
    for j in range(N_PHASE):
        for tile in range(j // 2, N_PHASE // 2):
            lag = 2 * tile - j
            pair = first_odd if lag < 0 else lag_pairs[lag]
            kt_ref[j * SUB:(j + 1) * SUB, tile * LANE:(tile + 1) * LANE] = pair.astype(BF16)
    for tile in range(N_PHASE // 2):
        cre, cim = c_pair(2 * tile + 1)
        wc_ref[0:ns, tile * LANE:(tile + 1) * LANE] = cre.T.astype(BF16)
        wc_ref[ns:2 * ns, tile * LANE:(tile + 1) * LANE] = (-cim).T.astype(BF16)
    pw_ref[...] = jnp.zeros_like(pw_ref)
    for r, n in enumerate((N_PHASE, N_PHASE * SEG)):
        pre, pim = lam_pow(n)
        pw_ref[r:r + 1, 0:ns] = pre
        pw_ref[r:r + 1, ns:2 * ns] = pim


def _ssm_prep(lam_re, lam_im, log_step, b_re, b_im, c_re, c_im, d_skip, riders=()):
    ns = SUB_STATES
    n_sub = N_SLAB * SUBS
    assert 2 * STATE_DIM == LANE and SUB_K == 2 * ns

    def twice(m):
        m = m.reshape(n_sub, SUB, STATE_DIM)
        return jnp.concatenate([m, m], axis=-1)

    def vec(v):
        return v.reshape(n_sub, 1, ns)

    mats = jnp.stack([twice(jnp.transpose(b_re, (0, 2, 1))), twice(jnp.transpose(b_im, (0, 2, 1))),
                      twice(c_re), twice(c_im)], axis=1)
    vecs = jnp.stack([vec(lam_re), vec(lam_im), vec(jnp.broadcast_to(log_step[:, None], lam_re.shape)),
                      jnp.tile(d_skip.reshape(n_sub, 1, SUB), (1, 1, ns // SUB))], axis=1)
    w_shape = jax.ShapeDtypeStruct((n_sub, SUB_K, 2 * ns), BF16)
    w_spec = pl.BlockSpec((None, SUB_K, 2 * ns), lambda q: (q, 0, 0))
    rider_in, rider_out, rider_shapes = _rider_specs(riders, n_sub)
    out = pl.pallas_call(
        functools.partial(_ssm_prep_body, n_riders=len(riders)),
        grid=(n_sub,),
        in_specs=[pl.BlockSpec((None, 4, SUB, LANE), lambda q: (q, 0, 0, 0)),
                  pl.BlockSpec((None, 4, 1, ns), lambda q: (q, 0, 0, 0))] + rider_in,
        out_specs=[w_spec, w_spec, w_spec, pl.BlockSpec((None, 8, 2 * ns), lambda q: (q, 0, 0))] + rider_out,
        out_shape=[w_shape, w_shape, w_shape, jax.ShapeDtypeStruct((n_sub, 8, 2 * ns), F32)] + rider_shapes,
        name="ssm_prep",
        compiler_params=_params(("arbitrary",)),
    )(mats, vecs, *riders)
    return tuple(w.reshape(N_SLAB, SUBS, *w.shape[1:]) for w in out[:4]), tuple(out[4:])


def _cmul(are, aim, bre, bim):
    return are * bre - aim * bim, are * bim + aim * bre


HALF = SUB_STATES // LANE
GROUP_ROWS = 8 * SEG


def _pair_halves(a, b):
    low = lax.broadcasted_iota(jnp.int32, a.shape, 1) < SUB
    return jnp.where(low, a, pltpu.roll(b, SUB, 1)), jnp.where(low, pltpu.roll(a, SUB, 1), b)


def _sub_slab_operands(u_ref):
    return [jnp.concatenate([u_ref[2 * k + h].reshape(-1, LANE) for k in range(N_PHASE // 2)], axis=1)
            for h in range(SUBS)]


def _slab_phase_tiles(subs):
    tiles = []
    for k in range(N_PHASE // 2):
        tiles += _pair_halves(subs[0][:, k * LANE:(k + 1) * LANE], subs[1][:, k * LANE:(k + 1) * LANE])
    return [y.astype(BF16) for y in tiles]


def _mult(pw_ref, r, c):
    return (pw_ref[r:r + 1, c * LANE:(c + 1) * LANE],
            pw_ref[r:r + 1, (HALF + c) * LANE:(HALF + c + 1) * LANE])


def _state_increments(lhs, wb_ref, e_ref):
    e = _dot(lhs, wb_ref[...])
    for lt in range(2 * HALF):
        e_ref[lt, :e.shape[0]] = e[:, lt * LANE:(lt + 1) * LANE]


def _scan_local(e_ref, p_ref, pw_ref, c, base):
    l8re, l8im = _mult(pw_ref, 0, c)
    idx = pl.ds(base, 8, stride=SEG)
    lre = e_ref[c, idx, :]
    lim = e_ref[HALF + c, idx, :]
    for i in range(1, SEG):
        idx = pl.ds(base + i, 8, stride=SEG)
        p_ref[c, idx, :] = lre
        p_ref[HALF + c, idx, :] = lim
        mre, mim = _cmul(l8re, l8im, lre, lim)
        lre = mre + e_ref[c, idx, :]
        lim = mim + e_ref[HALF + c, idx, :]
    return lre, lim


def _scan_fixup(p_ref, pw_ref, c, base, cre, cim):
    l8re, l8im = _mult(pw_ref, 0, c)
    idx = pl.ds(base, 8, stride=SEG)
    p_ref[c, idx, :] = cre
    p_ref[HALF + c, idx, :] = cim
    for i in range(1, SEG):
        idx = pl.ds(base + i, 8, stride=SEG)
        cre, cim = _cmul(l8re, l8im, cre, cim)
        p_ref[c, idx, :] = p_ref[c, idx, :] + cre
        p_ref[HALF + c, idx, :] = p_ref[HALF + c, idx, :] + cim


def _direct_terms(lhs, kt_ref):
    return _dot(lhs, kt_ref[...])


def _state_term(p_ref, n_rows, wc_ref):
    prev = jnp.concatenate([p_ref[lt, :n_rows] for lt in range(2 * HALF)], axis=1)
    return _dot(prev.astype(BF16), wc_ref[...])


def _ssm_prompt_body(u_ref, wb_ref, kt_ref, wc_ref, pw_ref, y_ref, sre_ref, sim_ref,
                     e_ref, p_ref, end_ref, cin_ref, carry_ref, *, n_seq, rows, tile):
    @pl.when((pl.program_id(0) == 0) & (tile == 0))
    def _():
        end_ref[...] = jnp.zeros_like(end_ref)

    @pl.when(tile == 0)
    def _():
        carry_ref[...] = jnp.zeros_like(carry_ref)

    operands = _sub_slab_operands(u_ref)
    groups = rows // GROUP_ROWS
    for h, lhs in enumerate(operands):
        _state_increments(lhs, wb_ref.at[h], e_ref.at[h])
    outs = [_direct_terms(lhs, kt_ref.at[h]) for h, lhs in enumerate(operands)]
    for h in range(SUBS):
        e, p, ends, cins, carry, pw = e_ref.at[h], p_ref.at[h], end_ref.at[h], cin_ref.at[h], carry_ref.at[h], pw_ref.at[h]
        lane0 = h * SUB_STATES
        for c in range(HALF):
            for b in range(n_seq):
                for j in range(groups):
                    lre, lim = _scan_local(e, p, pw, c, b * rows + j * GROUP_ROWS)
                    idx = pl.ds(j * 64 + b, 8, stride=8)
                    ends[c, idx, :] = lre
                    ends[HALF + c, idx, :] = lim
        for c in range(HALF):
            lsre, lsim = _mult(pw, 1, c)
            cre = carry[c]
            cim = carry[HALF + c]
            for s in range(rows // SEG):
                cins[c, 8 * s:8 * s + 8, :] = cre
                cins[HALF + c, 8 * s:8 * s + 8, :] = cim
                mre, mim = _cmul(lsre, lsim, cre, cim)
                cre = mre + ends[c, 8 * s:8 * s + 8, :]
                cim = mim + ends[HALF + c, 8 * s:8 * s + 8, :]
            carry[c] = cre
            carry[HALF + c] = cim
            sre_ref[:, lane0 + c * LANE:lane0 + (c + 1) * LANE] = cre[:n_seq]
            sim_ref[:, lane0 + c * LANE:lane0 + (c + 1) * LANE] = cim[:n_seq]
        for c in range(HALF):
            for b in range(n_seq):
                for j in range(groups):
                    idx = pl.ds(j * 64 + b, 8, stride=8)
                    _scan_fixup(p, pw, c, b * rows + j * GROUP_ROWS, cins[c, idx, :], cins[HALF + c, idx, :])
        outs[h] = outs[h] + _state_term(p, n_seq * rows, wc_ref.at[h])
    for t, y in enumerate(_slab_phase_tiles(outs)):
        y_ref[t] = y.reshape(n_seq, rows, LANE)


def _ssm_sample_body(u_ref, wb_ref, kt_ref, wc_ref, pw_ref, h0re_ref, h0im_ref,
                     y_ref, sre_ref, sim_ref, e_ref, p_ref, *, rows):
    operands = _sub_slab_operands(u_ref)
    for h, lhs in enumerate(operands):
        _state_increments(lhs, wb_ref.at[h], e_ref.at[h])
    outs = [_direct_terms(lhs, kt_ref.at[h]) for h, lhs in enumerate(operands)]
    for h in range(SUBS):
        e, p, pw = e_ref.at[h], p_ref.at[h], pw_ref.at[h]
        for c in range(HALF):
            lanes = slice(h * SUB_STATES + c * LANE, h * SUB_STATES + (c + 1) * LANE)
            lsre, lsim = _mult(pw, 1, c)
            for j in range(rows // GROUP_ROWS):
                lre, lim = _scan_local(e, p, pw, c, j * GROUP_ROWS)
                cre = h0re_ref[8 * j:8 * j + 8, lanes]
                cim = h0im_ref[8 * j:8 * j + 8, lanes]
                mre, mim = _cmul(lsre, lsim, cre, cim)
                sre_ref[8 * j:8 * j + 8, lanes] = mre + lre
                sim_ref[8 * j:8 * j + 8, lanes] = mim + lim
                _scan_fixup(p, pw, c, j * GROUP_ROWS, cre, cim)
        outs[h] = outs[h] + _state_term(p, rows, wc_ref.at[h])
    for t, y in enumerate(_slab_phase_tiles(outs)):
        y_ref[t] = y


def _ssm_body(u_ref, wb_ref, kt_ref, wc_ref, pw_ref, us_ref, h0re_ref, h0im_ref,
              y_ref, sre_ref, sim_ref, ys_ref, sres_ref, sims_ref,
              e_ref, p_ref, end_ref, cin_ref, carry_ref, *, n_seq, rows, rows_s):
    step = pl.program_id(1)

    @pl.when(step == 0)
    def _():
        _ssm_sample_body(us_ref, wb_ref, kt_ref, wc_ref, pw_ref, h0re_ref, h0im_ref,
                         ys_ref, sres_ref, sims_ref, e_ref, p_ref, rows=rows_s)

    @pl.when(step > 0)
    def _():
        _ssm_prompt_body(u_ref, wb_ref, kt_ref, wc_ref, pw_ref, y_ref, sre_ref, sim_ref,
                         e_ref, p_ref, end_ref, cin_ref, carry_ref, n_seq=n_seq, rows=rows, tile=step - 1)


def _ssm(u8_p, n_p, u8_s, n_s, h0, wb, kt, wc, pw):
    d = D_MODEL
    ns = SUBS * SUB_STATES
    rows_p, rows_s = u8_p.shape[1], u8_s.shape[1]
    per_seq = rows_p // n_p
    assert rows_s // n_s == SEG, "carried-state path scans one segment per sequence"
    assert n_p <= 8, "sequences ride the sublanes of the segment chain"
    rows = min(SSM_ROWS, per_seq)
    assert per_seq % rows == 0 and rows % GROUP_ROWS == 0 and rows_s % GROUP_ROWS == 0
    assert rows_s <= n_p * rows, "the carried-state step reuses the scan scratch"
    tiles = per_seq // rows
    slab = lambda q, i: (q, 0, 0, 0)
    p_spec = pl.BlockSpec((N_PHASE, n_p, rows, LANE), lambda q, i: (0, 0, jnp.maximum(i - 1, 0), q))
    s_spec = pl.BlockSpec((N_PHASE, rows_s, LANE), lambda q, i: (0, 0, q))
    w_spec = pl.BlockSpec((None, SUBS, SUB_K, 2 * SUB_STATES), slab)
    state_p = pl.BlockSpec((n_p, ns), lambda q, i: (0, q))
    state_s = pl.BlockSpec((n_s, ns), lambda q, i: (0, q))
    seg_rows = 8 * (rows // SEG)
    dims_p = (N_PHASE, n_p, per_seq, d)
    y_p, re_p, im_p, y_s, re_s, im_s = pl.pallas_call(
        functools.partial(_ssm_body, n_seq=n_p, rows=rows, rows_s=rows_s),
        grid=(N_SLAB, tiles + 1),
        in_specs=[p_spec, w_spec, w_spec, w_spec, pl.BlockSpec((None, SUBS, 8, 2 * SUB_STATES), slab),
                  s_spec, state_s, state_s],
        out_specs=[p_spec, state_p, state_p, s_spec, state_s, state_s],
        out_shape=[jax.ShapeDtypeStruct(dims_p, BF16),
                   jax.ShapeDtypeStruct((n_p, N_SLAB * ns), F32), jax.ShapeDtypeStruct((n_p, N_SLAB * ns), F32),
                   jax.ShapeDtypeStruct((N_PHASE, rows_s, d), BF16),
                   jax.ShapeDtypeStruct((n_s, N_SLAB * ns), F32), jax.ShapeDtypeStruct((n_s, N_SLAB * ns), F32)],
        scratch_shapes=[pltpu.VMEM((SUBS, 2 * HALF, n_p * rows, LANE), F32),
                        pltpu.VMEM((SUBS, 2 * HALF, n_p * rows, LANE), F32),
                        pltpu.VMEM((SUBS, 2 * HALF, seg_rows, LANE), F32),
                        pltpu.VMEM((SUBS, 2 * HALF, seg_rows, LANE), F32),
                        pltpu.VMEM((SUBS, 2 * HALF, 8, LANE), F32)],
        name="ssm",
        compiler_params=_params(("arbitrary", "arbitrary")),
    )(u8_p.reshape(dims_p), wb, kt, wc, pw, u8_s, h0[0].reshape(n_s, -1), h0[1].reshape(n_s, -1))
    return (y_p.reshape(N_PHASE, rows_p, d), re_p, im_p), (y_s, re_s, im_s)


def _mix_out_ffn_body(ya_hbm, aa_hbm, gsa_hbm, yb_hbm, ab_hbm, gsb_hbm, x_ref,
                      wglu_ref, wo_ref, g_ref, wg_ref, wu_ref, wd_ref, gf_ref,
                      oa_hbm, ob_hbm, inbuf_ref, ystage_ref, gelu_ref, ybuf_ref, sem_in, sem_y, sem_out,
                      *, rows, steps_a, n_steps, final_norm):
    d = D_MODEL
    step = pl.program_id(0)
    slot = step % 2

    def in_copies(srcs, row0, of_slot):
        return [pltpu.make_async_copy(src.at[:, pl.ds(row0, rows), :], inbuf_ref.at[of_slot, k], sem_in.at[of_slot])
                for k, src in enumerate(srcs)]

    def y_copy(src, row0, of_slot):
        return [pltpu.make_async_copy(src.at[:, pl.ds(row0, rows), :], ystage_ref.at[of_slot], sem_y.at[of_slot])]

    def out_copies(dst, row0, of_slot):
        return [pltpu.make_async_copy(ybuf_ref.at[of_slot, t], dst.at[pl.ds(row0, rows), t, :], sem_out.at[of_slot])
                for t in range(N_PHASE)]

    def start_for_group(of_step, copies_a, copies_b):
        @pl.when(of_step < steps_a)
        def _():
            for c in copies_a(of_step * rows):
                c.start()

        @pl.when(of_step >= steps_a)
        def _():
            for c in copies_b((of_step - steps_a) * rows):
                c.start()

    def start_in(of_step, of_slot):
        start_for_group(of_step, lambda r: in_copies((aa_hbm, gsa_hbm), r, of_slot),
                        lambda r: in_copies((ab_hbm, gsb_hbm), r, of_slot))

    def start_y(of_step, of_slot):
        of_step = jnp.minimum(of_step, n_steps - 1)
        start_for_group(of_step, lambda r: y_copy(ya_hbm, r, of_slot), lambda r: y_copy(yb_hbm, r, of_slot))

    def store_gelu(of_slot):
        for t in range(N_PHASE):
            gelu_ref[t] = jax.nn.gelu(ystage_ref[of_slot, t].astype(F32)).astype(BF16)

    @pl.when(step == 0)
    def _():
        start_in(step, slot)
        start_y(step, slot)
        y_copy(ya_hbm, 0, slot)[0].wait()
        store_gelu(slot)
        start_y(step + 1, 1 - slot)

    @pl.when(step + 1 < n_steps)
    def _():
        start_in(step + 1, 1 - slot)

    start_y(step + 2, slot)

    @pl.when(step >= 2)
    def _():
        for c in out_copies(oa_hbm, 0, slot):
            c.wait()

    for c in in_copies((aa_hbm, gsa_hbm), 0, slot):
        c.wait()
    y_copy(ya_hbm, 0, 1 - slot)[0].wait()
    a_ref, gs_ref = (inbuf_ref.at[slot, k] for k in range(2))
    part = rows // MIX_OUT_PARTS
    parts = range(MIX_OUT_PARTS)

    def stack(ref, p):
        return jnp.concatenate([ref[t, p * part:(p + 1) * part] for t in range(N_PHASE)], axis=0)

    glus = [_dot(stack(gelu_ref, p), wglu_ref[...]) for p in parts]
    mergeds = [(stack(a_ref, p).astype(F32) + stack(gs_ref, p).astype(F32) * (glu[:, :d] * jax.nn.sigmoid(glu[:, d:])))
               .astype(BF16) for p, glu in zip(parts, glus)]
    x2s = [stack(x_ref, p) + _dot(merged, wo_ref[...]) for p, merged in zip(parts, mergeds)]
    ys = _ffn_math(x2s, g_ref, wg_ref, wu_ref, wd_ref, side_work=lambda: store_gelu(1 - slot))
    for p, y in enumerate(ys):
        if final_norm:
            y = _rms(y, gf_ref[...])
        for t in range(N_PHASE):
            ybuf_ref[slot, t, p * part:(p + 1) * part] = y[t * part:(t + 1) * part]
    start_for_group(step, lambda r: out_copies(oa_hbm, r, slot), lambda r: out_copies(ob_hbm, r, slot))

    @pl.when(step == n_steps - 1)
    def _():
        y_copy(ya_hbm, 0, slot)[0].wait()
        if n_steps > 1:
            for c in out_copies(oa_hbm, 0, 1 - slot):
                c.wait()
        for c in out_copies(oa_hbm, 0, slot):
            c.wait()


def _mix_out_ffn(acts_a, acts_b, xp, w_glu, w_o, g, wg, wu, wd, gf, final_norm):
    rows_a, rows_b = acts_a[0].shape[1], acts_b[0].shape[1]
    d = D_MODEL
    rows = MIX_OUT_ROWS
    assert rows_a % rows == 0 and rows_b % rows == 0 and xp.shape[1] == rows_a + rows_b
    steps_a, n_steps = rows_a // rows, (rows_a + rows_b) // rows
    weights = (w_glu, w_o, g, wg, wu, wd, gf)
    any_spec = pl.BlockSpec(memory_space=pl.ANY)
    out_a, out_b = pl.pallas_call(
        functools.partial(_mix_out_ffn_body, rows=rows, steps_a=steps_a, n_steps=n_steps, final_norm=final_norm),
        grid=(n_steps,),
        in_specs=[any_spec] * 6 + [pl.BlockSpec((N_PHASE, rows, d), lambda i: (0, i, 0))]
        + [_const_spec(w.shape) for w in weights],
        out_specs=[any_spec, any_spec],
        out_shape=[jax.ShapeDtypeStruct((rows_a, N_PHASE, d), F32), jax.ShapeDtypeStruct((rows_b, N_PHASE, d), F32)],
        scratch_shapes=[pltpu.VMEM((2, 2, N_PHASE, rows, d), BF16),
                        pltpu.VMEM((2, N_PHASE, rows, d), BF16),
                        pltpu.VMEM((N_PHASE, rows, d), BF16),
                        pltpu.VMEM((2, N_PHASE, rows, d), F32),
                        pltpu.SemaphoreType.DMA((2,)), pltpu.SemaphoreType.DMA((2,)),
                        pltpu.SemaphoreType.DMA((2,))],
        name="mix_out_ffn",
        compiler_params=_params(("arbitrary",)),
    )(*acts_a, *acts_b, xp, *weights)
    return out_a.reshape(N_PHASE * rows_a, d), out_b.reshape(N_PHASE * rows_b, d)


def kernel(x_prompt, x_sample, state_conv, state_ssm_re, state_ssm_im, norm_ffn1, w_ffn1_gate, w_ffn1_up, w_ffn1_down, norm_mix, w_in, w_conv, w_conv_out, ssm_lambda_re, ssm_lambda_im, ssm_log_step, ssm_b_re, ssm_b_im, ssm_c_re, ssm_c_im, ssm_d, w_glu, w_o, norm_ffn2, w_ffn2_gate, w_ffn2_up, w_ffn2_down, norm_final):
    depth, d = w_in.shape[0], w_in.shape[1]
    row = lambda v: v.reshape(1, -1)
    nf = row(norm_final)
    xt_p = x_prompt.reshape(-1, d)
    xt_s = x_sample.reshape(-1, d)
    n_p, n_s = x_prompt.shape[0], x_sample.shape[0]
    outs_p, outs_s = [], []
    for l in range(depth):
        last = l == depth - 1
        ssm_w, (f1g, f1u, f1d) = _ssm_prep(
            ssm_lambda_re[l], ssm_lambda_im[l], ssm_log_step[l], ssm_b_re[l], ssm_b_im[l], ssm_c_re[l],
            ssm_c_im[l], ssm_d[l], riders=(w_ffn1_gate[l], w_ffn1_up[l], w_ffn1_down[l]))
        ffn1 = (row(norm_ffn1[l]), f1g, f1u, f1d)
        xp, (win, wco, wgl, wo) = _ffn_to_phase(xt_p, xt_s, *ffn1, riders=(w_in[l], w_conv_out[l], w_glu[l], w_o[l]))
        rows_p, rows_s = xt_p.shape[0] // N_PHASE, xt_s.shape[0] // N_PHASE
        mix_w = (row(norm_mix[l]), win, w_conv[l], wco)
        u_p, a_p, gs_p, conv_p, ffn2_w = _mix_in(xp, 0, rows_p, None, *mix_w, n_p,
                                                 riders=(w_ffn2_gate[l], w_ffn2_up[l], w_ffn2_down[l]))
        u_s, a_s, gs_s, conv_s, _ = _mix_in(xp, rows_p, rows_s, state_conv[l], *mix_w, n_s)
        (y_p, re_p, im_p), (y_s, re_s, im_s) = _ssm(u_p, n_p, u_s, n_s, (state_ssm_re[l], state_ssm_im[l]),
                                                    *ssm_w)
        tail = (wgl, wo, row(norm_ffn2[l]), *ffn2_w, nf, last)
        xt_p, xt_s = _mix_out_ffn((y_p, a_p, gs_p), (y_s, a_s, gs_s), xp, *tail)
        outs_p.append([conv_p, re_p.reshape(n_p, -1, STATE_DIM), im_p.reshape(n_p, -1, STATE_DIM)])
        outs_s.append([conv_s, re_s.reshape(n_s, -1, STATE_DIM), im_s.reshape(n_s, -1, STATE_DIM)])
    stack = lambda outs: tuple(jnp.stack(leaf) for leaf in zip(*outs))
    return (xt_p.reshape(x_prompt.shape), xt_s.reshape(x_sample.shape), *stack(outs_p), *stack(outs_s))
```

```python
import functools

import jax
import jax.numpy as jnp
from jax import lax
from jax.experimental import pallas as pl
from jax.experimental.pallas import tpu as pltpu

F32 = jnp.float32
BF16 = jnp.bfloat16

D_MODEL = 1024
N_PHASE = 8
LANE = 128
MXU_TILE = 256
SLAB = 128
N_SLAB = D_MODEL // SLAB
SUB = 64
SUBS = SLAB // SUB
GROUP_SIZE = 16
STATE_DIM = 64
SUB_STATES = (SUB // GROUP_SIZE) * STATE_DIM
SUB_K = N_PHASE * SUB
SEG = 4
RMS_EPS = 1e-6
VMEM_LIMIT = 60 * 1024 * 1024

FFN_ROWS = 1024
MIX_OUT_PARTS = 2
MIX_IN_ROWS = 128
MIX_OUT_ROWS = 64
SSM_ROWS = 256


def _rms(x, g):
    return x * lax.rsqrt(jnp.mean(x * x, axis=-1, keepdims=True) + RMS_EPS) * g


def _dot(a, b):
    return jnp.dot(a, b, preferred_element_type=F32)


def _const_spec(shape):
    zeros = (0,) * len(shape)
    return pl.BlockSpec(shape, lambda *_: zeros, pipeline_mode=pl.Buffered(1))


def _params(semantics):
    return pltpu.CompilerParams(dimension_semantics=semantics, vmem_limit_bytes=VMEM_LIMIT)


def _stack_phases(ref):
    return jnp.concatenate([ref[t] for t in range(N_PHASE)], axis=0)


def _ffn_math(xs, g_ref, wg_ref, wu_ref, wd_ref):
    hs = [_rms(x, g_ref[...]).astype(BF16) for x in xs]
    acts = [(jax.nn.silu(_dot(h, wg_ref[...])) * _dot(h, wu_ref[...])).astype(BF16) for h in hs]
    return [x + 0.5 * _dot(act, wd_ref[...]) for x, act in zip(xs, acts)]


def _rider_specs(weights, n_steps, step=lambda i: i):
    in_specs, out_specs, out_shapes = [], [], []
    for w in weights:
        n_blocks = n_steps
        while w.shape[0] % (16 * n_blocks):
            assert n_blocks % 2 == 0
            n_blocks //= 2
        spec = pl.BlockSpec((w.shape[0] // n_blocks, w.shape[1]),
                            lambda *idx, rep=n_steps // n_blocks: (step(*idx) // rep, 0))
        in_specs.append(spec)
        out_specs.append(spec)
        out_shapes.append(jax.ShapeDtypeStruct(w.shape, BF16))
    return in_specs, out_specs, out_shapes


def _convert_riders(in_refs, out_refs):
    for w_ref, o_ref in zip(in_refs, out_refs):
        o_ref[...] = w_ref[...].astype(BF16)


def _ffn_to_phase_body(*refs, rows, steps_a, n_steps, n_riders):
    xa_hbm, xb_hbm, g_ref, wg_ref, wu_ref, wd_ref = refs[:6]
    o_ref, xbuf_ref, sem = refs[6 + n_riders], refs[-2], refs[-1]
    step = pl.program_id(0)
    slot = step % 2

    def copies(src, row0, of_slot):
        return [pltpu.make_async_copy(src.at[pl.ds(row0, rows), t, :], xbuf_ref.at[of_slot, t], sem.at[of_slot])
                for t in range(N_PHASE)]

    def start(of_step, of_slot):
        @pl.when(of_step < steps_a)
        def _():
            for c in copies(xa_hbm, of_step * rows, of_slot):
                c.start()

        @pl.when(of_step >= steps_a)
        def _():
            for c in copies(xb_hbm, (of_step - steps_a) * rows, of_slot):
                c.start()

    @pl.when(step == 0)
    def _():
        start(step, slot)

    @pl.when(step + 1 < n_steps)
    def _():
        start(step + 1, 1 - slot)

    for c in copies(xa_hbm, 0, slot):
        c.wait()
    _convert_riders(refs[6:6 + n_riders], refs[7 + n_riders:7 + 2 * n_riders])
    x = jnp.concatenate([xbuf_ref[slot, t] for t in range(N_PHASE)], axis=0)
    (y,) = _ffn_math([x], g_ref, wg_ref, wu_ref, wd_ref)
    for t in range(N_PHASE):
        o_ref[t] = y[t * rows:(t + 1) * rows]


def _ffn_to_phase(x_a, x_b, g, wg, wu, wd, riders=()):
    d = x_a.shape[1]
    rows_a, rows_b = x_a.shape[0] // N_PHASE, x_b.shape[0] // N_PHASE
    n_rows = rows_a + rows_b
    d_ff = wg.shape[1]
    rows = FFN_ROWS // N_PHASE
    assert rows_a % rows == 0 and rows_b % rows == 0
    steps_a, steps_b = rows_a // rows, rows_b // rows
    n_steps = steps_a + steps_b
    rider_in, rider_out, rider_shapes = _rider_specs(riders, steps_a, lambda i: jnp.minimum(i, steps_a - 1))
    out = pl.pallas_call(
        functools.partial(_ffn_to_phase_body, rows=rows, steps_a=steps_a, n_steps=n_steps, n_riders=len(riders)),
        grid=(n_steps,),
        in_specs=[
            pl.BlockSpec(memory_space=pl.ANY),
            pl.BlockSpec(memory_space=pl.ANY),
            _const_spec((1, d)),
            _const_spec((d, d_ff)),
            _const_spec((d, d_ff)),
            _const_spec((d_ff, d)),
        ] + rider_in,
        out_specs=[pl.BlockSpec((N_PHASE, rows, d), lambda i: (0, i, 0))] + rider_out,
        out_shape=[jax.ShapeDtypeStruct((N_PHASE, n_rows, d), F32)] + rider_shapes,
        scratch_shapes=[pltpu.VMEM((2, N_PHASE, rows, d), F32), pltpu.SemaphoreType.DMA((2,))],
        name="ffn_to_phase",
        compiler_params=_params(("arbitrary",)),
    )(x_a.reshape(rows_a, N_PHASE, d), x_b.reshape(rows_b, N_PHASE, d), g, wg, wu, wd, *riders)
    return out[0], tuple(out[1:])


def _mix_in_compute(x_ref, g, win_ref, wcv, wco_ref, shift, rows):
    d = D_MODEL
    h = _rms(_stack_phases(x_ref), g).astype(BF16)

    def proj(k):
        return _dot(h, win_ref[:, k * d:(k + 1) * d])

    z = proj(1) * proj(0)
    b_gate = proj(2)
    u = proj(3)
    g_conv = jax.nn.sigmoid(proj(4))
    gs = jax.nn.sigmoid(proj(5))
    z6 = z[6 * rows:7 * rows]
    z7 = z[7 * rows:8 * rows]
    s6 = shift(z6, 0)
    s7 = shift(z7, 1)
    z1 = jnp.concatenate([s7, z[:7 * rows]], axis=0)
    z2 = jnp.concatenate([s6, s7, z[:6 * rows]], axis=0)
    conv = wcv[2:3] * z + wcv[1:2] * z1 + wcv[0:1] * z2
    a = g_conv * _dot((b_gate * conv).astype(BF16), wco_ref[...])
    return u, a, gs, z6, z7


def _store_phases(ref, val, rows):
    val = val.astype(ref.dtype)
    for t in range(N_PHASE):
        ref[t] = val[t * rows:(t + 1) * rows]


def _store_sub_slab_pairs(ref, val, rows, slab_major=False):
    for k in range(N_PHASE // 2):
        a = val[2 * k * rows:(2 * k + 1) * rows]
        b = val[(2 * k + 1) * rows:(2 * k + 2) * rows]
        for q in range(D_MODEL // LANE):
            lanes = slice(q * LANE, (q + 1) * LANE)
            for h, tile in enumerate(_pair_halves(a[:, lanes], b[:, lanes])):
                if slab_major:
                    ref[q, 2 * k + h] = tile.astype(ref.dtype)
                else:
                    ref[2 * k + h, :, lanes] = tile.astype(ref.dtype)


def _mix_in_prompt_body(*refs, rows, n_riders):
    x_ref, g_ref, win_ref, wcv_ref, wco_ref = refs[:5]
    u_ref, a_ref, gs_ref, z_ref = refs[5 + n_riders:9 + n_riders]
    carry_ref = refs[-1]
    _convert_riders(refs[5:5 + n_riders], refs[9 + n_riders:9 + 2 * n_riders])

    @pl.when(pl.program_id(1) == 0)
    def _():
        carry_ref[...] = jnp.zeros_like(carry_ref)

    row_id = lax.broadcasted_iota(jnp.int32, (rows, D_MODEL), 0)

    def shift(z, k):
        return jnp.where(row_id == 0, carry_ref[k:k + 1, :], pltpu.roll(z, 1, 0))

    u, a, gs, z6, z7 = _mix_in_compute(x_ref, g_ref[...], win_ref, wcv_ref[...], wco_ref, shift, rows)
    last6 = z6[rows - 1:rows]
    last7 = z7[rows - 1:rows]
    carry_ref[0:1, :] = last6
    carry_ref[1:2, :] = last7
    z_ref[0:1, :] = last6
    z_ref[1:2, :] = last7
    _store_sub_slab_pairs(u_ref, u, rows, slab_major=True)
    _store_phases(a_ref, a, rows)
    _store_phases(gs_ref, gs, rows)


def _mix_in_sample_body(x_ref, start_ref, g_ref, win_ref, wcv_ref, wco_ref,
                        u_ref, a_ref, gs_ref, z6_ref, z7_ref, slab_ref, *, rows, rows_per_seq):
    row_id = lax.broadcasted_iota(jnp.int32, (rows, D_MODEL), 0)

    def shift(z, k):
        return jnp.where(row_id % rows_per_seq == 0, start_ref[k], pltpu.roll(z, 1, 0))

    u, a, gs, z6, z7 = _mix_in_compute(x_ref, g_ref[...], win_ref, wcv_ref[...], wco_ref, shift, rows)
    for z, z_ref in ((z6, z6_ref), (z7, z7_ref)):
        for lt in range(D_MODEL // LANE):
            slab_ref[...] = z[:, lt * LANE:(lt + 1) * LANE]
            z_ref[:, lt * LANE:(lt + 1) * LANE] = slab_ref[pl.ds(rows_per_seq - 1, rows // rows_per_seq,
                                                                stride=rows_per_seq), :]
    _store_sub_slab_pairs(u_ref, u, rows)
    _store_phases(a_ref, a, rows)
    _store_phases(gs_ref, gs, rows)


def _mix_in(xp, row0, n_rows, conv_prev, g, w_in, w_conv, w_conv_out, n_seq, riders=()):
    d = D_MODEL
    rows_per_seq = n_rows // n_seq
    ph_shape = jax.ShapeDtypeStruct((N_PHASE, n_rows, d), BF16)
    weights = (g, w_in, w_conv, w_conv_out)
    weight_specs = [_const_spec(w.shape) for w in weights]
    if conv_prev is None:
        rows = min(MIX_IN_ROWS, rows_per_seq)
        assert rows_per_seq % rows == 0
        tiles = rows_per_seq // rows
        assert row0 % rows == 0
        ph_spec = pl.BlockSpec((N_PHASE, rows, d), lambda b, i: (0, b * tiles + i, 0))
        x_spec = pl.BlockSpec((N_PHASE, rows, d), lambda b, i: (0, row0 // rows + b * tiles + i, 0))
        last_spec = pl.BlockSpec((None, 2, d), lambda b, i: (b, 0, 0))
        rider_in, rider_out, rider_shapes = _rider_specs(riders, n_seq * tiles, lambda b, i: b * tiles + i)
        u_spec = pl.BlockSpec((d // LANE, N_PHASE, rows, LANE), lambda b, i: (0, 0, b * tiles + i, 0))
        u_shape = jax.ShapeDtypeStruct((d // LANE, N_PHASE, n_rows, LANE), BF16)
        u8, a8, gs8, conv_state, *converted = pl.pallas_call(
            functools.partial(_mix_in_prompt_body, rows=rows, n_riders=len(riders)),
            grid=(n_seq, tiles),
            in_specs=[x_spec] + weight_specs + rider_in,
            out_specs=[u_spec, ph_spec, ph_spec, last_spec] + rider_out,
            out_shape=[u_shape, ph_shape, ph_shape, jax.ShapeDtypeStruct((n_seq, 2, d), F32)] + rider_shapes,
            scratch_shapes=[pltpu.VMEM((8, d), F32)],
            name="mix_in_prompt",
            compiler_params=_params(("arbitrary", "arbitrary")),
        )(xp, *weights, *riders)
    else:
        assert not riders
        converted = []
        rows = n_rows
        assert row0 % rows == 0
        start = jnp.repeat(jnp.swapaxes(conv_prev, 0, 1), rows_per_seq, axis=1)
        full = lambda shape: pl.BlockSpec(shape, lambda i: (0,) * len(shape))
        x_spec = pl.BlockSpec((N_PHASE, rows, d), lambda i: (0, row0 // rows, 0))
        u8, a8, gs8, z6, z7 = pl.pallas_call(
            functools.partial(_mix_in_sample_body, rows=rows, rows_per_seq=rows_per_seq),
            grid=(1,),
            in_specs=[x_spec, full((2, rows, d))] + weight_specs,
            out_specs=[full((N_PHASE, rows, d))] * 3 + [full((n_seq, d))] * 2,
            out_shape=[ph_shape, ph_shape, ph_shape,
                       jax.ShapeDtypeStruct((n_seq, d), F32), jax.ShapeDtypeStruct((n_seq, d), F32)],
            scratch_shapes=[pltpu.VMEM((rows, LANE), F32)],
            name="mix_in_sample",
            compiler_params=_params(("arbitrary",)),
        )(xp, start, *weights)
        conv_state = jnp.stack([z6, z7], axis=1)
    return u8, a8, gs8, conv_state, tuple(converted)


def _split_bf16(a):
    hi = a.astype(BF16)
    return hi, (a - hi.astype(F32)).astype(BF16)


def _ssm_prep_body(*refs, n_riders):
    mats_ref, vecs_ref = refs[:2]
    wb_ref, kt_ref, wc_ref, pw_ref = refs[2 + n_riders:6 + n_riders]
    _convert_riders(refs[2:2 + n_riders], refs[6 + n_riders:])
    bbre_ref, bbim_ref, ctre_ref, ctim_ref = (mats_ref.at[i] for i in range(4))
    lre_ref, lim_ref, lstep_ref, d_ref = (vecs_ref.at[i] for i in range(4))
    ns = SUB_STATES
    row_group = lax.broadcasted_iota(jnp.int32, (SUB, LANE), 0) // GROUP_SIZE
    lane_half = lax.broadcasted_iota(jnp.int32, (SUB, LANE), 1) // STATE_DIM
    low_half = lax.broadcasted_iota(jnp.int32, (SUB, LANE), 1) < SUB

    def block_diag(ref):
        tiles = [jnp.where(row_group == 2 * k + lane_half, ref[...], 0.0) for k in range(ns // LANE)]
        return jnp.concatenate(tiles, axis=1)

    bbre, bbim = block_diag(bbre_ref), block_diag(bbim_ref)
    ctre, ctim = block_diag(ctre_ref), block_diag(ctim_ref)
    lre, lim = lre_ref[...], lim_ref[...]
    step = jnp.exp(lstep_ref[...])

    def lam_pow(n):
        mag = jnp.exp((n * lre) * step)
        ang = (n * lim) * step
        return mag * jnp.cos(ang), mag * jnp.sin(ang)

    l1re, l1im = lam_pow(1)
    den = lre * lre + lim * lim
    fre = ((l1re - 1.0) * lre + l1im * lim) / den
    fim = (l1im * lre - (l1re - 1.0) * lim) / den

    def dot_nt(a, b_split):
        nt = lambda p, q: lax.dot_general(p, q, (((1,), (1,)), ((), ())), preferred_element_type=F32)
        a_hi, a_lo = _split_bf16(a)
        b_hi, b_lo = b_split
        return nt(a_hi, b_hi) + nt(a_hi, b_lo) + nt(a_lo, b_hi)

    def c_pair(n):
        (p0re, p0im), (p1re, p1im) = lam_pow(n), lam_pow(n + 1)
        return (jnp.concatenate([ctre * p0re - ctim * p0im, ctre * p1re - ctim * p1im], axis=0),
                jnp.concatenate([ctre * p0im + ctim * p0re, ctre * p1im + ctim * p1re], axis=0))

    c01re, c01im = c_pair(0)
    c01re_split = _split_bf16(c01re)
    c01im_split = _split_bf16(c01im)
    lag_pairs = {}
    for k in range(N_PHASE):
        pre, pim = lam_pow(k)
        gre = fre * pre - fim * pim
        gim = fre * pim + fim * pre
        are = bbre * gre - bbim * gim
        aim = bbre * gim + bbim * gre
        j = N_PHASE - 1 - k
        wb_ref[j * SUB:(j + 1) * SUB, 0:ns] = are.astype(BF16)
        wb_ref[j * SUB:(j + 1) * SUB, ns:2 * ns] = aim.astype(BF16)
        if k < N_PHASE - 1:
            lag_pairs[k] = dot_nt(are, c01re_split) - dot_nt(aim, c01im_split)
    on_diagonal = (lax.broadcasted_iota(jnp.int32, (SUB, LANE), 0) == lax.broadcasted_iota(jnp.int32, (SUB, LANE), 1))
    lag_pairs[0] = lag_pairs[0] + jnp.where(on_diagonal, d_ref[:, :LANE], 0.0)
    kt_ref[...] = jnp.zeros_like(kt_ref)
    first_odd = jnp.where(low_half, 0.0, pltpu.roll(lag_pairs[0], SUB, 1))
    for j in range(N_PHASE):
        for tile in range(j // 2, N_PHASE // 2):
            lag = 2 * tile - j
            pair = first_odd if lag < 0 else lag_pairs[lag]
            kt_ref[j * SUB:(j + 1) * SUB, tile * LANE:(tile + 1) * LANE] = pair.astype(BF16)
    for tile in range(N_PHASE // 2):
        cre, cim = c_pair(2 * tile + 1)
        wc_ref[0:ns, tile * LANE:(tile + 1) * LANE] = cre.T.astype(BF16)
        wc_ref[ns:2 * ns, tile * LANE:(tile + 1) * LANE] = (-cim).T.astype(BF16)
    pw_ref[...] = jnp.zeros_like(pw_ref)
    for r, n in enumerate((N_PHASE, N_PHASE * SEG)):
        pre, pim = lam_pow(n)
        pw_ref[r:r + 1, 0:ns] = pre
        pw_ref[r:r + 1, ns:2 * ns] = pim


def _ssm_prep(lam_re, lam_im, log_step, b_re, b_im, c_re, c_im, d_skip, riders=()):
    ns = SUB_STATES
    n_sub = N_SLAB * SUBS
    assert 2 * STATE_DIM == LANE and SUB_K == 2 * ns

    def twice(m):
        m = m.reshape(n_sub, SUB, STATE_DIM)
        return jnp.concatenate([m, m], axis=-1)

    def vec(v):
        return v.reshape(n_sub, 1, ns)

    mats = jnp.stack([twice(jnp.transpose(b_re, (0, 2, 1))), twice(jnp.transpose(b_im, (0, 2, 1))),
                      twice(c_re), twice(c_im)], axis=1)
    vecs = jnp.stack([vec(lam_re), vec(lam_im), vec(jnp.broadcast_to(log_step[:, None], lam_re.shape)),
                      jnp.tile(d_skip.reshape(n_sub, 1, SUB), (1, 1, ns // SUB))], axis=1)
    w_shape = jax.ShapeDtypeStruct((n_sub, SUB_K, 2 * ns), BF16)
    w_spec = pl.BlockSpec((None, SUB_K, 2 * ns), lambda q: (q, 0, 0))
    rider_in, rider_out, rider_shapes = _rider_specs(riders, n_sub)
    out = pl.pallas_call(
        functools.partial(_ssm_prep_body, n_riders=len(riders)),
        grid=(n_sub,),
        in_specs=[pl.BlockSpec((None, 4, SUB, LANE), lambda q: (q, 0, 0, 0)),
                  pl.BlockSpec((None, 4, 1, ns), lambda q: (q, 0, 0, 0))] + rider_in,
        out_specs=[w_spec, w_spec, w_spec, pl.BlockSpec((None, 8, 2 * ns), lambda q: (q, 0, 0))] + rider_out,
        out_shape=[w_shape, w_shape, w_shape, jax.ShapeDtypeStruct((n_sub, 8, 2 * ns), F32)] + rider_shapes,
        name="ssm_prep",
        compiler_params=_params(("arbitrary",)),
    )(mats, vecs, *riders)
    return tuple(w.reshape(N_SLAB, SUBS, *w.shape[1:]) for w in out[:4]), tuple(out[4:])


def _cmul(are, aim, bre, bim):
    return are * bre - aim * bim, are * bim + aim * bre


HALF = SUB_STATES // LANE
GROUP_ROWS = 8 * SEG


def _pair_halves(a, b):
    low = lax.broadcasted_iota(jnp.int32, a.shape, 1) < SUB
    return jnp.where(low, a, pltpu.roll(b, SUB, 1)), jnp.where(low, pltpu.roll(a, SUB, 1), b)


def _sub_slab_operands(u_ref):
    return [jnp.concatenate([u_ref[2 * k + h].reshape(-1, LANE) for k in range(N_PHASE // 2)], axis=1)
            for h in range(SUBS)]


def _slab_phase_tiles(subs):
    tiles = []
    for k in range(N_PHASE // 2):
        tiles += _pair_halves(subs[0][:, k * LANE:(k + 1) * LANE], subs[1][:, k * LANE:(k + 1) * LANE])
    return [y.astype(BF16) for y in tiles]


def _mult(pw_ref, r, c):
    return (pw_ref[r:r + 1, c * LANE:(c + 1) * LANE],
            pw_ref[r:r + 1, (HALF + c) * LANE:(HALF + c + 1) * LANE])


def _state_increments(lhs, wb_ref, e_ref):
    e = _dot(lhs, wb_ref[...])
    for lt in range(2 * HALF):
        e_ref[lt, :e.shape[0]] = e[:, lt * LANE:(lt + 1) * LANE]


def _scan_local(e_ref, p_ref, pw_ref, c, base):
    l8re, l8im = _mult(pw_ref, 0, c)
    idx = pl.ds(base, 8, stride=SEG)
    lre = e_ref[c, idx, :]
    lim = e_ref[HALF + c, idx, :]
    for i in range(1, SEG):
        idx = pl.ds(base + i, 8, stride=SEG)
        p_ref[c, idx, :] = lre
        p_ref[HALF + c, idx, :] = lim
        mre, mim = _cmul(l8re, l8im, lre, lim)
        lre = mre + e_ref[c, idx, :]
        lim = mim + e_ref[HALF + c, idx, :]
    return lre, lim


def _scan_fixup(p_ref, pw_ref, c, base, cre, cim):
    l8re, l8im = _mult(pw_ref, 0, c)
    idx = pl.ds(base, 8, stride=SEG)
    p_ref[c, idx, :] = cre
    p_ref[HALF + c, idx, :] = cim
    for i in range(1, SEG):
        idx = pl.ds(base + i, 8, stride=SEG)
        cre, cim = _cmul(l8re, l8im, cre, cim)
        p_ref[c, idx, :] = p_ref[c, idx, :] + cre
        p_ref[HALF + c, idx, :] = p_ref[HALF + c, idx, :] + cim


def _direct_terms(lhs, kt_ref):
    return _dot(lhs, kt_ref[...])


def _state_term(p_ref, n_rows, wc_ref):
    prev = jnp.concatenate([p_ref[lt, :n_rows] for lt in range(2 * HALF)], axis=1)
    return _dot(prev.astype(BF16), wc_ref[...])


def _ssm_prompt_body(u_ref, wb_ref, kt_ref, wc_ref, pw_ref, y_ref, sre_ref, sim_ref,
                     e_ref, p_ref, end_ref, cin_ref, carry_ref, *, n_seq, rows, tile):
    @pl.when((pl.program_id(0) == 0) & (tile == 0))
    def _():
        end_ref[...] = jnp.zeros_like(end_ref)

    @pl.when(tile == 0)
    def _():
        carry_ref[...] = jnp.zeros_like(carry_ref)

    operands = _sub_slab_operands(u_ref)
    groups = rows // GROUP_ROWS
    for h, lhs in enumerate(operands):
        _state_increments(lhs, wb_ref.at[h], e_ref.at[h])
    outs = [_direct_terms(lhs, kt_ref.at[h]) for h, lhs in enumerate(operands)]
    for h in range(SUBS):
        e, p, ends, cins, carry, pw = e_ref.at[h], p_ref.at[h], end_ref.at[h], cin_ref.at[h], carry_ref.at[h], pw_ref.at[h]
        lane0 = h * SUB_STATES
        for c in range(HALF):
            for b in range(n_seq):
                for j in range(groups):
                    lre, lim = _scan_local(e, p, pw, c, b * rows + j * GROUP_ROWS)
                    idx = pl.ds(j * 64 + b, 8, stride=8)
                    ends[c, idx, :] = lre
                    ends[HALF + c, idx, :] = lim
        for c in range(HALF):
            lsre, lsim = _mult(pw, 1, c)
            cre = carry[c]
            cim = carry[HALF + c]
            for s in range(rows // SEG):
                cins[c, 8 * s:8 * s + 8, :] = cre
                cins[HALF + c, 8 * s:8 * s + 8, :] = cim
                mre, mim = _cmul(lsre, lsim, cre, cim)
                cre = mre + ends[c, 8 * s:8 * s + 8, :]
                cim = mim + ends[HALF + c, 8 * s:8 * s + 8, :]
            carry[c] = cre
            carry[HALF + c] = cim
            sre_ref[:, lane0 + c * LANE:lane0 + (c + 1) * LANE] = cre[:n_seq]
            sim_ref[:, lane0 + c * LANE:lane0 + (c + 1) * LANE] = cim[:n_seq]
        for c in range(HALF):
            for b in range(n_seq):
                for j in range(groups):
                    idx = pl.ds(j * 64 + b, 8, stride=8)
                    _scan_fixup(p, pw, c, b * rows + j * GROUP_ROWS, cins[c, idx, :], cins[HALF + c, idx, :])
        outs[h] = outs[h] + _state_term(p, n_seq * rows, wc_ref.at[h])
    for t, y in enumerate(_slab_phase_tiles(outs)):
        y_ref[t] = y.reshape(n_seq, rows, LANE)


def _ssm_sample_body(u_ref, wb_ref, kt_ref, wc_ref, pw_ref, h0re_ref, h0im_ref,
                     y_ref, sre_ref, sim_ref, e_ref, p_ref, *, rows):
    operands = _sub_slab_operands(u_ref)
    for h, lhs in enumerate(operands):
        _state_increments(lhs, wb_ref.at[h], e_ref.at[h])
    outs = [_direct_terms(lhs, kt_ref.at[h]) for h, lhs in enumerate(operands)]
    for h in range(SUBS):
        e, p, pw = e_ref.at[h], p_ref.at[h], pw_ref.at[h]
        for c in range(HALF):
            lanes = slice(h * SUB_STATES + c * LANE, h * SUB_STATES + (c + 1) * LANE)
            lsre, lsim = _mult(pw, 1, c)
            for j in range(rows // GROUP_ROWS):
                lre, lim = _scan_local(e, p, pw, c, j * GROUP_ROWS)
                cre = h0re_ref[8 * j:8 * j + 8, lanes]
                cim = h0im_ref[8 * j:8 * j + 8, lanes]
                mre, mim = _cmul(lsre, lsim, cre, cim)
                sre_ref[8 * j:8 * j + 8, lanes] = mre + lre
                sim_ref[8 * j:8 * j + 8, lanes] = mim + lim
                _scan_fixup(p, pw, c, j * GROUP_ROWS, cre, cim)
        outs[h] = outs[h] + _state_term(p, rows, wc_ref.at[h])
    for t, y in enumerate(_slab_phase_tiles(outs)):
        y_ref[t] = y


def _ssm_body(u_ref, wb_ref, kt_ref, wc_ref, pw_ref, us_ref, h0re_ref, h0im_ref,
              y_ref, sre_ref, sim_ref, ys_ref, sres_ref, sims_ref,
              e_ref, p_ref, end_ref, cin_ref, carry_ref, *, n_seq, rows, rows_s):
    step = pl.program_id(1)

    @pl.when(step == 0)
    def _():
        _ssm_sample_body(us_ref, wb_ref, kt_ref, wc_ref, pw_ref, h0re_ref, h0im_ref,
                         ys_ref, sres_ref, sims_ref, e_ref, p_ref, rows=rows_s)

    @pl.when(step > 0)
    def _():
        _ssm_prompt_body(u_ref, wb_ref, kt_ref, wc_ref, pw_ref, y_ref, sre_ref, sim_ref,
                         e_ref, p_ref, end_ref, cin_ref, carry_ref, n_seq=n_seq, rows=rows, tile=step - 1)


def _ssm(u8_p, n_p, u8_s, n_s, h0, wb, kt, wc, pw):
    d = D_MODEL
    ns = SUBS * SUB_STATES
    rows_p, rows_s = u8_p.shape[2], u8_s.shape[1]
    per_seq = rows_p // n_p
    assert rows_s // n_s == SEG, "carried-state path scans one segment per sequence"
    assert n_p <= 8, "sequences ride the sublanes of the segment chain"
    rows = min(SSM_ROWS, per_seq)
    assert per_seq % rows == 0 and rows % GROUP_ROWS == 0 and rows_s % GROUP_ROWS == 0
    assert rows_s <= n_p * rows, "the carried-state step reuses the scan scratch"
    tiles = per_seq // rows
    slab = lambda q, i: (q, 0, 0, 0)
    p_spec = pl.BlockSpec((None, N_PHASE, n_p, rows, LANE), lambda q, i: (q, 0, 0, jnp.maximum(i - 1, 0), 0))
    s_spec = pl.BlockSpec((N_PHASE, rows_s, LANE), lambda q, i: (0, 0, q))
    ys_spec = pl.BlockSpec((None, N_PHASE, rows_s, LANE), slab)
    w_spec = pl.BlockSpec((None, SUBS, SUB_K, 2 * SUB_STATES), slab)
    state_p = pl.BlockSpec((n_p, ns), lambda q, i: (0, q))
    state_s = pl.BlockSpec((n_s, ns), lambda q, i: (0, q))
    seg_rows = 8 * (rows // SEG)
    dims_p = (N_SLAB, N_PHASE, n_p, per_seq, LANE)
    y_p, re_p, im_p, y_s, re_s, im_s = pl.pallas_call(
        functools.partial(_ssm_body, n_seq=n_p, rows=rows, rows_s=rows_s),
        grid=(N_SLAB, tiles + 1),
        in_specs=[p_spec, w_spec, w_spec, w_spec, pl.BlockSpec((None, SUBS, 8, 2 * SUB_STATES), slab),
                  s_spec, state_s, state_s],
        out_specs=[p_spec, state_p, state_p, ys_spec, state_s, state_s],
        out_shape=[jax.ShapeDtypeStruct(dims_p, BF16),
                   jax.ShapeDtypeStruct((n_p, N_SLAB * ns), F32), jax.ShapeDtypeStruct((n_p, N_SLAB * ns), F32),
                   jax.ShapeDtypeStruct((N_SLAB, N_PHASE, rows_s, LANE), BF16),
                   jax.ShapeDtypeStruct((n_s, N_SLAB * ns), F32), jax.ShapeDtypeStruct((n_s, N_SLAB * ns), F32)],
        scratch_shapes=[pltpu.VMEM((SUBS, 2 * HALF, n_p * rows, LANE), F32),
                        pltpu.VMEM((SUBS, 2 * HALF, n_p * rows, LANE), F32),
                        pltpu.VMEM((SUBS, 2 * HALF, seg_rows, LANE), F32),
                        pltpu.VMEM((SUBS, 2 * HALF, seg_rows, LANE), F32),
                        pltpu.VMEM((SUBS, 2 * HALF, 8, LANE), F32)],
        name="ssm",
        compiler_params=_params(("arbitrary", "arbitrary")),
    )(u8_p.reshape(dims_p), wb, kt, wc, pw, u8_s, h0[0].reshape(n_s, -1), h0[1].reshape(n_s, -1))
    return (y_p.reshape(N_SLAB, N_PHASE, rows_p, LANE), re_p, im_p), (y_s, re_s, im_s)


def _mix_out_ffn_body(ya_hbm, aa_hbm, gsa_hbm, yb_hbm, ab_hbm, gsb_hbm, x_ref,
                      wglu_ref, wo_ref, g_ref, wg_ref, wu_ref, wd_ref, gf_ref,
                      oa_hbm, ob_hbm, inbuf_ref, ybuf_ref, sem_in, sem_out, *, rows, steps_a, n_steps, final_norm):
    d = D_MODEL
    step = pl.program_id(0)
    slot = step % 2

    def in_copies(srcs, row0, of_slot):
        y_src, *others = srcs
        return [pltpu.make_async_copy(y_src.at[q, :, pl.ds(row0, rows), :],
                                      inbuf_ref.at[of_slot, 0, :, :, pl.ds(q * LANE, LANE)], sem_in.at[of_slot])
                for q in range(d // LANE)] + [
            pltpu.make_async_copy(src.at[:, pl.ds(row0, rows), :], inbuf_ref.at[of_slot, k + 1], sem_in.at[of_slot])
            for k, src in enumerate(others)]

    def out_copies(dst, row0, of_slot):
        return [pltpu.make_async_copy(ybuf_ref.at[of_slot, t], dst.at[pl.ds(row0, rows), t, :], sem_out.at[of_slot])
                for t in range(N_PHASE)]

    def start_for_group(of_step, copies_a, copies_b):
        @pl.when(of_step < steps_a)
        def _():
            for c in copies_a(of_step * rows):
                c.start()

        @pl.when(of_step >= steps_a)
        def _():
            for c in copies_b((of_step - steps_a) * rows):
                c.start()

    def start_in(of_step, of_slot):
        start_for_group(of_step, lambda r: in_copies((ya_hbm, aa_hbm, gsa_hbm), r, of_slot),
                        lambda r: in_copies((yb_hbm, ab_hbm, gsb_hbm), r, of_slot))

    @pl.when(step == 0)
    def _():
        start_in(step, slot)

    @pl.when(step + 1 < n_steps)
    def _():
        start_in(step + 1, 1 - slot)

    @pl.when(step >= 2)
    def _():
        for c in out_copies(oa_hbm, 0, slot):
            c.wait()

    for c in in_copies((ya_hbm, aa_hbm, gsa_hbm), 0, slot):
        c.wait()
    y_ref, a_ref, gs_ref = (inbuf_ref.at[slot, k] for k in range(3))
    part = rows // MIX_OUT_PARTS
    parts = range(MIX_OUT_PARTS)

    def stack(ref, p):
        return jnp.concatenate([ref[t, p * part:(p + 1) * part] for t in range(N_PHASE)], axis=0)

    glus = [_dot(jax.nn.gelu(stack(y_ref, p).astype(F32)).astype(BF16), wglu_ref[...]) for p in parts]
    mergeds = [(stack(a_ref, p).astype(F32) + stack(gs_ref, p).astype(F32) * (glu[:, :d] * jax.nn.sigmoid(glu[:, d:])))
               .astype(BF16) for p, glu in zip(parts, glus)]
    x2s = [stack(x_ref, p) + _dot(merged, wo_ref[...]) for p, merged in zip(parts, mergeds)]
    for p, y in enumerate(_ffn_math(x2s, g_ref, wg_ref, wu_ref, wd_ref)):
        if final_norm:
            y = _rms(y, gf_ref[...])
        for t in range(N_PHASE):
            ybuf_ref[slot, t, p * part:(p + 1) * part] = y[t * part:(t + 1) * part]
    start_for_group(step, lambda r: out_copies(oa_hbm, r, slot), lambda r: out_copies(ob_hbm, r, slot))

    @pl.when(step == n_steps - 1)
    def _():
        if n_steps > 1:
            for c in out_copies(oa_hbm, 0, 1 - slot):
                c.wait()
        for c in out_copies(oa_hbm, 0, slot):
            c.wait()


def _mix_out_ffn(acts_a, acts_b, xp, w_glu, w_o, g, wg, wu, wd, gf, final_norm):
    rows_a, rows_b = acts_a[1].shape[1], acts_b[1].shape[1]
    d = D_MODEL
    rows = MIX_OUT_ROWS
    assert rows_a % rows == 0 and rows_b % rows == 0 and xp.shape[1] == rows_a + rows_b
    steps_a, n_steps = rows_a // rows, (rows_a + rows_b) // rows
    weights = (w_glu, w_o, g, wg, wu, wd, gf)
    any_spec = pl.BlockSpec(memory_space=pl.ANY)
    out_a, out_b = pl.pallas_call(
        functools.partial(_mix_out_ffn_body, rows=rows, steps_a=steps_a, n_steps=n_steps, final_norm=final_norm),
        grid=(n_steps,),
        in_specs=[any_spec] * 6 + [pl.BlockSpec((N_PHASE, rows, d), lambda i: (0, i, 0))]
        + [_const_spec(w.shape) for w in weights],
        out_specs=[any_spec, any_spec],
        out_shape=[jax.ShapeDtypeStruct((rows_a, N_PHASE, d), F32), jax.ShapeDtypeStruct((rows_b, N_PHASE, d), F32)],
        scratch_shapes=[pltpu.VMEM((2, 3, N_PHASE, rows, d), BF16),
                        pltpu.VMEM((2, N_PHASE, rows, d), F32),
                        pltpu.SemaphoreType.DMA((2,)), pltpu.SemaphoreType.DMA((2,))],
        name="mix_out_ffn",
        compiler_params=_params(("arbitrary",)),
    )(*acts_a, *acts_b, xp, *weights)
    return out_a.reshape(N_PHASE * rows_a, d), out_b.reshape(N_PHASE * rows_b, d)


def kernel(x_prompt, x_sample, state_conv, state_ssm_re, state_ssm_im, norm_ffn1, w_ffn1_gate, w_ffn1_up, w_ffn1_down, norm_mix, w_in, w_conv, w_conv_out, ssm_lambda_re, ssm_lambda_im, ssm_log_step, ssm_b_re, ssm_b_im, ssm_c_re, ssm_c_im, ssm_d, w_glu, w_o, norm_ffn2, w_ffn2_gate, w_ffn2_up, w_ffn2_down, norm_final):
    depth, d = w_in.shape[0], w_in.shape[1]
    row = lambda v: v.reshape(1, -1)
    nf = row(norm_final)
    xt_p = x_prompt.reshape(-1, d)
    xt_s = x_sample.reshape(-1, d)
    n_p, n_s = x_prompt.shape[0], x_sample.shape[0]
    outs_p, outs_s = [], []
    for l in range(depth):
        last = l == depth - 1
        ssm_w, (f1g, f1u, f1d) = _ssm_prep(
            ssm_lambda_re[l], ssm_lambda_im[l], ssm_log_step[l], ssm_b_re[l], ssm_b_im[l], ssm_c_re[l],
            ssm_c_im[l], ssm_d[l], riders=(w_ffn1_gate[l], w_ffn1_up[l], w_ffn1_down[l]))
        ffn1 = (row(norm_ffn1[l]), f1g, f1u, f1d)
        xp, (win, wco, wgl, wo) = _ffn_to_phase(xt_p, xt_s, *ffn1, riders=(w_in[l], w_conv_out[l], w_glu[l], w_o[l]))
        rows_p, rows_s = xt_p.shape[0] // N_PHASE, xt_s.shape[0] // N_PHASE
        mix_w = (row(norm_mix[l]), win, w_conv[l], wco)
        u_p, a_p, gs_p, conv_p, ffn2_w = _mix_in(xp, 0, rows_p, None, *mix_w, n_p,
                                                 riders=(w_ffn2_gate[l], w_ffn2_up[l], w_ffn2_down[l]))
        u_s, a_s, gs_s, conv_s, _ = _mix_in(xp, rows_p, rows_s, state_conv[l], *mix_w, n_s)
        (y_p, re_p, im_p), (y_s, re_s, im_s) = _ssm(u_p, n_p, u_s, n_s, (state_ssm_re[l], state_ssm_im[l]),
                                                    *ssm_w)
        tail = (wgl, wo, row(norm_ffn2[l]), *ffn2_w, nf, last)
        xt_p, xt_s = _mix_out_ffn((y_p, a_p, gs_p), (y_s, a_s, gs_s), xp, *tail)
        outs_p.append([conv_p, re_p.reshape(n_p, -1, STATE_DIM), im_p.reshape(n_p, -1, STATE_DIM)])
        outs_s.append([conv_s, re_s.reshape(n_s, -1, STATE_DIM), im_s.reshape(n_s, -1, STATE_DIM)])
    stack = lambda outs: tuple(jnp.stack(leaf) for leaf in zip(*outs))
    return (xt_p.reshape(x_prompt.shape), xt_s.reshape(x_sample.shape), *stack(outs_p), *stack(outs_s))
```
